```python
import jax, jax.numpy as jnp
from jax import lax
import numpy as np

D_MODEL = 1024
BATCH = 2
SEQ = 8192
DEPTH = 2

GRID_W = 64
CTX_LEN = 256
ROPE_BASE = 10000.0
Q_BLOCK = 128
DEEPNORM_ALPHA = (2 * DEPTH) ** 0.25
DEEPNORM_BETA = (8 * DEPTH) ** -0.25
LN_EPS = 1e-5

MLA_HEADS = 8
MLA_Q_RANK = 256
MLA_KV_RANK = 128
MLA_NOPE = 64
MLA_ROPE = 32
MLA_V = 64
MLA_IN = MLA_Q_RANK + MLA_KV_RANK + MLA_ROPE
MLA_OUT = MLA_HEADS * MLA_V

RWKV_HEADS = 8
RWKV_HEAD = 64
RWKV_DIM = RWKV_HEADS * RWKV_HEAD
RWKV_DECAY_LORA = 64
RWKV_AAA_LORA = 64
RWKV_GATE_LORA = 128
RWKV_IN = 3 * RWKV_DIM + 2 * RWKV_DECAY_LORA + 2 * RWKV_AAA_LORA + RWKV_GATE_LORA
RWKV_LNX_EPS = 64e-5

EVEN_IN = MLA_IN + RWKV_IN
EVEN_MIX = MLA_OUT + RWKV_DIM

RET_HEADS = 4
RET_KEY = 256
RET_VAL = 512
RET_CHUNK = 128
RET_QK = RET_HEADS * RET_KEY
RET_VD = RET_HEADS * RET_VAL
ODD_IN = 2 * RET_QK + 3 * RET_VD

D_FF = 2816
N_EXPERTS = 8
TOP_K = 2
D_FF_EXPERT = 3584
MOE_BLOCK = 128

kernel_name = 'hybrid_mla_rwkv7_retention_moe_block'


def _silu(t):
    return t * jax.nn.sigmoid(t)


def _normalize(t, eps):
    tf = t.astype(jnp.float32)
    mu = jnp.mean(tf, axis=-1, keepdims=True)
    var = jnp.mean(jnp.square(tf - mu), axis=-1, keepdims=True)
    return ((tf - mu) * lax.rsqrt(var + eps)).astype(t.dtype)


def _layer_norm(t, g, b):
    return _normalize(t, LN_EPS) * g + b


def _rms_norm(t, g, eps=1e-6):
    tf = t.astype(jnp.float32)
    return (tf * lax.rsqrt(jnp.mean(tf * tf, axis=-1, keepdims=True) + eps)).astype(t.dtype) * g


def _l2_normalize(t, eps=1e-12):
    tf = t.astype(jnp.float32)
    return (tf / jnp.maximum(jnp.linalg.norm(tf, axis=-1, keepdims=True), eps)).astype(t.dtype)


def _post_norm(x, update, g, b):
    return _layer_norm(DEEPNORM_ALPHA * x + update, g, b)


def _modulate(h, shift, scale):
    return h * (1.0 + scale) + shift


def _adaln(cond, mod_w, mod_b):
    return jnp.split(cond @ mod_w + mod_b, 6, axis=-1)


def _swiglu(h, w_gate, w_up, w_down):
    return (_silu(h @ w_gate) * (h @ w_up)) @ w_down


def _centred_conv3(t, w):
    tp = jnp.pad(t, ((0, 0), (1, 1), (0, 0)))
    return tp[:, :-2] * w[0] + tp[:, 1:-1] * w[1] + tp[:, 2:] * w[2]


def _grid_positions(n_tokens):
    rows = n_tokens // GRID_W
    row = jnp.repeat(jnp.arange(rows, dtype=jnp.float32), GRID_W)
    col = jnp.tile(jnp.arange(GRID_W, dtype=jnp.float32), rows)
    return row, col


def _rotate(t, pos):
    nf = t.shape[-1] // 2
    inv_freq = ROPE_BASE ** (-jnp.arange(nf, dtype=jnp.float32) / nf)
    ang = pos[:, None] * inv_freq[None, :]
    cos = jnp.cos(ang)[None, :, None, :].astype(t.dtype)
    sin = jnp.sin(ang)[None, :, None, :].astype(t.dtype)
    t1, t2 = t[..., :nf], t[..., nf:]
    return jnp.concatenate([t1 * cos - t2 * sin, t1 * sin + t2 * cos], axis=-1)


def _axial_rope(t, row, col):
    half = t.shape[-1] // 2
    return jnp.concatenate([_rotate(t[..., :half], row), _rotate(t[..., half:], col)], axis=-1)


def _block_attention(q, k, v):
    b, t, h, dq = q.shape
    nb = t // Q_BLOCK
    scale = dq ** -0.5
    qb = jnp.moveaxis(q.reshape(b, nb, Q_BLOCK, h, dq), 1, 0)

    def one_block(q_blk):
        s = jnp.einsum('bqhd,bkhd->bhqk', q_blk, k).astype(jnp.float32) * scale
        p = jax.nn.softmax(s, axis=-1).astype(v.dtype)
        return jnp.einsum('bhqk,bkhd->bqhd', p, v)

    ob = lax.map(one_block, qb)
    return jnp.moveaxis(ob, 0, 1).reshape(b, t, h, v.shape[-1])


def _mla_project(zz, q_norm, wq_up, kv_norm, wkv_up):
    b, t, _ = zz.shape
    cq, ckv, k_rope = jnp.split(zz, [MLA_Q_RANK, MLA_Q_RANK + MLA_KV_RANK], axis=-1)
    q = (_rms_norm(cq, q_norm) @ wq_up).reshape(b, t, MLA_HEADS, MLA_NOPE + MLA_ROPE)
    kv = (_rms_norm(ckv, kv_norm) @ wkv_up).reshape(b, t, MLA_HEADS, MLA_NOPE + MLA_V)
    return q, kv[..., :MLA_NOPE], k_rope[:, :, None, :], kv[..., MLA_NOPE:]


def _mla_keys(k_nope, k_rope):
    b, t, h, _ = k_nope.shape
    return jnp.concatenate([k_nope, jnp.broadcast_to(k_rope, (b, t, h, MLA_ROPE))], axis=-1)


def _mla_mixer(z, zc, q_norm, wq_up, kv_norm, wkv_up):
    b, t, _ = z.shape
    row, col = _grid_positions(t)
    q, k_nope, k_rope, v = _mla_project(z, q_norm, wq_up, kv_norm, wkv_up)
    qc, k_nope_c, k_rope_c, vc = _mla_project(zc, q_norm, wq_up, kv_norm, wkv_up)
    q = jnp.concatenate([q[..., :MLA_NOPE], _axial_rope(q[..., MLA_NOPE:], row, col)], axis=-1)
    k = _mla_keys(k_nope, _axial_rope(k_rope, row, col))
    kc = _mla_keys(k_nope_c, k_rope_c)
    o = _block_attention(q, jnp.concatenate([kc, k], axis=1), jnp.concatenate([vc, v], axis=1))
    oc = _block_attention(qc, kc, vc)
    return o.reshape(b, t, MLA_OUT), oc.reshape(b, zc.shape[1], MLA_OUT)


def _rwkv7_scan(s0, r, w, k, v, a, b, reverse):
    def step(s, xs):
        rt, wt, kt, vt, at, bt = xs
        sa = jnp.einsum('bhvk,bhk->bhv', s, at)
        s = s * wt[:, :, None, :] + sa[..., None] * bt[:, :, None, :] + vt[..., None] * kt[:, :, None, :]
        return s, jnp.einsum('bhvk,bhk->bhv', s, rt)

    xs = tuple(jnp.moveaxis(t.astype(jnp.float32), 1, 0) for t in (r, w, k, v, a, b))
    s_fin, y = lax.scan(step, s0, xs, reverse=reverse)
    return jnp.moveaxis(y, 0, 1), s_fin


def _rwkv7_features(z, conv_w, w0_f, w2_f, w0_b, w2_b, a0_f, a2_f, a0_b, a2_b, g2, k_k, k_a):
    b, t, _ = z.shape
    c0 = 3 * RWKV_DIM
    c1 = c0 + RWKV_DECAY_LORA
    c2 = c1 + RWKV_DECAY_LORA
    c3 = c2 + RWKV_AAA_LORA
    c4 = c3 + RWKV_AAA_LORA
    rkv_raw, wd_f, wd_b, ad_f, ad_b, gd = jnp.split(z, [c0, c1, c2, c3, c4], axis=-1)
    rkv = _centred_conv3(rkv_raw, conv_w)
    r, k, v = jnp.split(rkv, 3, axis=-1)
    g = jax.nn.sigmoid(gd) @ g2

    def heads(u):
        return u.reshape(b, t, RWKV_HEADS, RWKV_HEAD)

    kk = _l2_normalize(heads(k * k_k))

    def direction(wd, w0, w2, ad, a0, a2):
        logw = -jax.nn.softplus(-(w0 + jnp.tanh(wd) @ w2)) - 0.5
        decay = jnp.exp(-jnp.exp(logw.astype(jnp.float32)))
        lr = jax.nn.sigmoid(a0 + ad @ a2)
        kd = k * (1.0 + (lr - 1.0) * k_a)
        return heads(decay), heads(kd), heads(lr)

    fwd = direction(wd_f, w0_f, w2_f, ad_f, a0_f, a2_f)
    bwd = direction(wd_b, w0_b, w2_b, ad_b, a0_b, a2_b)
    return heads(r), heads(v), kk, g, fwd, bwd


def _rwkv7_readout(y_f, y_b, r, k_f, k_b, v, g, r_k, lnx_g, lnx_b):
    b, t = r.shape[:2]
    y = _normalize(y_f + y_b, RWKV_LNX_EPS).astype(r.dtype).reshape(b, t, RWKV_DIM) * lnx_g + lnx_b
    bonus = jnp.sum(r * (k_f + k_b) * r_k, axis=-1, keepdims=True) * v
    return (y + bonus.reshape(b, t, RWKV_DIM)) * g


def _rwkv7_mixer(z, zc, conv_w, w0_f, w2_f, w0_b, w2_b, a0_f, a2_f, a0_b, a2_b, g2, k_k, k_a,
                 r_k, lnx_g, lnx_b):
    p = (conv_w, w0_f, w2_f, w0_b, w2_b, a0_f, a2_f, a0_b, a2_b, g2, k_k, k_a)
    r, v, kk, g, (w_f, k_f, l_f), (w_b, k_b, l_b) = _rwkv7_features(z, *p)
    rc, vc, kkc, gc, (wc_f, kc_f, lc_f), (wc_b, kc_b, lc_b) = _rwkv7_features(zc, *p)
    s0 = jnp.zeros((zc.shape[0], RWKV_HEADS, RWKV_HEAD, RWKV_HEAD), jnp.float32)
    yc_f, sc_f = _rwkv7_scan(s0, rc, wc_f, kc_f, vc, -kkc, kkc * lc_f, reverse=False)
    yc_b, sc_b = _rwkv7_scan(s0, rc, wc_b, kc_b, vc, -kkc, kkc * lc_b, reverse=True)
    y_f, _ = _rwkv7_scan(sc_f, r, w_f, k_f, v, -kk, kk * l_f, reverse=False)
    y_b, _ = _rwkv7_scan(sc_b, r, w_b, k_b, v, -kk, kk * l_b, reverse=True)
    o = _rwkv7_readout(y_f, y_b, r, k_f, k_b, v, g, r_k, lnx_g, lnx_b)
    oc = _rwkv7_readout(yc_f, yc_b, rc, kc_f, kc_b, vc, gc, r_k, lnx_g, lnx_b)
    return o, oc


def _retention_chunkwise(q, k, v, gamma, s0):
    b, t, h, dk = q.shape
    dv = v.shape[-1]
    nc = t // RET_CHUNK
    log_g = jnp.log(gamma)
    pos = jnp.arange(RET_CHUNK, dtype=jnp.float32)
    rel = pos[:, None] - pos[None, :]
    intra = jnp.where(rel >= 0, jnp.exp(jnp.maximum(rel, 0.0)[None] * log_g[:, None, None]), 0.0)
    q_dec = jnp.exp((pos[:, None] + 1.0) * log_g[None, :])
    k_dec = jnp.exp((RET_CHUNK - 1.0 - pos[:, None]) * log_g[None, :])
    c_dec = jnp.exp(RET_CHUNK * log_g)

    def chunks(u):
        return jnp.moveaxis(u.astype(jnp.float32).reshape(b, nc, RET_CHUNK, h, u.shape[-1]), 1, 0)

    def step(s, xs):
        qi, ki, vi = xs
        att = jnp.einsum('bihd,bjhd->bhij', qi, ki) * intra
        o = (jnp.einsum('bhij,bjhe->bihe', att, vi)
             + jnp.einsum('bihd,bhde->bihe', qi * q_dec[None, :, :, None], s))
        s = s * c_dec[None, :, None, None] + jnp.einsum('bjhd,bjhe->bhde', ki * k_dec[None, :, :, None], vi)
        return s, o

    s_fin, o = lax.scan(step, s0, (chunks(q), chunks(k), chunks(v)))
    return jnp.moveaxis(o, 0, 1).reshape(b, t, h, dv).astype(q.dtype), s_fin


def _retention_state(k, v, gamma):
    n = k.shape[1]
    w = jnp.exp((n - 1.0 - jnp.arange(n, dtype=jnp.float32))[:, None] * jnp.log(gamma)[None, :])
    return jnp.einsum('blhd,lh,blhe->bhde', k.astype(jnp.float32), w, v.astype(jnp.float32))


def _retention_mixer(z, kc, vc, decay_f, decay_b):
    b, t, _ = z.shape
    row, col = _grid_positions(t)
    q, k, v, g_f, g_b = jnp.split(z, [RET_QK, 2 * RET_QK, 2 * RET_QK + RET_VD, 2 * RET_QK + 2 * RET_VD], axis=-1)
    q = _axial_rope(q.reshape(b, t, RET_HEADS, RET_KEY), row, col)
    k = _axial_rope(k.reshape(b, t, RET_HEADS, RET_KEY), row, col) * (RET_KEY ** -0.5)
    v = v.reshape(b, t, RET_HEADS, RET_VAL)
    kc = kc.reshape(b, kc.shape[1], RET_HEADS, RET_KEY) * (RET_KEY ** -0.5)
    vc = vc.reshape(b, vc.shape[1], RET_HEADS, RET_VAL)
    gamma_f = 1.0 - jnp.exp2(-decay_f.astype(jnp.float32))
    gamma_b = 1.0 - jnp.exp2(-decay_b.astype(jnp.float32))
    s_cf = _retention_state(kc, vc, gamma_f)
    s_cb = _retention_state(kc[:, ::-1], vc[:, ::-1], gamma_b)
    o_f, _ = _retention_chunkwise(q, k, v, gamma_f, s_cf)
    o_b, _ = _retention_chunkwise(q[:, ::-1], k[:, ::-1], v[:, ::-1], gamma_b, s_cb)
    o_b = o_b[:, ::-1]

    def group_norm(o):
        return _normalize(o, 1e-6).reshape(b, t, RET_VD)

    return _silu(g_f) * group_norm(o_f) + _silu(g_b) * group_norm(o_b)


def _moe_swiglu(h, router, w_gate, w_up, w_down):
    b, t, d = h.shape
    n = b * t
    hf = h.reshape(n, d)
    logits = (hf @ router).astype(jnp.float32)
    top_val, top_idx = lax.top_k(logits, TOP_K)
    gate = jax.nn.softmax(top_val, axis=-1)
    n_assign = n * TOP_K
    flat_e = top_idx.reshape(-1)
    order = jnp.argsort(flat_e)
    sorted_e = flat_e[order]
    sorted_tok = (order // TOP_K).astype(jnp.int32)
    sorted_gate = gate.reshape(-1)[order]
    counts = jnp.zeros((N_EXPERTS,), jnp.int32).at[flat_e].add(1)
    padded = (counts + MOE_BLOCK - 1) // MOE_BLOCK * MOE_BLOCK
    pad_end = jnp.cumsum(padded)
    pad_start = pad_end - padded
    start = jnp.cumsum(counts) - counts
    dest = pad_start[sorted_e] + jnp.arange(n_assign, dtype=jnp.int32) - start[sorted_e]
    n_blocks = n_assign // MOE_BLOCK + N_EXPERTS
    slots = n_blocks * MOE_BLOCK
    slot_tok = jnp.full((slots,), n, jnp.int32).at[dest].set(sorted_tok)
    slot_gate = jnp.zeros((slots,), jnp.float32).at[dest].set(sorted_gate)
    block_start = jnp.arange(n_blocks, dtype=jnp.int32) * MOE_BLOCK
    block_expert = jnp.minimum(jnp.sum(pad_end[None, :] <= block_start[:, None], axis=1), N_EXPERTS - 1)
    h_pad = jnp.concatenate([hf, jnp.zeros((1, d), hf.dtype)], axis=0)
    xb = h_pad[slot_tok].reshape(n_blocks, MOE_BLOCK, d)

    def expert_block(args):
        x_blk, e = args
        return _swiglu(x_blk, w_gate[e], w_up[e], w_down[e])

    yb = lax.map(expert_block, (xb, block_expert)).reshape(slots, d)
    out = jnp.zeros((n + 1, d), jnp.float32).at[slot_tok].add(yb.astype(jnp.float32) * slot_gate[:, None])
    return out[:n].astype(h.dtype).reshape(b, t, d)


def _even_layer(x, xc, sc, scc, mod_w, mod_b, w_in, mla_q_norm, mla_wq_up, mla_kv_norm, mla_wkv_up,
                rwkv_conv, rwkv_w0_f, rwkv_w2_f, rwkv_w0_b, rwkv_w2_b, rwkv_a0_f, rwkv_a2_f, rwkv_a0_b,
                rwkv_a2_b, rwkv_g2, rwkv_k_k, rwkv_k_a, rwkv_r_k, rwkv_lnx_g, rwkv_lnx_b, w_out,
                ln1_g, ln1_b, ffn_w_gate, ffn_w_up, ffn_w_down, ln2_g, ln2_b):
    m = [u[:, None, :] for u in _adaln(sc, mod_w, mod_b)]
    mc = _adaln(scc, mod_w, mod_b)
    z = _modulate(x, m[0], m[1]) @ w_in
    zc = _modulate(xc, mc[0], mc[1]) @ w_in
    o_mla, oc_mla = _mla_mixer(z[..., :MLA_IN], zc[..., :MLA_IN],
                               mla_q_norm, mla_wq_up, mla_kv_norm, mla_wkv_up)
    o_rwkv, oc_rwkv = _rwkv7_mixer(z[..., MLA_IN:], zc[..., MLA_IN:], rwkv_conv, rwkv_w0_f, rwkv_w2_f,
                                   rwkv_w0_b, rwkv_w2_b, rwkv_a0_f, rwkv_a2_f, rwkv_a0_b, rwkv_a2_b,
                                   rwkv_g2, rwkv_k_k, rwkv_k_a, rwkv_r_k, rwkv_lnx_g, rwkv_lnx_b)
    o = jnp.concatenate([o_mla, o_rwkv], axis=-1) @ w_out
    oc = jnp.concatenate([oc_mla, oc_rwkv], axis=-1) @ w_out
    x = _post_norm(x, m[2] * o, ln1_g, ln1_b)
    xc = _post_norm(xc, mc[2] * oc, ln1_g, ln1_b)
    x = _post_norm(x, m[5] * _swiglu(_modulate(x, m[3], m[4]), ffn_w_gate, ffn_w_up, ffn_w_down), ln2_g, ln2_b)
    xc = _post_norm(xc, mc[5] * _swiglu(_modulate(xc, mc[3], mc[4]), ffn_w_gate, ffn_w_up, ffn_w_down),
                    ln2_g, ln2_b)
    return x, xc


def _odd_layer(x, xc, sc, scc, mod_w, mod_b, w_in, ret_decay_f, ret_decay_b, w_out, ln1_g, ln1_b,
               router, moe_w_gate, moe_w_up, moe_w_down, ln2_g, ln2_b):
    m = [u[:, None, :] for u in _adaln(sc, mod_w, mod_b)]
    c_shift, c_scale = jnp.split(scc @ mod_w[:, :2 * D_MODEL] + mod_b[:2 * D_MODEL], 2)
    z = _modulate(x, m[0], m[1]) @ w_in
    zc = _modulate(xc, c_shift, c_scale) @ w_in[:, RET_QK:2 * RET_QK + RET_VD]
    kc, vc = jnp.split(zc, [RET_QK], axis=-1)
    o = _retention_mixer(z, kc, vc, ret_decay_f, ret_decay_b) @ w_out
    x = _post_norm(x, m[2] * o, ln1_g, ln1_b)
    y = _moe_swiglu(_modulate(x, m[3], m[4]), router, moe_w_gate, moe_w_up, moe_w_down)
    return _post_norm(x, m[5] * y, ln2_g, ln2_b)


def setup_inputs(seed: int = 0) -> dict:
    key = jax.random.key(seed)
    ks = jax.random.split(key, 64)
    counter = iter(range(64))

    def nrm(shape, scale):
        return scale * jax.random.normal(ks[next(counter)], shape, jnp.float32)

    def uni(shape, lo, hi):
        return jax.random.uniform(ks[next(counter)], shape, jnp.float32, lo, hi)

    def gain(n):
        return 1.0 + nrm((n,), 0.1)

    def bias(n):
        return nrm((n,), 0.01)

    d = D_MODEL
    d_inv = d ** -0.5
    return {
        'x': nrm((BATCH, SEQ, d), 1.0),
        'c': nrm((BATCH, d), 1.0),
        'ctx': nrm((BATCH, CTX_LEN, d), 1.0),
        'c_ctx': nrm((d,), 1.0),
        'l0_mod_w': nrm((d, 6 * d), 0.5 * d_inv),
        'l0_mod_b': nrm((6 * d,), 0.02),
        'l0_w_in': nrm((d, EVEN_IN), d_inv),
        'l0_mla_q_norm': gain(MLA_Q_RANK),
        'l0_mla_wq_up': nrm((MLA_Q_RANK, MLA_HEADS * (MLA_NOPE + MLA_ROPE)), MLA_Q_RANK ** -0.5),
        'l0_mla_kv_norm': gain(MLA_KV_RANK),
        'l0_mla_wkv_up': nrm((MLA_KV_RANK, MLA_HEADS * (MLA_NOPE + MLA_V)), MLA_KV_RANK ** -0.5),
        'l0_rwkv_conv': jnp.array([[0.15], [1.0], [0.15]], jnp.float32) + nrm((3, 3 * RWKV_DIM), 0.05),
        'l0_rwkv_w0_f': uni((RWKV_DIM,), -2.5, 0.5),
        'l0_rwkv_w2_f': nrm((RWKV_DECAY_LORA, RWKV_DIM), 0.5 * RWKV_DECAY_LORA ** -0.5),
        'l0_rwkv_w0_b': uni((RWKV_DIM,), -2.5, 0.5),
        'l0_rwkv_w2_b': nrm((RWKV_DECAY_LORA, RWKV_DIM), 0.5 * RWKV_DECAY_LORA ** -0.5),
        'l0_rwkv_a0_f': nrm((RWKV_DIM,), 0.5),
        'l0_rwkv_a2_f': nrm((RWKV_AAA_LORA, RWKV_DIM), 0.5 * RWKV_AAA_LORA ** -0.5),
        'l0_rwkv_a0_b': nrm((RWKV_DIM,), 0.5),
        'l0_rwkv_a2_b': nrm((RWKV_AAA_LORA, RWKV_DIM), 0.5 * RWKV_AAA_LORA ** -0.5),
        'l0_rwkv_g2': nrm((RWKV_GATE_LORA, RWKV_DIM), RWKV_GATE_LORA ** -0.5),
        'l0_rwkv_k_k': 0.85 + nrm((RWKV_DIM,), 0.1),
        'l0_rwkv_k_a': gain(RWKV_DIM),
        'l0_rwkv_r_k': nrm((RWKV_HEADS, RWKV_HEAD), 0.1),
        'l0_rwkv_lnx_g': gain(RWKV_DIM),
        'l0_rwkv_lnx_b': bias(RWKV_DIM),
        'l0_w_out': nrm((EVEN_MIX, d), EVEN_MIX ** -0.5 * DEEPNORM_BETA),
        'l0_ln1_g': gain(d),
        'l0_ln1_b': bias(d),
        'l0_ffn_w_gate': nrm((d, D_FF), d_inv),
        'l0_ffn_w_up': nrm((d, D_FF), d_inv),
        'l0_ffn_w_down': nrm((D_FF, d), D_FF ** -0.5 * DEEPNORM_BETA),
        'l0_ln2_g': gain(d),
        'l0_ln2_b': bias(d),
        'l1_mod_w': nrm((d, 6 * d), 0.5 * d_inv),
        'l1_mod_b': nrm((6 * d,), 0.02),
        'l1_w_in': nrm((d, ODD_IN), d_inv),
        'l1_ret_decay_f': 5.0 + jnp.arange(RET_HEADS, dtype=jnp.float32) + uni((RET_HEADS,), -0.25, 0.25),
        'l1_ret_decay_b': 5.0 + jnp.arange(RET_HEADS, dtype=jnp.float32) + uni((RET_HEADS,), -0.25, 0.25),
        'l1_w_out': nrm((RET_VD, d), RET_VD ** -0.5 * DEEPNORM_BETA),
        'l1_ln1_g': gain(d),
        'l1_ln1_b': bias(d),
        'l1_router': nrm((d, N_EXPERTS), d_inv),
        'l1_moe_w_gate': nrm((N_EXPERTS, d, D_FF_EXPERT), d_inv),
        'l1_moe_w_up': nrm((N_EXPERTS, d, D_FF_EXPERT), d_inv),
        'l1_moe_w_down': nrm((N_EXPERTS, D_FF_EXPERT, d), D_FF_EXPERT ** -0.5 * DEEPNORM_BETA),
        'l1_ln2_g': gain(d),
        'l1_ln2_b': bias(d),
    }


def reference(x, c, ctx, c_ctx,
              l0_mod_w, l0_mod_b, l0_w_in, l0_mla_q_norm, l0_mla_wq_up, l0_mla_kv_norm, l0_mla_wkv_up,
              l0_rwkv_conv, l0_rwkv_w0_f, l0_rwkv_w2_f, l0_rwkv_w0_b, l0_rwkv_w2_b, l0_rwkv_a0_f,
              l0_rwkv_a2_f, l0_rwkv_a0_b, l0_rwkv_a2_b, l0_rwkv_g2, l0_rwkv_k_k, l0_rwkv_k_a, l0_rwkv_r_k,
              l0_rwkv_lnx_g, l0_rwkv_lnx_b, l0_w_out, l0_ln1_g, l0_ln1_b, l0_ffn_w_gate, l0_ffn_w_up,
              l0_ffn_w_down, l0_ln2_g, l0_ln2_b,
              l1_mod_w, l1_mod_b, l1_w_in, l1_ret_decay_f, l1_ret_decay_b, l1_w_out, l1_ln1_g, l1_ln1_b,
              l1_router, l1_moe_w_gate, l1_moe_w_up, l1_moe_w_down, l1_ln2_g, l1_ln2_b):
    even_params = (l0_mod_w, l0_mod_b, l0_w_in, l0_mla_q_norm, l0_mla_wq_up, l0_mla_kv_norm, l0_mla_wkv_up,
                   l0_rwkv_conv, l0_rwkv_w0_f, l0_rwkv_w2_f, l0_rwkv_w0_b, l0_rwkv_w2_b, l0_rwkv_a0_f,
                   l0_rwkv_a2_f, l0_rwkv_a0_b, l0_rwkv_a2_b, l0_rwkv_g2, l0_rwkv_k_k, l0_rwkv_k_a, l0_rwkv_r_k,
                   l0_rwkv_lnx_g, l0_rwkv_lnx_b, l0_w_out, l0_ln1_g, l0_ln1_b, l0_ffn_w_gate, l0_ffn_w_up,
                   l0_ffn_w_down, l0_ln2_g, l0_ln2_b)
    odd_params = (l1_mod_w, l1_mod_b, l1_w_in, l1_ret_decay_f, l1_ret_decay_b, l1_w_out, l1_ln1_g, l1_ln1_b,
                  l1_router, l1_moe_w_gate, l1_moe_w_up, l1_moe_w_down, l1_ln2_g, l1_ln2_b)
    sc = _silu(c)
    scc = _silu(c_ctx)
    xc = ctx
    for layer in range(DEPTH):
        if layer % 2 == 0:
            x, xc = _even_layer(x, xc, sc, scc, *even_params)
        else:
            x = _odd_layer(x, xc, sc, scc, *odd_params)
    return x
```

```python
import functools

import jax
import jax.numpy as jnp
import numpy as np
from jax import lax
from jax.experimental import pallas as pl
from jax.experimental.pallas import tpu as pltpu

D_MODEL = 1024
DEPTH = 2
GRID_W = 64
ROPE_BASE = 10000.0
Q_BLOCK = 128
DEEPNORM_ALPHA = (2 * DEPTH) ** 0.25
LN_EPS = 1e-5

MLA_HEADS = 8
MLA_Q_RANK = 256
MLA_KV_RANK = 128
MLA_NOPE = 64
MLA_ROPE = 32
MLA_V = 64
MLA_IN = MLA_Q_RANK + MLA_KV_RANK + MLA_ROPE
MLA_OUT = MLA_HEADS * MLA_V

RWKV_HEADS = 8
RWKV_HEAD = 64
RWKV_DIM = RWKV_HEADS * RWKV_HEAD
RWKV_DECAY_LORA = 64
RWKV_AAA_LORA = 64
RWKV_GATE_LORA = 128
RWKV_LNX_EPS = 64e-5

RET_HEADS = 4
RET_KEY = 256
RET_VAL = 512
RET_CHUNK = 128
RET_QK = RET_HEADS * RET_KEY
RET_VD = RET_HEADS * RET_VAL

N_EXPERTS = 8
TOP_K = 2
MOE_BLOCK = 128

LANES = 128
VMEM_LIMIT_BYTES = 48 * 1024 * 1024


def _matmul_kernel(x_ref, w_ref, o_ref, acc_ref):
    @pl.when(pl.program_id(2) == 0)
    def _():
        acc_ref[...] = jnp.zeros_like(acc_ref)

    acc_ref[...] += jnp.dot(x_ref[...].astype(jnp.bfloat16), w_ref[...].astype(jnp.bfloat16),
                            preferred_element_type=jnp.float32)

    @pl.when(pl.program_id(2) == pl.num_programs(2) - 1)
    def _():
        o_ref[...] = acc_ref[...]


def _pick_tile(n, candidates):
    for c in candidates:
        if n % c == 0:
            return c
    raise ValueError(f"no tile for {n}")


def pmatmul(x, w):
    m, k = x.shape
    n = w.shape[1]
    n_pad = (-n) % LANES
    if n_pad:
        w = jnp.pad(w, ((0, 0), (0, n_pad)))
    np_ = n + n_pad
    tm = _pick_tile(m, (512, 256, 128, 8))
    tn = _pick_tile(np_, (512, 384, 256, 128))
    tk = k if k <= 1024 else _pick_tile(k, (512, 256, 128))
    out = pl.pallas_call(
        _matmul_kernel,
        grid=(m // tm, np_ // tn, k // tk),
        in_specs=[pl.BlockSpec((tm, tk), lambda i, j, kk: (i, kk)),
                  pl.BlockSpec((tk, tn), lambda i, j, kk: (kk, j))],
        out_specs=pl.BlockSpec((tm, tn), lambda i, j, kk: (i, j)),
        out_shape=jax.ShapeDtypeStruct((m, np_), jnp.float32),
        scratch_shapes=[pltpu.VMEM((tm, tn), jnp.float32)],
        compiler_params=pltpu.CompilerParams(
            dimension_semantics=("parallel", "parallel", "arbitrary"),
            vmem_limit_bytes=VMEM_LIMIT_BYTES),
        name="matmul",
    )(x, w)
    return out[:, :n] if n_pad else out


def mm(x, w):
    lead = x.shape[:-1]
    out = pmatmul(x.reshape(-1, x.shape[-1]), w)
    return out.reshape(*lead, w.shape[1])


def _silu(t):
    return t * jax.nn.sigmoid(t)


def _normalize(t, eps):
    mu = jnp.mean(t, axis=-1, keepdims=True)
    var = jnp.mean(jnp.square(t - mu), axis=-1, keepdims=True)
    return (t - mu) * lax.rsqrt(var + eps)


def _layer_norm(t, g, b):
    return _normalize(t, LN_EPS) * g + b


def _rms_norm(t, g, eps=1e-6):
    return t * lax.rsqrt(jnp.mean(t * t, axis=-1, keepdims=True) + eps) * g


def _l2_normalize(t, eps=1e-12):
    return t / jnp.maximum(jnp.linalg.norm(t, axis=-1, keepdims=True), eps)


def _post_norm(x, update, g, b):
    return _layer_norm(DEEPNORM_ALPHA * x + update, g, b)


def _modulate(h, shift, scale):
    return h * (1.0 + scale) + shift


def _adaln(cond, mod_w, mod_b):
    return jnp.split(jnp.dot(cond, mod_w, precision=lax.Precision.HIGHEST) + mod_b, 6, axis=-1)


def _swiglu(h, w_gate, w_up, w_down):
    return mm(_silu(mm(h, w_gate)) * mm(h, w_up), w_down)


def _centred_conv3(t, w):
    tp = jnp.pad(t, ((0, 0), (1, 1), (0, 0)))
    return tp[:, :-2] * w[0] + tp[:, 1:-1] * w[1] + tp[:, 2:] * w[2]


def _grid_positions(n_tokens):
    rows = n_tokens // GRID_W
    row = jnp.repeat(jnp.arange(rows, dtype=jnp.float32), GRID_W)
    col = jnp.tile(jnp.arange(GRID_W, dtype=jnp.float32), rows)
    return row, col


def _rotate(t, pos):
    nf = t.shape[-1] // 2
    inv_freq = ROPE_BASE ** (-jnp.arange(nf, dtype=jnp.float32) / nf)
    ang = pos[:, None] * inv_freq[None, :]
    cos = jnp.cos(ang)[None, :, None, :]
    sin = jnp.sin(ang)[None, :, None, :]
    t1, t2 = t[..., :nf], t[..., nf:]
    return jnp.concatenate([t1 * cos - t2 * sin, t1 * sin + t2 * cos], axis=-1)


def _axial_rope(t, row, col):
    half = t.shape[-1] // 2
    return jnp.concatenate([_rotate(t[..., :half], row), _rotate(t[..., half:], col)], axis=-1)


def _block_attention(q, k, v):
    b, t, h, dq = q.shape
    nb = t // Q_BLOCK
    scale = dq ** -0.5
    qb = jnp.moveaxis(q.reshape(b, nb, Q_BLOCK, h, dq), 1, 0)

    def one_block(q_blk):
        s = jnp.einsum('bqhd,bkhd->bhqk', q_blk, k).astype(jnp.float32) * scale
        p = jax.nn.softmax(s, axis=-1)
        return jnp.einsum('bhqk,bkhd->bqhd', p, v)

    ob = lax.map(one_block, qb)
    return jnp.moveaxis(ob, 0, 1).reshape(b, t, h, v.shape[-1])


def _mla_project(zz, q_norm, wq_up, kv_norm, wkv_up):
    b, t, _ = zz.shape
    cq, ckv, k_rope = jnp.split(zz, [MLA_Q_RANK, MLA_Q_RANK + MLA_KV_RANK], axis=-1)
    q = mm(_rms_norm(cq, q_norm), wq_up).reshape(b, t, MLA_HEADS, MLA_NOPE + MLA_ROPE)
    kv = mm(_rms_norm(ckv, kv_norm), wkv_up).reshape(b, t, MLA_HEADS, MLA_NOPE + MLA_V)
    return q, kv[..., :MLA_NOPE], k_rope[:, :, None, :], kv[..., MLA_NOPE:]


def _mla_keys(k_nope, k_rope):
    b, t, h, _ = k_nope.shape
    return jnp.concatenate([k_nope, jnp.broadcast_to(k_rope, (b, t, h, MLA_ROPE))], axis=-1)


def _mla_mixer(z, zc, q_norm, wq_up, kv_norm, wkv_up):
    b, t, _ = z.shape
    row, col = _grid_positions(t)
    q, k_nope, k_rope, v = _mla_project(z, q_norm, wq_up, kv_norm, wkv_up)
    qc, k_nope_c, k_rope_c, vc = _mla_project(zc, q_norm, wq_up, kv_norm, wkv_up)
    q = jnp.concatenate([q[..., :MLA_NOPE], _axial_rope(q[..., MLA_NOPE:], row, col)], axis=-1)
    k = _mla_keys(k_nope, _axial_rope(k_rope, row, col))
    kc = _mla_keys(k_nope_c, k_rope_c)
    o = _block_attention(q, jnp.concatenate([kc, k], axis=1), jnp.concatenate([vc, v], axis=1))
    oc = _block_attention(qc, kc, vc)
    return o.reshape(b, t, MLA_OUT), oc.reshape(b, zc.shape[1], MLA_OUT)


def _rwkv7_scan(s0, r, w, k, v, a, b, reverse):
    def step(s, xs):
        rt, wt, kt, vt, at, bt = xs
        sa = jnp.einsum('bhvk,bhk->bhv', s, at)
        s = s * wt[:, :, None, :] + sa[..., None] * bt[:, :, None, :] + vt[..., None] * kt[:, :, None, :]
        return s, jnp.einsum('bhvk,bhk->bhv', s, rt)

    xs = tuple(jnp.moveaxis(t, 1, 0) for t in (r, w, k, v, a, b))
    s_fin, y = lax.scan(step, s0, xs, reverse=reverse)
    return jnp.moveaxis(y, 0, 1), s_fin


def _rwkv7_features(z, conv_w, w0_f, w2_f, w0_b, w2_b, a0_f, a2_f, a0_b, a2_b, g2, k_k, k_a):
    b, t, _ = z.shape
    c0 = 3 * RWKV_DIM
    c1 = c0 + RWKV_DECAY_LORA
    c2 = c1 + RWKV_DECAY_LORA
    c3 = c2 + RWKV_AAA_LORA
    c4 = c3 + RWKV_AAA_LORA
    rkv_raw, wd_f, wd_b, ad_f, ad_b, gd = jnp.split(z, [c0, c1, c2, c3, c4], axis=-1)
    rkv = _centred_conv3(rkv_raw, conv_w)
    r, k, v = jnp.split(rkv, 3, axis=-1)
    g = mm(jax.nn.sigmoid(gd), g2)

    def heads(u):
        return u.reshape(b, t, RWKV_HEADS, RWKV_HEAD)

    kk = _l2_normalize(heads(k * k_k))

    def direction(wd, w0, w2, ad, a0, a2):
        logw = -jax.nn.softplus(-(w0 + mm(jnp.tanh(wd), w2))) - 0.5
        decay = jnp.exp(-jnp.exp(logw))
        lr = jax.nn.sigmoid(a0 + mm(ad, a2))
        kd = k * (1.0 + (lr - 1.0) * k_a)
        return heads(decay), heads(kd), heads(lr)

    fwd = direction(wd_f, w0_f, w2_f, ad_f, a0_f, a2_f)
    bwd = direction(wd_b, w0_b, w2_b, ad_b, a0_b, a2_b)
    return heads(r), heads(v), kk, g, fwd, bwd


def _rwkv7_readout(y_f, y_b, r, k_f, k_b, v, g, r_k, lnx_g, lnx_b):
    b, t = r.shape[:2]
    y = _normalize(y_f + y_b, RWKV_LNX_EPS).reshape(b, t, RWKV_DIM) * lnx_g + lnx_b
    bonus = jnp.sum(r * (k_f + k_b) * r_k, axis=-1, keepdims=True) * v
    return (y + bonus.reshape(b, t, RWKV_DIM)) * g


def _rwkv7_mixer(z, zc, conv_w, w0_f, w2_f, w0_b, w2_b, a0_f, a2_f, a0_b, a2_b, g2, k_k, k_a,
                 r_k, lnx_g, lnx_b):
    p = (conv_w, w0_f, w2_f, w0_b, w2_b, a0_f, a2_f, a0_b, a2_b, g2, k_k, k_a)
    r, v, kk, g, (w_f, k_f, l_f), (w_b, k_b, l_b) = _rwkv7_features(z, *p)
    rc, vc, kkc, gc, (wc_f, kc_f, lc_f), (wc_b, kc_b, lc_b) = _rwkv7_features(zc, *p)
    s0 = jnp.zeros((zc.shape[0], RWKV_HEADS, RWKV_HEAD, RWKV_HEAD), jnp.float32)
    yc_f, sc_f = _rwkv7_scan(s0, rc, wc_f, kc_f, vc, -kkc, kkc * lc_f, reverse=False)
    yc_b, sc_b = _rwkv7_scan(s0, rc, wc_b, kc_b, vc, -kkc, kkc * lc_b, reverse=True)
    y_f, _ = _rwkv7_scan(sc_f, r, w_f, k_f, v, -kk, kk * l_f, reverse=False)
    y_b, _ = _rwkv7_scan(sc_b, r, w_b, k_b, v, -kk, kk * l_b, reverse=True)
    o = _rwkv7_readout(y_f, y_b, r, k_f, k_b, v, g, r_k, lnx_g, lnx_b)
    oc = _rwkv7_readout(yc_f, yc_b, rc, kc_f, kc_b, vc, gc, r_k, lnx_g, lnx_b)
    return o, oc


def _retention_chunkwise(q, k, v, gamma, s0):
    b, t, h, dk = q.shape
    dv = v.shape[-1]
    nc = t // RET_CHUNK
    log_g = jnp.log(gamma)
    pos = jnp.arange(RET_CHUNK, dtype=jnp.float32)
    rel = pos[:, None] - pos[None, :]
    intra = jnp.where(rel >= 0, jnp.exp(jnp.maximum(rel, 0.0)[None] * log_g[:, None, None]), 0.0)
    q_dec = jnp.exp((pos[:, None] + 1.0) * log_g[None, :])
    k_dec = jnp.exp((RET_CHUNK - 1.0 - pos[:, None]) * log_g[None, :])
    c_dec = jnp.exp(RET_CHUNK * log_g)

    def chunks(u):
        return jnp.moveaxis(u.reshape(b, nc, RET_CHUNK, h, u.shape[-1]), 1, 0)

    def step(s, xs):
        qi, ki, vi = xs
        att = jnp.einsum('bihd,bjhd->bhij', qi, ki) * intra
        o = (jnp.einsum('bhij,bjhe->bihe', att, vi)
             + jnp.einsum('bihd,bhde->bihe', qi * q_dec[None, :, :, None], s))
        s = s * c_dec[None, :, None, None] + jnp.einsum('bjhd,bjhe->bhde', ki * k_dec[None, :, :, None], vi)
        return s, o

    s_fin, o = lax.scan(step, s0, (chunks(q), chunks(k), chunks(v)))
    return jnp.moveaxis(o, 0, 1).reshape(b, t, h, dv), s_fin


def _retention_state(k, v, gamma):
    n = k.shape[1]
    w = jnp.exp((n - 1.0 - jnp.arange(n, dtype=jnp.float32))[:, None] * jnp.log(gamma)[None, :])
    return jnp.einsum('blhd,lh,blhe->bhde', k, w, v)


def _retention_mixer(z, kc, vc, decay_f, decay_b):
    b, t, _ = z.shape
    row, col = _grid_positions(t)
    q, k, v, g_f, g_b = jnp.split(z, [RET_QK, 2 * RET_QK, 2 * RET_QK + RET_VD, 2 * RET_QK + 2 * RET_VD], axis=-1)
    q = _axial_rope(q.reshape(b, t, RET_HEADS, RET_KEY), row, col)
    k = _axial_rope(k.reshape(b, t, RET_HEADS, RET_KEY), row, col) * (RET_KEY ** -0.5)
    v = v.reshape(b, t, RET_HEADS, RET_VAL)
    kc = kc.reshape(b, kc.shape[1], RET_HEADS, RET_KEY) * (RET_KEY ** -0.5)
    vc = vc.reshape(b, vc.shape[1], RET_HEADS, RET_VAL)
    gamma_f = 1.0 - jnp.exp2(-decay_f)
    gamma_b = 1.0 - jnp.exp2(-decay_b)
    s_cf = _retention_state(kc, vc, gamma_f)
    s_cb = _retention_state(kc[:, ::-1], vc[:, ::-1], gamma_b)
    o_f, _ = _retention_chunkwise(q, k, v, gamma_f, s_cf)
    o_b, _ = _retention_chunkwise(q[:, ::-1], k[:, ::-1], v[:, ::-1], gamma_b, s_cb)
    o_b = o_b[:, ::-1]

    def group_norm(o):
        return _normalize(o, 1e-6).reshape(b, t, RET_VD)

    return _silu(g_f) * group_norm(o_f) + _silu(g_b) * group_norm(o_b)


def _moe_swiglu(h, router, w_gate, w_up, w_down):
    b, t, d = h.shape
    n = b * t
    hf = h.reshape(n, d)
    logits = jnp.dot(hf, router, precision=lax.Precision.HIGHEST)
    top_val, top_idx = lax.top_k(logits, TOP_K)
    gate = jax.nn.softmax(top_val, axis=-1)
    n_assign = n * TOP_K
    flat_e = top_idx.reshape(-1)
    order = jnp.argsort(flat_e)
    sorted_e = flat_e[order]
    sorted_tok = (order // TOP_K).astype(jnp.int32)
    sorted_gate = gate.reshape(-1)[order]
    counts = jnp.zeros((N_EXPERTS,), jnp.int32).at[flat_e].add(1)
    padded = (counts + MOE_BLOCK - 1) // MOE_BLOCK * MOE_BLOCK
    pad_end = jnp.cumsum(padded)
    pad_start = pad_end - padded
    start = jnp.cumsum(counts) - counts
    dest = pad_start[sorted_e] + jnp.arange(n_assign, dtype=jnp.int32) - start[sorted_e]
    n_blocks = n_assign // MOE_BLOCK + N_EXPERTS
    slots = n_blocks * MOE_BLOCK
    slot_tok = jnp.full((slots,), n, jnp.int32).at[dest].set(sorted_tok)
    slot_gate = jnp.zeros((slots,), jnp.float32).at[dest].set(sorted_gate)
    block_start = jnp.arange(n_blocks, dtype=jnp.int32) * MOE_BLOCK
    block_expert = jnp.minimum(jnp.sum(pad_end[None, :] <= block_start[:, None], axis=1), N_EXPERTS - 1)
    h_pad = jnp.concatenate([hf, jnp.zeros((1, d), hf.dtype)], axis=0)
    xb = h_pad[slot_tok].reshape(n_blocks, MOE_BLOCK, d)

    def expert_block(args):
        x_blk, e = args
        return _swiglu(x_blk, w_gate[e], w_up[e], w_down[e])

    yb = lax.map(expert_block, (xb, block_expert)).reshape(slots, d)
    out = jnp.zeros((n + 1, d), jnp.float32).at[slot_tok].add(yb * slot_gate[:, None])
    return out[:n].reshape(b, t, d)


def _even_layer(x, xc, sc, scc, mod_w, mod_b, w_in, mla_q_norm, mla_wq_up, mla_kv_norm, mla_wkv_up,
                rwkv_conv, rwkv_w0_f, rwkv_w2_f, rwkv_w0_b, rwkv_w2_b, rwkv_a0_f, rwkv_a2_f, rwkv_a0_b,
                rwkv_a2_b, rwkv_g2, rwkv_k_k, rwkv_k_a, rwkv_r_k, rwkv_lnx_g, rwkv_lnx_b, w_out,
                ln1_g, ln1_b, ffn_w_gate, ffn_w_up, ffn_w_down, ln2_g, ln2_b):
    m = [u[:, None, :] for u in _adaln(sc, mod_w, mod_b)]
    mc = _adaln(scc, mod_w, mod_b)
    z = mm(_modulate(x, m[0], m[1]), w_in)
    zc = mm(_modulate(xc, mc[0], mc[1]), w_in)
    o_mla, oc_mla = _mla_mixer(z[..., :MLA_IN], zc[..., :MLA_IN],
                               mla_q_norm, mla_wq_up, mla_kv_norm, mla_wkv_up)
    o_rwkv, oc_rwkv = _rwkv7_mixer(z[..., MLA_IN:], zc[..., MLA_IN:], rwkv_conv, rwkv_w0_f, rwkv_w2_f,
                                   rwkv_w0_b, rwkv_w2_b, rwkv_a0_f, rwkv_a2_f, rwkv_a0_b, rwkv_a2_b,
                                   rwkv_g2, rwkv_k_k, rwkv_k_a, rwkv_r_k, rwkv_lnx_g, rwkv_lnx_b)
    o = mm(jnp.concatenate([o_mla, o_rwkv], axis=-1), w_out)
    oc = mm(jnp.concatenate([oc_mla, oc_rwkv], axis=-1), w_out)
    x = _post_norm(x, m[2] * o, ln1_g, ln1_b)
    xc = _post_norm(xc, mc[2] * oc, ln1_g, ln1_b)
    x = _post_norm(x, m[5] * _swiglu(_modulate(x, m[3], m[4]), ffn_w_gate, ffn_w_up, ffn_w_down), ln2_g, ln2_b)
    xc = _post_norm(xc, mc[5] * _swiglu(_modulate(xc, mc[3], mc[4]), ffn_w_gate, ffn_w_up, ffn_w_down),
                    ln2_g, ln2_b)
    return x, xc


def _odd_layer(x, xc, sc, scc, mod_w, mod_b, w_in, ret_decay_f, ret_decay_b, w_out, ln1_g, ln1_b,
               router, moe_w_gate, moe_w_up, moe_w_down, ln2_g, ln2_b):
    m = [u[:, None, :] for u in _adaln(sc, mod_w, mod_b)]
    c_shift, c_scale = jnp.split(
        jnp.dot(scc, mod_w[:, :2 * D_MODEL], precision=lax.Precision.HIGHEST) + mod_b[:2 * D_MODEL], 2)
    z = mm(_modulate(x, m[0], m[1]), w_in)
    zc = mm(_modulate(xc, c_shift, c_scale), w_in[:, RET_QK:2 * RET_QK + RET_VD])
    kc, vc = jnp.split(zc, [RET_QK], axis=-1)
    o = mm(_retention_mixer(z, kc, vc, ret_decay_f, ret_decay_b), w_out)
    x = _post_norm(x, m[2] * o, ln1_g, ln1_b)
    y = _moe_swiglu(_modulate(x, m[3], m[4]), router, moe_w_gate, moe_w_up, moe_w_down)
    return _post_norm(x, m[5] * y, ln2_g, ln2_b)


def kernel(x, c, ctx, c_ctx, l0_mod_w, l0_mod_b, l0_w_in, l0_mla_q_norm, l0_mla_wq_up, l0_mla_kv_norm, l0_mla_wkv_up, l0_rwkv_conv, l0_rwkv_w0_f, l0_rwkv_w2_f, l0_rwkv_w0_b, l0_rwkv_w2_b, l0_rwkv_a0_f, l0_rwkv_a2_f, l0_rwkv_a0_b, l0_rwkv_a2_b, l0_rwkv_g2, l0_rwkv_k_k, l0_rwkv_k_a, l0_rwkv_r_k, l0_rwkv_lnx_g, l0_rwkv_lnx_b, l0_w_out, l0_ln1_g, l0_ln1_b, l0_ffn_w_gate, l0_ffn_w_up, l0_ffn_w_down, l0_ln2_g, l0_ln2_b, l1_mod_w, l1_mod_b, l1_w_in, l1_ret_decay_f, l1_ret_decay_b, l1_w_out, l1_ln1_g, l1_ln1_b, l1_router, l1_moe_w_gate, l1_moe_w_up, l1_moe_w_down, l1_ln2_g, l1_ln2_b):
    even_params = (l0_mod_w, l0_mod_b, l0_w_in, l0_mla_q_norm, l0_mla_wq_up, l0_mla_kv_norm, l0_mla_wkv_up,
                   l0_rwkv_conv, l0_rwkv_w0_f, l0_rwkv_w2_f, l0_rwkv_w0_b, l0_rwkv_w2_b, l0_rwkv_a0_f,
                   l0_rwkv_a2_f, l0_rwkv_a0_b, l0_rwkv_a2_b, l0_rwkv_g2, l0_rwkv_k_k, l0_rwkv_k_a, l0_rwkv_r_k,
                   l0_rwkv_lnx_g, l0_rwkv_lnx_b, l0_w_out, l0_ln1_g, l0_ln1_b, l0_ffn_w_gate, l0_ffn_w_up,
                   l0_ffn_w_down, l0_ln2_g, l0_ln2_b)
    odd_params = (l1_mod_w, l1_mod_b, l1_w_in, l1_ret_decay_f, l1_ret_decay_b, l1_w_out, l1_ln1_g, l1_ln1_b,
                  l1_router, l1_moe_w_gate, l1_moe_w_up, l1_moe_w_down, l1_ln2_g, l1_ln2_b)
    sc = _silu(c)
    scc = _silu(c_ctx)
    x, xc = _even_layer(x, ctx, sc, scc, *even_params)
    return _odd_layer(x, xc, sc, scc, *odd_params)
```

```python
import functools

import jax
import jax.numpy as jnp
import numpy as np
from jax import lax
from jax.experimental import pallas as pl
from jax.experimental.pallas import tpu as pltpu

D_MODEL = 1024
DEPTH = 2
GRID_W = 64
ROPE_BASE = 10000.0
Q_BLOCK = 128
DEEPNORM_ALPHA = (2 * DEPTH) ** 0.25
LN_EPS = 1e-5

MLA_HEADS = 8
MLA_Q_RANK = 256
MLA_KV_RANK = 128
MLA_NOPE = 64
MLA_ROPE = 32
MLA_V = 64
MLA_IN = MLA_Q_RANK + MLA_KV_RANK + MLA_ROPE
MLA_OUT = MLA_HEADS * MLA_V

RWKV_HEADS = 8
RWKV_HEAD = 64
RWKV_DIM = RWKV_HEADS * RWKV_HEAD
RWKV_DECAY_LORA = 64
RWKV_AAA_LORA = 64
RWKV_GATE_LORA = 128
RWKV_LNX_EPS = 64e-5

RET_HEADS = 4
RET_KEY = 256
RET_VAL = 512
RET_CHUNK = 128
RET_QK = RET_HEADS * RET_KEY
RET_VD = RET_HEADS * RET_VAL

N_EXPERTS = 8
TOP_K = 2
MOE_BLOCK = 128

LANES = 128
VMEM_LIMIT_BYTES = 48 * 1024 * 1024


def _matmul_kernel(x_ref, w_ref, o_ref, acc_ref):
    @pl.when(pl.program_id(2) == 0)
    def _():
        acc_ref[...] = jnp.zeros_like(acc_ref)

    acc_ref[...] += jnp.dot(x_ref[...].astype(jnp.bfloat16), w_ref[...].astype(jnp.bfloat16),
                            preferred_element_type=jnp.float32)

    @pl.when(pl.program_id(2) == pl.num_programs(2) - 1)
    def _():
        o_ref[...] = acc_ref[...]


def _pick_tile(n, candidates):
    for c in candidates:
        if n % c == 0:
            return c
    raise ValueError(f"no tile for {n}")


def pmatmul(x, w):
    m, k = x.shape
    n = w.shape[1]
    n_pad = (-n) % LANES
    if n_pad:
        w = jnp.pad(w, ((0, 0), (0, n_pad)))
    np_ = n + n_pad
    tm = _pick_tile(m, (512, 256, 128, 8))
    tn = _pick_tile(np_, (512, 384, 256, 128))
    tk = k if k <= 1024 else _pick_tile(k, (512, 256, 128))
    out = pl.pallas_call(
        _matmul_kernel,
        grid=(m // tm, np_ // tn, k // tk),
        in_specs=[pl.BlockSpec((tm, tk), lambda i, j, kk: (i, kk)),
                  pl.BlockSpec((tk, tn), lambda i, j, kk: (kk, j))],
        out_specs=pl.BlockSpec((tm, tn), lambda i, j, kk: (i, j)),
        out_shape=jax.ShapeDtypeStruct((m, np_), jnp.float32),
        scratch_shapes=[pltpu.VMEM((tm, tn), jnp.float32)],
        compiler_params=pltpu.CompilerParams(
            dimension_semantics=("parallel", "parallel", "arbitrary"),
            vmem_limit_bytes=VMEM_LIMIT_BYTES),
        name="matmul",
    )(x, w)
    return out[:, :n] if n_pad else out


def mm(x, w):
    lead = x.shape[:-1]
    out = pmatmul(x.reshape(-1, x.shape[-1]), w)
    return out.reshape(*lead, w.shape[1])


def _silu(t):
    return t * jax.nn.sigmoid(t)


def _normalize(t, eps):
    mu = jnp.mean(t, axis=-1, keepdims=True)
    var = jnp.mean(jnp.square(t - mu), axis=-1, keepdims=True)
    return (t - mu) * lax.rsqrt(var + eps)


def _layer_norm(t, g, b):
    return _normalize(t, LN_EPS) * g + b


def _rms_norm(t, g, eps=1e-6):
    return t * lax.rsqrt(jnp.mean(t * t, axis=-1, keepdims=True) + eps) * g


def _l2_normalize(t, eps=1e-12):
    return t / jnp.maximum(jnp.linalg.norm(t, axis=-1, keepdims=True), eps)


def _post_norm(x, update, g, b):
    return _layer_norm(DEEPNORM_ALPHA * x + update, g, b)


def _modulate(h, shift, scale):
    return h * (1.0 + scale) + shift


def _adaln(cond, mod_w, mod_b):
    return jnp.split(jnp.dot(cond, mod_w, precision=lax.Precision.HIGHEST) + mod_b, 6, axis=-1)


def _swiglu(h, w_gate, w_up, w_down):
    return mm(_silu(mm(h, w_gate)) * mm(h, w_up), w_down)


def _centred_conv3(t, w):
    tp = jnp.pad(t, ((0, 0), (1, 1), (0, 0)))
    return tp[:, :-2] * w[0] + tp[:, 1:-1] * w[1] + tp[:, 2:] * w[2]


def _grid_positions(n_tokens):
    rows = n_tokens // GRID_W
    row = jnp.repeat(jnp.arange(rows, dtype=jnp.float32), GRID_W)
    col = jnp.tile(jnp.arange(GRID_W, dtype=jnp.float32), rows)
    return row, col


def _rotate(t, pos):
    nf = t.shape[-1] // 2
    inv_freq = ROPE_BASE ** (-jnp.arange(nf, dtype=jnp.float32) / nf)
    ang = pos[:, None] * inv_freq[None, :]
    cos = jnp.cos(ang)[None, :, None, :]
    sin = jnp.sin(ang)[None, :, None, :]
    t1, t2 = t[..., :nf], t[..., nf:]
    return jnp.concatenate([t1 * cos - t2 * sin, t1 * sin + t2 * cos], axis=-1)


def _axial_rope(t, row, col):
    half = t.shape[-1] // 2
    return jnp.concatenate([_rotate(t[..., :half], row), _rotate(t[..., half:], col)], axis=-1)


def _block_attention(q, k, v):
    b, t, h, dq = q.shape
    nb = t // Q_BLOCK
    scale = dq ** -0.5
    qb = jnp.moveaxis(q.reshape(b, nb, Q_BLOCK, h, dq), 1, 0)

    def one_block(q_blk):
        s = jnp.einsum('bqhd,bkhd->bhqk', q_blk, k).astype(jnp.float32) * scale
        p = jax.nn.softmax(s, axis=-1)
        return jnp.einsum('bhqk,bkhd->bqhd', p, v)

    ob = lax.map(one_block, qb)
    return jnp.moveaxis(ob, 0, 1).reshape(b, t, h, v.shape[-1])


def _mla_project(zz, q_norm, wq_up, kv_norm, wkv_up):
    b, t, _ = zz.shape
    cq, ckv, k_rope = jnp.split(zz, [MLA_Q_RANK, MLA_Q_RANK + MLA_KV_RANK], axis=-1)
    q = mm(_rms_norm(cq, q_norm), wq_up).reshape(b, t, MLA_HEADS, MLA_NOPE + MLA_ROPE)
    kv = mm(_rms_norm(ckv, kv_norm), wkv_up).reshape(b, t, MLA_HEADS, MLA_NOPE + MLA_V)
    return q, kv[..., :MLA_NOPE], k_rope[:, :, None, :], kv[..., MLA_NOPE:]


def _mla_keys(k_nope, k_rope):
    b, t, h, _ = k_nope.shape
    return jnp.concatenate([k_nope, jnp.broadcast_to(k_rope, (b, t, h, MLA_ROPE))], axis=-1)


def _mla_mixer(z, zc, q_norm, wq_up, kv_norm, wkv_up):
    b, t, _ = z.shape
    row, col = _grid_positions(t)
    q, k_nope, k_rope, v = _mla_project(z, q_norm, wq_up, kv_norm, wkv_up)
    qc, k_nope_c, k_rope_c, vc = _mla_project(zc, q_norm, wq_up, kv_norm, wkv_up)
    q = jnp.concatenate([q[..., :MLA_NOPE], _axial_rope(q[..., MLA_NOPE:], row, col)], axis=-1)
    k = _mla_keys(k_nope, _axial_rope(k_rope, row, col))
    kc = _mla_keys(k_nope_c, k_rope_c)
    o = _block_attention(q, jnp.concatenate([kc, k], axis=1), jnp.concatenate([vc, v], axis=1))
    oc = _block_attention(qc, kc, vc)
    return o.reshape(b, t, MLA_OUT), oc.reshape(b, zc.shape[1], MLA_OUT)


SCAN_BLOCK = 128
N_PAIRS = RWKV_HEADS // 2
PAIR_ROWS = N_PAIRS * RWKV_HEAD


def _scan_kernel(r_ref, w_ref, k_ref, a_ref, b_ref, vt_ref, wred_ref, wsel_ref, y_ref, s_ref):
    nseq = s_ref.shape[0]

    @pl.when(pl.program_id(0) == 0)
    def _():
        s_ref[...] = jnp.zeros_like(s_ref)

    lane = lax.broadcasted_iota(jnp.int32, (PAIR_ROWS, LANES), 1)
    lane_t = lax.broadcasted_iota(jnp.int32, (PAIR_ROWS, 2 * SCAN_BLOCK), 1) % SCAN_BLOCK
    wred = wred_ref[...]
    wsel = wsel_ref[...]

    def rows(ref, n, t):
        row = ref[n, pl.ds(t, 1), :]
        return jnp.concatenate(
            [jnp.broadcast_to(row[:, p * LANES:(p + 1) * LANES], (RWKV_HEAD, LANES)) for p in range(N_PAIRS)],
            axis=0)

    def body(i, carry):
        tp = jnp.maximum(i - 1, 0)
        for n in range(nseq):
            s = s_ref[n]
            pa = (s * rows(a_ref, n, i)).astype(jnp.bfloat16)
            pr = (s * rows(r_ref, n, tp)).astype(jnp.bfloat16)
            red = jnp.dot(jnp.concatenate([pa, pr], axis=1), wred, preferred_element_type=jnp.float32)
            sab = red[:, :LANES]
            yb = red[:, LANES:]
            pv = jnp.where(lane_t == i, vt_ref[n, 0], jnp.zeros((), jnp.bfloat16))
            vb = jnp.dot(pv, wsel, preferred_element_type=jnp.float32)
            s_ref[n] = s * rows(w_ref, n, i) + sab * rows(b_ref, n, i) + vb * rows(k_ref, n, i)
            mask = jnp.logical_and(lane % RWKV_HEAD == tp % RWKV_HEAD, i >= 1)
            half = tp // RWKV_HEAD
            y_ref[n, 0, half] = jnp.where(mask, yb, y_ref[n, 0, half])
        return carry

    lax.fori_loop(0, SCAN_BLOCK, body, 0)

    t_last = SCAN_BLOCK - 1
    for n in range(nseq):
        pr = (s_ref[n] * rows(r_ref, n, t_last)).astype(jnp.bfloat16)
        red = jnp.dot(jnp.concatenate([pr, pr], axis=1), wred, preferred_element_type=jnp.float32)
        mask = lane % RWKV_HEAD == t_last % RWKV_HEAD
        y_ref[n, 0, 1] = jnp.where(mask, red[:, LANES:], y_ref[n, 0, 1])


def rwkv_scan(r, w, k, v, a, b):
    nseq, length, _ = r.shape
    nblk = length // SCAN_BLOCK
    vt = v.reshape(nseq, nblk, SCAN_BLOCK, N_PAIRS, 2, RWKV_HEAD)
    vt = jnp.transpose(vt, (0, 1, 3, 5, 4, 2)).reshape(nseq, nblk, PAIR_ROWS, 2 * SCAN_BLOCK).astype(jnp.bfloat16)
    j = np.arange(2 * LANES)
    wred = (j[:, None] // RWKV_HEAD) == (j[None, :] // RWKV_HEAD)
    wsel = (j[:, None] // SCAN_BLOCK) == (np.arange(LANES)[None, :] // RWKV_HEAD)
    row_spec = pl.BlockSpec((nseq, SCAN_BLOCK, RWKV_DIM), lambda i: (0, i, 0))
    y = pl.pallas_call(
        _scan_kernel,
        grid=(nblk,),
        in_specs=[row_spec] * 5 + [
            pl.BlockSpec((nseq, 1, PAIR_ROWS, 2 * SCAN_BLOCK), lambda i: (0, i, 0, 0)),
            pl.BlockSpec((2 * LANES, 2 * LANES), lambda i: (0, 0)),
            pl.BlockSpec((2 * LANES, LANES), lambda i: (0, 0))],
        out_specs=pl.BlockSpec((nseq, 1, 2, PAIR_ROWS, LANES), lambda i: (0, i, 0, 0, 0)),
        out_shape=jax.ShapeDtypeStruct((nseq, nblk, 2, PAIR_ROWS, LANES), jnp.float32),
        scratch_shapes=[pltpu.VMEM((nseq, PAIR_ROWS, LANES), jnp.float32)],
        compiler_params=pltpu.CompilerParams(dimension_semantics=("arbitrary",),
                                             vmem_limit_bytes=VMEM_LIMIT_BYTES),
        name="rwkv_scan",
    )(r, w, k, a, b, vt, jnp.asarray(wred, jnp.bfloat16), jnp.asarray(wsel, jnp.bfloat16))
    y = y.reshape(nseq, nblk, 2, N_PAIRS, RWKV_HEAD, 2, RWKV_HEAD)
    y = jnp.transpose(y, (0, 1, 2, 6, 3, 5, 4))
    return y.reshape(nseq, length, RWKV_DIM)


def _rwkv7_features(z, conv_w, w0_f, w2_f, w0_b, w2_b, a0_f, a2_f, a0_b, a2_b, g2, k_k, k_a):
    b, t, _ = z.shape
    c0 = 3 * RWKV_DIM
    c1 = c0 + RWKV_DECAY_LORA
    c2 = c1 + RWKV_DECAY_LORA
    c3 = c2 + RWKV_AAA_LORA
    c4 = c3 + RWKV_AAA_LORA
    rkv_raw, wd_f, wd_b, ad_f, ad_b, gd = jnp.split(z, [c0, c1, c2, c3, c4], axis=-1)
    rkv = _centred_conv3(rkv_raw, conv_w)
    r, k, v = jnp.split(rkv, 3, axis=-1)
    g = mm(jax.nn.sigmoid(gd), g2)

    def heads(u):
        return u.reshape(b, t, RWKV_HEADS, RWKV_HEAD)

    kk = _l2_normalize(heads(k * k_k))

    def direction(wd, w0, w2, ad, a0, a2):
        logw = -jax.nn.softplus(-(w0 + mm(jnp.tanh(wd), w2))) - 0.5
        decay = jnp.exp(-jnp.exp(logw))
        lr = jax.nn.sigmoid(a0 + mm(ad, a2))
        kd = k * (1.0 + (lr - 1.0) * k_a)
        return heads(decay), heads(kd), heads(lr)

    fwd = direction(wd_f, w0_f, w2_f, ad_f, a0_f, a2_f)
    bwd = direction(wd_b, w0_b, w2_b, ad_b, a0_b, a2_b)
    return heads(r), heads(v), kk, g, fwd, bwd


def _rwkv7_readout(y_f, y_b, r, k_f, k_b, v, g, r_k, lnx_g, lnx_b):
    b, t = r.shape[:2]
    y = _normalize(y_f + y_b, RWKV_LNX_EPS).reshape(b, t, RWKV_DIM) * lnx_g + lnx_b
    bonus = jnp.sum(r * (k_f + k_b) * r_k, axis=-1, keepdims=True) * v
    return (y + bonus.reshape(b, t, RWKV_DIM)) * g


def _rwkv7_mixer(z, zc, conv_w, w0_f, w2_f, w0_b, w2_b, a0_f, a2_f, a0_b, a2_b, g2, k_k, k_a,
                 r_k, lnx_g, lnx_b):
    p = (conv_w, w0_f, w2_f, w0_b, w2_b, a0_f, a2_f, a0_b, a2_b, g2, k_k, k_a)
    r, v, kk, g, (w_f, k_f, l_f), (w_b, k_b, l_b) = _rwkv7_features(z, *p)
    rc, vc, kkc, gc, (wc_f, kc_f, lc_f), (wc_b, kc_b, lc_b) = _rwkv7_features(zc, *p)
    nb, t = r.shape[:2]
    tc = rc.shape[1]

    def sequences(uc, u):
        uc = uc.reshape(nb, tc, RWKV_DIM)
        u = u.reshape(nb, t, RWKV_DIM)
        return jnp.concatenate([uc, u], axis=1), jnp.concatenate([uc[:, ::-1], u[:, ::-1]], axis=1)

    def both(uc, u):
        return jnp.concatenate(sequences(uc, u), axis=0)

    def per_direction(uc_f, u_f, uc_b, u_b):
        return jnp.concatenate([sequences(uc_f, u_f)[0], sequences(uc_b, u_b)[1]], axis=0)

    y_all = rwkv_scan(both(rc, r), per_direction(wc_f, w_f, wc_b, w_b), per_direction(kc_f, k_f, kc_b, k_b),
                      both(vc, v), both(-kkc, -kk), per_direction(kkc * lc_f, kk * l_f, kkc * lc_b, kk * l_b))
    y_all = y_all.reshape(2, nb, tc + t, RWKV_HEADS, RWKV_HEAD)
    yc_f, y_f = y_all[0, :, :tc], y_all[0, :, tc:]
    yc_b, y_b = y_all[1, :, :tc][:, ::-1], y_all[1, :, tc:][:, ::-1]
    o = _rwkv7_readout(y_f, y_b, r, k_f, k_b, v, g, r_k, lnx_g, lnx_b)
    oc = _rwkv7_readout(yc_f, yc_b, rc, kc_f, kc_b, vc, gc, r_k, lnx_g, lnx_b)
    return o, oc


def _retention_chunkwise(q, k, v, gamma, s0):
    b, t, h, dk = q.shape
    dv = v.shape[-1]
    nc = t // RET_CHUNK
    log_g = jnp.log(gamma)
    pos = jnp.arange(RET_CHUNK, dtype=jnp.float32)
    rel = pos[:, None] - pos[None, :]
    intra = jnp.where(rel >= 0, jnp.exp(jnp.maximum(rel, 0.0)[None] * log_g[:, None, None]), 0.0)
    q_dec = jnp.exp((pos[:, None] + 1.0) * log_g[None, :])
    k_dec = jnp.exp((RET_CHUNK - 1.0 - pos[:, None]) * log_g[None, :])
    c_dec = jnp.exp(RET_CHUNK * log_g)

    def chunks(u):
        return jnp.moveaxis(u.reshape(b, nc, RET_CHUNK, h, u.shape[-1]), 1, 0)

    def step(s, xs):
        qi, ki, vi = xs
        att = jnp.einsum('bihd,bjhd->bhij', qi, ki) * intra
        o = (jnp.einsum('bhij,bjhe->bihe', att, vi)
             + jnp.einsum('bihd,bhde->bihe', qi * q_dec[None, :, :, None], s))
        s = s * c_dec[None, :, None, None] + jnp.einsum('bjhd,bjhe->bhde', ki * k_dec[None, :, :, None], vi)
        return s, o

    s_fin, o = lax.scan(step, s0, (chunks(q), chunks(k), chunks(v)))
    return jnp.moveaxis(o, 0, 1).reshape(b, t, h, dv), s_fin


def _retention_state(k, v, gamma):
    n = k.shape[1]
    w = jnp.exp((n - 1.0 - jnp.arange(n, dtype=jnp.float32))[:, None] * jnp.log(gamma)[None, :])
    return jnp.einsum('blhd,lh,blhe->bhde', k, w, v)


def _retention_mixer(z, kc, vc, decay_f, decay_b):
    b, t, _ = z.shape
    row, col = _grid_positions(t)
    q, k, v, g_f, g_b = jnp.split(z, [RET_QK, 2 * RET_QK, 2 * RET_QK + RET_VD, 2 * RET_QK + 2 * RET_VD], axis=-1)
    q = _axial_rope(q.reshape(b, t, RET_HEADS, RET_KEY), row, col)
    k = _axial_rope(k.reshape(b, t, RET_HEADS, RET_KEY), row, col) * (RET_KEY ** -0.5)
    v = v.reshape(b, t, RET_HEADS, RET_VAL)
    kc = kc.reshape(b, kc.shape[1], RET_HEADS, RET_KEY) * (RET_KEY ** -0.5)
    vc = vc.reshape(b, vc.shape[1], RET_HEADS, RET_VAL)
    gamma_f = 1.0 - jnp.exp2(-decay_f)
    gamma_b = 1.0 - jnp.exp2(-decay_b)
    s_cf = _retention_state(kc, vc, gamma_f)
    s_cb = _retention_state(kc[:, ::-1], vc[:, ::-1], gamma_b)
    o_f, _ = _retention_chunkwise(q, k, v, gamma_f, s_cf)
    o_b, _ = _retention_chunkwise(q[:, ::-1], k[:, ::-1], v[:, ::-1], gamma_b, s_cb)
    o_b = o_b[:, ::-1]

    def group_norm(o):
        return _normalize(o, 1e-6).reshape(b, t, RET_VD)

    return _silu(g_f) * group_norm(o_f) + _silu(g_b) * group_norm(o_b)


def _moe_swiglu(h, router, w_gate, w_up, w_down):
    b, t, d = h.shape
    n = b * t
    hf = h.reshape(n, d)
    logits = jnp.dot(hf, router, precision=lax.Precision.HIGHEST)
    top_val, top_idx = lax.top_k(logits, TOP_K)
    gate = jax.nn.softmax(top_val, axis=-1)
    n_assign = n * TOP_K
    flat_e = top_idx.reshape(-1)
    order = jnp.argsort(flat_e)
    sorted_e = flat_e[order]
    sorted_tok = (order // TOP_K).astype(jnp.int32)
    sorted_gate = gate.reshape(-1)[order]
    counts = jnp.zeros((N_EXPERTS,), jnp.int32).at[flat_e].add(1)
    padded = (counts + MOE_BLOCK - 1) // MOE_BLOCK * MOE_BLOCK
    pad_end = jnp.cumsum(padded)
    pad_start = pad_end - padded
    start = jnp.cumsum(counts) - counts
    dest = pad_start[sorted_e] + jnp.arange(n_assign, dtype=jnp.int32) - start[sorted_e]
    n_blocks = n_assign // MOE_BLOCK + N_EXPERTS
    slots = n_blocks * MOE_BLOCK
    slot_tok = jnp.full((slots,), n, jnp.int32).at[dest].set(sorted_tok)
    slot_gate = jnp.zeros((slots,), jnp.float32).at[dest].set(sorted_gate)
    block_start = jnp.arange(n_blocks, dtype=jnp.int32) * MOE_BLOCK
    block_expert = jnp.minimum(jnp.sum(pad_end[None, :] <= block_start[:, None], axis=1), N_EXPERTS - 1)
    h_pad = jnp.concatenate([hf, jnp.zeros((1, d), hf.dtype)], axis=0)
    xb = h_pad[slot_tok].reshape(n_blocks, MOE_BLOCK, d)

    def expert_block(args):
        x_blk, e = args
        return _swiglu(x_blk, w_gate[e], w_up[e], w_down[e])

    yb = lax.map(expert_block, (xb, block_expert)).reshape(slots, d)
    out = jnp.zeros((n + 1, d), jnp.float32).at[slot_tok].add(yb * slot_gate[:, None])
    return out[:n].reshape(b, t, d)


def _even_layer(x, xc, sc, scc, mod_w, mod_b, w_in, mla_q_norm, mla_wq_up, mla_kv_norm, mla_wkv_up,
                rwkv_conv, rwkv_w0_f, rwkv_w2_f, rwkv_w0_b, rwkv_w2_b, rwkv_a0_f, rwkv_a2_f, rwkv_a0_b,
                rwkv_a2_b, rwkv_g2, rwkv_k_k, rwkv_k_a, rwkv_r_k, rwkv_lnx_g, rwkv_lnx_b, w_out,
                ln1_g, ln1_b, ffn_w_gate, ffn_w_up, ffn_w_down, ln2_g, ln2_b):
    m = [u[:, None, :] for u in _adaln(sc, mod_w, mod_b)]
    mc = _adaln(scc, mod_w, mod_b)
    z = mm(_modulate(x, m[0], m[1]), w_in)
    zc = mm(_modulate(xc, mc[0], mc[1]), w_in)
    o_mla, oc_mla = _mla_mixer(z[..., :MLA_IN], zc[..., :MLA_IN],
                               mla_q_norm, mla_wq_up, mla_kv_norm, mla_wkv_up)
    o_rwkv, oc_rwkv = _rwkv7_mixer(z[..., MLA_IN:], zc[..., MLA_IN:], rwkv_conv, rwkv_w0_f, rwkv_w2_f,
                                   rwkv_w0_b, rwkv_w2_b, rwkv_a0_f, rwkv_a2_f, rwkv_a0_b, rwkv_a2_b,
                                   rwkv_g2, rwkv_k_k, rwkv_k_a, rwkv_r_k, rwkv_lnx_g, rwkv_lnx_b)
    o = mm(jnp.concatenate([o_mla, o_rwkv], axis=-1), w_out)
    oc = mm(jnp.concatenate([oc_mla, oc_rwkv], axis=-1), w_out)
    x = _post_norm(x, m[2] * o, ln1_g, ln1_b)
    xc = _post_norm(xc, mc[2] * oc, ln1_g, ln1_b)
    x = _post_norm(x, m[5] * _swiglu(_modulate(x, m[3], m[4]), ffn_w_gate, ffn_w_up, ffn_w_down), ln2_g, ln2_b)
    xc = _post_norm(xc, mc[5] * _swiglu(_modulate(xc, mc[3], mc[4]), ffn_w_gate, ffn_w_up, ffn_w_down),
                    ln2_g, ln2_b)
    return x, xc


def _odd_layer(x, xc, sc, scc, mod_w, mod_b, w_in, ret_decay_f, ret_decay_b, w_out, ln1_g, ln1_b,
               router, moe_w_gate, moe_w_up, moe_w_down, ln2_g, ln2_b):
    m = [u[:, None, :] for u in _adaln(sc, mod_w, mod_b)]
    c_shift, c_scale = jnp.split(
        jnp.dot(scc, mod_w[:, :2 * D_MODEL], precision=lax.Precision.HIGHEST) + mod_b[:2 * D_MODEL], 2)
    z = mm(_modulate(x, m[0], m[1]), w_in)
    zc = mm(_modulate(xc, c_shift, c_scale), w_in[:, RET_QK:2 * RET_QK + RET_VD])
    kc, vc = jnp.split(zc, [RET_QK], axis=-1)
    o = mm(_retention_mixer(z, kc, vc, ret_decay_f, ret_decay_b), w_out)
    x = _post_norm(x, m[2] * o, ln1_g, ln1_b)
    y = _moe_swiglu(_modulate(x, m[3], m[4]), router, moe_w_gate, moe_w_up, moe_w_down)
    return _post_norm(x, m[5] * y, ln2_g, ln2_b)


def kernel(x, c, ctx, c_ctx, l0_mod_w, l0_mod_b, l0_w_in, l0_mla_q_norm, l0_mla_wq_up, l0_mla_kv_norm, l0_mla_wkv_up, l0_rwkv_conv, l0_rwkv_w0_f, l0_rwkv_w2_f, l0_rwkv_w0_b, l0_rwkv_w2_b, l0_rwkv_a0_f, l0_rwkv_a2_f, l0_rwkv_a0_b, l0_rwkv_a2_b, l0_rwkv_g2, l0_rwkv_k_k, l0_rwkv_k_a, l0_rwkv_r_k, l0_rwkv_lnx_g, l0_rwkv_lnx_b, l0_w_out, l0_ln1_g, l0_ln1_b, l0_ffn_w_gate, l0_ffn_w_up, l0_ffn_w_down, l0_ln2_g, l0_ln2_b, l1_mod_w, l1_mod_b, l1_w_in, l1_ret_decay_f, l1_ret_decay_b, l1_w_out, l1_ln1_g, l1_ln1_b, l1_router, l1_moe_w_gate, l1_moe_w_up, l1_moe_w_down, l1_ln2_g, l1_ln2_b):
    even_params = (l0_mod_w, l0_mod_b, l0_w_in, l0_mla_q_norm, l0_mla_wq_up, l0_mla_kv_norm, l0_mla_wkv_up,
                   l0_rwkv_conv, l0_rwkv_w0_f, l0_rwkv_w2_f, l0_rwkv_w0_b, l0_rwkv_w2_b, l0_rwkv_a0_f,
                   l0_rwkv_a2_f, l0_rwkv_a0_b, l0_rwkv_a2_b, l0_rwkv_g2, l0_rwkv_k_k, l0_rwkv_k_a, l0_rwkv_r_k,
                   l0_rwkv_lnx_g, l0_rwkv_lnx_b, l0_w_out, l0_ln1_g, l0_ln1_b, l0_ffn_w_gate, l0_ffn_w_up,
                   l0_ffn_w_down, l0_ln2_g, l0_ln2_b)
    odd_params = (l1_mod_w, l1_mod_b, l1_w_in, l1_ret_decay_f, l1_ret_decay_b, l1_w_out, l1_ln1_g, l1_ln1_b,
                  l1_router, l1_moe_w_gate, l1_moe_w_up, l1_moe_w_down, l1_ln2_g, l1_ln2_b)
    sc = _silu(c)
    scc = _silu(c_ctx)
    x, xc = _even_layer(x, ctx, sc, scc, *even_params)
    return _odd_layer(x, xc, sc, scc, *odd_params)
```

```python
import functools

import jax
import jax.numpy as jnp
import numpy as np
from jax import lax
from jax.experimental import pallas as pl
from jax.experimental.pallas import tpu as pltpu

D_MODEL = 1024
DEPTH = 2
GRID_W = 64
ROPE_BASE = 10000.0
DEEPNORM_ALPHA = (2 * DEPTH) ** 0.25
LN_EPS = 1e-5

MLA_HEADS = 8
MLA_Q_RANK = 256
MLA_KV_RANK = 128
MLA_NOPE = 64
MLA_ROPE = 32
MLA_V = 64
MLA_IN = MLA_Q_RANK + MLA_KV_RANK + MLA_ROPE
MLA_OUT = MLA_HEADS * MLA_V

RWKV_HEADS = 8
RWKV_HEAD = 64
RWKV_DIM = RWKV_HEADS * RWKV_HEAD
RWKV_DECAY_LORA = 64
RWKV_AAA_LORA = 64
RWKV_GATE_LORA = 128
RWKV_LNX_EPS = 64e-5

RET_HEADS = 4
RET_KEY = 256
RET_VAL = 512
RET_TC = 256
RET_QK = RET_HEADS * RET_KEY
RET_VD = RET_HEADS * RET_VAL

N_EXPERTS = 8
TOP_K = 2
MOE_TM = 512
MOE_TF = 512

LANES = 128
VMEM_LIMIT_BYTES = 48 * 1024 * 1024


def _matmul_kernel(x_ref, w_ref, o_ref, acc_ref):
    @pl.when(pl.program_id(2) == 0)
    def _():
        acc_ref[...] = jnp.zeros_like(acc_ref)

    acc_ref[...] += jnp.dot(x_ref[...].astype(jnp.bfloat16), w_ref[...].astype(jnp.bfloat16),
                            preferred_element_type=jnp.float32)

    @pl.when(pl.program_id(2) == pl.num_programs(2) - 1)
    def _():
        o_ref[...] = acc_ref[...]


def _pick_tile(n, candidates):
    for c in candidates:
        if n % c == 0:
            return c
    raise ValueError(f"no tile for {n}")


def pmatmul(x, w):
    m, k = x.shape
    n = w.shape[1]
    n_pad = (-n) % LANES
    if n_pad:
        w = jnp.pad(w, ((0, 0), (0, n_pad)))
    np_ = n + n_pad
    tm = _pick_tile(m, (512, 256, 128, 8))
    tn = _pick_tile(np_, (512, 384, 256, 128))
    tk = k if k <= 1024 else _pick_tile(k, (512, 256, 128))
    out = pl.pallas_call(
        _matmul_kernel,
        grid=(m // tm, np_ // tn, k // tk),
        in_specs=[pl.BlockSpec((tm, tk), lambda i, j, kk: (i, kk)),
                  pl.BlockSpec((tk, tn), lambda i, j, kk: (kk, j))],
        out_specs=pl.BlockSpec((tm, tn), lambda i, j, kk: (i, j)),
        out_shape=jax.ShapeDtypeStruct((m, np_), jnp.float32),
        scratch_shapes=[pltpu.VMEM((tm, tn), jnp.float32)],
        compiler_params=pltpu.CompilerParams(
            dimension_semantics=("parallel", "parallel", "arbitrary"),
            vmem_limit_bytes=VMEM_LIMIT_BYTES),
        name="matmul",
    )(x, w)
    return out[:, :n] if n_pad else out


def mm(x, w):
    lead = x.shape[:-1]
    out = pmatmul(x.reshape(-1, x.shape[-1]), w)
    return out.reshape(*lead, w.shape[1])


def _silu(t):
    return t * jax.nn.sigmoid(t)


def _normalize(t, eps):
    mu = jnp.mean(t, axis=-1, keepdims=True)
    var = jnp.mean(jnp.square(t - mu), axis=-1, keepdims=True)
    return (t - mu) * lax.rsqrt(var + eps)


def _layer_norm(t, g, b):
    return _normalize(t, LN_EPS) * g + b


def _rms_norm(t, g, eps=1e-6):
    return t * lax.rsqrt(jnp.mean(t * t, axis=-1, keepdims=True) + eps) * g


def _l2_normalize(t, eps=1e-12):
    return t / jnp.maximum(jnp.linalg.norm(t, axis=-1, keepdims=True), eps)


def _post_norm(x, update, g, b):
    return _layer_norm(DEEPNORM_ALPHA * x + update, g, b)


def _modulate(h, shift, scale):
    return h * (1.0 + scale) + shift


def _adaln(cond, mod_w, mod_b):
    return jnp.split(jnp.dot(cond, mod_w, precision=lax.Precision.HIGHEST) + mod_b, 6, axis=-1)


def _swiglu(h, w_gate, w_up, w_down):
    return mm(_silu(mm(h, w_gate)) * mm(h, w_up), w_down)


def _centred_conv3(t, w):
    tp = jnp.pad(t, ((0, 0), (1, 1), (0, 0)))
    return tp[:, :-2] * w[0] + tp[:, 1:-1] * w[1] + tp[:, 2:] * w[2]


def _grid_positions(n_tokens):
    rows = n_tokens // GRID_W
    row = jnp.repeat(jnp.arange(rows, dtype=jnp.float32), GRID_W)
    col = jnp.tile(jnp.arange(GRID_W, dtype=jnp.float32), rows)
    return row, col


def _rotate(t, pos):
    nf = t.shape[-1] // 2
    inv_freq = ROPE_BASE ** (-jnp.arange(nf, dtype=jnp.float32) / nf)
    ang = pos[:, None] * inv_freq[None, :]
    cos = jnp.cos(ang)[None, :, None, :]
    sin = jnp.sin(ang)[None, :, None, :]
    t1, t2 = t[..., :nf], t[..., nf:]
    return jnp.concatenate([t1 * cos - t2 * sin, t1 * sin + t2 * cos], axis=-1)


def _axial_rope(t, row, col):
    half = t.shape[-1] // 2
    return jnp.concatenate([_rotate(t[..., :half], row), _rotate(t[..., half:], col)], axis=-1)


SOFTMAX_FLOOR = -1e30


def _flash_kernel(q_ref, k_ref, v_ref, o_ref, m_ref, l_ref, acc_ref):
    j = pl.program_id(3)

    @pl.when(j == 0)
    def _():
        m_ref[...] = jnp.full_like(m_ref, SOFTMAX_FLOOR)
        l_ref[...] = jnp.zeros_like(l_ref)
        acc_ref[...] = jnp.zeros_like(acc_ref)

    s = lax.dot_general(q_ref[0, 0], k_ref[0, 0], (((1,), (1,)), ((), ())),
                        preferred_element_type=jnp.float32)
    m_prev = m_ref[...]
    m_new = jnp.maximum(m_prev, jnp.max(s, axis=-1, keepdims=True))
    alpha = jnp.exp(m_prev - m_new)
    p = jnp.exp(s - m_new)
    l_ref[...] = alpha * l_ref[...] + jnp.sum(p, axis=-1, keepdims=True)
    acc_ref[...] = alpha * acc_ref[...] + jnp.dot(p.astype(jnp.bfloat16), v_ref[0, 0],
                                                  preferred_element_type=jnp.float32)
    m_ref[...] = m_new

    @pl.when(j == pl.num_programs(3) - 1)
    def _():
        o_ref[0, 0] = acc_ref[...] / l_ref[...]


def flash_attention(q, k, v, tq, tk):
    b, h, t, dq = q.shape
    s = k.shape[2]
    dv = v.shape[3]
    return pl.pallas_call(
        _flash_kernel,
        grid=(b, h, t // tq, s // tk),
        in_specs=[pl.BlockSpec((1, 1, tq, dq), lambda bi, hi, i, j: (bi, hi, i, 0)),
                  pl.BlockSpec((1, 1, tk, dq), lambda bi, hi, i, j: (bi, hi, j, 0)),
                  pl.BlockSpec((1, 1, tk, dv), lambda bi, hi, i, j: (bi, hi, j, 0))],
        out_specs=pl.BlockSpec((1, 1, tq, dv), lambda bi, hi, i, j: (bi, hi, i, 0)),
        out_shape=jax.ShapeDtypeStruct((b, h, t, dv), jnp.float32),
        scratch_shapes=[pltpu.VMEM((tq, 1), jnp.float32), pltpu.VMEM((tq, 1), jnp.float32),
                        pltpu.VMEM((tq, dv), jnp.float32)],
        compiler_params=pltpu.CompilerParams(
            dimension_semantics=("parallel", "parallel", "parallel", "arbitrary"),
            vmem_limit_bytes=VMEM_LIMIT_BYTES),
        name="flash_attention",
    )(q, k, v)


def _block_attention(q, k, v):
    dq = q.shape[-1]
    to_heads = lambda u: jnp.transpose(u, (0, 2, 1, 3)).astype(jnp.bfloat16)
    tq = _pick_tile(q.shape[1], (512, 256))
    tk = _pick_tile(k.shape[1], (768, 512, 256))
    o = flash_attention(to_heads(q * dq ** -0.5), to_heads(k), to_heads(v), tq, tk)
    return jnp.transpose(o, (0, 2, 1, 3))


def _mla_project(zz, q_norm, wq_up, kv_norm, wkv_up):
    b, t, _ = zz.shape
    cq, ckv, k_rope = jnp.split(zz, [MLA_Q_RANK, MLA_Q_RANK + MLA_KV_RANK], axis=-1)
    q = mm(_rms_norm(cq, q_norm), wq_up).reshape(b, t, MLA_HEADS, MLA_NOPE + MLA_ROPE)
    kv = mm(_rms_norm(ckv, kv_norm), wkv_up).reshape(b, t, MLA_HEADS, MLA_NOPE + MLA_V)
    return q, kv[..., :MLA_NOPE], k_rope[:, :, None, :], kv[..., MLA_NOPE:]


def _mla_keys(k_nope, k_rope):
    b, t, h, _ = k_nope.shape
    return jnp.concatenate([k_nope, jnp.broadcast_to(k_rope, (b, t, h, MLA_ROPE))], axis=-1)


def _mla_mixer(z, zc, q_norm, wq_up, kv_norm, wkv_up):
    b, t, _ = z.shape
    row, col = _grid_positions(t)
    q, k_nope, k_rope, v = _mla_project(z, q_norm, wq_up, kv_norm, wkv_up)
    qc, k_nope_c, k_rope_c, vc = _mla_project(zc, q_norm, wq_up, kv_norm, wkv_up)
    q = jnp.concatenate([q[..., :MLA_NOPE], _axial_rope(q[..., MLA_NOPE:], row, col)], axis=-1)
    k = _mla_keys(k_nope, _axial_rope(k_rope, row, col))
    kc = _mla_keys(k_nope_c, k_rope_c)
    o = _block_attention(q, jnp.concatenate([kc, k], axis=1), jnp.concatenate([vc, v], axis=1))
    oc = _block_attention(qc, kc, vc)
    return o.reshape(b, t, MLA_OUT), oc.reshape(b, zc.shape[1], MLA_OUT)


SCAN_BLOCK = 128
N_PAIRS = RWKV_HEADS // 2
PAIR_ROWS = N_PAIRS * RWKV_HEAD


def _scan_kernel(rf_ref, rb_ref, af_ref, ab_ref, vtf_ref, vtb_ref, wf_ref, wb_ref, kf_ref, kb_ref,
                 bf_ref, bb_ref, wred_ref, wsel_ref, yf_ref, yb_ref, s_ref):
    nb = rf_ref.shape[0]

    @pl.when(pl.program_id(0) == 0)
    def _():
        s_ref[...] = jnp.zeros_like(s_ref)

    lane = lax.broadcasted_iota(jnp.int32, (PAIR_ROWS, LANES), 1)
    lane_t = lax.broadcasted_iota(jnp.int32, (PAIR_ROWS, 2 * SCAN_BLOCK), 1) % SCAN_BLOCK
    wred = wred_ref[...]
    wsel = wsel_ref[...]
    refs = ((rf_ref, af_ref, vtf_ref, wf_ref, kf_ref, bf_ref, yf_ref),
            (rb_ref, ab_ref, vtb_ref, wb_ref, kb_ref, bb_ref, yb_ref))

    def rows(ref, bi, t):
        row = ref[bi, pl.ds(t, 1), :]
        return jnp.concatenate(
            [jnp.broadcast_to(row[:, p * LANES:(p + 1) * LANES], (RWKV_HEAD, LANES)) for p in range(N_PAIRS)],
            axis=0)

    def collect(y_ref, bi, t, yb, valid):
        mask = jnp.logical_and(lane % RWKV_HEAD == t % RWKV_HEAD, valid)
        half = t // RWKV_HEAD
        y_ref[bi, 0, half] = jnp.where(mask, yb, y_ref[bi, 0, half])

    def body(i, carry):
        for d in range(2):
            r_ref, a_ref, vt_ref, w_ref, k_ref, b_ref, y_ref = refs[d]
            t = i if d == 0 else SCAN_BLOCK - 1 - i
            tp = jnp.maximum(i - 1, 0) if d == 0 else jnp.minimum(SCAN_BLOCK - i, SCAN_BLOCK - 1)
            for bi in range(nb):
                s = s_ref[d, bi]
                pa = (s * rows(a_ref, bi, t)).astype(jnp.bfloat16)
                pr = (s * rows(r_ref, bi, tp)).astype(jnp.bfloat16)
                red = jnp.dot(jnp.concatenate([pa, pr], axis=1), wred, preferred_element_type=jnp.float32)
                pv = jnp.where(lane_t == t, vt_ref[bi, 0], jnp.zeros((), jnp.bfloat16))
                vb = jnp.dot(pv, wsel, preferred_element_type=jnp.float32)
                s_ref[d, bi] = (s * rows(w_ref, bi, t) + red[:, :LANES] * rows(b_ref, bi, t)
                                + vb * rows(k_ref, bi, t))
                collect(y_ref, bi, tp, red[:, LANES:], i >= 1)
        return carry

    lax.fori_loop(0, SCAN_BLOCK, body, 0)

    for d in range(2):
        r_ref, y_ref = refs[d][0], refs[d][6]
        t_last = SCAN_BLOCK - 1 if d == 0 else 0
        for bi in range(nb):
            pr = (s_ref[d, bi] * rows(r_ref, bi, t_last)).astype(jnp.bfloat16)
            red = jnp.dot(jnp.concatenate([pr, pr], axis=1), wred, preferred_element_type=jnp.float32)
            collect(y_ref, bi, t_last, red[:, LANES:], True)


def rwkv_scan(r, v, a, w_f, k_f, b_f, w_b, k_b, b_b, n_ctx):
    nb, length, _ = r.shape
    nblk = length // SCAN_BLOCK
    nblk_ctx = n_ctx // SCAN_BLOCK
    vt = v.reshape(nb, nblk, SCAN_BLOCK, N_PAIRS, 2, RWKV_HEAD)
    vt = jnp.transpose(vt, (0, 1, 3, 5, 4, 2)).reshape(nb, nblk, PAIR_ROWS, 2 * SCAN_BLOCK).astype(jnp.bfloat16)
    j = np.arange(2 * LANES)
    wred = (j[:, None] // RWKV_HEAD) == (j[None, :] // RWKV_HEAD)
    wsel = (j[:, None] // SCAN_BLOCK) == (np.arange(LANES)[None, :] // RWKV_HEAD)

    def fwd(i):
        return i

    def bwd(i):
        return jnp.where(i < nblk_ctx, nblk_ctx - 1 - i, nblk + nblk_ctx - 1 - i)

    def row_spec(blk):
        return pl.BlockSpec((nb, SCAN_BLOCK, RWKV_DIM), lambda i: (0, blk(i), 0))

    def vt_spec(blk):
        return pl.BlockSpec((nb, 1, PAIR_ROWS, 2 * SCAN_BLOCK), lambda i: (0, blk(i), 0, 0))

    def y_spec(blk):
        return pl.BlockSpec((nb, 1, 2, PAIR_ROWS, LANES), lambda i: (0, blk(i), 0, 0, 0))

    y_shape = jax.ShapeDtypeStruct((nb, nblk, 2, PAIR_ROWS, LANES), jnp.float32)
    ys = pl.pallas_call(
        _scan_kernel,
        grid=(nblk,),
        in_specs=[row_spec(fwd), row_spec(bwd), row_spec(fwd), row_spec(bwd), vt_spec(fwd), vt_spec(bwd),
                  row_spec(fwd), row_spec(bwd), row_spec(fwd), row_spec(bwd), row_spec(fwd), row_spec(bwd),
                  pl.BlockSpec((2 * LANES, 2 * LANES), lambda i: (0, 0)),
                  pl.BlockSpec((2 * LANES, LANES), lambda i: (0, 0))],
        out_specs=[y_spec(fwd), y_spec(bwd)],
        out_shape=[y_shape, y_shape],
        scratch_shapes=[pltpu.VMEM((2, nb, PAIR_ROWS, LANES), jnp.float32)],
        compiler_params=pltpu.CompilerParams(dimension_semantics=("arbitrary",),
                                             vmem_limit_bytes=VMEM_LIMIT_BYTES),
        name="rwkv_scan",
    )(r, r, a, a, vt, vt, w_f, w_b, k_f, k_b, b_f, b_b,
      jnp.asarray(wred, jnp.bfloat16), jnp.asarray(wsel, jnp.bfloat16))

    def untile(y):
        y = y.reshape(nb, nblk, 2, N_PAIRS, RWKV_HEAD, 2, RWKV_HEAD)
        return jnp.transpose(y, (0, 1, 2, 6, 3, 5, 4)).reshape(nb, length, RWKV_DIM)

    return untile(ys[0]), untile(ys[1])


def _rwkv7_features(z, conv_w, w0_f, w2_f, w0_b, w2_b, a0_f, a2_f, a0_b, a2_b, g2, k_k, k_a):
    b, t, _ = z.shape
    c0 = 3 * RWKV_DIM
    c1 = c0 + RWKV_DECAY_LORA
    c2 = c1 + RWKV_DECAY_LORA
    c3 = c2 + RWKV_AAA_LORA
    c4 = c3 + RWKV_AAA_LORA
    rkv_raw, wd_f, wd_b, ad_f, ad_b, gd = jnp.split(z, [c0, c1, c2, c3, c4], axis=-1)
    rkv = _centred_conv3(rkv_raw, conv_w)
    r, k, v = jnp.split(rkv, 3, axis=-1)
    g = mm(jax.nn.sigmoid(gd), g2)

    def heads(u):
        return u.reshape(b, t, RWKV_HEADS, RWKV_HEAD)

    kk = _l2_normalize(heads(k * k_k))

    def direction(wd, w0, w2, ad, a0, a2):
        logw = -jax.nn.softplus(-(w0 + mm(jnp.tanh(wd), w2))) - 0.5
        decay = jnp.exp(-jnp.exp(logw))
        lr = jax.nn.sigmoid(a0 + mm(ad, a2))
        kd = k * (1.0 + (lr - 1.0) * k_a)
        return heads(decay), heads(kd), heads(lr)

    fwd = direction(wd_f, w0_f, w2_f, ad_f, a0_f, a2_f)
    bwd = direction(wd_b, w0_b, w2_b, ad_b, a0_b, a2_b)
    return heads(r), heads(v), kk, g, fwd, bwd


def _rwkv7_readout(y_f, y_b, r, k_f, k_b, v, g, r_k, lnx_g, lnx_b):
    b, t = r.shape[:2]
    y = _normalize(y_f + y_b, RWKV_LNX_EPS).reshape(b, t, RWKV_DIM) * lnx_g + lnx_b
    bonus = jnp.sum(r * (k_f + k_b) * r_k, axis=-1, keepdims=True) * v
    return (y + bonus.reshape(b, t, RWKV_DIM)) * g


def _rwkv7_mixer(z, zc, conv_w, w0_f, w2_f, w0_b, w2_b, a0_f, a2_f, a0_b, a2_b, g2, k_k, k_a,
                 r_k, lnx_g, lnx_b):
    p = (conv_w, w0_f, w2_f, w0_b, w2_b, a0_f, a2_f, a0_b, a2_b, g2, k_k, k_a)
    r, v, kk, g, (w_f, k_f, l_f), (w_b, k_b, l_b) = _rwkv7_features(z, *p)
    rc, vc, kkc, gc, (wc_f, kc_f, lc_f), (wc_b, kc_b, lc_b) = _rwkv7_features(zc, *p)
    nb, t = r.shape[:2]
    tc = rc.shape[1]

    def cat(uc, u):
        return jnp.concatenate([uc.reshape(nb, tc, RWKV_DIM), u.reshape(nb, t, RWKV_DIM)], axis=1)

    ys_f, ys_b = rwkv_scan(cat(rc, r), cat(vc, v), cat(-kkc, -kk),
                           cat(wc_f, w_f), cat(kc_f, k_f), cat(kkc * lc_f, kk * l_f),
                           cat(wc_b, w_b), cat(kc_b, k_b), cat(kkc * lc_b, kk * l_b), tc)
    ys_f = ys_f.reshape(nb, tc + t, RWKV_HEADS, RWKV_HEAD)
    ys_b = ys_b.reshape(nb, tc + t, RWKV_HEADS, RWKV_HEAD)
    yc_f, y_f = ys_f[:, :tc], ys_f[:, tc:]
    yc_b, y_b = ys_b[:, :tc], ys_b[:, tc:]
    o = _rwkv7_readout(y_f, y_b, r, k_f, k_b, v, g, r_k, lnx_g, lnx_b)
    oc = _rwkv7_readout(yc_f, yc_b, rc, kc_f, kc_b, vc, gc, r_k, lnx_g, lnx_b)
    return o, oc


def _rope_halves(t, cos, sin):
    parts = []
    for s in range(2):
        u = t[:, s * LANES:(s + 1) * LANES]
        parts.append(u * cos[:, s * LANES:(s + 1) * LANES]
                     + pltpu.roll(u, LANES // 2, axis=1) * sin[:, s * LANES:(s + 1) * LANES])
    return jnp.concatenate(parts, axis=1)


def _retention_kernel(*refs, has_prev):
    if has_prev:
        (q_ref, k_ref, v_ref, g_ref, kc_ref, vc_ref, cos_ref, sin_ref, dmat_ref, qdec_ref, kdec_ref,
         kcdec_ref, cdec_ref, prev_ref, o_ref, s_ref) = refs
    else:
        (q_ref, k_ref, v_ref, g_ref, kc_ref, vc_ref, cos_ref, sin_ref, dmat_ref, qdec_ref, kdec_ref,
         kcdec_ref, cdec_ref, o_ref, s_ref) = refs
        prev_ref = None
    scale = RET_KEY ** -0.5

    @pl.when(pl.program_id(2) == 0)
    def _():
        kc = (kc_ref[0] * kcdec_ref[0] * scale).astype(jnp.bfloat16)
        s_ref[...] = lax.dot_general(kc, vc_ref[0].astype(jnp.bfloat16), (((0,), (0,)), ((), ())),
                                     preferred_element_type=jnp.float32)

    cos = cos_ref[...]
    sin = sin_ref[...]
    q = _rope_halves(q_ref[0], cos, sin)
    k = _rope_halves(k_ref[0], cos, sin) * scale
    v = v_ref[0].astype(jnp.bfloat16)
    s = s_ref[...]
    att = lax.dot_general(q.astype(jnp.bfloat16), k.astype(jnp.bfloat16), (((1,), (1,)), ((), ())),
                          preferred_element_type=jnp.float32) * dmat_ref[0]
    o = (jnp.dot(att.astype(jnp.bfloat16), v, preferred_element_type=jnp.float32)
         + jnp.dot((q * qdec_ref[0]).astype(jnp.bfloat16), s.astype(jnp.bfloat16),
                   preferred_element_type=jnp.float32))
    s_ref[...] = s * cdec_ref[0] + lax.dot_general((k * kdec_ref[0]).astype(jnp.bfloat16), v,
                                                   (((0,), (0,)), ((), ())),
                                                   preferred_element_type=jnp.float32)
    mu = jnp.mean(o, axis=-1, keepdims=True)
    oc = o - mu
    var = jnp.mean(oc * oc, axis=-1, keepdims=True)
    out = _silu(g_ref[0]) * (oc * lax.rsqrt(var + 1e-6))
    if prev_ref is not None:
        out = out + prev_ref[0]
    o_ref[0] = out


def _retention_tables(gamma, reverse, n_ctx):
    log_g = jnp.log(gamma)[:, None, None]
    i = jnp.arange(RET_TC, dtype=jnp.float32)
    rel = (i[None, :] - i[:, None]) if reverse else (i[:, None] - i[None, :])
    dmat = jnp.where(rel >= 0, jnp.exp(jnp.maximum(rel, 0.0)[None] * log_g), 0.0)
    q_pow = (RET_TC - i) if reverse else (i + 1.0)
    k_pow = i if reverse else (RET_TC - 1.0 - i)
    m = jnp.arange(n_ctx, dtype=jnp.float32)
    c_pow = m if reverse else (n_ctx - 1.0 - m)
    bc = lambda p: jnp.broadcast_to(jnp.exp(p[None, :, None] * log_g), (RET_HEADS, p.shape[0], RET_KEY))
    cdec = jnp.broadcast_to(jnp.exp(RET_TC * log_g), (RET_HEADS, 1, RET_VAL))
    return dmat, bc(q_pow), bc(k_pow), bc(c_pow), cdec


def _rope_tables(n_tokens):
    pos_row = (jnp.arange(n_tokens) // GRID_W).astype(jnp.float32)
    pos_col = (jnp.arange(n_tokens) % GRID_W).astype(jnp.float32)
    nf = RET_KEY // 4
    inv_freq = ROPE_BASE ** (-jnp.arange(nf, dtype=jnp.float32) / nf)
    cos, sin = [], []
    for pos in (pos_row, pos_col):
        ang = pos[:, None] * inv_freq[None, :]
        cos += [jnp.cos(ang), jnp.cos(ang)]
        sin += [-jnp.sin(ang), jnp.sin(ang)]
    return jnp.concatenate(cos, axis=1), jnp.concatenate(sin, axis=1)


def _retention_direction(z, zc, cos, sin, gamma, reverse, prev):
    b, t, _ = z.shape
    n_ctx = zc.shape[1]
    nc = t // RET_TC
    dmat, qdec, kdec, kcdec, cdec = _retention_tables(gamma, reverse, n_ctx)
    ch = (lambda c: nc - 1 - c) if reverse else (lambda c: c)
    kq, kv = RET_QK // RET_KEY, (2 * RET_QK) // RET_VAL
    g_off = (2 * RET_QK + (2 if reverse else 1) * RET_VD) // RET_VAL
    in_specs = [
        pl.BlockSpec((1, RET_TC, RET_KEY), lambda bi, h, c: (bi, ch(c), h)),
        pl.BlockSpec((1, RET_TC, RET_KEY), lambda bi, h, c: (bi, ch(c), kq + h)),
        pl.BlockSpec((1, RET_TC, RET_VAL), lambda bi, h, c: (bi, ch(c), kv + h)),
        pl.BlockSpec((1, RET_TC, RET_VAL), lambda bi, h, c: (bi, ch(c), g_off + h)),
        pl.BlockSpec((1, n_ctx, RET_KEY), lambda bi, h, c: (bi, 0, h)),
        pl.BlockSpec((1, n_ctx, RET_VAL), lambda bi, h, c: (bi, 0, RET_QK // RET_VAL + h)),
        pl.BlockSpec((RET_TC, RET_KEY), lambda bi, h, c: (ch(c), 0)),
        pl.BlockSpec((RET_TC, RET_KEY), lambda bi, h, c: (ch(c), 0)),
        pl.BlockSpec((1, RET_TC, RET_TC), lambda bi, h, c: (h, 0, 0)),
        pl.BlockSpec((1, RET_TC, RET_KEY), lambda bi, h, c: (h, 0, 0)),
        pl.BlockSpec((1, RET_TC, RET_KEY), lambda bi, h, c: (h, 0, 0)),
        pl.BlockSpec((1, n_ctx, RET_KEY), lambda bi, h, c: (h, 0, 0)),
        pl.BlockSpec((1, 1, RET_VAL), lambda bi, h, c: (h, 0, 0)),
    ]
    args = [z, z, z, z, zc, zc, cos, sin, dmat, qdec, kdec, kcdec, cdec]
    if prev is not None:
        in_specs.append(pl.BlockSpec((1, RET_TC, RET_VAL), lambda bi, h, c: (bi, ch(c), h)))
        args.append(prev)
    return pl.pallas_call(
        functools.partial(_retention_kernel, has_prev=prev is not None),
        grid=(b, RET_HEADS, nc),
        in_specs=in_specs,
        out_specs=pl.BlockSpec((1, RET_TC, RET_VAL), lambda bi, h, c: (bi, ch(c), h)),
        out_shape=jax.ShapeDtypeStruct((b, t, RET_VD), jnp.float32),
        scratch_shapes=[pltpu.VMEM((RET_KEY, RET_VAL), jnp.float32)],
        compiler_params=pltpu.CompilerParams(dimension_semantics=("parallel", "parallel", "arbitrary"),
                                             vmem_limit_bytes=VMEM_LIMIT_BYTES),
        name="retention_bwd" if reverse else "retention_fwd",
    )(*args)


def retention_mixer(z, zc, decay_f, decay_b):
    cos, sin = _rope_tables(z.shape[1])
    gamma_f = 1.0 - jnp.exp2(-decay_f)
    gamma_b = 1.0 - jnp.exp2(-decay_b)
    part = _retention_direction(z, zc, cos, sin, gamma_b, True, None)
    return _retention_direction(z, zc, cos, sin, gamma_f, False, part)


def _moe_kernel(te_ref, tv_ref, x_ref, gate_ref, wg_ref, wu_ref, wd_ref, o_ref, acc_ref):
    i = pl.program_id(0)
    j = pl.program_id(1)

    @pl.when(tv_ref[i] > 0)
    def _():
        @pl.when(j == 0)
        def _():
            acc_ref[...] = jnp.zeros_like(acc_ref)
        x = x_ref[...]
        g = jnp.dot(x, wg_ref[0], preferred_element_type=jnp.float32)
        u = jnp.dot(x, wu_ref[0], preferred_element_type=jnp.float32)
        a = (_silu(g) * u).astype(jnp.bfloat16)
        acc_ref[...] += jnp.dot(a, wd_ref[0], preferred_element_type=jnp.float32)

    last = j == pl.num_programs(1) - 1

    @pl.when(jnp.logical_and(last, tv_ref[i] > 0))
    def _():
        o_ref[...] = acc_ref[...] * gate_ref[...]

    @pl.when(jnp.logical_and(last, tv_ref[i] == 0))
    def _():
        o_ref[...] = jnp.zeros_like(o_ref)


def moe_experts(xb, slot_gate, tile_expert, tile_valid, w_gate, w_up, w_down):
    slots, d = xb.shape
    ff = w_gate.shape[2]
    n_tiles = slots // MOE_TM
    grid_spec = pltpu.PrefetchScalarGridSpec(
        num_scalar_prefetch=2,
        grid=(n_tiles, ff // MOE_TF),
        in_specs=[
            pl.BlockSpec((MOE_TM, d), lambda i, j, te, tv: (i, 0)),
            pl.BlockSpec((MOE_TM, 1), lambda i, j, te, tv: (i, 0)),
            pl.BlockSpec((1, d, MOE_TF), lambda i, j, te, tv: (te[i], 0, j)),
            pl.BlockSpec((1, d, MOE_TF), lambda i, j, te, tv: (te[i], 0, j)),
            pl.BlockSpec((1, MOE_TF, d), lambda i, j, te, tv: (te[i], j, 0)),
        ],
        out_specs=pl.BlockSpec((MOE_TM, d), lambda i, j, te, tv: (i, 0)),
        scratch_shapes=[pltpu.VMEM((MOE_TM, d), jnp.float32)],
    )
    return pl.pallas_call(
        _moe_kernel,
        grid_spec=grid_spec,
        out_shape=jax.ShapeDtypeStruct((slots, d), jnp.float32),
        compiler_params=pltpu.CompilerParams(dimension_semantics=("arbitrary", "arbitrary"),
                                             vmem_limit_bytes=VMEM_LIMIT_BYTES),
        name="moe_experts",
    )(tile_expert, tile_valid, xb, slot_gate, w_gate, w_up, w_down)


def moe_swiglu(h, router, w_gate, w_up, w_down):
    b, t, d = h.shape
    n = b * t
    hf = h.reshape(n, d)
    logits = jnp.dot(hf, router, precision=lax.Precision.HIGHEST)
    top_val, top_idx = lax.top_k(logits, TOP_K)
    gate = jax.nn.softmax(top_val, axis=-1)
    flat_e = top_idx.reshape(-1)
    onehot = (flat_e[:, None] == jnp.arange(N_EXPERTS, dtype=flat_e.dtype)[None, :]).astype(jnp.int32)
    csum = jnp.cumsum(onehot, axis=0)
    counts = csum[-1]
    rank = jnp.sum((csum - onehot) * onehot, axis=1)
    padded = (counts + MOE_TM - 1) // MOE_TM * MOE_TM
    pad_end = jnp.cumsum(padded)
    pad_start = pad_end - padded
    slot = (pad_start[flat_e] + rank).astype(jnp.int32)
    n_tiles = (n * TOP_K) // MOE_TM + N_EXPERTS
    slots = n_tiles * MOE_TM
    tok = jnp.arange(n * TOP_K, dtype=jnp.int32) // TOP_K
    slot_tok = jnp.full((slots,), n, jnp.int32).at[slot].set(tok)
    slot_gate = jnp.zeros((slots,), jnp.float32).at[slot].set(gate.reshape(-1))
    tile_start = jnp.arange(n_tiles, dtype=jnp.int32) * MOE_TM
    tile_expert = jnp.minimum(jnp.sum(pad_end[None, :] <= tile_start[:, None], axis=1),
                              N_EXPERTS - 1).astype(jnp.int32)
    tile_valid = (tile_start < pad_end[-1]).astype(jnp.int32)
    h_pad = jnp.concatenate([hf.astype(jnp.bfloat16), jnp.zeros((1, d), jnp.bfloat16)], axis=0)
    xb = h_pad[slot_tok]
    yb = moe_experts(xb, slot_gate[:, None], tile_expert, tile_valid, w_gate, w_up, w_down)
    slot2 = slot.reshape(n, TOP_K)
    out = yb[slot2[:, 0]] + yb[slot2[:, 1]]
    return out.reshape(b, t, d)


def _even_layer(x, xc, sc, scc, mod_w, mod_b, w_in, mla_q_norm, mla_wq_up, mla_kv_norm, mla_wkv_up,
                rwkv_conv, rwkv_w0_f, rwkv_w2_f, rwkv_w0_b, rwkv_w2_b, rwkv_a0_f, rwkv_a2_f, rwkv_a0_b,
                rwkv_a2_b, rwkv_g2, rwkv_k_k, rwkv_k_a, rwkv_r_k, rwkv_lnx_g, rwkv_lnx_b, w_out,
                ln1_g, ln1_b, ffn_w_gate, ffn_w_up, ffn_w_down, ln2_g, ln2_b):
    m = [u[:, None, :] for u in _adaln(sc, mod_w, mod_b)]
    mc = _adaln(scc, mod_w, mod_b)
    z = mm(_modulate(x, m[0], m[1]), w_in)
    zc = mm(_modulate(xc, mc[0], mc[1]), w_in)
    o_mla, oc_mla = _mla_mixer(z[..., :MLA_IN], zc[..., :MLA_IN],
                               mla_q_norm, mla_wq_up, mla_kv_norm, mla_wkv_up)
    o_rwkv, oc_rwkv = _rwkv7_mixer(z[..., MLA_IN:], zc[..., MLA_IN:], rwkv_conv, rwkv_w0_f, rwkv_w2_f,
                                   rwkv_w0_b, rwkv_w2_b, rwkv_a0_f, rwkv_a2_f, rwkv_a0_b, rwkv_a2_b,
                                   rwkv_g2, rwkv_k_k, rwkv_k_a, rwkv_r_k, rwkv_lnx_g, rwkv_lnx_b)
    o = mm(jnp.concatenate([o_mla, o_rwkv], axis=-1), w_out)
    oc = mm(jnp.concatenate([oc_mla, oc_rwkv], axis=-1), w_out)
    x = _post_norm(x, m[2] * o, ln1_g, ln1_b)
    xc = _post_norm(xc, mc[2] * oc, ln1_g, ln1_b)
    x = _post_norm(x, m[5] * _swiglu(_modulate(x, m[3], m[4]), ffn_w_gate, ffn_w_up, ffn_w_down), ln2_g, ln2_b)
    xc = _post_norm(xc, mc[5] * _swiglu(_modulate(xc, mc[3], mc[4]), ffn_w_gate, ffn_w_up, ffn_w_down),
                    ln2_g, ln2_b)
    return x, xc


def _odd_layer(x, xc, sc, scc, mod_w, mod_b, w_in, ret_decay_f, ret_decay_b, w_out, ln1_g, ln1_b,
               router, moe_w_gate, moe_w_up, moe_w_down, ln2_g, ln2_b):
    m = [u[:, None, :] for u in _adaln(sc, mod_w, mod_b)]
    c_shift, c_scale = jnp.split(
        jnp.dot(scc, mod_w[:, :2 * D_MODEL], precision=lax.Precision.HIGHEST) + mod_b[:2 * D_MODEL], 2)
    z = mm(_modulate(x, m[0], m[1]), w_in)
    zc = mm(_modulate(xc, c_shift, c_scale), w_in[:, RET_QK:2 * RET_QK + RET_VD])
    o = mm(retention_mixer(z, zc, ret_decay_f, ret_decay_b), w_out)
    x = _post_norm(x, m[2] * o, ln1_g, ln1_b)
    y = moe_swiglu(_modulate(x, m[3], m[4]), router, moe_w_gate.astype(jnp.bfloat16),
                   moe_w_up.astype(jnp.bfloat16), moe_w_down.astype(jnp.bfloat16))
    return _post_norm(x, m[5] * y, ln2_g, ln2_b)


def kernel(x, c, ctx, c_ctx, l0_mod_w, l0_mod_b, l0_w_in, l0_mla_q_norm, l0_mla_wq_up, l0_mla_kv_norm, l0_mla_wkv_up, l0_rwkv_conv, l0_rwkv_w0_f, l0_rwkv_w2_f, l0_rwkv_w0_b, l0_rwkv_w2_b, l0_rwkv_a0_f, l0_rwkv_a2_f, l0_rwkv_a0_b, l0_rwkv_a2_b, l0_rwkv_g2, l0_rwkv_k_k, l0_rwkv_k_a, l0_rwkv_r_k, l0_rwkv_lnx_g, l0_rwkv_lnx_b, l0_w_out, l0_ln1_g, l0_ln1_b, l0_ffn_w_gate, l0_ffn_w_up, l0_ffn_w_down, l0_ln2_g, l0_ln2_b, l1_mod_w, l1_mod_b, l1_w_in, l1_ret_decay_f, l1_ret_decay_b, l1_w_out, l1_ln1_g, l1_ln1_b, l1_router, l1_moe_w_gate, l1_moe_w_up, l1_moe_w_down, l1_ln2_g, l1_ln2_b):
    even_params = (l0_mod_w, l0_mod_b, l0_w_in, l0_mla_q_norm, l0_mla_wq_up, l0_mla_kv_norm, l0_mla_wkv_up,
                   l0_rwkv_conv, l0_rwkv_w0_f, l0_rwkv_w2_f, l0_rwkv_w0_b, l0_rwkv_w2_b, l0_rwkv_a0_f,
                   l0_rwkv_a2_f, l0_rwkv_a0_b, l0_rwkv_a2_b, l0_rwkv_g2, l0_rwkv_k_k, l0_rwkv_k_a, l0_rwkv_r_k,
                   l0_rwkv_lnx_g, l0_rwkv_lnx_b, l0_w_out, l0_ln1_g, l0_ln1_b, l0_ffn_w_gate, l0_ffn_w_up,
                   l0_ffn_w_down, l0_ln2_g, l0_ln2_b)
    odd_params = (l1_mod_w, l1_mod_b, l1_w_in, l1_ret_decay_f, l1_ret_decay_b, l1_w_out, l1_ln1_g, l1_ln1_b,
                  l1_router, l1_moe_w_gate, l1_moe_w_up, l1_moe_w_down, l1_ln2_g, l1_ln2_b)
    sc = _silu(c)
    scc = _silu(c_ctx)
    x, xc = _even_layer(x, ctx, sc, scc, *even_params)
    return _odd_layer(x, xc, sc, scc, *odd_params)
```

```python
import functools

import jax
import jax.numpy as jnp
import numpy as np
from jax import lax
from jax.experimental import pallas as pl
from jax.experimental.pallas import tpu as pltpu

D_MODEL = 1024
DEPTH = 2
GRID_W = 64
ROPE_BASE = 10000.0
DEEPNORM_ALPHA = (2 * DEPTH) ** 0.25
LN_EPS = 1e-5

MLA_HEADS = 8
MLA_Q_RANK = 256
MLA_KV_RANK = 128
MLA_NOPE = 64
MLA_ROPE = 32
MLA_V = 64
MLA_IN = MLA_Q_RANK + MLA_KV_RANK + MLA_ROPE
MLA_OUT = MLA_HEADS * MLA_V

RWKV_HEADS = 8
RWKV_HEAD = 64
RWKV_DIM = RWKV_HEADS * RWKV_HEAD
RWKV_DECAY_LORA = 64
RWKV_AAA_LORA = 64
RWKV_GATE_LORA = 128
RWKV_LNX_EPS = 64e-5

RET_HEADS = 4
RET_KEY = 256
RET_VAL = 512
RET_TC = 256
RET_QK = RET_HEADS * RET_KEY
RET_VD = RET_HEADS * RET_VAL

N_EXPERTS = 8
TOP_K = 2
MOE_TM = 512
MOE_TF = 512

LANES = 128
VMEM_LIMIT_BYTES = 48 * 1024 * 1024


def _matmul_kernel(x_ref, w_ref, o_ref):
    o_ref[...] = jnp.dot(x_ref[...].astype(jnp.bfloat16), w_ref[...], preferred_element_type=jnp.float32)


def _pick_tile(n, candidates):
    for c in candidates:
        if n % c == 0:
            return c
    raise ValueError(f"no tile for {n}")


def pmatmul(x, w):
    m, k = x.shape
    n = w.shape[1]
    n_pad = (-n) % LANES
    w = w.astype(jnp.bfloat16)
    if n_pad:
        w = jnp.pad(w, ((0, 0), (0, n_pad)))
    np_ = n + n_pad
    tm = _pick_tile(m, (1024, 512, 256, 128, 8) if k <= 1024 else (512, 256, 128, 8))
    tn = _pick_tile(np_, (1024, 768, 640, 512, 384, 256, 128))
    out = pl.pallas_call(
        _matmul_kernel,
        grid=(m // tm, np_ // tn),
        in_specs=[pl.BlockSpec((tm, k), lambda i, j: (i, 0)),
                  pl.BlockSpec((k, tn), lambda i, j: (0, j))],
        out_specs=pl.BlockSpec((tm, tn), lambda i, j: (i, j)),
        out_shape=jax.ShapeDtypeStruct((m, np_), jnp.float32),
        compiler_params=pltpu.CompilerParams(
            dimension_semantics=("parallel", "parallel"),
            vmem_limit_bytes=VMEM_LIMIT_BYTES),
        name="matmul",
    )(x, w)
    return out[:, :n] if n_pad else out


def mm(x, w):
    lead = x.shape[:-1]
    out = pmatmul(x.reshape(-1, x.shape[-1]), w)
    return out.reshape(*lead, w.shape[1])


def _silu(t):
    return t * jax.nn.sigmoid(t)


def _normalize(t, eps):
    mu = jnp.mean(t, axis=-1, keepdims=True)
    var = jnp.mean(jnp.square(t - mu), axis=-1, keepdims=True)
    return (t - mu) * lax.rsqrt(var + eps)


def _layer_norm(t, g, b):
    return _normalize(t, LN_EPS) * g + b


def _rms_norm(t, g, eps=1e-6):
    return t * lax.rsqrt(jnp.mean(t * t, axis=-1, keepdims=True) + eps) * g


def _l2_normalize(t, eps=1e-12):
    return t / jnp.maximum(jnp.linalg.norm(t, axis=-1, keepdims=True), eps)


def _post_norm(x, update, g, b):
    return _layer_norm(DEEPNORM_ALPHA * x + update, g, b)


def _modulate(h, shift, scale):
    return h * (1.0 + scale) + shift


def _adaln(cond, mod_w, mod_b):
    return jnp.split(jnp.dot(cond, mod_w, precision=lax.Precision.HIGHEST) + mod_b, 6, axis=-1)


def _swiglu(h, w_gate, w_up, w_down):
    return mm(_silu(mm(h, w_gate)) * mm(h, w_up), w_down)


def _centred_conv3(t, w):
    tp = jnp.pad(t, ((0, 0), (1, 1), (0, 0)))
    return tp[:, :-2] * w[0] + tp[:, 1:-1] * w[1] + tp[:, 2:] * w[2]


def _grid_positions(n_tokens):
    rows = n_tokens // GRID_W
    row = jnp.repeat(jnp.arange(rows, dtype=jnp.float32), GRID_W)
    col = jnp.tile(jnp.arange(GRID_W, dtype=jnp.float32), rows)
    return row, col


def _rotate(t, pos):
    nf = t.shape[-1] // 2
    inv_freq = ROPE_BASE ** (-jnp.arange(nf, dtype=jnp.float32) / nf)
    ang = pos[:, None] * inv_freq[None, :]
    cos = jnp.cos(ang)[None, :, None, :]
    sin = jnp.sin(ang)[None, :, None, :]
    t1, t2 = t[..., :nf], t[..., nf:]
    return jnp.concatenate([t1 * cos - t2 * sin, t1 * sin + t2 * cos], axis=-1)


def _axial_rope(t, row, col):
    half = t.shape[-1] // 2
    return jnp.concatenate([_rotate(t[..., :half], row), _rotate(t[..., half:], col)], axis=-1)


SOFTMAX_FLOOR = -1e30


FLASH_ROW_GROUPS = 4


def _flash_kernel(q_ref, kt_ref, v_ref, o_ref, m_ref, acc_ref, *, dv):
    j = pl.program_id(3)

    @pl.when(j == 0)
    def _():
        m_ref[...] = jnp.full_like(m_ref, SOFTMAX_FLOOR)
        acc_ref[...] = jnp.zeros_like(acc_ref)

    rows = q_ref.shape[2] // FLASH_ROW_GROUPS
    kt = kt_ref[0, 0]
    v = v_ref[0, 0]
    for u in range(FLASH_ROW_GROUPS):
        sl = slice(u * rows, (u + 1) * rows)
        s = jnp.dot(q_ref[0, 0, sl, :], kt, preferred_element_type=jnp.float32)
        m_prev = m_ref[sl, :]
        m_new = jnp.maximum(m_prev, jnp.max(s, axis=-1, keepdims=True))
        alpha = jnp.exp(m_prev - m_new)
        p = jnp.exp(s - m_new).astype(jnp.bfloat16)
        acc_ref[sl, :] = alpha * acc_ref[sl, :] + jnp.dot(p, v, preferred_element_type=jnp.float32)
        m_ref[sl, :] = m_new

    @pl.when(j == pl.num_programs(3) - 1)
    def _():
        acc = acc_ref[...]
        o_ref[0, 0] = acc[:, :dv] / acc[:, dv:dv + 1]


def flash_attention(q, kt, v1, dv, tq, tk):
    b, h, t, dq = q.shape
    s = kt.shape[3]
    return pl.pallas_call(
        functools.partial(_flash_kernel, dv=dv),
        grid=(b, h, t // tq, s // tk),
        in_specs=[pl.BlockSpec((1, 1, tq, dq), lambda bi, hi, i, j: (bi, hi, i, 0)),
                  pl.BlockSpec((1, 1, dq, tk), lambda bi, hi, i, j: (bi, hi, 0, j)),
                  pl.BlockSpec((1, 1, tk, LANES), lambda bi, hi, i, j: (bi, hi, j, 0))],
        out_specs=pl.BlockSpec((1, 1, tq, dv), lambda bi, hi, i, j: (bi, hi, i, 0)),
        out_shape=jax.ShapeDtypeStruct((b, h, t, dv), jnp.float32),
        scratch_shapes=[pltpu.VMEM((tq, 1), jnp.float32), pltpu.VMEM((tq, LANES), jnp.float32)],
        compiler_params=pltpu.CompilerParams(
            dimension_semantics=("parallel", "parallel", "parallel", "arbitrary"),
            vmem_limit_bytes=VMEM_LIMIT_BYTES),
        name="flash_attention",
    )(q, kt, v1)


def _block_attention(q, k, v):
    dq = q.shape[-1]
    dv = v.shape[-1]
    b, s, h, _ = k.shape
    qh = jnp.transpose(q * dq ** -0.5, (0, 2, 1, 3)).astype(jnp.bfloat16)
    kt = jnp.transpose(k, (0, 2, 3, 1)).astype(jnp.bfloat16)
    v1 = jnp.concatenate([v, jnp.ones((b, s, h, 1), v.dtype), jnp.zeros((b, s, h, LANES - dv - 1), v.dtype)],
                         axis=-1)
    v1 = jnp.transpose(v1, (0, 2, 1, 3)).astype(jnp.bfloat16)
    tq = _pick_tile(q.shape[1], (1024, 256))
    o = flash_attention(qh, kt, v1, dv, tq, s)
    return jnp.transpose(o, (0, 2, 1, 3))


def _mla_project(zz, q_norm, wq_up, kv_norm, wkv_up):
    b, t, _ = zz.shape
    cq, ckv, k_rope = jnp.split(zz, [MLA_Q_RANK, MLA_Q_RANK + MLA_KV_RANK], axis=-1)
    q = mm(_rms_norm(cq, q_norm), wq_up).reshape(b, t, MLA_HEADS, MLA_NOPE + MLA_ROPE)
    kv = mm(_rms_norm(ckv, kv_norm), wkv_up).reshape(b, t, MLA_HEADS, MLA_NOPE + MLA_V)
    return q, kv[..., :MLA_NOPE], k_rope[:, :, None, :], kv[..., MLA_NOPE:]


def _mla_keys(k_nope, k_rope):
    b, t, h, _ = k_nope.shape
    return jnp.concatenate([k_nope, jnp.broadcast_to(k_rope, (b, t, h, MLA_ROPE))], axis=-1)


def _mla_mixer(z, zc, q_norm, wq_up, kv_norm, wkv_up):
    b, t, _ = z.shape
    row, col = _grid_positions(t)
    q, k_nope, k_rope, v = _mla_project(z, q_norm, wq_up, kv_norm, wkv_up)
    qc, k_nope_c, k_rope_c, vc = _mla_project(zc, q_norm, wq_up, kv_norm, wkv_up)
    q = jnp.concatenate([q[..., :MLA_NOPE], _axial_rope(q[..., MLA_NOPE:], row, col)], axis=-1)
    k = _mla_keys(k_nope, _axial_rope(k_rope, row, col))
    kc = _mla_keys(k_nope_c, k_rope_c)
    o = _block_attention(q, jnp.concatenate([kc, k], axis=1), jnp.concatenate([vc, v], axis=1))
    oc = _block_attention(qc, kc, vc)
    return o.reshape(b, t, MLA_OUT), oc.reshape(b, zc.shape[1], MLA_OUT)


SCAN_BLOCK = 128
SCAN_UNROLL = 8
N_PAIRS = RWKV_HEADS // 2
PAIR_ROWS = N_PAIRS * RWKV_HEAD


def _scan_kernel(rf_ref, rb_ref, af_ref, ab_ref, vtf_ref, vtb_ref, wf_ref, wb_ref, kf_ref, kb_ref,
                 bf_ref, bb_ref, wred_ref, yf_ref, yb_ref, s_ref):
    nb = rf_ref.shape[0]

    @pl.when(pl.program_id(0) == 0)
    def _():
        s_ref[...] = jnp.zeros_like(s_ref)

    lane = lax.broadcasted_iota(jnp.int32, (PAIR_ROWS, LANES), 1)
    wred = wred_ref[...]
    refs = ((rf_ref, af_ref, vtf_ref, wf_ref, kf_ref, bf_ref, yf_ref),
            (rb_ref, ab_ref, vtb_ref, wb_ref, kb_ref, bb_ref, yb_ref))

    def rows(ref, bi, t):
        return jnp.concatenate(
            [jnp.broadcast_to(ref[bi, p, pl.ds(t, 1), :], (RWKV_HEAD, LANES)) for p in range(N_PAIRS)], axis=0)

    def collect(y_ref, bi, t, yb, valid):
        mask = jnp.logical_and(lane % RWKV_HEAD == t % RWKV_HEAD, valid)
        half = t // RWKV_HEAD
        y_ref[bi, 0, half] = jnp.where(mask, yb, y_ref[bi, 0, half])

    def body(i, carry):
        for d in range(2):
            r_ref, a_ref, vt_ref, w_ref, k_ref, b_ref, y_ref = refs[d]
            t = i if d == 0 else SCAN_BLOCK - 1 - i
            tp = jnp.maximum(i - 1, 0) if d == 0 else jnp.minimum(SCAN_BLOCK - i, SCAN_BLOCK - 1)
            sel = lane % RWKV_HEAD == t % RWKV_HEAD
            prs = []
            for bi in range(nb):
                s = s_ref[d, bi]
                pa = (s * rows(a_ref, bi, t)).astype(jnp.bfloat16)
                prs.append((s * rows(r_ref, bi, tp)).astype(jnp.bfloat16))
                pv = jnp.where(sel, vt_ref[bi, 0, t // RWKV_HEAD], jnp.zeros((), jnp.bfloat16))
                red = jnp.dot(jnp.concatenate([pa, pv], axis=1), wred, preferred_element_type=jnp.float32)
                s_ref[d, bi] = (s * rows(w_ref, bi, t) + red[:, :LANES] * rows(b_ref, bi, t)
                                + red[:, LANES:] * rows(k_ref, bi, t))
            for b0 in range(0, nb, 2):
                ys = jnp.dot(jnp.concatenate(prs[b0:b0 + 2], axis=1), wred, preferred_element_type=jnp.float32)
                collect(y_ref, b0, tp, ys[:, :LANES], i >= 1)
                collect(y_ref, b0 + 1, tp, ys[:, LANES:], i >= 1)
        return carry

    lax.fori_loop(0, SCAN_BLOCK, body, 0, unroll=SCAN_UNROLL)

    for d in range(2):
        r_ref, y_ref = refs[d][0], refs[d][6]
        t_last = SCAN_BLOCK - 1 if d == 0 else 0
        for b0 in range(0, nb, 2):
            prs = [(s_ref[d, bi] * rows(r_ref, bi, t_last)).astype(jnp.bfloat16) for bi in (b0, b0 + 1)]
            ys = jnp.dot(jnp.concatenate(prs, axis=1), wred, preferred_element_type=jnp.float32)
            collect(y_ref, b0, t_last, ys[:, :LANES], True)
            collect(y_ref, b0 + 1, t_last, ys[:, LANES:], True)


def rwkv_scan(r, v, a, w_f, k_f, b_f, w_b, k_b, b_b, n_ctx):
    nb, length, _ = r.shape
    nblk = length // SCAN_BLOCK
    nblk_ctx = n_ctx // SCAN_BLOCK
    vt = v.reshape(nb, nblk, 2, RWKV_HEAD, N_PAIRS, 2, RWKV_HEAD)
    vt = jnp.transpose(vt, (0, 1, 2, 4, 6, 5, 3)).reshape(nb, nblk, 2, PAIR_ROWS, LANES).astype(jnp.bfloat16)
    j = np.arange(2 * LANES)
    wred = (j[:, None] // RWKV_HEAD) == (j[None, :] // RWKV_HEAD)

    def fwd(i):
        return i

    def bwd(i):
        return jnp.where(i < nblk_ctx, nblk_ctx - 1 - i, nblk + nblk_ctx - 1 - i)

    def row_spec(blk):
        return pl.BlockSpec((nb, N_PAIRS, SCAN_BLOCK, LANES), lambda i: (0, 0, blk(i), 0))

    def pairs(u):
        return jnp.transpose(u.reshape(nb, length, N_PAIRS, LANES), (0, 2, 1, 3))

    r, a, w_f, k_f, b_f, w_b, k_b, b_b = (pairs(u) for u in (r, a, w_f, k_f, b_f, w_b, k_b, b_b))

    def vt_spec(blk):
        return pl.BlockSpec((nb, 1, 2, PAIR_ROWS, LANES), lambda i: (0, blk(i), 0, 0, 0))

    def y_spec(blk):
        return pl.BlockSpec((nb, 1, 2, PAIR_ROWS, LANES), lambda i: (0, blk(i), 0, 0, 0))

    y_shape = jax.ShapeDtypeStruct((nb, nblk, 2, PAIR_ROWS, LANES), jnp.float32)
    ys = pl.pallas_call(
        _scan_kernel,
        grid=(nblk,),
        in_specs=[row_spec(fwd), row_spec(bwd), row_spec(fwd), row_spec(bwd), vt_spec(fwd), vt_spec(bwd),
                  row_spec(fwd), row_spec(bwd), row_spec(fwd), row_spec(bwd), row_spec(fwd), row_spec(bwd),
                  pl.BlockSpec((2 * LANES, 2 * LANES), lambda i: (0, 0))],
        out_specs=[y_spec(fwd), y_spec(bwd)],
        out_shape=[y_shape, y_shape],
        scratch_shapes=[pltpu.VMEM((2, nb, PAIR_ROWS, LANES), jnp.float32)],
        compiler_params=pltpu.CompilerParams(dimension_semantics=("arbitrary",),
                                             vmem_limit_bytes=VMEM_LIMIT_BYTES),
        name="rwkv_scan",
    )(r, r, a, a, vt, vt, w_f, w_b, k_f, k_b, b_f, b_b, jnp.asarray(wred, jnp.bfloat16))

    def untile(y):
        y = y.reshape(nb, nblk, 2, N_PAIRS, RWKV_HEAD, 2, RWKV_HEAD)
        return jnp.transpose(y, (0, 1, 2, 6, 3, 5, 4)).reshape(nb, length, RWKV_DIM)

    return untile(ys[0]), untile(ys[1])


def _rwkv7_features(z, conv_w, w0_f, w2_f, w0_b, w2_b, a0_f, a2_f, a0_b, a2_b, g2, k_k, k_a):
    b, t, _ = z.shape
    c0 = 3 * RWKV_DIM
    c1 = c0 + RWKV_DECAY_LORA
    c2 = c1 + RWKV_DECAY_LORA
    c3 = c2 + RWKV_AAA_LORA
    c4 = c3 + RWKV_AAA_LORA
    rkv_raw, wd_f, wd_b, ad_f, ad_b, gd = jnp.split(z, [c0, c1, c2, c3, c4], axis=-1)
    rkv = _centred_conv3(rkv_raw, conv_w)
    r, k, v = jnp.split(rkv, 3, axis=-1)
    g = mm(jax.nn.sigmoid(gd), g2)

    def heads(u):
        return u.reshape(b, t, RWKV_HEADS, RWKV_HEAD)

    kk = _l2_normalize(heads(k * k_k))

    def direction(wd, w0, w2, ad, a0, a2):
        logw = -jax.nn.softplus(-(w0 + mm(jnp.tanh(wd), w2))) - 0.5
        decay = jnp.exp(-jnp.exp(logw))
        lr = jax.nn.sigmoid(a0 + mm(ad, a2))
        kd = k * (1.0 + (lr - 1.0) * k_a)
        return heads(decay), heads(kd), heads(lr)

    fwd = direction(wd_f, w0_f, w2_f, ad_f, a0_f, a2_f)
    bwd = direction(wd_b, w0_b, w2_b, ad_b, a0_b, a2_b)
    return heads(r), heads(v), kk, g, fwd, bwd


def _rwkv7_readout(y_f, y_b, r, k_f, k_b, v, g, r_k, lnx_g, lnx_b):
    b, t = r.shape[:2]
    y = _normalize(y_f + y_b, RWKV_LNX_EPS).reshape(b, t, RWKV_DIM) * lnx_g + lnx_b
    bonus = jnp.sum(r * (k_f + k_b) * r_k, axis=-1, keepdims=True) * v
    return (y + bonus.reshape(b, t, RWKV_DIM)) * g


def _rwkv7_mixer(z, zc, conv_w, w0_f, w2_f, w0_b, w2_b, a0_f, a2_f, a0_b, a2_b, g2, k_k, k_a,
                 r_k, lnx_g, lnx_b):
    p = (conv_w, w0_f, w2_f, w0_b, w2_b, a0_f, a2_f, a0_b, a2_b, g2, k_k, k_a)
    r, v, kk, g, (w_f, k_f, l_f), (w_b, k_b, l_b) = _rwkv7_features(z, *p)
    rc, vc, kkc, gc, (wc_f, kc_f, lc_f), (wc_b, kc_b, lc_b) = _rwkv7_features(zc, *p)
    nb, t = r.shape[:2]
    tc = rc.shape[1]

    def cat(uc, u):
        return jnp.concatenate([uc.reshape(nb, tc, RWKV_DIM), u.reshape(nb, t, RWKV_DIM)], axis=1)

    ys_f, ys_b = rwkv_scan(cat(rc, r), cat(vc, v), cat(-kkc, -kk),
                           cat(wc_f, w_f), cat(kc_f, k_f), cat(kkc * lc_f, kk * l_f),
                           cat(wc_b, w_b), cat(kc_b, k_b), cat(kkc * lc_b, kk * l_b), tc)
    ys_f = ys_f.reshape(nb, tc + t, RWKV_HEADS, RWKV_HEAD)
    ys_b = ys_b.reshape(nb, tc + t, RWKV_HEADS, RWKV_HEAD)
    yc_f, y_f = ys_f[:, :tc], ys_f[:, tc:]
    yc_b, y_b = ys_b[:, :tc], ys_b[:, tc:]
    o = _rwkv7_readout(y_f, y_b, r, k_f, k_b, v, g, r_k, lnx_g, lnx_b)
    oc = _rwkv7_readout(yc_f, yc_b, rc, kc_f, kc_b, vc, gc, r_k, lnx_g, lnx_b)
    return o, oc


def _rope_halves(t, cos, sin):
    parts = []
    for s in range(2):
        u = t[:, s * LANES:(s + 1) * LANES]
        parts.append(u * cos[:, s * LANES:(s + 1) * LANES]
                     + pltpu.roll(u, LANES // 2, axis=1) * sin[:, s * LANES:(s + 1) * LANES])
    return jnp.concatenate(parts, axis=1)


def _retention_kernel(*refs, has_prev):
    if has_prev:
        (q_ref, k_ref, v_ref, g_ref, kc_ref, vc_ref, cos_ref, sin_ref, dmat_ref, qdec_ref, kdec_ref,
         kcdec_ref, cdec_ref, prev_ref, o_ref, s_ref) = refs
    else:
        (q_ref, k_ref, v_ref, g_ref, kc_ref, vc_ref, cos_ref, sin_ref, dmat_ref, qdec_ref, kdec_ref,
         kcdec_ref, cdec_ref, o_ref, s_ref) = refs
        prev_ref = None
    scale = RET_KEY ** -0.5

    @pl.when(pl.program_id(2) == 0)
    def _():
        kc = (kc_ref[0] * kcdec_ref[0] * scale).astype(jnp.bfloat16)
        s_ref[...] = lax.dot_general(kc, vc_ref[0].astype(jnp.bfloat16), (((0,), (0,)), ((), ())),
                                     preferred_element_type=jnp.float32)

    cos = cos_ref[...]
    sin = sin_ref[...]
    q = _rope_halves(q_ref[0], cos, sin)
    k = _rope_halves(k_ref[0], cos, sin) * scale
    v = v_ref[0].astype(jnp.bfloat16)
    s = s_ref[...]
    att = lax.dot_general(q.astype(jnp.bfloat16), k.astype(jnp.bfloat16), (((1,), (1,)), ((), ())),
                          preferred_element_type=jnp.float32) * dmat_ref[0]
    o = (jnp.dot(att.astype(jnp.bfloat16), v, preferred_element_type=jnp.float32)
         + jnp.dot((q * qdec_ref[0]).astype(jnp.bfloat16), s.astype(jnp.bfloat16),
                   preferred_element_type=jnp.float32))
    s_ref[...] = s * cdec_ref[0] + lax.dot_general((k * kdec_ref[0]).astype(jnp.bfloat16), v,
                                                   (((0,), (0,)), ((), ())),
                                                   preferred_element_type=jnp.float32)
    mu = jnp.mean(o, axis=-1, keepdims=True)
    oc = o - mu
    var = jnp.mean(oc * oc, axis=-1, keepdims=True)
    out = _silu(g_ref[0]) * (oc * lax.rsqrt(var + 1e-6))
    if prev_ref is not None:
        out = out + prev_ref[0]
    o_ref[0] = out


def _retention_tables(gamma, reverse, n_ctx):
    log_g = jnp.log(gamma)[:, None, None]
    i = jnp.arange(RET_TC, dtype=jnp.float32)
    rel = (i[None, :] - i[:, None]) if reverse else (i[:, None] - i[None, :])
    dmat = jnp.where(rel >= 0, jnp.exp(jnp.maximum(rel, 0.0)[None] * log_g), 0.0)
    q_pow = (RET_TC - i) if reverse else (i + 1.0)
    k_pow = i if reverse else (RET_TC - 1.0 - i)
    m = jnp.arange(n_ctx, dtype=jnp.float32)
    c_pow = m if reverse else (n_ctx - 1.0 - m)
    bc = lambda p: jnp.broadcast_to(jnp.exp(p[None, :, None] * log_g), (RET_HEADS, p.shape[0], RET_KEY))
    cdec = jnp.broadcast_to(jnp.exp(RET_TC * log_g), (RET_HEADS, 1, RET_VAL))
    return dmat, bc(q_pow), bc(k_pow), bc(c_pow), cdec


def _rope_tables(n_tokens):
    pos_row = (jnp.arange(n_tokens) // GRID_W).astype(jnp.float32)
    pos_col = (jnp.arange(n_tokens) % GRID_W).astype(jnp.float32)
    nf = RET_KEY // 4
    inv_freq = ROPE_BASE ** (-jnp.arange(nf, dtype=jnp.float32) / nf)
    cos, sin = [], []
    for pos in (pos_row, pos_col):
        ang = pos[:, None] * inv_freq[None, :]
        cos += [jnp.cos(ang), jnp.cos(ang)]
        sin += [-jnp.sin(ang), jnp.sin(ang)]
    return jnp.concatenate(cos, axis=1), jnp.concatenate(sin, axis=1)


def _retention_direction(z, zc, cos, sin, gamma, reverse, prev):
    b, t, _ = z.shape
    n_ctx = zc.shape[1]
    nc = t // RET_TC
    dmat, qdec, kdec, kcdec, cdec = _retention_tables(gamma, reverse, n_ctx)
    ch = (lambda c: nc - 1 - c) if reverse else (lambda c: c)
    kq, kv = RET_QK // RET_KEY, (2 * RET_QK) // RET_VAL
    g_off = (2 * RET_QK + (2 if reverse else 1) * RET_VD) // RET_VAL
    in_specs = [
        pl.BlockSpec((1, RET_TC, RET_KEY), lambda bi, h, c: (bi, ch(c), h)),
        pl.BlockSpec((1, RET_TC, RET_KEY), lambda bi, h, c: (bi, ch(c), kq + h)),
        pl.BlockSpec((1, RET_TC, RET_VAL), lambda bi, h, c: (bi, ch(c), kv + h)),
        pl.BlockSpec((1, RET_TC, RET_VAL), lambda bi, h, c: (bi, ch(c), g_off + h)),
        pl.BlockSpec((1, n_ctx, RET_KEY), lambda bi, h, c: (bi, 0, h)),
        pl.BlockSpec((1, n_ctx, RET_VAL), lambda bi, h, c: (bi, 0, RET_QK // RET_VAL + h)),
        pl.BlockSpec((RET_TC, RET_KEY), lambda bi, h, c: (ch(c), 0)),
        pl.BlockSpec((RET_TC, RET_KEY), lambda bi, h, c: (ch(c), 0)),
        pl.BlockSpec((1, RET_TC, RET_TC), lambda bi, h, c: (h, 0, 0)),
        pl.BlockSpec((1, RET_TC, RET_KEY), lambda bi, h, c: (h, 0, 0)),
        pl.BlockSpec((1, RET_TC, RET_KEY), lambda bi, h, c: (h, 0, 0)),
        pl.BlockSpec((1, n_ctx, RET_KEY), lambda bi, h, c: (h, 0, 0)),
        pl.BlockSpec((1, 1, RET_VAL), lambda bi, h, c: (h, 0, 0)),
    ]
    args = [z, z, z, z, zc, zc, cos, sin, dmat, qdec, kdec, kcdec, cdec]
    if prev is not None:
        in_specs.append(pl.BlockSpec((1, RET_TC, RET_VAL), lambda bi, h, c: (bi, ch(c), h)))
        args.append(prev)
    return pl.pallas_call(
        functools.partial(_retention_kernel, has_prev=prev is not None),
        grid=(b, RET_HEADS, nc),
        in_specs=in_specs,
        out_specs=pl.BlockSpec((1, RET_TC, RET_VAL), lambda bi, h, c: (bi, ch(c), h)),
        out_shape=jax.ShapeDtypeStruct((b, t, RET_VD), jnp.float32),
        scratch_shapes=[pltpu.VMEM((RET_KEY, RET_VAL), jnp.float32)],
        compiler_params=pltpu.CompilerParams(dimension_semantics=("parallel", "parallel", "arbitrary"),
                                             vmem_limit_bytes=VMEM_LIMIT_BYTES),
        name="retention_bwd" if reverse else "retention_fwd",
    )(*args)


def retention_mixer(z, zc, decay_f, decay_b):
    cos, sin = _rope_tables(z.shape[1])
    gamma_f = 1.0 - jnp.exp2(-decay_f)
    gamma_b = 1.0 - jnp.exp2(-decay_b)
    part = _retention_direction(z, zc, cos, sin, gamma_b, True, None)
    return _retention_direction(z, zc, cos, sin, gamma_f, False, part)


def _moe_kernel(te_ref, tv_ref, x_ref, gate_ref, wg_ref, wu_ref, wd_ref, o_ref, acc_ref):
    i = pl.program_id(0)
    j = pl.program_id(1)

    @pl.when(tv_ref[i] > 0)
    def _():
        @pl.when(j == 0)
        def _():
            acc_ref[...] = jnp.zeros_like(acc_ref)
        x = x_ref[...]
        g = jnp.dot(x, wg_ref[0], preferred_element_type=jnp.float32)
        u = jnp.dot(x, wu_ref[0], preferred_element_type=jnp.float32)
        a = (_silu(g) * u).astype(jnp.bfloat16)
        acc_ref[...] += jnp.dot(a, wd_ref[0], preferred_element_type=jnp.float32)

    last = j == pl.num_programs(1) - 1

    @pl.when(jnp.logical_and(last, tv_ref[i] > 0))
    def _():
        o_ref[...] = acc_ref[...] * gate_ref[...]

    @pl.when(jnp.logical_and(last, tv_ref[i] == 0))
    def _():
        o_ref[...] = jnp.zeros_like(o_ref)


def moe_experts(xb, slot_gate, tile_expert, tile_valid, w_gate, w_up, w_down):
    slots, d = xb.shape
    ff = w_gate.shape[2]
    n_tiles = slots // MOE_TM
    grid_spec = pltpu.PrefetchScalarGridSpec(
        num_scalar_prefetch=2,
        grid=(n_tiles, ff // MOE_TF),
        in_specs=[
            pl.BlockSpec((MOE_TM, d), lambda i, j, te, tv: (i, 0)),
            pl.BlockSpec((MOE_TM, 1), lambda i, j, te, tv: (i, 0)),
            pl.BlockSpec((1, d, MOE_TF), lambda i, j, te, tv: (te[i], 0, j)),
            pl.BlockSpec((1, d, MOE_TF), lambda i, j, te, tv: (te[i], 0, j)),
            pl.BlockSpec((1, MOE_TF, d), lambda i, j, te, tv: (te[i], j, 0)),
        ],
        out_specs=pl.BlockSpec((MOE_TM, d), lambda i, j, te, tv: (i, 0)),
        scratch_shapes=[pltpu.VMEM((MOE_TM, d), jnp.float32)],
    )
    return pl.pallas_call(
        _moe_kernel,
        grid_spec=grid_spec,
        out_shape=jax.ShapeDtypeStruct((slots, d), jnp.float32),
        compiler_params=pltpu.CompilerParams(dimension_semantics=("arbitrary", "arbitrary"),
                                             vmem_limit_bytes=VMEM_LIMIT_BYTES),
        name="moe_experts",
    )(tile_expert, tile_valid, xb, slot_gate, w_gate, w_up, w_down)


def moe_swiglu(h, router, w_gate, w_up, w_down):
    b, t, d = h.shape
    n = b * t
    hf = h.reshape(n, d)
    logits = jnp.dot(hf, router, precision=lax.Precision.HIGHEST)
    top_val, top_idx = lax.top_k(logits, TOP_K)
    gate = jax.nn.softmax(top_val, axis=-1)
    flat_e = top_idx.reshape(-1)
    onehot = (flat_e[:, None] == jnp.arange(N_EXPERTS, dtype=flat_e.dtype)[None, :]).astype(jnp.int32)
    csum = jnp.cumsum(onehot, axis=0)
    counts = csum[-1]
    rank = jnp.sum((csum - onehot) * onehot, axis=1)
    padded = (counts + MOE_TM - 1) // MOE_TM * MOE_TM
    pad_end = jnp.cumsum(padded)
    pad_start = pad_end - padded
    slot = (pad_start[flat_e] + rank).astype(jnp.int32)
    n_tiles = (n * TOP_K) // MOE_TM + N_EXPERTS
    slots = n_tiles * MOE_TM
    tok = jnp.arange(n * TOP_K, dtype=jnp.int32) // TOP_K
    slot_tok = jnp.full((slots,), n, jnp.int32).at[slot].set(tok)
    slot_gate = jnp.zeros((slots,), jnp.float32).at[slot].set(gate.reshape(-1))
    tile_start = jnp.arange(n_tiles, dtype=jnp.int32) * MOE_TM
    tile_expert = jnp.minimum(jnp.sum(pad_end[None, :] <= tile_start[:, None], axis=1),
                              N_EXPERTS - 1).astype(jnp.int32)
    tile_valid = (tile_start < pad_end[-1]).astype(jnp.int32)
    h_pad = jnp.concatenate([hf.astype(jnp.bfloat16), jnp.zeros((1, d), jnp.bfloat16)], axis=0)
    xb = h_pad[slot_tok]
    yb = moe_experts(xb, slot_gate[:, None], tile_expert, tile_valid, w_gate, w_up, w_down)
    slot2 = slot.reshape(n, TOP_K)
    out = yb[slot2[:, 0]] + yb[slot2[:, 1]]
    return out.reshape(b, t, d)


def _even_layer(x, xc, sc, scc, mod_w, mod_b, w_in, mla_q_norm, mla_wq_up, mla_kv_norm, mla_wkv_up,
                rwkv_conv, rwkv_w0_f, rwkv_w2_f, rwkv_w0_b, rwkv_w2_b, rwkv_a0_f, rwkv_a2_f, rwkv_a0_b,
                rwkv_a2_b, rwkv_g2, rwkv_k_k, rwkv_k_a, rwkv_r_k, rwkv_lnx_g, rwkv_lnx_b, w_out,
                ln1_g, ln1_b, ffn_w_gate, ffn_w_up, ffn_w_down, ln2_g, ln2_b):
    m = [u[:, None, :] for u in _adaln(sc, mod_w, mod_b)]
    mc = _adaln(scc, mod_w, mod_b)
    z = mm(_modulate(x, m[0], m[1]), w_in)
    zc = mm(_modulate(xc, mc[0], mc[1]), w_in)
    o_mla, oc_mla = _mla_mixer(z[..., :MLA_IN], zc[..., :MLA_IN],
                               mla_q_norm, mla_wq_up, mla_kv_norm, mla_wkv_up)
    o_rwkv, oc_rwkv = _rwkv7_mixer(z[..., MLA_IN:], zc[..., MLA_IN:], rwkv_conv, rwkv_w0_f, rwkv_w2_f,
                                   rwkv_w0_b, rwkv_w2_b, rwkv_a0_f, rwkv_a2_f, rwkv_a0_b, rwkv_a2_b,
                                   rwkv_g2, rwkv_k_k, rwkv_k_a, rwkv_r_k, rwkv_lnx_g, rwkv_lnx_b)
    o = mm(jnp.concatenate([o_mla, o_rwkv], axis=-1), w_out)
    oc = mm(jnp.concatenate([oc_mla, oc_rwkv], axis=-1), w_out)
    x = _post_norm(x, m[2] * o, ln1_g, ln1_b)
    xc = _post_norm(xc, mc[2] * oc, ln1_g, ln1_b)
    x = _post_norm(x, m[5] * _swiglu(_modulate(x, m[3], m[4]), ffn_w_gate, ffn_w_up, ffn_w_down), ln2_g, ln2_b)
    xc = _post_norm(xc, mc[5] * _swiglu(_modulate(xc, mc[3], mc[4]), ffn_w_gate, ffn_w_up, ffn_w_down),
                    ln2_g, ln2_b)
    return x, xc


def _odd_layer(x, xc, sc, scc, mod_w, mod_b, w_in, ret_decay_f, ret_decay_b, w_out, ln1_g, ln1_b,
               router, moe_w_gate, moe_w_up, moe_w_down, ln2_g, ln2_b):
    m = [u[:, None, :] for u in _adaln(sc, mod_w, mod_b)]
    c_shift, c_scale = jnp.split(
        jnp.dot(scc, mod_w[:, :2 * D_MODEL], precision=lax.Precision.HIGHEST) + mod_b[:2 * D_MODEL], 2)
    z = mm(_modulate(x, m[0], m[1]), w_in)
    zc = mm(_modulate(xc, c_shift, c_scale), w_in[:, RET_QK:2 * RET_QK + RET_VD])
    o = mm(retention_mixer(z, zc, ret_decay_f, ret_decay_b), w_out)
    x = _post_norm(x, m[2] * o, ln1_g, ln1_b)
    y = moe_swiglu(_modulate(x, m[3], m[4]), router, moe_w_gate.astype(jnp.bfloat16),
                   moe_w_up.astype(jnp.bfloat16), moe_w_down.astype(jnp.bfloat16))
    return _post_norm(x, m[5] * y, ln2_g, ln2_b)


def kernel(x, c, ctx, c_ctx, l0_mod_w, l0_mod_b, l0_w_in, l0_mla_q_norm, l0_mla_wq_up, l0_mla_kv_norm, l0_mla_wkv_up, l0_rwkv_conv, l0_rwkv_w0_f, l0_rwkv_w2_f, l0_rwkv_w0_b, l0_rwkv_w2_b, l0_rwkv_a0_f, l0_rwkv_a2_f, l0_rwkv_a0_b, l0_rwkv_a2_b, l0_rwkv_g2, l0_rwkv_k_k, l0_rwkv_k_a, l0_rwkv_r_k, l0_rwkv_lnx_g, l0_rwkv_lnx_b, l0_w_out, l0_ln1_g, l0_ln1_b, l0_ffn_w_gate, l0_ffn_w_up, l0_ffn_w_down, l0_ln2_g, l0_ln2_b, l1_mod_w, l1_mod_b, l1_w_in, l1_ret_decay_f, l1_ret_decay_b, l1_w_out, l1_ln1_g, l1_ln1_b, l1_router, l1_moe_w_gate, l1_moe_w_up, l1_moe_w_down, l1_ln2_g, l1_ln2_b):
    even_params = (l0_mod_w, l0_mod_b, l0_w_in, l0_mla_q_norm, l0_mla_wq_up, l0_mla_kv_norm, l0_mla_wkv_up,
                   l0_rwkv_conv, l0_rwkv_w0_f, l0_rwkv_w2_f, l0_rwkv_w0_b, l0_rwkv_w2_b, l0_rwkv_a0_f,
                   l0_rwkv_a2_f, l0_rwkv_a0_b, l0_rwkv_a2_b, l0_rwkv_g2, l0_rwkv_k_k, l0_rwkv_k_a, l0_rwkv_r_k,
                   l0_rwkv_lnx_g, l0_rwkv_lnx_b, l0_w_out, l0_ln1_g, l0_ln1_b, l0_ffn_w_gate, l0_ffn_w_up,
                   l0_ffn_w_down, l0_ln2_g, l0_ln2_b)
    odd_params = (l1_mod_w, l1_mod_b, l1_w_in, l1_ret_decay_f, l1_ret_decay_b, l1_w_out, l1_ln1_g, l1_ln1_b,
                  l1_router, l1_moe_w_gate, l1_moe_w_up, l1_moe_w_down, l1_ln2_g, l1_ln2_b)
    sc = _silu(c)
    scc = _silu(c_ctx)
    x, xc = _even_layer(x, ctx, sc, scc, *even_params)
    return _odd_layer(x, xc, sc, scc, *odd_params)
```

```python
import functools

import jax
import jax.numpy as jnp
import numpy as np
from jax import lax
from jax.experimental import pallas as pl
from jax.experimental.pallas import tpu as pltpu

D_MODEL = 1024
DEPTH = 2
GRID_W = 64
ROPE_BASE = 10000.0
DEEPNORM_ALPHA = (2 * DEPTH) ** 0.25
LN_EPS = 1e-5

MLA_HEADS = 8
MLA_Q_RANK = 256
MLA_KV_RANK = 128
MLA_NOPE = 64
MLA_ROPE = 32
MLA_V = 64
MLA_IN = MLA_Q_RANK + MLA_KV_RANK + MLA_ROPE
MLA_OUT = MLA_HEADS * MLA_V

RWKV_HEADS = 8
RWKV_HEAD = 64
RWKV_DIM = RWKV_HEADS * RWKV_HEAD
RWKV_DECAY_LORA = 64
RWKV_AAA_LORA = 64
RWKV_GATE_LORA = 128
RWKV_LNX_EPS = 64e-5

RET_HEADS = 4
RET_KEY = 256
RET_VAL = 512
RET_TC = 256
RET_QK = RET_HEADS * RET_KEY
RET_VD = RET_HEADS * RET_VAL

N_EXPERTS = 8
TOP_K = 2
MOE_TM = 512
MOE_TF = 512

LANES = 128
VMEM_LIMIT_BYTES = 48 * 1024 * 1024


def _matmul_kernel(x_ref, w_ref, o_ref):
    o_ref[...] = jnp.dot(x_ref[...].astype(jnp.bfloat16), w_ref[...], preferred_element_type=jnp.float32)


def _pick_tile(n, candidates):
    for c in candidates:
        if n % c == 0:
            return c
    raise ValueError(f"no tile for {n}")


def pmatmul(x, w):
    m, k = x.shape
    n = w.shape[1]
    n_pad = (-n) % LANES
    w = w.astype(jnp.bfloat16)
    if n_pad:
        w = jnp.pad(w, ((0, 0), (0, n_pad)))
    np_ = n + n_pad
    tm = _pick_tile(m, (1024, 512, 256, 128, 8) if k <= 1024 else (512, 256, 128, 8))
    tn = _pick_tile(np_, (1024, 768, 640, 512, 384, 256, 128))
    out = pl.pallas_call(
        _matmul_kernel,
        grid=(m // tm, np_ // tn),
        in_specs=[pl.BlockSpec((tm, k), lambda i, j: (i, 0)),
                  pl.BlockSpec((k, tn), lambda i, j: (0, j))],
        out_specs=pl.BlockSpec((tm, tn), lambda i, j: (i, j)),
        out_shape=jax.ShapeDtypeStruct((m, np_), jnp.float32),
        compiler_params=pltpu.CompilerParams(
            dimension_semantics=("parallel", "parallel"),
            vmem_limit_bytes=VMEM_LIMIT_BYTES),
        name="matmul",
    )(x, w)
    return out[:, :n] if n_pad else out


def mm(x, w):
    lead = x.shape[:-1]
    out = pmatmul(x.reshape(-1, x.shape[-1]).astype(jnp.bfloat16), w)
    return out.reshape(*lead, w.shape[1])


def _silu(t):
    return t * jax.nn.sigmoid(t)


def _normalize(t, eps):
    mu = jnp.mean(t, axis=-1, keepdims=True)
    var = jnp.mean(jnp.square(t - mu), axis=-1, keepdims=True)
    return (t - mu) * lax.rsqrt(var + eps)


def _layer_norm(t, g, b):
    return _normalize(t, LN_EPS) * g + b


def _rms_norm(t, g, eps=1e-6):
    return t * lax.rsqrt(jnp.mean(t * t, axis=-1, keepdims=True) + eps) * g


def _l2_normalize(t, eps=1e-12):
    return t / jnp.maximum(jnp.linalg.norm(t, axis=-1, keepdims=True), eps)


def _post_norm(x, update, g, b):
    return _layer_norm(DEEPNORM_ALPHA * x + update, g, b)


def _modulate(h, shift, scale):
    return h * (1.0 + scale) + shift


def _adaln(cond, mod_w, mod_b):
    return jnp.split(jnp.dot(cond, mod_w, precision=lax.Precision.HIGHEST) + mod_b, 6, axis=-1)


def _grid_positions(n_tokens):
    rows = n_tokens // GRID_W
    row = jnp.repeat(jnp.arange(rows, dtype=jnp.float32), GRID_W)
    col = jnp.tile(jnp.arange(GRID_W, dtype=jnp.float32), rows)
    return row, col


def _rotate(t, pos):
    nf = t.shape[-1] // 2
    inv_freq = ROPE_BASE ** (-jnp.arange(nf, dtype=jnp.float32) / nf)
    ang = pos[:, None] * inv_freq[None, :]
    cos = jnp.cos(ang)[None, :, None, :]
    sin = jnp.sin(ang)[None, :, None, :]
    t1, t2 = t[..., :nf], t[..., nf:]
    return jnp.concatenate([t1 * cos - t2 * sin, t1 * sin + t2 * cos], axis=-1)


def _axial_rope(t, row, col):
    half = t.shape[-1] // 2
    return jnp.concatenate([_rotate(t[..., :half], row), _rotate(t[..., half:], col)], axis=-1)


SOFTMAX_FLOOR = -1e30


FLASH_ROW_GROUPS = 4


def _flash_kernel(q_ref, kt_ref, v_ref, o_ref, m_ref, acc_ref, *, dv):
    j = pl.program_id(3)

    @pl.when(j == 0)
    def _():
        m_ref[...] = jnp.full_like(m_ref, SOFTMAX_FLOOR)
        acc_ref[...] = jnp.zeros_like(acc_ref)

    rows = q_ref.shape[2] // FLASH_ROW_GROUPS
    kt = kt_ref[0, 0]
    v = v_ref[0, 0]
    for u in range(FLASH_ROW_GROUPS):
        sl = slice(u * rows, (u + 1) * rows)
        s = jnp.dot(q_ref[0, 0, sl, :], kt, preferred_element_type=jnp.float32)
        m_prev = m_ref[sl, :]
        m_new = jnp.maximum(m_prev, jnp.max(s, axis=-1, keepdims=True))
        alpha = jnp.exp(m_prev - m_new)
        p = jnp.exp(s - m_new).astype(jnp.bfloat16)
        acc_ref[sl, :] = alpha * acc_ref[sl, :] + jnp.dot(p, v, preferred_element_type=jnp.float32)
        m_ref[sl, :] = m_new

    @pl.when(j == pl.num_programs(3) - 1)
    def _():
        acc = acc_ref[...]
        o_ref[0, 0] = acc[:, :dv] / acc[:, dv:dv + 1]


def flash_attention(q, kt, v1, dv, tq, tk):
    b, h, t, dq = q.shape
    s = kt.shape[3]
    return pl.pallas_call(
        functools.partial(_flash_kernel, dv=dv),
        grid=(b, h, t // tq, s // tk),
        in_specs=[pl.BlockSpec((1, 1, tq, dq), lambda bi, hi, i, j: (bi, hi, i, 0)),
                  pl.BlockSpec((1, 1, dq, tk), lambda bi, hi, i, j: (bi, hi, 0, j)),
                  pl.BlockSpec((1, 1, tk, LANES), lambda bi, hi, i, j: (bi, hi, j, 0))],
        out_specs=pl.BlockSpec((1, 1, tq, dv), lambda bi, hi, i, j: (bi, hi, i, 0)),
        out_shape=jax.ShapeDtypeStruct((b, h, t, dv), jnp.float32),
        scratch_shapes=[pltpu.VMEM((tq, 1), jnp.float32), pltpu.VMEM((tq, LANES), jnp.float32)],
        compiler_params=pltpu.CompilerParams(
            dimension_semantics=("parallel", "parallel", "parallel", "arbitrary"),
            vmem_limit_bytes=VMEM_LIMIT_BYTES),
        name="flash_attention",
    )(q, kt, v1)


def _block_attention(q, k, v):
    dq = q.shape[-1]
    dv = v.shape[-1]
    b, s, h, _ = k.shape
    qh = jnp.transpose(q * dq ** -0.5, (0, 2, 1, 3)).astype(jnp.bfloat16)
    kt = jnp.transpose(k, (0, 2, 3, 1)).astype(jnp.bfloat16)
    v1 = jnp.concatenate([v, jnp.ones((b, s, h, 1), v.dtype), jnp.zeros((b, s, h, LANES - dv - 1), v.dtype)],
                         axis=-1)
    v1 = jnp.transpose(v1, (0, 2, 1, 3)).astype(jnp.bfloat16)
    tq = _pick_tile(q.shape[1], (1024, 256))
    o = flash_attention(qh, kt, v1, dv, tq, s)
    return jnp.transpose(o, (0, 2, 1, 3))


def _mla_mixer(z, n_ctx, q_norm, wq_up, kv_norm, wkv_up):
    b, length, _ = z.shape
    t = length - n_ctx
    row, col = _grid_positions(t)
    zero = jnp.zeros((n_ctx,), jnp.float32)
    row, col = jnp.concatenate([zero, row]), jnp.concatenate([zero, col])
    cq, ckv, k_rope = jnp.split(z, [MLA_Q_RANK, MLA_Q_RANK + MLA_KV_RANK], axis=-1)
    q = mm(_rms_norm(cq, q_norm), wq_up).reshape(b, length, MLA_HEADS, MLA_NOPE + MLA_ROPE)
    kv = mm(_rms_norm(ckv, kv_norm), wkv_up).reshape(b, length, MLA_HEADS, MLA_NOPE + MLA_V)
    q = jnp.concatenate([q[..., :MLA_NOPE], _axial_rope(q[..., MLA_NOPE:], row, col)], axis=-1)
    k_rope = _axial_rope(k_rope[:, :, None, :], row, col)
    k = jnp.concatenate([kv[..., :MLA_NOPE], jnp.broadcast_to(k_rope, (b, length, MLA_HEADS, MLA_ROPE))], axis=-1)
    v = kv[..., MLA_NOPE:]
    o = _block_attention(q[:, n_ctx:], k, v)
    oc = _block_attention(q[:, :n_ctx], k[:, :n_ctx], v[:, :n_ctx])
    return jnp.concatenate([oc, o], axis=1).reshape(b, length, MLA_OUT)


SCAN_BLOCK = 128
SCAN_UNROLL = 8
N_PAIRS = RWKV_HEADS // 2
PAIR_ROWS = N_PAIRS * RWKV_HEAD


def _scan_kernel(rf_ref, rb_ref, af_ref, ab_ref, vtf_ref, vtb_ref, wf_ref, wb_ref, kf_ref, kb_ref,
                 bf_ref, bb_ref, wred_ref, yf_ref, yb_ref, s_ref):
    nb = rf_ref.shape[0]

    @pl.when(pl.program_id(0) == 0)
    def _():
        s_ref[...] = jnp.zeros_like(s_ref)

    lane = lax.broadcasted_iota(jnp.int32, (PAIR_ROWS, LANES), 1)
    wred = wred_ref[...]
    refs = ((rf_ref, af_ref, vtf_ref, wf_ref, kf_ref, bf_ref, yf_ref),
            (rb_ref, ab_ref, vtb_ref, wb_ref, kb_ref, bb_ref, yb_ref))

    def rows(ref, bi, t):
        return jnp.concatenate(
            [jnp.broadcast_to(ref[bi, pl.ds(t, 1), p, :], (RWKV_HEAD, LANES)) for p in range(N_PAIRS)], axis=0)

    def collect(y_ref, bi, t, yb, valid):
        mask = jnp.logical_and(lane % RWKV_HEAD == t % RWKV_HEAD, valid)
        half = t // RWKV_HEAD
        y_ref[bi, 0, half] = jnp.where(mask, yb, y_ref[bi, 0, half])

    def body(i, carry):
        for d in range(2):
            r_ref, a_ref, vt_ref, w_ref, k_ref, b_ref, y_ref = refs[d]
            t = i if d == 0 else SCAN_BLOCK - 1 - i
            tp = jnp.maximum(i - 1, 0) if d == 0 else jnp.minimum(SCAN_BLOCK - i, SCAN_BLOCK - 1)
            sel = lane % RWKV_HEAD == t % RWKV_HEAD
            prs = []
            for bi in range(nb):
                s = s_ref[d, bi]
                pa = (s * rows(a_ref, bi, t)).astype(jnp.bfloat16)
                prs.append((s * rows(r_ref, bi, tp)).astype(jnp.bfloat16))
                pv = jnp.where(sel, vt_ref[bi, 0, t // RWKV_HEAD], jnp.zeros((), jnp.bfloat16))
                red = jnp.dot(jnp.concatenate([pa, pv], axis=1), wred, preferred_element_type=jnp.float32)
                s_ref[d, bi] = (s * rows(w_ref, bi, t) + red[:, :LANES] * rows(b_ref, bi, t)
                                + red[:, LANES:] * rows(k_ref, bi, t))
            for b0 in range(0, nb, 2):
                ys = jnp.dot(jnp.concatenate(prs[b0:b0 + 2], axis=1), wred, preferred_element_type=jnp.float32)
                collect(y_ref, b0, tp, ys[:, :LANES], i >= 1)
                collect(y_ref, b0 + 1, tp, ys[:, LANES:], i >= 1)
        return carry

    lax.fori_loop(0, SCAN_BLOCK, body, 0, unroll=SCAN_UNROLL)

    for d in range(2):
        r_ref, y_ref = refs[d][0], refs[d][6]
        t_last = SCAN_BLOCK - 1 if d == 0 else 0
        for b0 in range(0, nb, 2):
            prs = [(s_ref[d, bi] * rows(r_ref, bi, t_last)).astype(jnp.bfloat16) for bi in (b0, b0 + 1)]
            ys = jnp.dot(jnp.concatenate(prs, axis=1), wred, preferred_element_type=jnp.float32)
            collect(y_ref, b0, t_last, ys[:, :LANES], True)
            collect(y_ref, b0 + 1, t_last, ys[:, LANES:], True)


def rwkv_scan(r, v, a, w_f, k_f, b_f, w_b, k_b, b_b, n_ctx):
    nb, length, _ = r.shape
    nblk = length // SCAN_BLOCK
    nblk_ctx = n_ctx // SCAN_BLOCK
    vt = v.reshape(nb, nblk, 2, RWKV_HEAD, N_PAIRS, 2, RWKV_HEAD)
    vt = jnp.transpose(vt, (0, 1, 2, 4, 6, 5, 3)).reshape(nb, nblk, 2, PAIR_ROWS, LANES).astype(jnp.bfloat16)
    j = np.arange(2 * LANES)
    wred = (j[:, None] // RWKV_HEAD) == (j[None, :] // RWKV_HEAD)

    def fwd(i):
        return i

    def bwd(i):
        return jnp.where(i < nblk_ctx, nblk_ctx - 1 - i, nblk + nblk_ctx - 1 - i)

    def row_spec(blk):
        return pl.BlockSpec((nb, SCAN_BLOCK, N_PAIRS, LANES), lambda i: (0, blk(i), 0, 0))

    def pairs(u):
        return u.reshape(nb, length, N_PAIRS, LANES)

    r, a, w_f, k_f, b_f, w_b, k_b, b_b = (pairs(u) for u in (r, a, w_f, k_f, b_f, w_b, k_b, b_b))

    def vt_spec(blk):
        return pl.BlockSpec((nb, 1, 2, PAIR_ROWS, LANES), lambda i: (0, blk(i), 0, 0, 0))

    def y_spec(blk):
        return pl.BlockSpec((nb, 1, 2, PAIR_ROWS, LANES), lambda i: (0, blk(i), 0, 0, 0))

    y_shape = jax.ShapeDtypeStruct((nb, nblk, 2, PAIR_ROWS, LANES), jnp.float32)
    ys = pl.pallas_call(
        _scan_kernel,
        grid=(nblk,),
        in_specs=[row_spec(fwd), row_spec(bwd), row_spec(fwd), row_spec(bwd), vt_spec(fwd), vt_spec(bwd),
                  row_spec(fwd), row_spec(bwd), row_spec(fwd), row_spec(bwd), row_spec(fwd), row_spec(bwd),
                  pl.BlockSpec((2 * LANES, 2 * LANES), lambda i: (0, 0))],
        out_specs=[y_spec(fwd), y_spec(bwd)],
        out_shape=[y_shape, y_shape],
        scratch_shapes=[pltpu.VMEM((2, nb, PAIR_ROWS, LANES), jnp.float32)],
        compiler_params=pltpu.CompilerParams(dimension_semantics=("arbitrary",),
                                             vmem_limit_bytes=VMEM_LIMIT_BYTES),
        name="rwkv_scan",
    )(r, r, a, a, vt, vt, w_f, w_b, k_f, k_b, b_f, b_b, jnp.asarray(wred, jnp.bfloat16))

    def untile(y):
        y = y.reshape(nb, nblk, 2, N_PAIRS, RWKV_HEAD, 2, RWKV_HEAD)
        return jnp.transpose(y, (0, 1, 2, 6, 3, 5, 4)).reshape(nb, length, RWKV_DIM)

    return untile(ys[0]), untile(ys[1])


def _rwkv7_mixer(z, n_ctx, conv_w, w0_f, w2_f, w0_b, w2_b, a0_f, a2_f, a0_b, a2_b, g2, k_k, k_a,
                 r_k, lnx_g, lnx_b):
    b, length, _ = z.shape
    c0 = 3 * RWKV_DIM
    c1 = c0 + RWKV_DECAY_LORA
    c2 = c1 + RWKV_DECAY_LORA
    c3 = c2 + RWKV_AAA_LORA
    c4 = c3 + RWKV_AAA_LORA
    rkv_raw, wd_f, wd_b, ad_f, ad_b, gd = jnp.split(z, [c0, c1, c2, c3, c4], axis=-1)
    pos = jnp.arange(length)
    has_prev = ((pos != 0) & (pos != n_ctx)).astype(jnp.float32)[None, :, None]
    has_next = ((pos != n_ctx - 1) & (pos != length - 1)).astype(jnp.float32)[None, :, None]
    tp = jnp.pad(rkv_raw, ((0, 0), (1, 1), (0, 0)))
    rkv = tp[:, :-2] * has_prev * conv_w[0] + tp[:, 1:-1] * conv_w[1] + tp[:, 2:] * has_next * conv_w[2]
    r, k, v = jnp.split(rkv, 3, axis=-1)
    g = mm(jax.nn.sigmoid(gd), g2)

    def heads(u):
        return u.reshape(b, length, RWKV_HEADS, RWKV_HEAD)

    kk = _l2_normalize(heads(k * k_k)).reshape(b, length, RWKV_DIM)

    def direction(wd, w0, w2, ad, a0, a2):
        logw = -jax.nn.softplus(-(w0 + mm(jnp.tanh(wd), w2))) - 0.5
        decay = jnp.exp(-jnp.exp(logw))
        lr = jax.nn.sigmoid(a0 + mm(ad, a2))
        kd = k * (1.0 + (lr - 1.0) * k_a)
        return decay, kd, kk * lr

    w_f, k_f, b_f = direction(wd_f, w0_f, w2_f, ad_f, a0_f, a2_f)
    w_b, k_b, b_b = direction(wd_b, w0_b, w2_b, ad_b, a0_b, a2_b)
    y_f, y_b = rwkv_scan(r, v, -kk, w_f, k_f, b_f, w_b, k_b, b_b, n_ctx)
    y = _normalize(heads(y_f + y_b), RWKV_LNX_EPS).reshape(b, length, RWKV_DIM) * lnx_g + lnx_b
    bonus = jnp.sum(heads(r * (k_f + k_b)) * r_k, axis=-1, keepdims=True) * heads(v)
    return (y + bonus.reshape(b, length, RWKV_DIM)) * g


def _rope_halves(t, cos, sin):
    parts = []
    for s in range(2):
        u = t[:, s * LANES:(s + 1) * LANES]
        parts.append(u * cos[:, s * LANES:(s + 1) * LANES]
                     + pltpu.roll(u, LANES // 2, axis=1) * sin[:, s * LANES:(s + 1) * LANES])
    return jnp.concatenate(parts, axis=1)


def _retention_kernel(*refs, has_prev):
    if has_prev:
        (q_ref, k_ref, v_ref, g_ref, kc_ref, vc_ref, cos_ref, sin_ref, dmat_ref, qdec_ref, kdec_ref,
         kcdec_ref, cdec_ref, prev_ref, o_ref, s_ref) = refs
    else:
        (q_ref, k_ref, v_ref, g_ref, kc_ref, vc_ref, cos_ref, sin_ref, dmat_ref, qdec_ref, kdec_ref,
         kcdec_ref, cdec_ref, o_ref, s_ref) = refs
        prev_ref = None
    scale = RET_KEY ** -0.5

    @pl.when(pl.program_id(2) == 0)
    def _():
        kc = (kc_ref[0] * kcdec_ref[0] * scale).astype(jnp.bfloat16)
        s_ref[...] = lax.dot_general(kc, vc_ref[0].astype(jnp.bfloat16), (((0,), (0,)), ((), ())),
                                     preferred_element_type=jnp.float32)

    cos = cos_ref[...]
    sin = sin_ref[...]
    q = _rope_halves(q_ref[0], cos, sin)
    k = _rope_halves(k_ref[0], cos, sin) * scale
    v = v_ref[0].astype(jnp.bfloat16)
    s = s_ref[...]
    att = lax.dot_general(q.astype(jnp.bfloat16), k.astype(jnp.bfloat16), (((1,), (1,)), ((), ())),
                          preferred_element_type=jnp.float32) * dmat_ref[0]
    o = (jnp.dot(att.astype(jnp.bfloat16), v, preferred_element_type=jnp.float32)
         + jnp.dot((q * qdec_ref[0]).astype(jnp.bfloat16), s.astype(jnp.bfloat16),
                   preferred_element_type=jnp.float32))
    s_ref[...] = s * cdec_ref[0] + lax.dot_general((k * kdec_ref[0]).astype(jnp.bfloat16), v,
                                                   (((0,), (0,)), ((), ())),
                                                   preferred_element_type=jnp.float32)
    mu = jnp.mean(o, axis=-1, keepdims=True)
    oc = o - mu
    var = jnp.mean(oc * oc, axis=-1, keepdims=True)
    out = _silu(g_ref[0]) * (oc * lax.rsqrt(var + 1e-6))
    if prev_ref is not None:
        out = out + prev_ref[0]
    o_ref[0] = out.astype(o_ref.dtype)


def _retention_tables(gamma, reverse, n_ctx):
    log_g = jnp.log(gamma)[:, None, None]
    i = jnp.arange(RET_TC, dtype=jnp.float32)
    rel = (i[None, :] - i[:, None]) if reverse else (i[:, None] - i[None, :])
    dmat = jnp.where(rel >= 0, jnp.exp(jnp.maximum(rel, 0.0)[None] * log_g), 0.0)
    q_pow = (RET_TC - i) if reverse else (i + 1.0)
    k_pow = i if reverse else (RET_TC - 1.0 - i)
    m = jnp.arange(n_ctx, dtype=jnp.float32)
    c_pow = m if reverse else (n_ctx - 1.0 - m)
    bc = lambda p: jnp.broadcast_to(jnp.exp(p[None, :, None] * log_g), (RET_HEADS, p.shape[0], RET_KEY))
    cdec = jnp.broadcast_to(jnp.exp(RET_TC * log_g), (RET_HEADS, 1, RET_VAL))
    return dmat, bc(q_pow), bc(k_pow), bc(c_pow), cdec


def _rope_tables(n_tokens):
    pos_row = (jnp.arange(n_tokens) // GRID_W).astype(jnp.float32)
    pos_col = (jnp.arange(n_tokens) % GRID_W).astype(jnp.float32)
    nf = RET_KEY // 4
    inv_freq = ROPE_BASE ** (-jnp.arange(nf, dtype=jnp.float32) / nf)
    cos, sin = [], []
    for pos in (pos_row, pos_col):
        ang = pos[:, None] * inv_freq[None, :]
        cos += [jnp.cos(ang), jnp.cos(ang)]
        sin += [-jnp.sin(ang), jnp.sin(ang)]
    return jnp.concatenate(cos, axis=1), jnp.concatenate(sin, axis=1)


def _retention_direction(z, zc, cos, sin, gamma, reverse, prev):
    b, t, _ = z.shape
    n_ctx = zc.shape[1]
    nc = t // RET_TC
    dmat, qdec, kdec, kcdec, cdec = _retention_tables(gamma, reverse, n_ctx)
    ch = (lambda c: nc - 1 - c) if reverse else (lambda c: c)
    kq, kv = RET_QK // RET_KEY, (2 * RET_QK) // RET_VAL
    g_off = (2 * RET_QK + (2 if reverse else 1) * RET_VD) // RET_VAL
    in_specs = [
        pl.BlockSpec((1, RET_TC, RET_KEY), lambda bi, h, c: (bi, ch(c), h)),
        pl.BlockSpec((1, RET_TC, RET_KEY), lambda bi, h, c: (bi, ch(c), kq + h)),
        pl.BlockSpec((1, RET_TC, RET_VAL), lambda bi, h, c: (bi, ch(c), kv + h)),
        pl.BlockSpec((1, RET_TC, RET_VAL), lambda bi, h, c: (bi, ch(c), g_off + h)),
        pl.BlockSpec((1, n_ctx, RET_KEY), lambda bi, h, c: (bi, 0, h)),
        pl.BlockSpec((1, n_ctx, RET_VAL), lambda bi, h, c: (bi, 0, RET_QK // RET_VAL + h)),
        pl.BlockSpec((RET_TC, RET_KEY), lambda bi, h, c: (ch(c), 0)),
        pl.BlockSpec((RET_TC, RET_KEY), lambda bi, h, c: (ch(c), 0)),
        pl.BlockSpec((1, RET_TC, RET_TC), lambda bi, h, c: (h, 0, 0)),
        pl.BlockSpec((1, RET_TC, RET_KEY), lambda bi, h, c: (h, 0, 0)),
        pl.BlockSpec((1, RET_TC, RET_KEY), lambda bi, h, c: (h, 0, 0)),
        pl.BlockSpec((1, n_ctx, RET_KEY), lambda bi, h, c: (h, 0, 0)),
        pl.BlockSpec((1, 1, RET_VAL), lambda bi, h, c: (h, 0, 0)),
    ]
    args = [z, z, z, z, zc, zc, cos, sin, dmat, qdec, kdec, kcdec, cdec]
    if prev is not None:
        in_specs.append(pl.BlockSpec((1, RET_TC, RET_VAL), lambda bi, h, c: (bi, ch(c), h)))
        args.append(prev)
    return pl.pallas_call(
        functools.partial(_retention_kernel, has_prev=prev is not None),
        grid=(b, RET_HEADS, nc),
        in_specs=in_specs,
        out_specs=pl.BlockSpec((1, RET_TC, RET_VAL), lambda bi, h, c: (bi, ch(c), h)),
        out_shape=jax.ShapeDtypeStruct((b, t, RET_VD), jnp.float32 if prev is None else jnp.bfloat16),
        scratch_shapes=[pltpu.VMEM((RET_KEY, RET_VAL), jnp.float32)],
        compiler_params=pltpu.CompilerParams(dimension_semantics=("parallel", "parallel", "arbitrary"),
                                             vmem_limit_bytes=VMEM_LIMIT_BYTES),
        name="retention_bwd" if reverse else "retention_fwd",
    )(*args)


def retention_mixer(z, zc, decay_f, decay_b):
    cos, sin = _rope_tables(z.shape[1])
    gamma_f = 1.0 - jnp.exp2(-decay_f)
    gamma_b = 1.0 - jnp.exp2(-decay_b)
    part = _retention_direction(z, zc, cos, sin, gamma_b, True, None)
    return _retention_direction(z, zc, cos, sin, gamma_f, False, part)


def _moe_kernel(te_ref, tv_ref, x_ref, gate_ref, wg_ref, wu_ref, wd_ref, o_ref, acc_ref):
    i = pl.program_id(0)
    j = pl.program_id(1)

    @pl.when(tv_ref[i] > 0)
    def _():
        @pl.when(j == 0)
        def _():
            acc_ref[...] = jnp.zeros_like(acc_ref)
        x = x_ref[...]
        g = jnp.dot(x, wg_ref[0], preferred_element_type=jnp.float32)
        u = jnp.dot(x, wu_ref[0], preferred_element_type=jnp.float32)
        a = (_silu(g) * u).astype(jnp.bfloat16)
        acc_ref[...] += jnp.dot(a, wd_ref[0], preferred_element_type=jnp.float32)

    last = j == pl.num_programs(1) - 1

    @pl.when(jnp.logical_and(last, tv_ref[i] > 0))
    def _():
        o_ref[...] = acc_ref[...] * gate_ref[...]

    @pl.when(jnp.logical_and(last, tv_ref[i] == 0))
    def _():
        o_ref[...] = jnp.zeros_like(o_ref)


def moe_experts(xb, slot_gate, tile_expert, tile_valid, w_gate, w_up, w_down):
    slots, d = xb.shape
    ff = w_gate.shape[2]
    n_tiles = slots // MOE_TM
    grid_spec = pltpu.PrefetchScalarGridSpec(
        num_scalar_prefetch=2,
        grid=(n_tiles, ff // MOE_TF),
        in_specs=[
            pl.BlockSpec((MOE_TM, d), lambda i, j, te, tv: (i, 0)),
            pl.BlockSpec((MOE_TM, 1), lambda i, j, te, tv: (i, 0)),
            pl.BlockSpec((1, d, MOE_TF), lambda i, j, te, tv: (te[i], 0, j)),
            pl.BlockSpec((1, d, MOE_TF), lambda i, j, te, tv: (te[i], 0, j)),
            pl.BlockSpec((1, MOE_TF, d), lambda i, j, te, tv: (te[i], j, 0)),
        ],
        out_specs=pl.BlockSpec((MOE_TM, d), lambda i, j, te, tv: (i, 0)),
        scratch_shapes=[pltpu.VMEM((MOE_TM, d), jnp.float32)],
    )
    return pl.pallas_call(
        _moe_kernel,
        grid_spec=grid_spec,
        out_shape=jax.ShapeDtypeStruct((slots, d), jnp.float32),
        compiler_params=pltpu.CompilerParams(dimension_semantics=("arbitrary", "arbitrary"),
                                             vmem_limit_bytes=VMEM_LIMIT_BYTES),
        name="moe_experts",
    )(tile_expert, tile_valid, xb, slot_gate, w_gate, w_up, w_down)


def moe_swiglu(h, router, w_gate, w_up, w_down):
    b, t, d = h.shape
    n = b * t
    hf = h.reshape(n, d)
    logits = jnp.dot(hf, router, precision=lax.Precision.HIGHEST)
    top_val, top_idx = lax.top_k(logits, TOP_K)
    gate = jax.nn.softmax(top_val, axis=-1)
    flat_e = top_idx.reshape(-1)
    onehot = (flat_e[:, None] == jnp.arange(N_EXPERTS, dtype=flat_e.dtype)[None, :]).astype(jnp.int32)
    csum = jnp.cumsum(onehot, axis=0)
    counts = csum[-1]
    rank = jnp.sum((csum - onehot) * onehot, axis=1)
    padded = (counts + MOE_TM - 1) // MOE_TM * MOE_TM
    pad_end = jnp.cumsum(padded)
    pad_start = pad_end - padded
    slot = (pad_start[flat_e] + rank).astype(jnp.int32)
    n_tiles = (n * TOP_K) // MOE_TM + N_EXPERTS
    slots = n_tiles * MOE_TM
    tok = jnp.arange(n * TOP_K, dtype=jnp.int32) // TOP_K
    slot_tok = jnp.full((slots,), n, jnp.int32).at[slot].set(tok)
    slot_gate = jnp.zeros((slots,), jnp.float32).at[slot].set(gate.reshape(-1))
    tile_start = jnp.arange(n_tiles, dtype=jnp.int32) * MOE_TM
    tile_expert = jnp.minimum(jnp.sum(pad_end[None, :] <= tile_start[:, None], axis=1),
                              N_EXPERTS - 1).astype(jnp.int32)
    tile_valid = (tile_start < pad_end[-1]).astype(jnp.int32)
    h_pad = jnp.concatenate([hf.astype(jnp.bfloat16), jnp.zeros((1, d), jnp.bfloat16)], axis=0)
    xb = h_pad[slot_tok]
    yb = moe_experts(xb, slot_gate[:, None], tile_expert, tile_valid, w_gate, w_up, w_down)
    slot2 = slot.reshape(n, TOP_K)
    out = yb[slot2[:, 0]] + yb[slot2[:, 1]]
    return out.reshape(b, t, d)


def _row_select(i, tm, n_ctx, ctx_vec, lat_vec):
    if n_ctx == 0:
        return lat_vec
    row = i * tm + lax.broadcasted_iota(jnp.int32, (tm, 1), 0)
    return jnp.where(row < n_ctx, ctx_vec, lat_vec)


def _post_norm_rows(x, update, g, b):
    y = DEEPNORM_ALPHA * x + update
    mu = jnp.mean(y, axis=-1, keepdims=True)
    yc = y - mu
    var = jnp.mean(yc * yc, axis=-1, keepdims=True)
    return yc * lax.rsqrt(var + LN_EPS) * g + b


def _matmul_postnorm_kernel(a_ref, w_ref, x_ref, lat_ref, ctx_ref, g_ref, b_ref, o_ref, *, n_ctx):
    tm = a_ref.shape[1]
    o = jnp.dot(a_ref[0].astype(jnp.bfloat16), w_ref[...], preferred_element_type=jnp.float32)
    gate = _row_select(pl.program_id(1), tm, n_ctx, ctx_ref[...], lat_ref[0])
    o_ref[0] = _post_norm_rows(x_ref[0], gate * o, g_ref[...], b_ref[...])


def matmul_postnorm(a, w, x, gate_lat, gate_ctx, ln_g, ln_b, n_ctx):
    b, length, k = a.shape
    d = w.shape[1]
    tm = _pick_tile(length, (768, 512, 256))
    vec = pl.BlockSpec((1, d), lambda bi, i: (0, 0))
    return pl.pallas_call(
        functools.partial(_matmul_postnorm_kernel, n_ctx=n_ctx),
        grid=(b, length // tm),
        in_specs=[pl.BlockSpec((1, tm, k), lambda bi, i: (bi, i, 0)),
                  pl.BlockSpec((k, d), lambda bi, i: (0, 0)),
                  pl.BlockSpec((1, tm, d), lambda bi, i: (bi, i, 0)),
                  pl.BlockSpec((1, 1, d), lambda bi, i: (bi, 0, 0)), vec, vec, vec],
        out_specs=pl.BlockSpec((1, tm, d), lambda bi, i: (bi, i, 0)),
        out_shape=jax.ShapeDtypeStruct((b, length, d), jnp.float32),
        compiler_params=pltpu.CompilerParams(dimension_semantics=("parallel", "parallel"),
                                             vmem_limit_bytes=VMEM_LIMIT_BYTES),
        name="matmul_postnorm",
    )(a, w.astype(jnp.bfloat16), x, gate_lat[:, None, :], gate_ctx[None, :], ln_g[None, :], ln_b[None, :])


def _ffn_postnorm_kernel(x_ref, lat_ref, ctx_ref, wg_ref, wu_ref, wd_ref, g_ref, b_ref, o_ref, h_ref, acc_ref,
                         *, n_ctx):
    i = pl.program_id(1)
    j = pl.program_id(2)
    tm = x_ref.shape[1]

    def vec(k):
        return _row_select(i, tm, n_ctx, ctx_ref[k:k + 1, :], lat_ref[0, k:k + 1, :])

    @pl.when(j == 0)
    def _():
        h_ref[...] = (x_ref[0] * (1.0 + vec(1)) + vec(0)).astype(jnp.bfloat16)
        acc_ref[...] = jnp.zeros_like(acc_ref)

    h = h_ref[...]
    gt = jnp.dot(h, wg_ref[...], preferred_element_type=jnp.float32)
    up = jnp.dot(h, wu_ref[...], preferred_element_type=jnp.float32)
    acc_ref[...] += jnp.dot((_silu(gt) * up).astype(jnp.bfloat16), wd_ref[...], preferred_element_type=jnp.float32)

    @pl.when(j == pl.num_programs(2) - 1)
    def _():
        o_ref[0] = _post_norm_rows(x_ref[0], vec(2) * acc_ref[...], g_ref[...], b_ref[...])


def ffn_postnorm(x, mod_lat, mod_ctx, w_gate, w_up, w_down, ln_g, ln_b, n_ctx):
    b, length, d = x.shape
    ff = w_gate.shape[1]
    tm = _pick_tile(length, (384, 512, 256))
    tf = _pick_tile(ff, (1408, 512, 256))
    vec = pl.BlockSpec((1, d), lambda bi, i, j: (0, 0))
    return pl.pallas_call(
        functools.partial(_ffn_postnorm_kernel, n_ctx=n_ctx),
        grid=(b, length // tm, ff // tf),
        in_specs=[pl.BlockSpec((1, tm, d), lambda bi, i, j: (bi, i, 0)),
                  pl.BlockSpec((1, 3, d), lambda bi, i, j: (bi, 0, 0)),
                  pl.BlockSpec((3, d), lambda bi, i, j: (0, 0)),
                  pl.BlockSpec((d, tf), lambda bi, i, j: (0, j)),
                  pl.BlockSpec((d, tf), lambda bi, i, j: (0, j)),
                  pl.BlockSpec((tf, d), lambda bi, i, j: (j, 0)), vec, vec],
        out_specs=pl.BlockSpec((1, tm, d), lambda bi, i, j: (bi, i, 0)),
        out_shape=jax.ShapeDtypeStruct((b, length, d), jnp.float32),
        scratch_shapes=[pltpu.VMEM((tm, d), jnp.bfloat16), pltpu.VMEM((tm, d), jnp.float32)],
        compiler_params=pltpu.CompilerParams(dimension_semantics=("parallel", "parallel", "arbitrary"),
                                             vmem_limit_bytes=VMEM_LIMIT_BYTES),
        name="ffn_postnorm",
    )(x, mod_lat, mod_ctx, w_gate.astype(jnp.bfloat16), w_up.astype(jnp.bfloat16), w_down.astype(jnp.bfloat16),
      ln_g[None, :], ln_b[None, :])


def _even_layer(x, xc, sc, scc, mod_w, mod_b, w_in, mla_q_norm, mla_wq_up, mla_kv_norm, mla_wkv_up,
                rwkv_conv, rwkv_w0_f, rwkv_w2_f, rwkv_w0_b, rwkv_w2_b, rwkv_a0_f, rwkv_a2_f, rwkv_a0_b,
                rwkv_a2_b, rwkv_g2, rwkv_k_k, rwkv_k_a, rwkv_r_k, rwkv_lnx_g, rwkv_lnx_b, w_out,
                ln1_g, ln1_b, ffn_w_gate, ffn_w_up, ffn_w_down, ln2_g, ln2_b):
    n_ctx = xc.shape[1]
    m = _adaln(sc, mod_w, mod_b)
    mc = _adaln(scc, mod_w, mod_b)
    xa = jnp.concatenate([xc, x], axis=1)
    is_ctx = (jnp.arange(xa.shape[1]) < n_ctx)[None, :, None]

    def rows(k):
        return jnp.where(is_ctx, mc[k][None, None, :], m[k][:, None, :])

    z = mm(_modulate(xa, rows(0), rows(1)), w_in)
    o_mla = _mla_mixer(z[..., :MLA_IN], n_ctx, mla_q_norm, mla_wq_up, mla_kv_norm, mla_wkv_up)
    o_rwkv = _rwkv7_mixer(z[..., MLA_IN:], n_ctx, rwkv_conv, rwkv_w0_f, rwkv_w2_f, rwkv_w0_b, rwkv_w2_b,
                          rwkv_a0_f, rwkv_a2_f, rwkv_a0_b, rwkv_a2_b, rwkv_g2, rwkv_k_k, rwkv_k_a,
                          rwkv_r_k, rwkv_lnx_g, rwkv_lnx_b)
    mix = jnp.concatenate([o_mla, o_rwkv], axis=-1).astype(jnp.bfloat16)
    xa = matmul_postnorm(mix, w_out, xa, m[2], mc[2], ln1_g, ln1_b, n_ctx)
    xa = ffn_postnorm(xa, jnp.stack(m[3:6], axis=1), jnp.stack(mc[3:6], axis=0),
                      ffn_w_gate, ffn_w_up, ffn_w_down, ln2_g, ln2_b, n_ctx)
    return xa[:, n_ctx:], xa[:, :n_ctx]


def _odd_layer(x, xc, sc, scc, mod_w, mod_b, w_in, ret_decay_f, ret_decay_b, w_out, ln1_g, ln1_b,
               router, moe_w_gate, moe_w_up, moe_w_down, ln2_g, ln2_b):
    m = _adaln(sc, mod_w, mod_b)
    c_shift, c_scale = jnp.split(
        jnp.dot(scc, mod_w[:, :2 * D_MODEL], precision=lax.Precision.HIGHEST) + mod_b[:2 * D_MODEL], 2)
    z = mm(_modulate(x, m[0][:, None, :], m[1][:, None, :]), w_in)
    zc = mm(_modulate(xc, c_shift, c_scale), w_in[:, RET_QK:2 * RET_QK + RET_VD])
    mix = retention_mixer(z, zc, ret_decay_f, ret_decay_b)
    x = matmul_postnorm(mix, w_out, x, m[2], jnp.zeros_like(m[2][0]), ln1_g, ln1_b, 0)
    y = moe_swiglu(_modulate(x, m[3][:, None, :], m[4][:, None, :]), router, moe_w_gate.astype(jnp.bfloat16),
                   moe_w_up.astype(jnp.bfloat16), moe_w_down.astype(jnp.bfloat16))
    return _post_norm(x, m[5][:, None, :] * y, ln2_g, ln2_b)


def kernel(x, c, ctx, c_ctx, l0_mod_w, l0_mod_b, l0_w_in, l0_mla_q_norm, l0_mla_wq_up, l0_mla_kv_norm, l0_mla_wkv_up, l0_rwkv_conv, l0_rwkv_w0_f, l0_rwkv_w2_f, l0_rwkv_w0_b, l0_rwkv_w2_b, l0_rwkv_a0_f, l0_rwkv_a2_f, l0_rwkv_a0_b, l0_rwkv_a2_b, l0_rwkv_g2, l0_rwkv_k_k, l0_rwkv_k_a, l0_rwkv_r_k, l0_rwkv_lnx_g, l0_rwkv_lnx_b, l0_w_out, l0_ln1_g, l0_ln1_b, l0_ffn_w_gate, l0_ffn_w_up, l0_ffn_w_down, l0_ln2_g, l0_ln2_b, l1_mod_w, l1_mod_b, l1_w_in, l1_ret_decay_f, l1_ret_decay_b, l1_w_out, l1_ln1_g, l1_ln1_b, l1_router, l1_moe_w_gate, l1_moe_w_up, l1_moe_w_down, l1_ln2_g, l1_ln2_b):
    even_params = (l0_mod_w, l0_mod_b, l0_w_in, l0_mla_q_norm, l0_mla_wq_up, l0_mla_kv_norm, l0_mla_wkv_up,
                   l0_rwkv_conv, l0_rwkv_w0_f, l0_rwkv_w2_f, l0_rwkv_w0_b, l0_rwkv_w2_b, l0_rwkv_a0_f,
                   l0_rwkv_a2_f, l0_rwkv_a0_b, l0_rwkv_a2_b, l0_rwkv_g2, l0_rwkv_k_k, l0_rwkv_k_a, l0_rwkv_r_k,
                   l0_rwkv_lnx_g, l0_rwkv_lnx_b, l0_w_out, l0_ln1_g, l0_ln1_b, l0_ffn_w_gate, l0_ffn_w_up,
                   l0_ffn_w_down, l0_ln2_g, l0_ln2_b)
    odd_params = (l1_mod_w, l1_mod_b, l1_w_in, l1_ret_decay_f, l1_ret_decay_b, l1_w_out, l1_ln1_g, l1_ln1_b,
                  l1_router, l1_moe_w_gate, l1_moe_w_up, l1_moe_w_down, l1_ln2_g, l1_ln2_b)
    sc = _silu(c)
    scc = _silu(c_ctx)
    x, xc = _even_layer(x, ctx, sc, scc, *even_params)
    return _odd_layer(x, xc, sc, scc, *odd_params)
```

```python
import functools

import jax
import jax.numpy as jnp
import numpy as np
from jax import lax
from jax.experimental import pallas as pl
from jax.experimental.pallas import tpu as pltpu

D_MODEL = 1024
DEPTH = 2
GRID_W = 64
ROPE_BASE = 10000.0
DEEPNORM_ALPHA = (2 * DEPTH) ** 0.25
LN_EPS = 1e-5

MLA_HEADS = 8
MLA_Q_RANK = 256
MLA_KV_RANK = 128
MLA_NOPE = 64
MLA_ROPE = 32
MLA_V = 64
MLA_IN = MLA_Q_RANK + MLA_KV_RANK + MLA_ROPE
MLA_OUT = MLA_HEADS * MLA_V

RWKV_HEADS = 8
RWKV_HEAD = 64
RWKV_DIM = RWKV_HEADS * RWKV_HEAD
RWKV_DECAY_LORA = 64
RWKV_AAA_LORA = 64
RWKV_GATE_LORA = 128
RWKV_LNX_EPS = 64e-5

RET_HEADS = 4
RET_KEY = 256
RET_VAL = 512
RET_TC = 256
RET_QK = RET_HEADS * RET_KEY
RET_VD = RET_HEADS * RET_VAL

N_EXPERTS = 8
TOP_K = 2
MOE_TM = 512
MOE_TF = 512

LANES = 128
VMEM_LIMIT_BYTES = 48 * 1024 * 1024


def _matmul_kernel(x_ref, w_ref, o_ref):
    o_ref[...] = jnp.dot(x_ref[...].astype(jnp.bfloat16), w_ref[...], preferred_element_type=jnp.float32)


def _pick_tile(n, candidates):
    for c in candidates:
        if n % c == 0:
            return c
    raise ValueError(f"no tile for {n}")


def pmatmul(x, w):
    m, k = x.shape
    n = w.shape[1]
    n_pad = (-n) % LANES
    w = w.astype(jnp.bfloat16)
    if n_pad:
        w = jnp.pad(w, ((0, 0), (0, n_pad)))
    np_ = n + n_pad
    tm = _pick_tile(m, (1024, 512, 256, 128, 8) if k <= 1024 else (512, 256, 128, 8))
    tn = _pick_tile(np_, (1024, 768, 640, 512, 384, 256, 128))
    out = pl.pallas_call(
        _matmul_kernel,
        grid=(m // tm, np_ // tn),
        in_specs=[pl.BlockSpec((tm, k), lambda i, j: (i, 0)),
                  pl.BlockSpec((k, tn), lambda i, j: (0, j))],
        out_specs=pl.BlockSpec((tm, tn), lambda i, j: (i, j)),
        out_shape=jax.ShapeDtypeStruct((m, np_), jnp.float32),
        compiler_params=pltpu.CompilerParams(
            dimension_semantics=("parallel", "parallel"),
            vmem_limit_bytes=VMEM_LIMIT_BYTES),
        name="matmul",
    )(x, w)
    return out[:, :n] if n_pad else out


def mm(x, w):
    lead = x.shape[:-1]
    out = pmatmul(x.reshape(-1, x.shape[-1]).astype(jnp.bfloat16), w)
    return out.reshape(*lead, w.shape[1])


def _silu(t):
    return t * jax.nn.sigmoid(t)


def _normalize(t, eps):
    mu = jnp.mean(t, axis=-1, keepdims=True)
    var = jnp.mean(jnp.square(t - mu), axis=-1, keepdims=True)
    return (t - mu) * lax.rsqrt(var + eps)


def _layer_norm(t, g, b):
    return _normalize(t, LN_EPS) * g + b


def _rms_norm(t, g, eps=1e-6):
    return t * lax.rsqrt(jnp.mean(t * t, axis=-1, keepdims=True) + eps) * g


def _l2_normalize(t, eps=1e-12):
    return t / jnp.maximum(jnp.linalg.norm(t, axis=-1, keepdims=True), eps)


def _post_norm(x, update, g, b):
    return _layer_norm(DEEPNORM_ALPHA * x + update, g, b)


def _modulate(h, shift, scale):
    return h * (1.0 + scale) + shift


def _adaln(cond, mod_w, mod_b):
    return jnp.split(jnp.dot(cond, mod_w, precision=lax.Precision.HIGHEST) + mod_b, 6, axis=-1)


def _grid_positions(n_tokens):
    rows = n_tokens // GRID_W
    row = jnp.repeat(jnp.arange(rows, dtype=jnp.float32), GRID_W)
    col = jnp.tile(jnp.arange(GRID_W, dtype=jnp.float32), rows)
    return row, col


def _rotate(t, pos):
    nf = t.shape[-1] // 2
    inv_freq = ROPE_BASE ** (-jnp.arange(nf, dtype=jnp.float32) / nf)
    ang = pos[:, None] * inv_freq[None, :]
    cos = jnp.cos(ang)[None, :, None, :]
    sin = jnp.sin(ang)[None, :, None, :]
    t1, t2 = t[..., :nf], t[..., nf:]
    return jnp.concatenate([t1 * cos - t2 * sin, t1 * sin + t2 * cos], axis=-1)


def _axial_rope(t, row, col):
    half = t.shape[-1] // 2
    return jnp.concatenate([_rotate(t[..., :half], row), _rotate(t[..., half:], col)], axis=-1)


SOFTMAX_FLOOR = -1e30


FLASH_ROW_GROUPS = 4


def _flash_kernel(q_ref, kt_ref, v_ref, o_ref, m_ref, acc_ref, *, dv):
    j = pl.program_id(3)

    @pl.when(j == 0)
    def _():
        m_ref[...] = jnp.full_like(m_ref, SOFTMAX_FLOOR)
        acc_ref[...] = jnp.zeros_like(acc_ref)

    rows = q_ref.shape[2] // FLASH_ROW_GROUPS
    kt = kt_ref[0, 0]
    v = v_ref[0, 0]
    for u in range(FLASH_ROW_GROUPS):
        sl = slice(u * rows, (u + 1) * rows)
        s = jnp.dot(q_ref[0, 0, sl, :], kt, preferred_element_type=jnp.float32)
        m_prev = m_ref[sl, :]
        m_new = jnp.maximum(m_prev, jnp.max(s, axis=-1, keepdims=True))
        alpha = jnp.exp(m_prev - m_new)
        p = jnp.exp(s - m_new).astype(jnp.bfloat16)
        acc_ref[sl, :] = alpha * acc_ref[sl, :] + jnp.dot(p, v, preferred_element_type=jnp.float32)
        m_ref[sl, :] = m_new

    @pl.when(j == pl.num_programs(3) - 1)
    def _():
        acc = acc_ref[...]
        o_ref[0, 0] = acc[:, :dv] / acc[:, dv:dv + 1]


def flash_attention(q, kt, v1, dv, tq, tk):
    b, h, t, dq = q.shape
    s = kt.shape[3]
    return pl.pallas_call(
        functools.partial(_flash_kernel, dv=dv),
        grid=(b, h, t // tq, s // tk),
        in_specs=[pl.BlockSpec((1, 1, tq, dq), lambda bi, hi, i, j: (bi, hi, i, 0)),
                  pl.BlockSpec((1, 1, dq, tk), lambda bi, hi, i, j: (bi, hi, 0, j)),
                  pl.BlockSpec((1, 1, tk, LANES), lambda bi, hi, i, j: (bi, hi, j, 0))],
        out_specs=pl.BlockSpec((1, 1, tq, dv), lambda bi, hi, i, j: (bi, hi, i, 0)),
        out_shape=jax.ShapeDtypeStruct((b, h, t, dv), jnp.float32),
        scratch_shapes=[pltpu.VMEM((tq, 1), jnp.float32), pltpu.VMEM((tq, LANES), jnp.float32)],
        compiler_params=pltpu.CompilerParams(
            dimension_semantics=("parallel", "parallel", "parallel", "arbitrary"),
            vmem_limit_bytes=VMEM_LIMIT_BYTES),
        name="flash_attention",
    )(q, kt, v1)


def _block_attention(q, k, v):
    dq = q.shape[-1]
    dv = v.shape[-1]
    b, s, h, _ = k.shape
    qh = jnp.transpose(q * dq ** -0.5, (0, 2, 1, 3)).astype(jnp.bfloat16)
    kt = jnp.transpose(k, (0, 2, 3, 1)).astype(jnp.bfloat16)
    v1 = jnp.concatenate([v, jnp.ones((b, s, h, 1), v.dtype), jnp.zeros((b, s, h, LANES - dv - 1), v.dtype)],
                         axis=-1)
    v1 = jnp.transpose(v1, (0, 2, 1, 3)).astype(jnp.bfloat16)
    tq = _pick_tile(q.shape[1], (1024, 256))
    o = flash_attention(qh, kt, v1, dv, tq, s)
    return jnp.transpose(o, (0, 2, 1, 3))


def _mla_mixer(z, n_ctx, q_norm, wq_up, kv_norm, wkv_up):
    b, length, _ = z.shape
    t = length - n_ctx
    row, col = _grid_positions(t)
    zero = jnp.zeros((n_ctx,), jnp.float32)
    row, col = jnp.concatenate([zero, row]), jnp.concatenate([zero, col])
    cq, ckv, k_rope = jnp.split(z, [MLA_Q_RANK, MLA_Q_RANK + MLA_KV_RANK], axis=-1)
    q = mm(_rms_norm(cq, q_norm), wq_up).reshape(b, length, MLA_HEADS, MLA_NOPE + MLA_ROPE)
    kv = mm(_rms_norm(ckv, kv_norm), wkv_up).reshape(b, length, MLA_HEADS, MLA_NOPE + MLA_V)
    q = jnp.concatenate([q[..., :MLA_NOPE], _axial_rope(q[..., MLA_NOPE:], row, col)], axis=-1)
    k_rope = _axial_rope(k_rope[:, :, None, :], row, col)
    k = jnp.concatenate([kv[..., :MLA_NOPE], jnp.broadcast_to(k_rope, (b, length, MLA_HEADS, MLA_ROPE))], axis=-1)
    v = kv[..., MLA_NOPE:]
    o = _block_attention(q[:, n_ctx:], k, v)
    oc = _block_attention(q[:, :n_ctx], k[:, :n_ctx], v[:, :n_ctx])
    return jnp.concatenate([oc, o], axis=1).reshape(b, length, MLA_OUT)


SCAN_BLOCK = 128
SCAN_UNROLL = 4
N_PAIRS = RWKV_HEADS // 2
PAIR_ROWS = N_PAIRS * RWKV_HEAD


def _scan_kernel(rf_ref, rb_ref, af_ref, ab_ref, vtf_ref, vtb_ref, wf_ref, wb_ref, kf_ref, kb_ref,
                 bf_ref, bb_ref, wred_ref, yf_ref, yb_ref, s_ref):
    nb = rf_ref.shape[0]

    @pl.when(pl.program_id(0) == 0)
    def _():
        s_ref[...] = jnp.zeros_like(s_ref)

    lane = lax.broadcasted_iota(jnp.int32, (PAIR_ROWS, LANES), 1)
    wred = wred_ref[...]
    refs = ((rf_ref, af_ref, vtf_ref, wf_ref, kf_ref, bf_ref, yf_ref),
            (rb_ref, ab_ref, vtb_ref, wb_ref, kb_ref, bb_ref, yb_ref))

    def rows(ref, bi, t):
        row = ref[bi, pl.ds(t, 1), :]
        return jnp.concatenate(
            [jnp.broadcast_to(row[:, p * LANES:(p + 1) * LANES], (RWKV_HEAD, LANES)) for p in range(N_PAIRS)],
            axis=0)

    def collect(y_ref, bi, t, yb, valid):
        mask = jnp.logical_and(lane % RWKV_HEAD == t % RWKV_HEAD, valid)
        half = t // RWKV_HEAD
        y_ref[bi, 0, half] = jnp.where(mask, yb, y_ref[bi, 0, half])

    def body(i, carry):
        for d in range(2):
            r_ref, a_ref, vt_ref, w_ref, k_ref, b_ref, y_ref = refs[d]
            t = i if d == 0 else SCAN_BLOCK - 1 - i
            tp = jnp.maximum(i - 1, 0) if d == 0 else jnp.minimum(SCAN_BLOCK - i, SCAN_BLOCK - 1)
            sel = lane % RWKV_HEAD == t % RWKV_HEAD
            prs = []
            for bi in range(nb):
                s = s_ref[d, bi]
                pa = (s * rows(a_ref, bi, t)).astype(jnp.bfloat16)
                prs.append((s * rows(r_ref, bi, tp)).astype(jnp.bfloat16))
                pv = jnp.where(sel, vt_ref[bi, 0, t // RWKV_HEAD], jnp.zeros((), jnp.bfloat16))
                red = jnp.dot(jnp.concatenate([pa, pv], axis=1), wred, preferred_element_type=jnp.float32)
                s_ref[d, bi] = (s * rows(w_ref, bi, t) + red[:, :LANES] * rows(b_ref, bi, t)
                                + red[:, LANES:] * rows(k_ref, bi, t))
            for b0 in range(0, nb, 2):
                ys = jnp.dot(jnp.concatenate(prs[b0:b0 + 2], axis=1), wred, preferred_element_type=jnp.float32)
                collect(y_ref, b0, tp, ys[:, :LANES], i >= 1)
                collect(y_ref, b0 + 1, tp, ys[:, LANES:], i >= 1)
        return carry

    lax.fori_loop(0, SCAN_BLOCK, body, 0, unroll=SCAN_UNROLL)

    for d in range(2):
        r_ref, y_ref = refs[d][0], refs[d][6]
        t_last = SCAN_BLOCK - 1 if d == 0 else 0
        for b0 in range(0, nb, 2):
            prs = [(s_ref[d, bi] * rows(r_ref, bi, t_last)).astype(jnp.bfloat16) for bi in (b0, b0 + 1)]
            ys = jnp.dot(jnp.concatenate(prs, axis=1), wred, preferred_element_type=jnp.float32)
            collect(y_ref, b0, t_last, ys[:, :LANES], True)
            collect(y_ref, b0 + 1, t_last, ys[:, LANES:], True)


def rwkv_scan(r, v, a, w_f, k_f, b_f, w_b, k_b, b_b, n_ctx):
    nb, length, _ = r.shape
    nblk = length // SCAN_BLOCK
    nblk_ctx = n_ctx // SCAN_BLOCK
    vt = v.reshape(nb, nblk, 2, RWKV_HEAD, N_PAIRS, 2, RWKV_HEAD)
    vt = jnp.transpose(vt, (0, 1, 2, 4, 6, 5, 3)).reshape(nb, nblk, 2, PAIR_ROWS, LANES).astype(jnp.bfloat16)
    j = np.arange(2 * LANES)
    wred = (j[:, None] // RWKV_HEAD) == (j[None, :] // RWKV_HEAD)

    def fwd(i):
        return i

    def bwd(i):
        return jnp.where(i < nblk_ctx, nblk_ctx - 1 - i, nblk + nblk_ctx - 1 - i)

    def row_spec(blk):
        return pl.BlockSpec((nb, SCAN_BLOCK, RWKV_DIM), lambda i: (0, blk(i), 0))

    def vt_spec(blk):
        return pl.BlockSpec((nb, 1, 2, PAIR_ROWS, LANES), lambda i: (0, blk(i), 0, 0, 0))

    def y_spec(blk):
        return pl.BlockSpec((nb, 1, 2, PAIR_ROWS, LANES), lambda i: (0, blk(i), 0, 0, 0))

    y_shape = jax.ShapeDtypeStruct((nb, nblk, 2, PAIR_ROWS, LANES), jnp.float32)
    ys = pl.pallas_call(
        _scan_kernel,
        grid=(nblk,),
        in_specs=[row_spec(fwd), row_spec(bwd), row_spec(fwd), row_spec(bwd), vt_spec(fwd), vt_spec(bwd),
                  row_spec(fwd), row_spec(bwd), row_spec(fwd), row_spec(bwd), row_spec(fwd), row_spec(bwd),
                  pl.BlockSpec((2 * LANES, 2 * LANES), lambda i: (0, 0))],
        out_specs=[y_spec(fwd), y_spec(bwd)],
        out_shape=[y_shape, y_shape],
        scratch_shapes=[pltpu.VMEM((2, nb, PAIR_ROWS, LANES), jnp.float32)],
        compiler_params=pltpu.CompilerParams(dimension_semantics=("arbitrary",),
                                             vmem_limit_bytes=VMEM_LIMIT_BYTES),
        name="rwkv_scan",
    )(r, r, a, a, vt, vt, w_f, w_b, k_f, k_b, b_f, b_b, jnp.asarray(wred, jnp.bfloat16))

    def untile(y):
        y = y.reshape(nb, nblk, 2, N_PAIRS, RWKV_HEAD, 2, RWKV_HEAD)
        return jnp.transpose(y, (0, 1, 2, 6, 3, 5, 4)).reshape(nb, length, RWKV_DIM)

    return untile(ys[0]), untile(ys[1])


def _rwkv7_mixer(z, n_ctx, conv_w, w0_f, w2_f, w0_b, w2_b, a0_f, a2_f, a0_b, a2_b, g2, k_k, k_a,
                 r_k, lnx_g, lnx_b):
    b, length, _ = z.shape
    c0 = 3 * RWKV_DIM
    c1 = c0 + RWKV_DECAY_LORA
    c2 = c1 + RWKV_DECAY_LORA
    c3 = c2 + RWKV_AAA_LORA
    c4 = c3 + RWKV_AAA_LORA
    rkv_raw, wd_f, wd_b, ad_f, ad_b, gd = jnp.split(z, [c0, c1, c2, c3, c4], axis=-1)
    pos = jnp.arange(length)
    has_prev = ((pos != 0) & (pos != n_ctx)).astype(jnp.float32)[None, :, None]
    has_next = ((pos != n_ctx - 1) & (pos != length - 1)).astype(jnp.float32)[None, :, None]
    tp = jnp.pad(rkv_raw, ((0, 0), (1, 1), (0, 0)))
    rkv = tp[:, :-2] * has_prev * conv_w[0] + tp[:, 1:-1] * conv_w[1] + tp[:, 2:] * has_next * conv_w[2]
    r, k, v = jnp.split(rkv, 3, axis=-1)
    g = mm(jax.nn.sigmoid(gd), g2)

    def heads(u):
        return u.reshape(b, length, RWKV_HEADS, RWKV_HEAD)

    kk = _l2_normalize(heads(k * k_k)).reshape(b, length, RWKV_DIM)

    def direction(wd, w0, w2, ad, a0, a2):
        logw = -jax.nn.softplus(-(w0 + mm(jnp.tanh(wd), w2))) - 0.5
        decay = jnp.exp(-jnp.exp(logw))
        lr = jax.nn.sigmoid(a0 + mm(ad, a2))
        kd = k * (1.0 + (lr - 1.0) * k_a)
        return decay, kd, kk * lr

    w_f, k_f, b_f = direction(wd_f, w0_f, w2_f, ad_f, a0_f, a2_f)
    w_b, k_b, b_b = direction(wd_b, w0_b, w2_b, ad_b, a0_b, a2_b)
    y_f, y_b = rwkv_scan(r, v, -kk, w_f, k_f, b_f, w_b, k_b, b_b, n_ctx)
    y = _normalize(heads(y_f + y_b), RWKV_LNX_EPS).reshape(b, length, RWKV_DIM) * lnx_g + lnx_b
    bonus = jnp.sum(heads(r * (k_f + k_b)) * r_k, axis=-1, keepdims=True) * heads(v)
    return (y + bonus.reshape(b, length, RWKV_DIM)) * g


def _rope_halves(t, cos, sin):
    parts = []
    for s in range(2):
        u = t[:, s * LANES:(s + 1) * LANES]
        parts.append(u * cos[:, s * LANES:(s + 1) * LANES]
                     + pltpu.roll(u, LANES // 2, axis=1) * sin[:, s * LANES:(s + 1) * LANES])
    return jnp.concatenate(parts, axis=1)


def _retention_kernel(*refs, has_prev):
    if has_prev:
        (q_ref, k_ref, v_ref, g_ref, kc_ref, vc_ref, cos_ref, sin_ref, dmat_ref, qdec_ref, kdec_ref,
         kcdec_ref, cdec_ref, prev_ref, o_ref, s_ref) = refs
    else:
        (q_ref, k_ref, v_ref, g_ref, kc_ref, vc_ref, cos_ref, sin_ref, dmat_ref, qdec_ref, kdec_ref,
         kcdec_ref, cdec_ref, o_ref, s_ref) = refs
        prev_ref = None
    scale = RET_KEY ** -0.5

    @pl.when(pl.program_id(2) == 0)
    def _():
        kc = (kc_ref[0] * kcdec_ref[0] * scale).astype(jnp.bfloat16)
        s_ref[...] = lax.dot_general(kc, vc_ref[0].astype(jnp.bfloat16), (((0,), (0,)), ((), ())),
                                     preferred_element_type=jnp.float32)

    cos = cos_ref[...]
    sin = sin_ref[...]
    q = _rope_halves(q_ref[0], cos, sin)
    k = _rope_halves(k_ref[0], cos, sin) * scale
    v = v_ref[0].astype(jnp.bfloat16)
    s = s_ref[...]
    att = lax.dot_general(q.astype(jnp.bfloat16), k.astype(jnp.bfloat16), (((1,), (1,)), ((), ())),
                          preferred_element_type=jnp.float32) * dmat_ref[0]
    o = (jnp.dot(att.astype(jnp.bfloat16), v, preferred_element_type=jnp.float32)
         + jnp.dot((q * qdec_ref[0]).astype(jnp.bfloat16), s.astype(jnp.bfloat16),
                   preferred_element_type=jnp.float32))
    s_ref[...] = s * cdec_ref[0] + lax.dot_general((k * kdec_ref[0]).astype(jnp.bfloat16), v,
                                                   (((0,), (0,)), ((), ())),
                                                   preferred_element_type=jnp.float32)
    mu = jnp.mean(o, axis=-1, keepdims=True)
    oc = o - mu
    var = jnp.mean(oc * oc, axis=-1, keepdims=True)
    out = _silu(g_ref[0]) * (oc * lax.rsqrt(var + 1e-6))
    if prev_ref is not None:
        out = out + prev_ref[0]
    o_ref[0] = out.astype(o_ref.dtype)


def _retention_tables(gamma, reverse, n_ctx):
    log_g = jnp.log(gamma)[:, None, None]
    i = jnp.arange(RET_TC, dtype=jnp.float32)
    rel = (i[None, :] - i[:, None]) if reverse else (i[:, None] - i[None, :])
    dmat = jnp.where(rel >= 0, jnp.exp(jnp.maximum(rel, 0.0)[None] * log_g), 0.0)
    q_pow = (RET_TC - i) if reverse else (i + 1.0)
    k_pow = i if reverse else (RET_TC - 1.0 - i)
    m = jnp.arange(n_ctx, dtype=jnp.float32)
    c_pow = m if reverse else (n_ctx - 1.0 - m)
    bc = lambda p: jnp.broadcast_to(jnp.exp(p[None, :, None] * log_g), (RET_HEADS, p.shape[0], RET_KEY))
    cdec = jnp.broadcast_to(jnp.exp(RET_TC * log_g), (RET_HEADS, 1, RET_VAL))
    return dmat, bc(q_pow), bc(k_pow), bc(c_pow), cdec


def _rope_tables(n_tokens):
    pos_row = (jnp.arange(n_tokens) // GRID_W).astype(jnp.float32)
    pos_col = (jnp.arange(n_tokens) % GRID_W).astype(jnp.float32)
    nf = RET_KEY // 4
    inv_freq = ROPE_BASE ** (-jnp.arange(nf, dtype=jnp.float32) / nf)
    cos, sin = [], []
    for pos in (pos_row, pos_col):
        ang = pos[:, None] * inv_freq[None, :]
        cos += [jnp.cos(ang), jnp.cos(ang)]
        sin += [-jnp.sin(ang), jnp.sin(ang)]
    return jnp.concatenate(cos, axis=1), jnp.concatenate(sin, axis=1)


def _retention_direction(z, zc, cos, sin, gamma, reverse, prev):
    b, t, _ = z.shape
    n_ctx = zc.shape[1]
    nc = t // RET_TC
    dmat, qdec, kdec, kcdec, cdec = _retention_tables(gamma, reverse, n_ctx)
    ch = (lambda c: nc - 1 - c) if reverse else (lambda c: c)
    kq, kv = RET_QK // RET_KEY, (2 * RET_QK) // RET_VAL
    g_off = (2 * RET_QK + (2 if reverse else 1) * RET_VD) // RET_VAL
    in_specs = [
        pl.BlockSpec((1, RET_TC, RET_KEY), lambda bi, h, c: (bi, ch(c), h)),
        pl.BlockSpec((1, RET_TC, RET_KEY), lambda bi, h, c: (bi, ch(c), kq + h)),
        pl.BlockSpec((1, RET_TC, RET_VAL), lambda bi, h, c: (bi, ch(c), kv + h)),
        pl.BlockSpec((1, RET_TC, RET_VAL), lambda bi, h, c: (bi, ch(c), g_off + h)),
        pl.BlockSpec((1, n_ctx, RET_KEY), lambda bi, h, c: (bi, 0, h)),
        pl.BlockSpec((1, n_ctx, RET_VAL), lambda bi, h, c: (bi, 0, RET_QK // RET_VAL + h)),
        pl.BlockSpec((RET_TC, RET_KEY), lambda bi, h, c: (ch(c), 0)),
        pl.BlockSpec((RET_TC, RET_KEY), lambda bi, h, c: (ch(c), 0)),
        pl.BlockSpec((1, RET_TC, RET_TC), lambda bi, h, c: (h, 0, 0)),
        pl.BlockSpec((1, RET_TC, RET_KEY), lambda bi, h, c: (h, 0, 0)),
        pl.BlockSpec((1, RET_TC, RET_KEY), lambda bi, h, c: (h, 0, 0)),
        pl.BlockSpec((1, n_ctx, RET_KEY), lambda bi, h, c: (h, 0, 0)),
        pl.BlockSpec((1, 1, RET_VAL), lambda bi, h, c: (h, 0, 0)),
    ]
    args = [z, z, z, z, zc, zc, cos, sin, dmat, qdec, kdec, kcdec, cdec]
    if prev is not None:
        in_specs.append(pl.BlockSpec((1, RET_TC, RET_VAL), lambda bi, h, c: (bi, ch(c), h)))
        args.append(prev)
    return pl.pallas_call(
        functools.partial(_retention_kernel, has_prev=prev is not None),
        grid=(b, RET_HEADS, nc),
        in_specs=in_specs,
        out_specs=pl.BlockSpec((1, RET_TC, RET_VAL), lambda bi, h, c: (bi, ch(c), h)),
        out_shape=jax.ShapeDtypeStruct((b, t, RET_VD), jnp.float32 if prev is None else jnp.bfloat16),
        scratch_shapes=[pltpu.VMEM((RET_KEY, RET_VAL), jnp.float32)],
        compiler_params=pltpu.CompilerParams(dimension_semantics=("parallel", "parallel", "arbitrary"),
                                             vmem_limit_bytes=VMEM_LIMIT_BYTES),
        name="retention_bwd" if reverse else "retention_fwd",
    )(*args)


def retention_mixer(z, zc, decay_f, decay_b):
    cos, sin = _rope_tables(z.shape[1])
    gamma_f = 1.0 - jnp.exp2(-decay_f)
    gamma_b = 1.0 - jnp.exp2(-decay_b)
    part = _retention_direction(z, zc, cos, sin, gamma_b, True, None)
    return _retention_direction(z, zc, cos, sin, gamma_f, False, part)


def _moe_kernel(te_ref, tv_ref, x_ref, wg_ref, wu_ref, wd_ref, o_ref, acc_ref):
    i = pl.program_id(0)
    j = pl.program_id(1)

    @pl.when(tv_ref[i] > 0)
    def _():
        @pl.when(j == 0)
        def _():
            acc_ref[...] = jnp.zeros_like(acc_ref)
        x = x_ref[...].astype(jnp.bfloat16)
        g = jnp.dot(x, wg_ref[0], preferred_element_type=jnp.float32)
        u = jnp.dot(x, wu_ref[0], preferred_element_type=jnp.float32)
        a = (_silu(g) * u).astype(jnp.bfloat16)
        acc_ref[...] += jnp.dot(a, wd_ref[0], preferred_element_type=jnp.float32)

    last = j == pl.num_programs(1) - 1

    @pl.when(jnp.logical_and(last, tv_ref[i] > 0))
    def _():
        o_ref[...] = acc_ref[...]

    @pl.when(jnp.logical_and(last, tv_ref[i] == 0))
    def _():
        o_ref[...] = jnp.zeros_like(o_ref)


def moe_experts(xb, tile_expert, tile_valid, w_gate, w_up, w_down):
    slots, d = xb.shape
    ff = w_gate.shape[2]
    n_tiles = slots // MOE_TM
    grid_spec = pltpu.PrefetchScalarGridSpec(
        num_scalar_prefetch=2,
        grid=(n_tiles, ff // MOE_TF),
        in_specs=[
            pl.BlockSpec((MOE_TM, d), lambda i, j, te, tv: (i, 0)),
            pl.BlockSpec((1, d, MOE_TF), lambda i, j, te, tv: (te[i], 0, j)),
            pl.BlockSpec((1, d, MOE_TF), lambda i, j, te, tv: (te[i], 0, j)),
            pl.BlockSpec((1, MOE_TF, d), lambda i, j, te, tv: (te[i], j, 0)),
        ],
        out_specs=pl.BlockSpec((MOE_TM, d), lambda i, j, te, tv: (i, 0)),
        scratch_shapes=[pltpu.VMEM((MOE_TM, d), jnp.float32)],
    )
    return pl.pallas_call(
        _moe_kernel,
        grid_spec=grid_spec,
        out_shape=jax.ShapeDtypeStruct((slots, d), jnp.float32),
        compiler_params=pltpu.CompilerParams(dimension_semantics=("arbitrary", "arbitrary"),
                                             vmem_limit_bytes=VMEM_LIMIT_BYTES),
        name="moe_experts",
    )(tile_expert, tile_valid, xb, w_gate, w_up, w_down)


def moe_swiglu(h, router, w_gate, w_up, w_down):
    b, t, d = h.shape
    n = b * t
    hf = h.reshape(n, d)
    logits = jnp.dot(hf, router, precision=lax.Precision.HIGHEST)
    top_val, top_idx = lax.top_k(logits, TOP_K)
    gate = jax.nn.softmax(top_val, axis=-1)
    flat_e = top_idx.reshape(-1)
    onehot = (flat_e[:, None] == jnp.arange(N_EXPERTS, dtype=flat_e.dtype)[None, :]).astype(jnp.int32)
    csum = jnp.cumsum(onehot, axis=0)
    counts = csum[-1]
    rank = jnp.sum((csum - onehot) * onehot, axis=1)
    padded = (counts + MOE_TM - 1) // MOE_TM * MOE_TM
    pad_end = jnp.cumsum(padded)
    pad_start = pad_end - padded
    slot = (pad_start[flat_e] + rank).astype(jnp.int32)
    n_tiles = (n * TOP_K) // MOE_TM + N_EXPERTS
    slots = n_tiles * MOE_TM
    tok = jnp.arange(n * TOP_K, dtype=jnp.int32) // TOP_K
    slot_tok = jnp.zeros((slots,), jnp.int32).at[slot].set(tok)
    tile_start = jnp.arange(n_tiles, dtype=jnp.int32) * MOE_TM
    tile_expert = jnp.minimum(jnp.sum(pad_end[None, :] <= tile_start[:, None], axis=1),
                              N_EXPERTS - 1).astype(jnp.int32)
    tile_valid = (tile_start < pad_end[-1]).astype(jnp.int32)
    yb = moe_experts(hf[slot_tok], tile_expert, tile_valid, w_gate, w_up, w_down)
    slot2 = slot.reshape(n, TOP_K)
    out = gate[:, 0:1] * yb[slot2[:, 0]] + gate[:, 1:2] * yb[slot2[:, 1]]
    return out.reshape(b, t, d)


def _row_select(i, tm, n_ctx, ctx_vec, lat_vec):
    if n_ctx == 0:
        return lat_vec
    row = i * tm + lax.broadcasted_iota(jnp.int32, (tm, 1), 0)
    return jnp.where(row < n_ctx, ctx_vec, lat_vec)


def _post_norm_rows(x, update, g, b):
    y = DEEPNORM_ALPHA * x + update
    mu = jnp.mean(y, axis=-1, keepdims=True)
    yc = y - mu
    var = jnp.mean(yc * yc, axis=-1, keepdims=True)
    return yc * lax.rsqrt(var + LN_EPS) * g + b


def _matmul_postnorm_kernel(a_ref, w_ref, x_ref, lat_ref, ctx_ref, g_ref, b_ref, o_ref, *, n_ctx):
    tm = a_ref.shape[1]
    o = jnp.dot(a_ref[0].astype(jnp.bfloat16), w_ref[...], preferred_element_type=jnp.float32)
    gate = _row_select(pl.program_id(1), tm, n_ctx, ctx_ref[...], lat_ref[0])
    o_ref[0] = _post_norm_rows(x_ref[0], gate * o, g_ref[...], b_ref[...])


def matmul_postnorm(a, w, x, gate_lat, gate_ctx, ln_g, ln_b, n_ctx):
    b, length, k = a.shape
    d = w.shape[1]
    tm = _pick_tile(length, (768, 512, 256))
    vec = pl.BlockSpec((1, d), lambda bi, i: (0, 0))
    return pl.pallas_call(
        functools.partial(_matmul_postnorm_kernel, n_ctx=n_ctx),
        grid=(b, length // tm),
        in_specs=[pl.BlockSpec((1, tm, k), lambda bi, i: (bi, i, 0)),
                  pl.BlockSpec((k, d), lambda bi, i: (0, 0)),
                  pl.BlockSpec((1, tm, d), lambda bi, i: (bi, i, 0)),
                  pl.BlockSpec((1, 1, d), lambda bi, i: (bi, 0, 0)), vec, vec, vec],
        out_specs=pl.BlockSpec((1, tm, d), lambda bi, i: (bi, i, 0)),
        out_shape=jax.ShapeDtypeStruct((b, length, d), jnp.float32),
        compiler_params=pltpu.CompilerParams(dimension_semantics=("parallel", "parallel"),
                                             vmem_limit_bytes=VMEM_LIMIT_BYTES),
        name="matmul_postnorm",
    )(a, w.astype(jnp.bfloat16), x, gate_lat[:, None, :], gate_ctx[None, :], ln_g[None, :], ln_b[None, :])


def _ffn_postnorm_kernel(x_ref, lat_ref, ctx_ref, wg_ref, wu_ref, wd_ref, g_ref, b_ref, o_ref, h_ref, acc_ref,
                         *, n_ctx):
    i = pl.program_id(1)
    j = pl.program_id(2)
    tm = x_ref.shape[1]

    def vec(k):
        return _row_select(i, tm, n_ctx, ctx_ref[k:k + 1, :], lat_ref[0, k:k + 1, :])

    @pl.when(j == 0)
    def _():
        h_ref[...] = (x_ref[0] * (1.0 + vec(1)) + vec(0)).astype(jnp.bfloat16)
        acc_ref[...] = jnp.zeros_like(acc_ref)

    h = h_ref[...]
    gt = jnp.dot(h, wg_ref[...], preferred_element_type=jnp.float32)
    up = jnp.dot(h, wu_ref[...], preferred_element_type=jnp.float32)
    acc_ref[...] += jnp.dot((_silu(gt) * up).astype(jnp.bfloat16), wd_ref[...], preferred_element_type=jnp.float32)

    @pl.when(j == pl.num_programs(2) - 1)
    def _():
        o_ref[0] = _post_norm_rows(x_ref[0], vec(2) * acc_ref[...], g_ref[...], b_ref[...])


def ffn_postnorm(x, mod_lat, mod_ctx, w_gate, w_up, w_down, ln_g, ln_b, n_ctx):
    b, length, d = x.shape
    ff = w_gate.shape[1]
    tm = _pick_tile(length, (384, 512, 256))
    tf = _pick_tile(ff, (1408, 512, 256))
    vec = pl.BlockSpec((1, d), lambda bi, i, j: (0, 0))
    return pl.pallas_call(
        functools.partial(_ffn_postnorm_kernel, n_ctx=n_ctx),
        grid=(b, length // tm, ff // tf),
        in_specs=[pl.BlockSpec((1, tm, d), lambda bi, i, j: (bi, i, 0)),
                  pl.BlockSpec((1, 3, d), lambda bi, i, j: (bi, 0, 0)),
                  pl.BlockSpec((3, d), lambda bi, i, j: (0, 0)),
                  pl.BlockSpec((d, tf), lambda bi, i, j: (0, j)),
                  pl.BlockSpec((d, tf), lambda bi, i, j: (0, j)),
                  pl.BlockSpec((tf, d), lambda bi, i, j: (j, 0)), vec, vec],
        out_specs=pl.BlockSpec((1, tm, d), lambda bi, i, j: (bi, i, 0)),
        out_shape=jax.ShapeDtypeStruct((b, length, d), jnp.float32),
        scratch_shapes=[pltpu.VMEM((tm, d), jnp.bfloat16), pltpu.VMEM((tm, d), jnp.float32)],
        compiler_params=pltpu.CompilerParams(dimension_semantics=("parallel", "parallel", "arbitrary"),
                                             vmem_limit_bytes=VMEM_LIMIT_BYTES),
        name="ffn_postnorm",
    )(x, mod_lat, mod_ctx, w_gate.astype(jnp.bfloat16), w_up.astype(jnp.bfloat16), w_down.astype(jnp.bfloat16),
      ln_g[None, :], ln_b[None, :])


def _even_layer(x, xc, sc, scc, mod_w, mod_b, w_in, mla_q_norm, mla_wq_up, mla_kv_norm, mla_wkv_up,
                rwkv_conv, rwkv_w0_f, rwkv_w2_f, rwkv_w0_b, rwkv_w2_b, rwkv_a0_f, rwkv_a2_f, rwkv_a0_b,
                rwkv_a2_b, rwkv_g2, rwkv_k_k, rwkv_k_a, rwkv_r_k, rwkv_lnx_g, rwkv_lnx_b, w_out,
                ln1_g, ln1_b, ffn_w_gate, ffn_w_up, ffn_w_down, ln2_g, ln2_b):
    n_ctx = xc.shape[1]
    m = _adaln(sc, mod_w, mod_b)
    mc = _adaln(scc, mod_w, mod_b)
    xa = jnp.concatenate([xc, x], axis=1)
    is_ctx = (jnp.arange(xa.shape[1]) < n_ctx)[None, :, None]

    def rows(k):
        return jnp.where(is_ctx, mc[k][None, None, :], m[k][:, None, :])

    z = mm(_modulate(xa, rows(0), rows(1)), w_in)
    o_mla = _mla_mixer(z[..., :MLA_IN], n_ctx, mla_q_norm, mla_wq_up, mla_kv_norm, mla_wkv_up)
    o_rwkv = _rwkv7_mixer(z[..., MLA_IN:], n_ctx, rwkv_conv, rwkv_w0_f, rwkv_w2_f, rwkv_w0_b, rwkv_w2_b,
                          rwkv_a0_f, rwkv_a2_f, rwkv_a0_b, rwkv_a2_b, rwkv_g2, rwkv_k_k, rwkv_k_a,
                          rwkv_r_k, rwkv_lnx_g, rwkv_lnx_b)
    mix = jnp.concatenate([o_mla, o_rwkv], axis=-1).astype(jnp.bfloat16)
    xa = matmul_postnorm(mix, w_out, xa, m[2], mc[2], ln1_g, ln1_b, n_ctx)
    xa = ffn_postnorm(xa, jnp.stack(m[3:6], axis=1), jnp.stack(mc[3:6], axis=0),
                      ffn_w_gate, ffn_w_up, ffn_w_down, ln2_g, ln2_b, n_ctx)
    return xa[:, n_ctx:], xa[:, :n_ctx]


def _odd_layer(x, xc, sc, scc, mod_w, mod_b, w_in, ret_decay_f, ret_decay_b, w_out, ln1_g, ln1_b,
               router, moe_w_gate, moe_w_up, moe_w_down, ln2_g, ln2_b):
    m = _adaln(sc, mod_w, mod_b)
    c_shift, c_scale = jnp.split(
        jnp.dot(scc, mod_w[:, :2 * D_MODEL], precision=lax.Precision.HIGHEST) + mod_b[:2 * D_MODEL], 2)
    z = mm(_modulate(x, m[0][:, None, :], m[1][:, None, :]), w_in)
    zc = mm(_modulate(xc, c_shift, c_scale), w_in[:, RET_QK:2 * RET_QK + RET_VD])
    mix = retention_mixer(z, zc, ret_decay_f, ret_decay_b)
    x = matmul_postnorm(mix, w_out, x, m[2], jnp.zeros_like(m[2][0]), ln1_g, ln1_b, 0)
    y = moe_swiglu(_modulate(x, m[3][:, None, :], m[4][:, None, :]), router, moe_w_gate.astype(jnp.bfloat16),
                   moe_w_up.astype(jnp.bfloat16), moe_w_down.astype(jnp.bfloat16))
    return _post_norm(x, m[5][:, None, :] * y, ln2_g, ln2_b)


def kernel(x, c, ctx, c_ctx, l0_mod_w, l0_mod_b, l0_w_in, l0_mla_q_norm, l0_mla_wq_up, l0_mla_kv_norm, l0_mla_wkv_up, l0_rwkv_conv, l0_rwkv_w0_f, l0_rwkv_w2_f, l0_rwkv_w0_b, l0_rwkv_w2_b, l0_rwkv_a0_f, l0_rwkv_a2_f, l0_rwkv_a0_b, l0_rwkv_a2_b, l0_rwkv_g2, l0_rwkv_k_k, l0_rwkv_k_a, l0_rwkv_r_k, l0_rwkv_lnx_g, l0_rwkv_lnx_b, l0_w_out, l0_ln1_g, l0_ln1_b, l0_ffn_w_gate, l0_ffn_w_up, l0_ffn_w_down, l0_ln2_g, l0_ln2_b, l1_mod_w, l1_mod_b, l1_w_in, l1_ret_decay_f, l1_ret_decay_b, l1_w_out, l1_ln1_g, l1_ln1_b, l1_router, l1_moe_w_gate, l1_moe_w_up, l1_moe_w_down, l1_ln2_g, l1_ln2_b):
    even_params = (l0_mod_w, l0_mod_b, l0_w_in, l0_mla_q_norm, l0_mla_wq_up, l0_mla_kv_norm, l0_mla_wkv_up,
                   l0_rwkv_conv, l0_rwkv_w0_f, l0_rwkv_w2_f, l0_rwkv_w0_b, l0_rwkv_w2_b, l0_rwkv_a0_f,
                   l0_rwkv_a2_f, l0_rwkv_a0_b, l0_rwkv_a2_b, l0_rwkv_g2, l0_rwkv_k_k, l0_rwkv_k_a, l0_rwkv_r_k,
                   l0_rwkv_lnx_g, l0_rwkv_lnx_b, l0_w_out, l0_ln1_g, l0_ln1_b, l0_ffn_w_gate, l0_ffn_w_up,
                   l0_ffn_w_down, l0_ln2_g, l0_ln2_b)
    odd_params = (l1_mod_w, l1_mod_b, l1_w_in, l1_ret_decay_f, l1_ret_decay_b, l1_w_out, l1_ln1_g, l1_ln1_b,
                  l1_router, l1_moe_w_gate, l1_moe_w_up, l1_moe_w_down, l1_ln2_g, l1_ln2_b)
    sc = _silu(c)
    scc = _silu(c_ctx)
    x, xc = _even_layer(x, ctx, sc, scc, *even_params)
    return _odd_layer(x, xc, sc, scc, *odd_params)
```

```python
import functools

import jax
import jax.numpy as jnp
import numpy as np
from jax import lax
from jax.experimental import pallas as pl
from jax.experimental.pallas import tpu as pltpu

D_MODEL = 1024
DEPTH = 2
GRID_W = 64
ROPE_BASE = 10000.0
DEEPNORM_ALPHA = (2 * DEPTH) ** 0.25
LN_EPS = 1e-5

MLA_HEADS = 8
MLA_Q_RANK = 256
MLA_KV_RANK = 128
MLA_NOPE = 64
MLA_ROPE = 32
MLA_V = 64
MLA_IN = MLA_Q_RANK + MLA_KV_RANK + MLA_ROPE
MLA_OUT = MLA_HEADS * MLA_V

RWKV_HEADS = 8
RWKV_HEAD = 64
RWKV_DIM = RWKV_HEADS * RWKV_HEAD
RWKV_DECAY_LORA = 64
RWKV_AAA_LORA = 64
RWKV_GATE_LORA = 128
RWKV_LNX_EPS = 64e-5

RET_HEADS = 4
RET_KEY = 256
RET_VAL = 512
RET_TC = 256
RET_QK = RET_HEADS * RET_KEY
RET_VD = RET_HEADS * RET_VAL

N_EXPERTS = 8
TOP_K = 2
MOE_TM = 1024
MOE_TF = 512

LANES = 128
VMEM_LIMIT_BYTES = 48 * 1024 * 1024


def _matmul_kernel(x_ref, w_ref, o_ref):
    o_ref[...] = jnp.dot(x_ref[...].astype(jnp.bfloat16), w_ref[...], preferred_element_type=jnp.float32)


def _pick_tile(n, candidates):
    for c in candidates:
        if n % c == 0:
            return c
    raise ValueError(f"no tile for {n}")


def pmatmul(x, w):
    m, k = x.shape
    n = w.shape[1]
    n_pad = (-n) % LANES
    w = w.astype(jnp.bfloat16)
    if n_pad:
        w = jnp.pad(w, ((0, 0), (0, n_pad)))
    np_ = n + n_pad
    tm = _pick_tile(m, (1024, 512, 256, 128, 8) if k <= 1024 else (512, 256, 128, 8))
    tn = _pick_tile(np_, (1024, 768, 640, 512, 384, 256, 128))
    out = pl.pallas_call(
        _matmul_kernel,
        grid=(m // tm, np_ // tn),
        in_specs=[pl.BlockSpec((tm, k), lambda i, j: (i, 0)),
                  pl.BlockSpec((k, tn), lambda i, j: (0, j))],
        out_specs=pl.BlockSpec((tm, tn), lambda i, j: (i, j)),
        out_shape=jax.ShapeDtypeStruct((m, np_), jnp.float32),
        compiler_params=pltpu.CompilerParams(
            dimension_semantics=("parallel", "parallel"),
            vmem_limit_bytes=VMEM_LIMIT_BYTES),
        name="matmul",
    )(x, w)
    return out[:, :n] if n_pad else out


def mm(x, w):
    lead = x.shape[:-1]
    out = pmatmul(x.reshape(-1, x.shape[-1]).astype(jnp.bfloat16), w)
    return out.reshape(*lead, w.shape[1])


def _silu(t):
    return t * jax.nn.sigmoid(t)


def _normalize(t, eps):
    mu = jnp.mean(t, axis=-1, keepdims=True)
    var = jnp.mean(jnp.square(t - mu), axis=-1, keepdims=True)
    return (t - mu) * lax.rsqrt(var + eps)


def _layer_norm(t, g, b):
    return _normalize(t, LN_EPS) * g + b


def _rms_norm(t, g, eps=1e-6):
    return t * lax.rsqrt(jnp.mean(t * t, axis=-1, keepdims=True) + eps) * g


def _l2_normalize(t, eps=1e-12):
    return t / jnp.maximum(jnp.linalg.norm(t, axis=-1, keepdims=True), eps)


def _post_norm(x, update, g, b):
    return _layer_norm(DEEPNORM_ALPHA * x + update, g, b)


def _modulate(h, shift, scale):
    return h * (1.0 + scale) + shift


def _adaln(cond, mod_w, mod_b):
    return jnp.split(jnp.dot(cond, mod_w, precision=lax.Precision.HIGHEST) + mod_b, 6, axis=-1)


def _grid_positions(n_tokens):
    rows = n_tokens // GRID_W
    row = jnp.repeat(jnp.arange(rows, dtype=jnp.float32), GRID_W)
    col = jnp.tile(jnp.arange(GRID_W, dtype=jnp.float32), rows)
    return row, col


def _rotate(t, pos):
    nf = t.shape[-1] // 2
    inv_freq = ROPE_BASE ** (-jnp.arange(nf, dtype=jnp.float32) / nf)
    ang = pos[:, None] * inv_freq[None, :]
    cos = jnp.cos(ang)[None, :, None, :]
    sin = jnp.sin(ang)[None, :, None, :]
    t1, t2 = t[..., :nf], t[..., nf:]
    return jnp.concatenate([t1 * cos - t2 * sin, t1 * sin + t2 * cos], axis=-1)


def _axial_rope(t, row, col):
    half = t.shape[-1] // 2
    return jnp.concatenate([_rotate(t[..., :half], row), _rotate(t[..., half:], col)], axis=-1)


SOFTMAX_FLOOR = -1e30


FLASH_ROW_GROUPS = 4


def _flash_kernel(q_ref, kt_ref, v_ref, o_ref, m_ref, acc_ref, *, dv):
    j = pl.program_id(3)

    @pl.when(j == 0)
    def _():
        m_ref[...] = jnp.full_like(m_ref, SOFTMAX_FLOOR)
        acc_ref[...] = jnp.zeros_like(acc_ref)

    rows = q_ref.shape[2] // FLASH_ROW_GROUPS
    kt = kt_ref[0, 0]
    v = v_ref[0, 0]
    for u in range(FLASH_ROW_GROUPS):
        sl = slice(u * rows, (u + 1) * rows)
        s = jnp.dot(q_ref[0, 0, sl, :], kt, preferred_element_type=jnp.float32)
        m_prev = m_ref[sl, :]
        m_new = jnp.maximum(m_prev, jnp.max(s, axis=-1, keepdims=True))
        alpha = jnp.exp(m_prev - m_new)
        p = jnp.exp(s - m_new).astype(jnp.bfloat16)
        acc_ref[sl, :] = alpha * acc_ref[sl, :] + jnp.dot(p, v, preferred_element_type=jnp.float32)
        m_ref[sl, :] = m_new

    @pl.when(j == pl.num_programs(3) - 1)
    def _():
        acc = acc_ref[...]
        o_ref[0, 0] = acc[:, :dv] / acc[:, dv:dv + 1]


def flash_attention(q, kt, v1, dv, tq, tk):
    b, h, t, dq = q.shape
    s = kt.shape[3]
    return pl.pallas_call(
        functools.partial(_flash_kernel, dv=dv),
        grid=(b, h, t // tq, s // tk),
        in_specs=[pl.BlockSpec((1, 1, tq, dq), lambda bi, hi, i, j: (bi, hi, i, 0)),
                  pl.BlockSpec((1, 1, dq, tk), lambda bi, hi, i, j: (bi, hi, 0, j)),
                  pl.BlockSpec((1, 1, tk, LANES), lambda bi, hi, i, j: (bi, hi, j, 0))],
        out_specs=pl.BlockSpec((1, 1, tq, dv), lambda bi, hi, i, j: (bi, hi, i, 0)),
        out_shape=jax.ShapeDtypeStruct((b, h, t, dv), jnp.float32),
        scratch_shapes=[pltpu.VMEM((tq, 1), jnp.float32), pltpu.VMEM((tq, LANES), jnp.float32)],
        compiler_params=pltpu.CompilerParams(
            dimension_semantics=("parallel", "parallel", "parallel", "arbitrary"),
            vmem_limit_bytes=VMEM_LIMIT_BYTES),
        name="flash_attention",
    )(q, kt, v1)


def _block_attention(q, k, v):
    dq = q.shape[-1]
    dv = v.shape[-1]
    b, s, h, _ = k.shape
    qh = jnp.transpose(q * dq ** -0.5, (0, 2, 1, 3)).astype(jnp.bfloat16)
    kt = jnp.transpose(k, (0, 2, 3, 1)).astype(jnp.bfloat16)
    v1 = jnp.concatenate([v, jnp.ones((b, s, h, 1), v.dtype), jnp.zeros((b, s, h, LANES - dv - 1), v.dtype)],
                         axis=-1)
    v1 = jnp.transpose(v1, (0, 2, 1, 3)).astype(jnp.bfloat16)
    tq = _pick_tile(q.shape[1], (1024, 256))
    o = flash_attention(qh, kt, v1, dv, tq, s)
    return jnp.transpose(o, (0, 2, 1, 3))


def _mla_mixer(z, n_ctx, q_norm, wq_up, kv_norm, wkv_up):
    b, length, _ = z.shape
    t = length - n_ctx
    row, col = _grid_positions(t)
    zero = jnp.zeros((n_ctx,), jnp.float32)
    row, col = jnp.concatenate([zero, row]), jnp.concatenate([zero, col])
    cq, ckv, k_rope = jnp.split(z, [MLA_Q_RANK, MLA_Q_RANK + MLA_KV_RANK], axis=-1)
    q = mm(_rms_norm(cq, q_norm), wq_up).reshape(b, length, MLA_HEADS, MLA_NOPE + MLA_ROPE)
    kv = mm(_rms_norm(ckv, kv_norm), wkv_up).reshape(b, length, MLA_HEADS, MLA_NOPE + MLA_V)
    q = jnp.concatenate([q[..., :MLA_NOPE], _axial_rope(q[..., MLA_NOPE:], row, col)], axis=-1)
    k_rope = _axial_rope(k_rope[:, :, None, :], row, col)
    k = jnp.concatenate([kv[..., :MLA_NOPE], jnp.broadcast_to(k_rope, (b, length, MLA_HEADS, MLA_ROPE))], axis=-1)
    v = kv[..., MLA_NOPE:]
    o = _block_attention(q[:, n_ctx:], k, v)
    oc = _block_attention(q[:, :n_ctx], k[:, :n_ctx], v[:, :n_ctx])
    return jnp.concatenate([oc, o], axis=1).reshape(b, length, MLA_OUT)


SCAN_BLOCK = 128
SCAN_UNROLL = 4
N_PAIRS = RWKV_HEADS // 2
PAIR_ROWS = N_PAIRS * RWKV_HEAD


def _scan_kernel(rf_ref, rb_ref, af_ref, ab_ref, vtf_ref, vtb_ref, wf_ref, wb_ref, kf_ref, kb_ref,
                 bf_ref, bb_ref, wred_ref, yf_ref, yb_ref, s_ref):
    nb = rf_ref.shape[0]

    @pl.when(pl.program_id(0) == 0)
    def _():
        s_ref[...] = jnp.zeros_like(s_ref)

    lane = lax.broadcasted_iota(jnp.int32, (PAIR_ROWS, LANES), 1)
    wred = wred_ref[...]
    refs = ((rf_ref, af_ref, vtf_ref, wf_ref, kf_ref, bf_ref, yf_ref),
            (rb_ref, ab_ref, vtb_ref, wb_ref, kb_ref, bb_ref, yb_ref))

    def rows(ref, bi, t):
        row = ref[bi, pl.ds(t, 1), :]
        return jnp.concatenate(
            [jnp.broadcast_to(row[:, p * LANES:(p + 1) * LANES], (RWKV_HEAD, LANES)) for p in range(N_PAIRS)],
            axis=0)

    def collect(y_ref, bi, t, yb, valid):
        mask = jnp.logical_and(lane % RWKV_HEAD == t % RWKV_HEAD, valid)
        half = t // RWKV_HEAD
        y_ref[bi, 0, half] = jnp.where(mask, yb, y_ref[bi, 0, half])

    def body(i, carry):
        for d in range(2):
            r_ref, a_ref, vt_ref, w_ref, k_ref, b_ref, y_ref = refs[d]
            t = i if d == 0 else SCAN_BLOCK - 1 - i
            tp = jnp.maximum(i - 1, 0) if d == 0 else jnp.minimum(SCAN_BLOCK - i, SCAN_BLOCK - 1)
            sel = lane % RWKV_HEAD == t % RWKV_HEAD
            prs = []
            for bi in range(nb):
                s = s_ref[d, bi]
                pa = (s * rows(a_ref, bi, t)).astype(jnp.bfloat16)
                prs.append((s * rows(r_ref, bi, tp)).astype(jnp.bfloat16))
                pv = jnp.where(sel, vt_ref[bi, 0, t // RWKV_HEAD], jnp.zeros((), jnp.bfloat16))
                red = jnp.dot(jnp.concatenate([pa, pv], axis=1), wred, preferred_element_type=jnp.float32)
                s_ref[d, bi] = (s * rows(w_ref, bi, t) + red[:, :LANES] * rows(b_ref, bi, t)
                                + red[:, LANES:] * rows(k_ref, bi, t))
            for b0 in range(0, nb, 2):
                ys = jnp.dot(jnp.concatenate(prs[b0:b0 + 2], axis=1), wred, preferred_element_type=jnp.float32)
                collect(y_ref, b0, tp, ys[:, :LANES], i >= 1)
                collect(y_ref, b0 + 1, tp, ys[:, LANES:], i >= 1)
        return carry

    lax.fori_loop(0, SCAN_BLOCK, body, 0, unroll=SCAN_UNROLL)

    for d in range(2):
        r_ref, y_ref = refs[d][0], refs[d][6]
        t_last = SCAN_BLOCK - 1 if d == 0 else 0
        for b0 in range(0, nb, 2):
            prs = [(s_ref[d, bi] * rows(r_ref, bi, t_last)).astype(jnp.bfloat16) for bi in (b0, b0 + 1)]
            ys = jnp.dot(jnp.concatenate(prs, axis=1), wred, preferred_element_type=jnp.float32)
            collect(y_ref, b0, t_last, ys[:, :LANES], True)
            collect(y_ref, b0 + 1, t_last, ys[:, LANES:], True)


def rwkv_scan(r, v, a, w_f, k_f, b_f, w_b, k_b, b_b, n_ctx):
    nb, length, _ = r.shape
    nblk = length // SCAN_BLOCK
    nblk_ctx = n_ctx // SCAN_BLOCK
    vt = v.reshape(nb, nblk, 2, RWKV_HEAD, N_PAIRS, 2, RWKV_HEAD)
    vt = jnp.transpose(vt, (0, 1, 2, 4, 6, 5, 3)).reshape(nb, nblk, 2, PAIR_ROWS, LANES).astype(jnp.bfloat16)
    j = np.arange(2 * LANES)
    wred = (j[:, None] // RWKV_HEAD) == (j[None, :] // RWKV_HEAD)

    def fwd(i):
        return i

    def bwd(i):
        return jnp.where(i < nblk_ctx, nblk_ctx - 1 - i, nblk + nblk_ctx - 1 - i)

    def row_spec(blk):
        return pl.BlockSpec((nb, SCAN_BLOCK, RWKV_DIM), lambda i: (0, blk(i), 0))

    def vt_spec(blk):
        return pl.BlockSpec((nb, 1, 2, PAIR_ROWS, LANES), lambda i: (0, blk(i), 0, 0, 0))

    def y_spec(blk):
        return pl.BlockSpec((nb, 1, 2, PAIR_ROWS, LANES), lambda i: (0, blk(i), 0, 0, 0))

    y_shape = jax.ShapeDtypeStruct((nb, nblk, 2, PAIR_ROWS, LANES), jnp.float32)
    ys = pl.pallas_call(
        _scan_kernel,
        grid=(nblk,),
        in_specs=[row_spec(fwd), row_spec(bwd), row_spec(fwd), row_spec(bwd), vt_spec(fwd), vt_spec(bwd),
                  row_spec(fwd), row_spec(bwd), row_spec(fwd), row_spec(bwd), row_spec(fwd), row_spec(bwd),
                  pl.BlockSpec((2 * LANES, 2 * LANES), lambda i: (0, 0))],
        out_specs=[y_spec(fwd), y_spec(bwd)],
        out_shape=[y_shape, y_shape],
        scratch_shapes=[pltpu.VMEM((2, nb, PAIR_ROWS, LANES), jnp.float32)],
        compiler_params=pltpu.CompilerParams(dimension_semantics=("arbitrary",),
                                             vmem_limit_bytes=VMEM_LIMIT_BYTES),
        name="rwkv_scan",
    )(r, r, a, a, vt, vt, w_f, w_b, k_f, k_b, b_f, b_b, jnp.asarray(wred, jnp.bfloat16))

    def untile(y):
        y = y.reshape(nb, nblk, 2, N_PAIRS, RWKV_HEAD, 2, RWKV_HEAD)
        return jnp.transpose(y, (0, 1, 2, 6, 3, 5, 4)).reshape(nb, length, RWKV_DIM)

    return untile(ys[0]), untile(ys[1])


def _rwkv7_mixer(z, n_ctx, conv_w, w0_f, w2_f, w0_b, w2_b, a0_f, a2_f, a0_b, a2_b, g2, k_k, k_a,
                 r_k, lnx_g, lnx_b):
    b, length, _ = z.shape
    c0 = 3 * RWKV_DIM
    c1 = c0 + RWKV_DECAY_LORA
    c2 = c1 + RWKV_DECAY_LORA
    c3 = c2 + RWKV_AAA_LORA
    c4 = c3 + RWKV_AAA_LORA
    rkv_raw, wd_f, wd_b, ad_f, ad_b, gd = jnp.split(z, [c0, c1, c2, c3, c4], axis=-1)
    pos = jnp.arange(length)
    has_prev = ((pos != 0) & (pos != n_ctx)).astype(jnp.float32)[None, :, None]
    has_next = ((pos != n_ctx - 1) & (pos != length - 1)).astype(jnp.float32)[None, :, None]
    tp = jnp.pad(rkv_raw, ((0, 0), (1, 1), (0, 0)))
    rkv = tp[:, :-2] * has_prev * conv_w[0] + tp[:, 1:-1] * conv_w[1] + tp[:, 2:] * has_next * conv_w[2]
    r, k, v = jnp.split(rkv, 3, axis=-1)
    g = mm(jax.nn.sigmoid(gd), g2)

    def heads(u):
        return u.reshape(b, length, RWKV_HEADS, RWKV_HEAD)

    kk = _l2_normalize(heads(k * k_k)).reshape(b, length, RWKV_DIM)

    def direction(wd, w0, w2, ad, a0, a2):
        logw = -jax.nn.softplus(-(w0 + mm(jnp.tanh(wd), w2))) - 0.5
        decay = jnp.exp(-jnp.exp(logw))
        lr = jax.nn.sigmoid(a0 + mm(ad, a2))
        kd = k * (1.0 + (lr - 1.0) * k_a)
        return decay, kd, kk * lr

    w_f, k_f, b_f = direction(wd_f, w0_f, w2_f, ad_f, a0_f, a2_f)
    w_b, k_b, b_b = direction(wd_b, w0_b, w2_b, ad_b, a0_b, a2_b)
    y_f, y_b = rwkv_scan(r, v, -kk, w_f, k_f, b_f, w_b, k_b, b_b, n_ctx)
    y = _normalize(heads(y_f + y_b), RWKV_LNX_EPS).reshape(b, length, RWKV_DIM) * lnx_g + lnx_b
    bonus = jnp.sum(heads(r * (k_f + k_b)) * r_k, axis=-1, keepdims=True) * heads(v)
    return (y + bonus.reshape(b, length, RWKV_DIM)) * g


def _rope_halves(t, cos, sin):
    parts = []
    for s in range(2):
        u = t[:, s * LANES:(s + 1) * LANES]
        parts.append(u * cos[:, s * LANES:(s + 1) * LANES]
                     + pltpu.roll(u, LANES // 2, axis=1) * sin[:, s * LANES:(s + 1) * LANES])
    return jnp.concatenate(parts, axis=1)


def _retention_kernel(*refs, has_prev):
    if has_prev:
        (q_ref, k_ref, v_ref, g_ref, kc_ref, vc_ref, cos_ref, sin_ref, dmat_ref, qdec_ref, kdec_ref,
         kcdec_ref, cdec_ref, prev_ref, o_ref, s_ref) = refs
    else:
        (q_ref, k_ref, v_ref, g_ref, kc_ref, vc_ref, cos_ref, sin_ref, dmat_ref, qdec_ref, kdec_ref,
         kcdec_ref, cdec_ref, o_ref, s_ref) = refs
        prev_ref = None
    scale = RET_KEY ** -0.5

    @pl.when(pl.program_id(2) == 0)
    def _():
        kc = (kc_ref[0] * kcdec_ref[0] * scale).astype(jnp.bfloat16)
        s_ref[...] = lax.dot_general(kc, vc_ref[0].astype(jnp.bfloat16), (((0,), (0,)), ((), ())),
                                     preferred_element_type=jnp.float32)

    cos = cos_ref[...]
    sin = sin_ref[...]
    q = _rope_halves(q_ref[0], cos, sin)
    k = _rope_halves(k_ref[0], cos, sin) * scale
    v = v_ref[0].astype(jnp.bfloat16)
    s = s_ref[...]
    att = lax.dot_general(q.astype(jnp.bfloat16), k.astype(jnp.bfloat16), (((1,), (1,)), ((), ())),
                          preferred_element_type=jnp.float32) * dmat_ref[0]
    o = (jnp.dot(att.astype(jnp.bfloat16), v, preferred_element_type=jnp.float32)
         + jnp.dot((q * qdec_ref[0]).astype(jnp.bfloat16), s.astype(jnp.bfloat16),
                   preferred_element_type=jnp.float32))
    s_ref[...] = s * cdec_ref[0] + lax.dot_general((k * kdec_ref[0]).astype(jnp.bfloat16), v,
                                                   (((0,), (0,)), ((), ())),
                                                   preferred_element_type=jnp.float32)
    mu = jnp.mean(o, axis=-1, keepdims=True)
    oc = o - mu
    var = jnp.mean(oc * oc, axis=-1, keepdims=True)
    out = _silu(g_ref[0]) * (oc * lax.rsqrt(var + 1e-6))
    if prev_ref is not None:
        out = out + prev_ref[0]
    o_ref[0] = out.astype(o_ref.dtype)


def _retention_tables(gamma, reverse, n_ctx):
    log_g = jnp.log(gamma)[:, None, None]
    i = jnp.arange(RET_TC, dtype=jnp.float32)
    rel = (i[None, :] - i[:, None]) if reverse else (i[:, None] - i[None, :])
    dmat = jnp.where(rel >= 0, jnp.exp(jnp.maximum(rel, 0.0)[None] * log_g), 0.0)
    q_pow = (RET_TC - i) if reverse else (i + 1.0)
    k_pow = i if reverse else (RET_TC - 1.0 - i)
    m = jnp.arange(n_ctx, dtype=jnp.float32)
    c_pow = m if reverse else (n_ctx - 1.0 - m)
    bc = lambda p: jnp.broadcast_to(jnp.exp(p[None, :, None] * log_g), (RET_HEADS, p.shape[0], RET_KEY))
    cdec = jnp.broadcast_to(jnp.exp(RET_TC * log_g), (RET_HEADS, 1, RET_VAL))
    return dmat, bc(q_pow), bc(k_pow), bc(c_pow), cdec


def _rope_tables(n_tokens):
    pos_row = (jnp.arange(n_tokens) // GRID_W).astype(jnp.float32)
    pos_col = (jnp.arange(n_tokens) % GRID_W).astype(jnp.float32)
    nf = RET_KEY // 4
    inv_freq = ROPE_BASE ** (-jnp.arange(nf, dtype=jnp.float32) / nf)
    cos, sin = [], []
    for pos in (pos_row, pos_col):
        ang = pos[:, None] * inv_freq[None, :]
        cos += [jnp.cos(ang), jnp.cos(ang)]
        sin += [-jnp.sin(ang), jnp.sin(ang)]
    return jnp.concatenate(cos, axis=1), jnp.concatenate(sin, axis=1)


def _retention_direction(z, zc, cos, sin, gamma, reverse, prev):
    b, t, _ = z.shape
    n_ctx = zc.shape[1]
    nc = t // RET_TC
    dmat, qdec, kdec, kcdec, cdec = _retention_tables(gamma, reverse, n_ctx)
    ch = (lambda c: nc - 1 - c) if reverse else (lambda c: c)
    kq, kv = RET_QK // RET_KEY, (2 * RET_QK) // RET_VAL
    g_off = (2 * RET_QK + (2 if reverse else 1) * RET_VD) // RET_VAL
    in_specs = [
        pl.BlockSpec((1, RET_TC, RET_KEY), lambda bi, h, c: (bi, ch(c), h)),
        pl.BlockSpec((1, RET_TC, RET_KEY), lambda bi, h, c: (bi, ch(c), kq + h)),
        pl.BlockSpec((1, RET_TC, RET_VAL), lambda bi, h, c: (bi, ch(c), kv + h)),
        pl.BlockSpec((1, RET_TC, RET_VAL), lambda bi, h, c: (bi, ch(c), g_off + h)),
        pl.BlockSpec((1, n_ctx, RET_KEY), lambda bi, h, c: (bi, 0, h)),
        pl.BlockSpec((1, n_ctx, RET_VAL), lambda bi, h, c: (bi, 0, RET_QK // RET_VAL + h)),
        pl.BlockSpec((RET_TC, RET_KEY), lambda bi, h, c: (ch(c), 0)),
        pl.BlockSpec((RET_TC, RET_KEY), lambda bi, h, c: (ch(c), 0)),
        pl.BlockSpec((1, RET_TC, RET_TC), lambda bi, h, c: (h, 0, 0)),
        pl.BlockSpec((1, RET_TC, RET_KEY), lambda bi, h, c: (h, 0, 0)),
        pl.BlockSpec((1, RET_TC, RET_KEY), lambda bi, h, c: (h, 0, 0)),
        pl.BlockSpec((1, n_ctx, RET_KEY), lambda bi, h, c: (h, 0, 0)),
        pl.BlockSpec((1, 1, RET_VAL), lambda bi, h, c: (h, 0, 0)),
    ]
    args = [z, z, z, z, zc, zc, cos, sin, dmat, qdec, kdec, kcdec, cdec]
    if prev is not None:
        in_specs.append(pl.BlockSpec((1, RET_TC, RET_VAL), lambda bi, h, c: (bi, ch(c), h)))
        args.append(prev)
    return pl.pallas_call(
        functools.partial(_retention_kernel, has_prev=prev is not None),
        grid=(b, RET_HEADS, nc),
        in_specs=in_specs,
        out_specs=pl.BlockSpec((1, RET_TC, RET_VAL), lambda bi, h, c: (bi, ch(c), h)),
        out_shape=jax.ShapeDtypeStruct((b, t, RET_VD), jnp.float32 if prev is None else jnp.bfloat16),
        scratch_shapes=[pltpu.VMEM((RET_KEY, RET_VAL), jnp.float32)],
        compiler_params=pltpu.CompilerParams(dimension_semantics=("parallel", "parallel", "arbitrary"),
                                             vmem_limit_bytes=VMEM_LIMIT_BYTES),
        name="retention_bwd" if reverse else "retention_fwd",
    )(*args)


def retention_mixer(z, zc, decay_f, decay_b):
    cos, sin = _rope_tables(z.shape[1])
    gamma_f = 1.0 - jnp.exp2(-decay_f)
    gamma_b = 1.0 - jnp.exp2(-decay_b)
    part = _retention_direction(z, zc, cos, sin, gamma_b, True, None)
    return _retention_direction(z, zc, cos, sin, gamma_f, False, part)


def _moe_kernel(te_ref, tv_ref, x_ref, wg_ref, wu_ref, wd_ref, o_ref, acc_ref):
    i = pl.program_id(0)
    j = pl.program_id(1)

    @pl.when(tv_ref[i] > 0)
    def _():
        @pl.when(j == 0)
        def _():
            acc_ref[...] = jnp.zeros_like(acc_ref)
        x = x_ref[...].astype(jnp.bfloat16)
        g = jnp.dot(x, wg_ref[0].astype(jnp.bfloat16), preferred_element_type=jnp.float32)
        u = jnp.dot(x, wu_ref[0].astype(jnp.bfloat16), preferred_element_type=jnp.float32)
        a = (_silu(g) * u).astype(jnp.bfloat16)
        acc_ref[...] += jnp.dot(a, wd_ref[0].astype(jnp.bfloat16), preferred_element_type=jnp.float32)

    last = j == pl.num_programs(1) - 1

    @pl.when(jnp.logical_and(last, tv_ref[i] > 0))
    def _():
        o_ref[...] = acc_ref[...]

    @pl.when(jnp.logical_and(last, tv_ref[i] == 0))
    def _():
        o_ref[...] = jnp.zeros_like(o_ref)


def moe_experts(xb, tile_expert, tile_valid, w_gate, w_up, w_down):
    slots, d = xb.shape
    ff = w_gate.shape[2]
    n_tiles = slots // MOE_TM
    grid_spec = pltpu.PrefetchScalarGridSpec(
        num_scalar_prefetch=2,
        grid=(n_tiles, ff // MOE_TF),
        in_specs=[
            pl.BlockSpec((MOE_TM, d), lambda i, j, te, tv: (i, 0)),
            pl.BlockSpec((1, d, MOE_TF), lambda i, j, te, tv: (te[i], 0, j)),
            pl.BlockSpec((1, d, MOE_TF), lambda i, j, te, tv: (te[i], 0, j)),
            pl.BlockSpec((1, MOE_TF, d), lambda i, j, te, tv: (te[i], j, 0)),
        ],
        out_specs=pl.BlockSpec((MOE_TM, d), lambda i, j, te, tv: (i, 0)),
        scratch_shapes=[pltpu.VMEM((MOE_TM, d), jnp.float32)],
    )
    return pl.pallas_call(
        _moe_kernel,
        grid_spec=grid_spec,
        out_shape=jax.ShapeDtypeStruct((slots, d), jnp.float32),
        compiler_params=pltpu.CompilerParams(dimension_semantics=("arbitrary", "arbitrary"),
                                             vmem_limit_bytes=VMEM_LIMIT_BYTES),
        name="moe_experts",
    )(tile_expert, tile_valid, xb, w_gate, w_up, w_down)


def moe_swiglu(h, router, w_gate, w_up, w_down):
    b, t, d = h.shape
    n = b * t
    hf = h.reshape(n, d)
    logits = jnp.dot(hf, router, precision=lax.Precision.HIGHEST)
    top_val, top_idx = lax.top_k(logits, TOP_K)
    gate = jax.nn.softmax(top_val, axis=-1)
    flat_e = top_idx.reshape(-1)
    onehot = (flat_e[:, None] == jnp.arange(N_EXPERTS, dtype=flat_e.dtype)[None, :]).astype(jnp.int32)
    csum = jnp.cumsum(onehot, axis=0)
    counts = csum[-1]
    rank = jnp.sum((csum - onehot) * onehot, axis=1)
    padded = (counts + MOE_TM - 1) // MOE_TM * MOE_TM
    pad_end = jnp.cumsum(padded)
    pad_start = pad_end - padded
    slot = (pad_start[flat_e] + rank).astype(jnp.int32)
    n_tiles = (n * TOP_K) // MOE_TM + N_EXPERTS
    slots = n_tiles * MOE_TM
    tok = jnp.arange(n * TOP_K, dtype=jnp.int32) // TOP_K
    slot_tok = jnp.zeros((slots,), jnp.int32).at[slot].set(tok)
    tile_start = jnp.arange(n_tiles, dtype=jnp.int32) * MOE_TM
    tile_expert = jnp.minimum(jnp.sum(pad_end[None, :] <= tile_start[:, None], axis=1),
                              N_EXPERTS - 1).astype(jnp.int32)
    tile_valid = (tile_start < pad_end[-1]).astype(jnp.int32)
    yb = moe_experts(hf[slot_tok], tile_expert, tile_valid, w_gate, w_up, w_down)
    slot2 = slot.reshape(n, TOP_K)
    out = gate[:, 0:1] * yb[slot2[:, 0]] + gate[:, 1:2] * yb[slot2[:, 1]]
    return out.reshape(b, t, d)


def _row_select(i, tm, n_ctx, ctx_vec, lat_vec):
    if n_ctx == 0:
        return lat_vec
    row = i * tm + lax.broadcasted_iota(jnp.int32, (tm, 1), 0)
    return jnp.where(row < n_ctx, ctx_vec, lat_vec)


def _post_norm_rows(x, update, g, b):
    y = DEEPNORM_ALPHA * x + update
    mu = jnp.mean(y, axis=-1, keepdims=True)
    yc = y - mu
    var = jnp.mean(yc * yc, axis=-1, keepdims=True)
    return yc * lax.rsqrt(var + LN_EPS) * g + b


def _matmul_postnorm_kernel(a_ref, w_ref, x_ref, lat_ref, ctx_ref, g_ref, b_ref, o_ref, *, n_ctx):
    tm = a_ref.shape[1]
    o = jnp.dot(a_ref[0].astype(jnp.bfloat16), w_ref[...], preferred_element_type=jnp.float32)
    gate = _row_select(pl.program_id(1), tm, n_ctx, ctx_ref[...], lat_ref[0])
    o_ref[0] = _post_norm_rows(x_ref[0], gate * o, g_ref[...], b_ref[...])


def matmul_postnorm(a, w, x, gate_lat, gate_ctx, ln_g, ln_b, n_ctx):
    b, length, k = a.shape
    d = w.shape[1]
    tm = _pick_tile(length, (768, 512, 256))
    vec = pl.BlockSpec((1, d), lambda bi, i: (0, 0))
    return pl.pallas_call(
        functools.partial(_matmul_postnorm_kernel, n_ctx=n_ctx),
        grid=(b, length // tm),
        in_specs=[pl.BlockSpec((1, tm, k), lambda bi, i: (bi, i, 0)),
                  pl.BlockSpec((k, d), lambda bi, i: (0, 0)),
                  pl.BlockSpec((1, tm, d), lambda bi, i: (bi, i, 0)),
                  pl.BlockSpec((1, 1, d), lambda bi, i: (bi, 0, 0)), vec, vec, vec],
        out_specs=pl.BlockSpec((1, tm, d), lambda bi, i: (bi, i, 0)),
        out_shape=jax.ShapeDtypeStruct((b, length, d), jnp.float32),
        compiler_params=pltpu.CompilerParams(dimension_semantics=("parallel", "parallel"),
                                             vmem_limit_bytes=VMEM_LIMIT_BYTES),
        name="matmul_postnorm",
    )(a, w.astype(jnp.bfloat16), x, gate_lat[:, None, :], gate_ctx[None, :], ln_g[None, :], ln_b[None, :])


def _ffn_postnorm_kernel(x_ref, lat_ref, ctx_ref, wg_ref, wu_ref, wd_ref, g_ref, b_ref, o_ref, h_ref, acc_ref,
                         *, n_ctx):
    i = pl.program_id(1)
    j = pl.program_id(2)
    tm = x_ref.shape[1]

    def vec(k):
        return _row_select(i, tm, n_ctx, ctx_ref[k:k + 1, :], lat_ref[0, k:k + 1, :])

    @pl.when(j == 0)
    def _():
        h_ref[...] = (x_ref[0] * (1.0 + vec(1)) + vec(0)).astype(jnp.bfloat16)
        acc_ref[...] = jnp.zeros_like(acc_ref)

    h = h_ref[...]
    gt = jnp.dot(h, wg_ref[...], preferred_element_type=jnp.float32)
    up = jnp.dot(h, wu_ref[...], preferred_element_type=jnp.float32)
    acc_ref[...] += jnp.dot((_silu(gt) * up).astype(jnp.bfloat16), wd_ref[...], preferred_element_type=jnp.float32)

    @pl.when(j == pl.num_programs(2) - 1)
    def _():
        o_ref[0] = _post_norm_rows(x_ref[0], vec(2) * acc_ref[...], g_ref[...], b_ref[...])


def ffn_postnorm(x, mod_lat, mod_ctx, w_gate, w_up, w_down, ln_g, ln_b, n_ctx):
    b, length, d = x.shape
    ff = w_gate.shape[1]
    tm = _pick_tile(length, (384, 512, 256))
    tf = _pick_tile(ff, (1408, 512, 256))
    vec = pl.BlockSpec((1, d), lambda bi, i, j: (0, 0))
    return pl.pallas_call(
        functools.partial(_ffn_postnorm_kernel, n_ctx=n_ctx),
        grid=(b, length // tm, ff // tf),
        in_specs=[pl.BlockSpec((1, tm, d), lambda bi, i, j: (bi, i, 0)),
                  pl.BlockSpec((1, 3, d), lambda bi, i, j: (bi, 0, 0)),
                  pl.BlockSpec((3, d), lambda bi, i, j: (0, 0)),
                  pl.BlockSpec((d, tf), lambda bi, i, j: (0, j)),
                  pl.BlockSpec((d, tf), lambda bi, i, j: (0, j)),
                  pl.BlockSpec((tf, d), lambda bi, i, j: (j, 0)), vec, vec],
        out_specs=pl.BlockSpec((1, tm, d), lambda bi, i, j: (bi, i, 0)),
        out_shape=jax.ShapeDtypeStruct((b, length, d), jnp.float32),
        scratch_shapes=[pltpu.VMEM((tm, d), jnp.bfloat16), pltpu.VMEM((tm, d), jnp.float32)],
        compiler_params=pltpu.CompilerParams(dimension_semantics=("parallel", "parallel", "arbitrary"),
                                             vmem_limit_bytes=VMEM_LIMIT_BYTES),
        name="ffn_postnorm",
    )(x, mod_lat, mod_ctx, w_gate.astype(jnp.bfloat16), w_up.astype(jnp.bfloat16), w_down.astype(jnp.bfloat16),
      ln_g[None, :], ln_b[None, :])


def _even_layer(x, xc, sc, scc, mod_w, mod_b, w_in, mla_q_norm, mla_wq_up, mla_kv_norm, mla_wkv_up,
                rwkv_conv, rwkv_w0_f, rwkv_w2_f, rwkv_w0_b, rwkv_w2_b, rwkv_a0_f, rwkv_a2_f, rwkv_a0_b,
                rwkv_a2_b, rwkv_g2, rwkv_k_k, rwkv_k_a, rwkv_r_k, rwkv_lnx_g, rwkv_lnx_b, w_out,
                ln1_g, ln1_b, ffn_w_gate, ffn_w_up, ffn_w_down, ln2_g, ln2_b):
    n_ctx = xc.shape[1]
    m = _adaln(sc, mod_w, mod_b)
    mc = _adaln(scc, mod_w, mod_b)
    xa = jnp.concatenate([xc, x], axis=1)
    is_ctx = (jnp.arange(xa.shape[1]) < n_ctx)[None, :, None]

    def rows(k):
        return jnp.where(is_ctx, mc[k][None, None, :], m[k][:, None, :])

    h = _modulate(xa, rows(0), rows(1)).astype(jnp.bfloat16)
    o_mla = _mla_mixer(mm(h, w_in[:, :MLA_IN]), n_ctx, mla_q_norm, mla_wq_up, mla_kv_norm, mla_wkv_up)
    o_rwkv = _rwkv7_mixer(mm(h, w_in[:, MLA_IN:]), n_ctx, rwkv_conv, rwkv_w0_f, rwkv_w2_f, rwkv_w0_b, rwkv_w2_b,
                          rwkv_a0_f, rwkv_a2_f, rwkv_a0_b, rwkv_a2_b, rwkv_g2, rwkv_k_k, rwkv_k_a,
                          rwkv_r_k, rwkv_lnx_g, rwkv_lnx_b)
    mix = jnp.concatenate([o_mla, o_rwkv], axis=-1).astype(jnp.bfloat16)
    xa = matmul_postnorm(mix, w_out, xa, m[2], mc[2], ln1_g, ln1_b, n_ctx)
    xa = ffn_postnorm(xa, jnp.stack(m[3:6], axis=1), jnp.stack(mc[3:6], axis=0),
                      ffn_w_gate, ffn_w_up, ffn_w_down, ln2_g, ln2_b, n_ctx)
    return xa[:, n_ctx:], xa[:, :n_ctx]


def _odd_layer(x, xc, sc, scc, mod_w, mod_b, w_in, ret_decay_f, ret_decay_b, w_out, ln1_g, ln1_b,
               router, moe_w_gate, moe_w_up, moe_w_down, ln2_g, ln2_b):
    m = _adaln(sc, mod_w, mod_b)
    c_shift, c_scale = jnp.split(
        jnp.dot(scc, mod_w[:, :2 * D_MODEL], precision=lax.Precision.HIGHEST) + mod_b[:2 * D_MODEL], 2)
    z = mm(_modulate(x, m[0][:, None, :], m[1][:, None, :]), w_in)
    zc = mm(_modulate(xc, c_shift, c_scale), w_in[:, RET_QK:2 * RET_QK + RET_VD])
    mix = retention_mixer(z, zc, ret_decay_f, ret_decay_b)
    x = matmul_postnorm(mix, w_out, x, m[2], jnp.zeros_like(m[2][0]), ln1_g, ln1_b, 0)
    y = moe_swiglu(_modulate(x, m[3][:, None, :], m[4][:, None, :]), router, moe_w_gate, moe_w_up, moe_w_down)
    return _post_norm(x, m[5][:, None, :] * y, ln2_g, ln2_b)


def kernel(x, c, ctx, c_ctx, l0_mod_w, l0_mod_b, l0_w_in, l0_mla_q_norm, l0_mla_wq_up, l0_mla_kv_norm, l0_mla_wkv_up, l0_rwkv_conv, l0_rwkv_w0_f, l0_rwkv_w2_f, l0_rwkv_w0_b, l0_rwkv_w2_b, l0_rwkv_a0_f, l0_rwkv_a2_f, l0_rwkv_a0_b, l0_rwkv_a2_b, l0_rwkv_g2, l0_rwkv_k_k, l0_rwkv_k_a, l0_rwkv_r_k, l0_rwkv_lnx_g, l0_rwkv_lnx_b, l0_w_out, l0_ln1_g, l0_ln1_b, l0_ffn_w_gate, l0_ffn_w_up, l0_ffn_w_down, l0_ln2_g, l0_ln2_b, l1_mod_w, l1_mod_b, l1_w_in, l1_ret_decay_f, l1_ret_decay_b, l1_w_out, l1_ln1_g, l1_ln1_b, l1_router, l1_moe_w_gate, l1_moe_w_up, l1_moe_w_down, l1_ln2_g, l1_ln2_b):
    even_params = (l0_mod_w, l0_mod_b, l0_w_in, l0_mla_q_norm, l0_mla_wq_up, l0_mla_kv_norm, l0_mla_wkv_up,
                   l0_rwkv_conv, l0_rwkv_w0_f, l0_rwkv_w2_f, l0_rwkv_w0_b, l0_rwkv_w2_b, l0_rwkv_a0_f,
                   l0_rwkv_a2_f, l0_rwkv_a0_b, l0_rwkv_a2_b, l0_rwkv_g2, l0_rwkv_k_k, l0_rwkv_k_a, l0_rwkv_r_k,
                   l0_rwkv_lnx_g, l0_rwkv_lnx_b, l0_w_out, l0_ln1_g, l0_ln1_b, l0_ffn_w_gate, l0_ffn_w_up,
                   l0_ffn_w_down, l0_ln2_g, l0_ln2_b)
    odd_params = (l1_mod_w, l1_mod_b, l1_w_in, l1_ret_decay_f, l1_ret_decay_b, l1_w_out, l1_ln1_g, l1_ln1_b,
                  l1_router, l1_moe_w_gate, l1_moe_w_up, l1_moe_w_down, l1_ln2_g, l1_ln2_b)
    sc = _silu(c)
    scc = _silu(c_ctx)
    x, xc = _even_layer(x, ctx, sc, scc, *even_params)
    return _odd_layer(x, xc, sc, scc, *odd_params)
```

```python
import functools

import jax
import jax.numpy as jnp
import numpy as np
from jax import lax
from jax.experimental import pallas as pl
from jax.experimental.pallas import tpu as pltpu

D_MODEL = 1024
DEPTH = 2
GRID_W = 64
ROPE_BASE = 10000.0
DEEPNORM_ALPHA = (2 * DEPTH) ** 0.25
LN_EPS = 1e-5

MLA_HEADS = 8
MLA_Q_RANK = 256
MLA_KV_RANK = 128
MLA_NOPE = 64
MLA_ROPE = 32
MLA_V = 64
MLA_IN = MLA_Q_RANK + MLA_KV_RANK + MLA_ROPE
MLA_OUT = MLA_HEADS * MLA_V

RWKV_HEADS = 8
RWKV_HEAD = 64
RWKV_DIM = RWKV_HEADS * RWKV_HEAD
RWKV_DECAY_LORA = 64
RWKV_AAA_LORA = 64
RWKV_GATE_LORA = 128
RWKV_LNX_EPS = 64e-5

RET_HEADS = 4
RET_KEY = 256
RET_VAL = 512
RET_TC = 256
RET_QK = RET_HEADS * RET_KEY
RET_VD = RET_HEADS * RET_VAL

N_EXPERTS = 8
TOP_K = 2
MOE_TM = 1024
MOE_TF = 512

LANES = 128
VMEM_LIMIT_BYTES = 48 * 1024 * 1024


def _matmul_kernel(x_ref, w_ref, o_ref):
    o_ref[...] = jnp.dot(x_ref[...].astype(jnp.bfloat16), w_ref[...], preferred_element_type=jnp.float32)


def _pick_tile(n, candidates):
    for c in candidates:
        if n % c == 0:
            return c
    raise ValueError(f"no tile for {n}")


def pmatmul(x, w):
    m, k = x.shape
    n = w.shape[1]
    n_pad = (-n) % LANES
    w = w.astype(jnp.bfloat16)
    if n_pad:
        w = jnp.pad(w, ((0, 0), (0, n_pad)))
    np_ = n + n_pad
    tm = _pick_tile(m, (1024, 512, 256, 128, 8) if k <= 1024 else (512, 256, 128, 8))
    tn = _pick_tile(np_, (1024, 768, 640, 512, 384, 256, 128))
    out = pl.pallas_call(
        _matmul_kernel,
        grid=(m // tm, np_ // tn),
        in_specs=[pl.BlockSpec((tm, k), lambda i, j: (i, 0)),
                  pl.BlockSpec((k, tn), lambda i, j: (0, j))],
        out_specs=pl.BlockSpec((tm, tn), lambda i, j: (i, j)),
        out_shape=jax.ShapeDtypeStruct((m, np_), jnp.float32),
        compiler_params=pltpu.CompilerParams(
            dimension_semantics=("parallel", "parallel"),
            vmem_limit_bytes=VMEM_LIMIT_BYTES),
        name="matmul",
    )(x, w)
    return out[:, :n] if n_pad else out


def mm(x, w):
    lead = x.shape[:-1]
    out = pmatmul(x.reshape(-1, x.shape[-1]).astype(jnp.bfloat16), w)
    return out.reshape(*lead, w.shape[1])


def _silu(t):
    return t * jax.nn.sigmoid(t)


def _normalize(t, eps):
    mu = jnp.mean(t, axis=-1, keepdims=True)
    var = jnp.mean(jnp.square(t - mu), axis=-1, keepdims=True)
    return (t - mu) * lax.rsqrt(var + eps)


def _layer_norm(t, g, b):
    return _normalize(t, LN_EPS) * g + b


def _rms_norm(t, g, eps=1e-6):
    return t * lax.rsqrt(jnp.mean(t * t, axis=-1, keepdims=True) + eps) * g


def _l2_normalize(t, eps=1e-12):
    return t / jnp.maximum(jnp.linalg.norm(t, axis=-1, keepdims=True), eps)


def _post_norm(x, update, g, b):
    return _layer_norm(DEEPNORM_ALPHA * x + update, g, b)


def _modulate(h, shift, scale):
    return h * (1.0 + scale) + shift


def _adaln(cond, mod_w, mod_b):
    return jnp.split(jnp.dot(cond, mod_w, precision=lax.Precision.HIGHEST) + mod_b, 6, axis=-1)


def _grid_positions(n_tokens):
    rows = n_tokens // GRID_W
    row = jnp.repeat(jnp.arange(rows, dtype=jnp.float32), GRID_W)
    col = jnp.tile(jnp.arange(GRID_W, dtype=jnp.float32), rows)
    return row, col


SOFTMAX_FLOOR = -1e30


FLASH_ROW_GROUPS = 4


def _flash_kernel(q_ref, kt_ref, v_ref, o_ref, m_ref, acc_ref, *, dv):
    j = pl.program_id(3)

    @pl.when(j == 0)
    def _():
        m_ref[...] = jnp.full_like(m_ref, SOFTMAX_FLOOR)
        acc_ref[...] = jnp.zeros_like(acc_ref)

    rows = q_ref.shape[2] // FLASH_ROW_GROUPS
    kt = kt_ref[0, 0]
    v = v_ref[0, 0]
    for u in range(FLASH_ROW_GROUPS):
        sl = slice(u * rows, (u + 1) * rows)
        s = jnp.dot(q_ref[0, 0, sl, :], kt, preferred_element_type=jnp.float32)
        m_prev = m_ref[sl, :]
        m_new = jnp.maximum(m_prev, jnp.max(s, axis=-1, keepdims=True))
        alpha = jnp.exp(m_prev - m_new)
        p = jnp.exp(s - m_new).astype(jnp.bfloat16)
        acc_ref[sl, :] = alpha * acc_ref[sl, :] + jnp.dot(p, v, preferred_element_type=jnp.float32)
        m_ref[sl, :] = m_new

    @pl.when(j == pl.num_programs(3) - 1)
    def _():
        acc = acc_ref[...]
        o_ref[0, 0] = acc[:, :dv] / acc[:, dv:dv + 1]


def flash_attention(q, kt, v1, dv, tq, tk):
    b, h, t, dq = q.shape
    s = kt.shape[3]
    return pl.pallas_call(
        functools.partial(_flash_kernel, dv=dv),
        grid=(b, h, t // tq, s // tk),
        in_specs=[pl.BlockSpec((1, 1, tq, dq), lambda bi, hi, i, j: (bi, hi, i, 0)),
                  pl.BlockSpec((1, 1, dq, tk), lambda bi, hi, i, j: (bi, hi, 0, j)),
                  pl.BlockSpec((1, 1, tk, LANES), lambda bi, hi, i, j: (bi, hi, j, 0))],
        out_specs=pl.BlockSpec((1, 1, tq, dv), lambda bi, hi, i, j: (bi, hi, i, 0)),
        out_shape=jax.ShapeDtypeStruct((b, h, t, dv), jnp.float32),
        scratch_shapes=[pltpu.VMEM((tq, 1), jnp.float32), pltpu.VMEM((tq, LANES), jnp.float32)],
        compiler_params=pltpu.CompilerParams(
            dimension_semantics=("parallel", "parallel", "parallel", "arbitrary"),
            vmem_limit_bytes=VMEM_LIMIT_BYTES),
        name="flash_attention",
    )(q, kt, v1)


def _block_attention(q, k, v):
    dq = q.shape[-1]
    dv = v.shape[-1]
    b, s, h, _ = k.shape
    qh = jnp.transpose(q * dq ** -0.5, (0, 2, 1, 3)).astype(jnp.bfloat16)
    kt = jnp.transpose(k, (0, 2, 3, 1)).astype(jnp.bfloat16)
    v1 = jnp.concatenate([v, jnp.ones((b, s, h, 1), v.dtype), jnp.zeros((b, s, h, LANES - dv - 1), v.dtype)],
                         axis=-1)
    v1 = jnp.transpose(v1, (0, 2, 1, 3)).astype(jnp.bfloat16)
    tq = _pick_tile(q.shape[1], (1024, 256))
    o = flash_attention(qh, kt, v1, dv, tq, s)
    return jnp.transpose(o, (0, 2, 1, 3))


_Q8 = MLA_ROPE // 4
ROPE_PARTNER = np.concatenate([np.arange(_Q8, 2 * _Q8), np.arange(0, _Q8),
                               np.arange(3 * _Q8, 4 * _Q8), np.arange(2 * _Q8, 3 * _Q8)])
ROPE_SIGN = np.concatenate([-np.ones(_Q8), np.ones(_Q8), -np.ones(_Q8), np.ones(_Q8)]).astype(np.float32)


def _mla_rope_tables(n_ctx, t):
    row, col = _grid_positions(t)
    zero = jnp.zeros((n_ctx,), jnp.float32)
    row, col = jnp.concatenate([zero, row]), jnp.concatenate([zero, col])
    inv_freq = ROPE_BASE ** (-jnp.arange(_Q8, dtype=jnp.float32) / _Q8)
    ang = jnp.concatenate([row[:, None] * inv_freq] * 2 + [col[:, None] * inv_freq] * 2, axis=1)
    return jnp.cos(ang), jnp.sin(ang)


def _mla_mixer(h, n_ctx, w_in, q_norm, wq_up, kv_norm, wkv_up):
    b, length, _ = h.shape
    cos, sin = _mla_rope_tables(n_ctx, length - n_ctx)
    k_cols = np.arange(MLA_Q_RANK + MLA_KV_RANK, MLA_IN)
    w_z = jnp.concatenate([w_in[:, :MLA_IN], w_in[:, k_cols[ROPE_PARTNER]] * ROPE_SIGN], axis=1)
    z = mm(h, w_z)
    cq, ckv = z[..., :MLA_Q_RANK], z[..., MLA_Q_RANK:MLA_Q_RANK + MLA_KV_RANK]
    k_rope = z[..., k_cols[0]:MLA_IN] * cos + z[..., MLA_IN:] * sin
    dq = MLA_NOPE + MLA_ROPE
    head_cols = np.arange(MLA_HEADS)[:, None] * dq
    q_rope_cols = (head_cols + MLA_NOPE + np.arange(MLA_ROPE)[None, :])
    wq_partner = jnp.zeros_like(wq_up).at[:, q_rope_cols.reshape(-1)].set(
        wq_up[:, (head_cols + MLA_NOPE + ROPE_PARTNER[None, :]).reshape(-1)] * jnp.tile(ROPE_SIGN, MLA_HEADS))
    q2 = mm(_rms_norm(cq, q_norm), jnp.concatenate([wq_up, wq_partner], axis=1))
    one, nil = jnp.ones((length, MLA_NOPE), jnp.float32), jnp.zeros((length, MLA_NOPE), jnp.float32)
    cos_q = jnp.tile(jnp.concatenate([one, cos], axis=1), (1, MLA_HEADS))
    sin_q = jnp.tile(jnp.concatenate([nil, sin], axis=1), (1, MLA_HEADS))
    q = (q2[..., :MLA_HEADS * dq] * cos_q + q2[..., MLA_HEADS * dq:] * sin_q).reshape(b, length, MLA_HEADS, dq)
    kv = mm(_rms_norm(ckv, kv_norm), wkv_up).reshape(b, length, MLA_HEADS, MLA_NOPE + MLA_V)
    k = jnp.concatenate([kv[..., :MLA_NOPE],
                         jnp.broadcast_to(k_rope[:, :, None, :], (b, length, MLA_HEADS, MLA_ROPE))], axis=-1)
    v = kv[..., MLA_NOPE:]
    o = _block_attention(q[:, n_ctx:], k, v)
    oc = _block_attention(q[:, :n_ctx], k[:, :n_ctx], v[:, :n_ctx])
    return jnp.concatenate([oc, o], axis=1).reshape(b, length, MLA_OUT)


SCAN_BLOCK = 128
SCAN_UNROLL = 4
N_PAIRS = RWKV_HEADS // 2
PAIR_ROWS = N_PAIRS * RWKV_HEAD


def _to_state_tiles(vblk):
    halves = [[], []]
    for p in range(N_PAIRS):
        tr = vblk[:, p * LANES:(p + 1) * LANES].T
        for half in range(2):
            c = slice(half * RWKV_HEAD, (half + 1) * RWKV_HEAD)
            halves[half].append(jnp.concatenate([tr[:RWKV_HEAD, c], tr[RWKV_HEAD:, c]], axis=1))
    return [jnp.concatenate(h, axis=0) for h in halves]


def _from_state_tiles(tiles):
    cols = []
    for p in range(N_PAIRS):
        r = slice(p * RWKV_HEAD, (p + 1) * RWKV_HEAD)
        tr = jnp.concatenate([tiles[0][r, :], tiles[1][r, :]], axis=0).T
        cols.append(jnp.concatenate(
            [jnp.concatenate([tr[:RWKV_HEAD, :RWKV_HEAD], tr[:RWKV_HEAD, RWKV_HEAD:]], axis=0),
             jnp.concatenate([tr[RWKV_HEAD:, :RWKV_HEAD], tr[RWKV_HEAD:, RWKV_HEAD:]], axis=0)], axis=1))
    return jnp.concatenate(cols, axis=1)


def _scan_kernel(rf_ref, rb_ref, af_ref, ab_ref, vf_ref, vb_ref, wf_ref, wb_ref, kf_ref, kb_ref,
                 bf_ref, bb_ref, wred_ref, yf_ref, yb_ref, s_ref, vt_ref, yc_ref):
    nb = rf_ref.shape[0]

    @pl.when(pl.program_id(0) == 0)
    def _():
        s_ref[...] = jnp.zeros_like(s_ref)

    lane = lax.broadcasted_iota(jnp.int32, (PAIR_ROWS, LANES), 1)
    wred = wred_ref[...]
    refs = ((rf_ref, af_ref, vf_ref, wf_ref, kf_ref, bf_ref, yf_ref),
            (rb_ref, ab_ref, vb_ref, wb_ref, kb_ref, bb_ref, yb_ref))

    for d in range(2):
        for bi in range(nb):
            tiles = _to_state_tiles(refs[d][2][bi])
            for half in range(2):
                vt_ref[d, bi, half] = tiles[half].astype(jnp.bfloat16)

    def rows(ref, bi, t):
        row = ref[bi, pl.ds(t, 1), :]
        return jnp.concatenate(
            [jnp.broadcast_to(row[:, p * LANES:(p + 1) * LANES], (RWKV_HEAD, LANES)) for p in range(N_PAIRS)],
            axis=0)

    def collect(d, bi, t, yb, valid):
        mask = jnp.logical_and(lane % RWKV_HEAD == t % RWKV_HEAD, valid)
        half = t // RWKV_HEAD
        yc_ref[d, bi, half] = jnp.where(mask, yb, yc_ref[d, bi, half])

    def body(i, carry):
        for d in range(2):
            r_ref, a_ref, _, w_ref, k_ref, b_ref, _ = refs[d]
            t = i if d == 0 else SCAN_BLOCK - 1 - i
            tp = jnp.maximum(i - 1, 0) if d == 0 else jnp.minimum(SCAN_BLOCK - i, SCAN_BLOCK - 1)
            sel = lane % RWKV_HEAD == t % RWKV_HEAD
            prs = []
            for bi in range(nb):
                s = s_ref[d, bi]
                pa = (s * rows(a_ref, bi, t)).astype(jnp.bfloat16)
                prs.append((s * rows(r_ref, bi, tp)).astype(jnp.bfloat16))
                pv = jnp.where(sel, vt_ref[d, bi, t // RWKV_HEAD], jnp.zeros((), jnp.bfloat16))
                red = jnp.dot(jnp.concatenate([pa, pv], axis=1), wred, preferred_element_type=jnp.float32)
                s_ref[d, bi] = (s * rows(w_ref, bi, t) + red[:, :LANES] * rows(b_ref, bi, t)
                                + red[:, LANES:] * rows(k_ref, bi, t))
            for b0 in range(0, nb, 2):
                ys = jnp.dot(jnp.concatenate(prs[b0:b0 + 2], axis=1), wred, preferred_element_type=jnp.float32)
                collect(d, b0, tp, ys[:, :LANES], i >= 1)
                collect(d, b0 + 1, tp, ys[:, LANES:], i >= 1)
        return carry

    lax.fori_loop(0, SCAN_BLOCK, body, 0, unroll=SCAN_UNROLL)

    for d in range(2):
        r_ref, y_ref = refs[d][0], refs[d][6]
        t_last = SCAN_BLOCK - 1 if d == 0 else 0
        for b0 in range(0, nb, 2):
            prs = [(s_ref[d, bi] * rows(r_ref, bi, t_last)).astype(jnp.bfloat16) for bi in (b0, b0 + 1)]
            ys = jnp.dot(jnp.concatenate(prs, axis=1), wred, preferred_element_type=jnp.float32)
            collect(d, b0, t_last, ys[:, :LANES], True)
            collect(d, b0 + 1, t_last, ys[:, LANES:], True)
        for bi in range(nb):
            y_ref[bi] = _from_state_tiles([yc_ref[d, bi, 0], yc_ref[d, bi, 1]])


def rwkv_scan(r, v, a, w_f, k_f, b_f, w_b, k_b, b_b, n_ctx):
    nb, length, _ = r.shape
    nblk = length // SCAN_BLOCK
    nblk_ctx = n_ctx // SCAN_BLOCK
    j = np.arange(2 * LANES)
    wred = (j[:, None] // RWKV_HEAD) == (j[None, :] // RWKV_HEAD)

    def fwd(i):
        return i

    def bwd(i):
        return jnp.where(i < nblk_ctx, nblk_ctx - 1 - i, nblk + nblk_ctx - 1 - i)

    def row_spec(blk):
        return pl.BlockSpec((nb, SCAN_BLOCK, RWKV_DIM), lambda i: (0, blk(i), 0))

    y_shape = jax.ShapeDtypeStruct((nb, length, RWKV_DIM), jnp.float32)
    tile_shape = (2, nb, 2, PAIR_ROWS, LANES)
    return pl.pallas_call(
        _scan_kernel,
        grid=(nblk,),
        in_specs=[row_spec(fwd), row_spec(bwd)] * 6 + [pl.BlockSpec((2 * LANES, 2 * LANES), lambda i: (0, 0))],
        out_specs=[row_spec(fwd), row_spec(bwd)],
        out_shape=[y_shape, y_shape],
        scratch_shapes=[pltpu.VMEM((2, nb, PAIR_ROWS, LANES), jnp.float32),
                        pltpu.VMEM(tile_shape, jnp.bfloat16), pltpu.VMEM(tile_shape, jnp.float32)],
        compiler_params=pltpu.CompilerParams(dimension_semantics=("arbitrary",),
                                             vmem_limit_bytes=VMEM_LIMIT_BYTES),
        name="rwkv_scan",
    )(r, r, a, a, v, v, w_f, w_b, k_f, k_b, b_f, b_b, jnp.asarray(wred, jnp.bfloat16))


def _rwkv7_mixer(z, n_ctx, conv_w, w0_f, w2_f, w0_b, w2_b, a0_f, a2_f, a0_b, a2_b, g2, k_k, k_a,
                 r_k, lnx_g, lnx_b):
    b, length, _ = z.shape
    c0 = 3 * RWKV_DIM
    c1 = c0 + RWKV_DECAY_LORA
    c2 = c1 + RWKV_DECAY_LORA
    c3 = c2 + RWKV_AAA_LORA
    c4 = c3 + RWKV_AAA_LORA
    rkv_raw, wd_f, wd_b, ad_f, ad_b, gd = jnp.split(z, [c0, c1, c2, c3, c4], axis=-1)
    pos = jnp.arange(length)
    has_prev = ((pos != 0) & (pos != n_ctx)).astype(jnp.float32)[None, :, None]
    has_next = ((pos != n_ctx - 1) & (pos != length - 1)).astype(jnp.float32)[None, :, None]
    tp = jnp.pad(rkv_raw, ((0, 0), (1, 1), (0, 0)))
    rkv = tp[:, :-2] * has_prev * conv_w[0] + tp[:, 1:-1] * conv_w[1] + tp[:, 2:] * has_next * conv_w[2]
    r, k, v = jnp.split(rkv, 3, axis=-1)
    g = mm(jax.nn.sigmoid(gd), g2)

    def heads(u):
        return u.reshape(b, length, RWKV_HEADS, RWKV_HEAD)

    kk = _l2_normalize(heads(k * k_k)).reshape(b, length, RWKV_DIM)

    def direction(wd, w0, w2, ad, a0, a2):
        logw = -jax.nn.softplus(-(w0 + mm(jnp.tanh(wd), w2))) - 0.5
        decay = jnp.exp(-jnp.exp(logw))
        lr = jax.nn.sigmoid(a0 + mm(ad, a2))
        kd = k * (1.0 + (lr - 1.0) * k_a)
        return decay, kd, kk * lr

    w_f, k_f, b_f = direction(wd_f, w0_f, w2_f, ad_f, a0_f, a2_f)
    w_b, k_b, b_b = direction(wd_b, w0_b, w2_b, ad_b, a0_b, a2_b)
    y_f, y_b = rwkv_scan(r, v, -kk, w_f, k_f, b_f, w_b, k_b, b_b, n_ctx)
    y = _normalize(heads(y_f + y_b), RWKV_LNX_EPS).reshape(b, length, RWKV_DIM) * lnx_g + lnx_b
    bonus = jnp.sum(heads(r * (k_f + k_b)) * r_k, axis=-1, keepdims=True) * heads(v)
    return (y + bonus.reshape(b, length, RWKV_DIM)) * g


def _rope_halves(t, cos, sin):
    parts = []
    for s in range(2):
        u = t[:, s * LANES:(s + 1) * LANES]
        parts.append(u * cos[:, s * LANES:(s + 1) * LANES]
                     + pltpu.roll(u, LANES // 2, axis=1) * sin[:, s * LANES:(s + 1) * LANES])
    return jnp.concatenate(parts, axis=1)


def _retention_kernel(*refs, has_prev):
    if has_prev:
        (q_ref, k_ref, v_ref, g_ref, kc_ref, vc_ref, cos_ref, sin_ref, dmat_ref, qdec_ref, kdec_ref,
         kcdec_ref, cdec_ref, prev_ref, o_ref, s_ref) = refs
    else:
        (q_ref, k_ref, v_ref, g_ref, kc_ref, vc_ref, cos_ref, sin_ref, dmat_ref, qdec_ref, kdec_ref,
         kcdec_ref, cdec_ref, o_ref, s_ref) = refs
        prev_ref = None
    scale = RET_KEY ** -0.5

    @pl.when(pl.program_id(2) == 0)
    def _():
        kc = (kc_ref[0] * kcdec_ref[0] * scale).astype(jnp.bfloat16)
        s_ref[...] = lax.dot_general(kc, vc_ref[0].astype(jnp.bfloat16), (((0,), (0,)), ((), ())),
                                     preferred_element_type=jnp.float32)

    cos = cos_ref[...]
    sin = sin_ref[...]
    q = _rope_halves(q_ref[0], cos, sin)
    k = _rope_halves(k_ref[0], cos, sin) * scale
    v = v_ref[0].astype(jnp.bfloat16)
    s = s_ref[...]
    att = lax.dot_general(q.astype(jnp.bfloat16), k.astype(jnp.bfloat16), (((1,), (1,)), ((), ())),
                          preferred_element_type=jnp.float32) * dmat_ref[0]
    o = (jnp.dot(att.astype(jnp.bfloat16), v, preferred_element_type=jnp.float32)
         + jnp.dot((q * qdec_ref[0]).astype(jnp.bfloat16), s.astype(jnp.bfloat16),
                   preferred_element_type=jnp.float32))
    s_ref[...] = s * cdec_ref[0] + lax.dot_general((k * kdec_ref[0]).astype(jnp.bfloat16), v,
                                                   (((0,), (0,)), ((), ())),
                                                   preferred_element_type=jnp.float32)
    mu = jnp.mean(o, axis=-1, keepdims=True)
    oc = o - mu
    var = jnp.mean(oc * oc, axis=-1, keepdims=True)
    out = _silu(g_ref[0]) * (oc * lax.rsqrt(var + 1e-6))
    if prev_ref is not None:
        out = out + prev_ref[0]
    o_ref[0] = out.astype(o_ref.dtype)


def _retention_tables(gamma, reverse, n_ctx):
    log_g = jnp.log(gamma)[:, None, None]
    i = jnp.arange(RET_TC, dtype=jnp.float32)
    rel = (i[None, :] - i[:, None]) if reverse else (i[:, None] - i[None, :])
    dmat = jnp.where(rel >= 0, jnp.exp(jnp.maximum(rel, 0.0)[None] * log_g), 0.0)
    q_pow = (RET_TC - i) if reverse else (i + 1.0)
    k_pow = i if reverse else (RET_TC - 1.0 - i)
    m = jnp.arange(n_ctx, dtype=jnp.float32)
    c_pow = m if reverse else (n_ctx - 1.0 - m)
    bc = lambda p: jnp.broadcast_to(jnp.exp(p[None, :, None] * log_g), (RET_HEADS, p.shape[0], RET_KEY))
    cdec = jnp.broadcast_to(jnp.exp(RET_TC * log_g), (RET_HEADS, 1, RET_VAL))
    return dmat, bc(q_pow), bc(k_pow), bc(c_pow), cdec


def _rope_tables(n_tokens):
    pos_row = (jnp.arange(n_tokens) // GRID_W).astype(jnp.float32)
    pos_col = (jnp.arange(n_tokens) % GRID_W).astype(jnp.float32)
    nf = RET_KEY // 4
    inv_freq = ROPE_BASE ** (-jnp.arange(nf, dtype=jnp.float32) / nf)
    cos, sin = [], []
    for pos in (pos_row, pos_col):
        ang = pos[:, None] * inv_freq[None, :]
        cos += [jnp.cos(ang), jnp.cos(ang)]
        sin += [-jnp.sin(ang), jnp.sin(ang)]
    return jnp.concatenate(cos, axis=1), jnp.concatenate(sin, axis=1)


def _retention_direction(z, zc, cos, sin, gamma, reverse, prev):
    b, t, _ = z.shape
    n_ctx = zc.shape[1]
    nc = t // RET_TC
    dmat, qdec, kdec, kcdec, cdec = _retention_tables(gamma, reverse, n_ctx)
    ch = (lambda c: nc - 1 - c) if reverse else (lambda c: c)
    kq, kv = RET_QK // RET_KEY, (2 * RET_QK) // RET_VAL
    g_off = (2 * RET_QK + (2 if reverse else 1) * RET_VD) // RET_VAL
    in_specs = [
        pl.BlockSpec((1, RET_TC, RET_KEY), lambda bi, h, c: (bi, ch(c), h)),
        pl.BlockSpec((1, RET_TC, RET_KEY), lambda bi, h, c: (bi, ch(c), kq + h)),
        pl.BlockSpec((1, RET_TC, RET_VAL), lambda bi, h, c: (bi, ch(c), kv + h)),
        pl.BlockSpec((1, RET_TC, RET_VAL), lambda bi, h, c: (bi, ch(c), g_off + h)),
        pl.BlockSpec((1, n_ctx, RET_KEY), lambda bi, h, c: (bi, 0, h)),
        pl.BlockSpec((1, n_ctx, RET_VAL), lambda bi, h, c: (bi, 0, RET_QK // RET_VAL + h)),
        pl.BlockSpec((RET_TC, RET_KEY), lambda bi, h, c: (ch(c), 0)),
        pl.BlockSpec((RET_TC, RET_KEY), lambda bi, h, c: (ch(c), 0)),
        pl.BlockSpec((1, RET_TC, RET_TC), lambda bi, h, c: (h, 0, 0)),
        pl.BlockSpec((1, RET_TC, RET_KEY), lambda bi, h, c: (h, 0, 0)),
        pl.BlockSpec((1, RET_TC, RET_KEY), lambda bi, h, c: (h, 0, 0)),
        pl.BlockSpec((1, n_ctx, RET_KEY), lambda bi, h, c: (h, 0, 0)),
        pl.BlockSpec((1, 1, RET_VAL), lambda bi, h, c: (h, 0, 0)),
    ]
    args = [z, z, z, z, zc, zc, cos, sin, dmat, qdec, kdec, kcdec, cdec]
    if prev is not None:
        in_specs.append(pl.BlockSpec((1, RET_TC, RET_VAL), lambda bi, h, c: (bi, ch(c), h)))
        args.append(prev)
    return pl.pallas_call(
        functools.partial(_retention_kernel, has_prev=prev is not None),
        grid=(b, RET_HEADS, nc),
        in_specs=in_specs,
        out_specs=pl.BlockSpec((1, RET_TC, RET_VAL), lambda bi, h, c: (bi, ch(c), h)),
        out_shape=jax.ShapeDtypeStruct((b, t, RET_VD), jnp.float32 if prev is None else jnp.bfloat16),
        scratch_shapes=[pltpu.VMEM((RET_KEY, RET_VAL), jnp.float32)],
        compiler_params=pltpu.CompilerParams(dimension_semantics=("parallel", "parallel", "arbitrary"),
                                             vmem_limit_bytes=VMEM_LIMIT_BYTES),
        name="retention_bwd" if reverse else "retention_fwd",
    )(*args)


def retention_mixer(z, zc, decay_f, decay_b):
    cos, sin = _rope_tables(z.shape[1])
    gamma_f = 1.0 - jnp.exp2(-decay_f)
    gamma_b = 1.0 - jnp.exp2(-decay_b)
    part = _retention_direction(z, zc, cos, sin, gamma_b, True, None)
    return _retention_direction(z, zc, cos, sin, gamma_f, False, part)


def _moe_kernel(te_ref, tv_ref, x_ref, wg_ref, wu_ref, wd_ref, o_ref, acc_ref):
    i = pl.program_id(0)
    j = pl.program_id(1)

    @pl.when(tv_ref[i] > 0)
    def _():
        @pl.when(j == 0)
        def _():
            acc_ref[...] = jnp.zeros_like(acc_ref)
        x = x_ref[...].astype(jnp.bfloat16)
        g = jnp.dot(x, wg_ref[0].astype(jnp.bfloat16), preferred_element_type=jnp.float32)
        u = jnp.dot(x, wu_ref[0].astype(jnp.bfloat16), preferred_element_type=jnp.float32)
        a = (_silu(g) * u).astype(jnp.bfloat16)
        acc_ref[...] += jnp.dot(a, wd_ref[0].astype(jnp.bfloat16), preferred_element_type=jnp.float32)

    last = j == pl.num_programs(1) - 1

    @pl.when(jnp.logical_and(last, tv_ref[i] > 0))
    def _():
        o_ref[...] = acc_ref[...]

    @pl.when(jnp.logical_and(last, tv_ref[i] == 0))
    def _():
        o_ref[...] = jnp.zeros_like(o_ref)


def moe_experts(xb, tile_expert, tile_valid, w_gate, w_up, w_down):
    slots, d = xb.shape
    ff = w_gate.shape[2]
    n_tiles = slots // MOE_TM
    grid_spec = pltpu.PrefetchScalarGridSpec(
        num_scalar_prefetch=2,
        grid=(n_tiles, ff // MOE_TF),
        in_specs=[
            pl.BlockSpec((MOE_TM, d), lambda i, j, te, tv: (i, 0)),
            pl.BlockSpec((1, d, MOE_TF), lambda i, j, te, tv: (te[i], 0, j)),
            pl.BlockSpec((1, d, MOE_TF), lambda i, j, te, tv: (te[i], 0, j)),
            pl.BlockSpec((1, MOE_TF, d), lambda i, j, te, tv: (te[i], j, 0)),
        ],
        out_specs=pl.BlockSpec((MOE_TM, d), lambda i, j, te, tv: (i, 0)),
        scratch_shapes=[pltpu.VMEM((MOE_TM, d), jnp.float32)],
    )
    return pl.pallas_call(
        _moe_kernel,
        grid_spec=grid_spec,
        out_shape=jax.ShapeDtypeStruct((slots, d), jnp.float32),
        compiler_params=pltpu.CompilerParams(dimension_semantics=("arbitrary", "arbitrary"),
                                             vmem_limit_bytes=VMEM_LIMIT_BYTES),
        name="moe_experts",
    )(tile_expert, tile_valid, xb, w_gate, w_up, w_down)


def moe_swiglu(h, router, w_gate, w_up, w_down):
    b, t, d = h.shape
    n = b * t
    hf = h.reshape(n, d)
    logits = jnp.dot(hf, router, precision=lax.Precision.HIGHEST)
    top_val, top_idx = lax.top_k(logits, TOP_K)
    gate = jax.nn.softmax(top_val, axis=-1)
    flat_e = top_idx.reshape(-1)
    onehot = (flat_e[:, None] == jnp.arange(N_EXPERTS, dtype=flat_e.dtype)[None, :]).astype(jnp.int32)
    csum = jnp.cumsum(onehot, axis=0)
    counts = csum[-1]
    rank = jnp.sum((csum - onehot) * onehot, axis=1)
    padded = (counts + MOE_TM - 1) // MOE_TM * MOE_TM
    pad_end = jnp.cumsum(padded)
    pad_start = pad_end - padded
    slot = (pad_start[flat_e] + rank).astype(jnp.int32)
    n_tiles = (n * TOP_K) // MOE_TM + N_EXPERTS
    slots = n_tiles * MOE_TM
    tok = jnp.arange(n * TOP_K, dtype=jnp.int32) // TOP_K
    slot_tok = jnp.zeros((slots,), jnp.int32).at[slot].set(tok)
    tile_start = jnp.arange(n_tiles, dtype=jnp.int32) * MOE_TM
    tile_expert = jnp.minimum(jnp.sum(pad_end[None, :] <= tile_start[:, None], axis=1),
                              N_EXPERTS - 1).astype(jnp.int32)
    tile_valid = (tile_start < pad_end[-1]).astype(jnp.int32)
    yb = moe_experts(hf[slot_tok], tile_expert, tile_valid, w_gate, w_up, w_down)
    slot2 = slot.reshape(n, TOP_K)
    out = gate[:, 0:1] * yb[slot2[:, 0]] + gate[:, 1:2] * yb[slot2[:, 1]]
    return out.reshape(b, t, d)


def _row_select(i, tm, n_ctx, ctx_vec, lat_vec):
    if n_ctx == 0:
        return lat_vec
    row = i * tm + lax.broadcasted_iota(jnp.int32, (tm, 1), 0)
    return jnp.where(row < n_ctx, ctx_vec, lat_vec)


def _post_norm_rows(x, update, g, b):
    y = DEEPNORM_ALPHA * x + update
    mu = jnp.mean(y, axis=-1, keepdims=True)
    yc = y - mu
    var = jnp.mean(yc * yc, axis=-1, keepdims=True)
    return yc * lax.rsqrt(var + LN_EPS) * g + b


def _matmul_postnorm_kernel(a_ref, w_ref, x_ref, lat_ref, ctx_ref, g_ref, b_ref, o_ref, *, n_ctx):
    tm = a_ref.shape[1]
    o = jnp.dot(a_ref[0].astype(jnp.bfloat16), w_ref[...], preferred_element_type=jnp.float32)
    gate = _row_select(pl.program_id(1), tm, n_ctx, ctx_ref[...], lat_ref[0])
    o_ref[0] = _post_norm_rows(x_ref[0], gate * o, g_ref[...], b_ref[...])


def matmul_postnorm(a, w, x, gate_lat, gate_ctx, ln_g, ln_b, n_ctx):
    b, length, k = a.shape
    d = w.shape[1]
    tm = _pick_tile(length, (768, 512, 256))
    vec = pl.BlockSpec((1, d), lambda bi, i: (0, 0))
    return pl.pallas_call(
        functools.partial(_matmul_postnorm_kernel, n_ctx=n_ctx),
        grid=(b, length // tm),
        in_specs=[pl.BlockSpec((1, tm, k), lambda bi, i: (bi, i, 0)),
                  pl.BlockSpec((k, d), lambda bi, i: (0, 0)),
                  pl.BlockSpec((1, tm, d), lambda bi, i: (bi, i, 0)),
                  pl.BlockSpec((1, 1, d), lambda bi, i: (bi, 0, 0)), vec, vec, vec],
        out_specs=pl.BlockSpec((1, tm, d), lambda bi, i: (bi, i, 0)),
        out_shape=jax.ShapeDtypeStruct((b, length, d), jnp.float32),
        compiler_params=pltpu.CompilerParams(dimension_semantics=("parallel", "parallel"),
                                             vmem_limit_bytes=VMEM_LIMIT_BYTES),
        name="matmul_postnorm",
    )(a, w.astype(jnp.bfloat16), x, gate_lat[:, None, :], gate_ctx[None, :], ln_g[None, :], ln_b[None, :])


def _ffn_postnorm_kernel(x_ref, lat_ref, ctx_ref, wg_ref, wu_ref, wd_ref, g_ref, b_ref, o_ref, h_ref, acc_ref,
                         *, n_ctx):
    i = pl.program_id(1)
    j = pl.program_id(2)
    tm = x_ref.shape[1]

    def vec(k):
        return _row_select(i, tm, n_ctx, ctx_ref[k:k + 1, :], lat_ref[0, k:k + 1, :])

    @pl.when(j == 0)
    def _():
        h_ref[...] = (x_ref[0] * (1.0 + vec(1)) + vec(0)).astype(jnp.bfloat16)
        acc_ref[...] = jnp.zeros_like(acc_ref)

    h = h_ref[...]
    gt = jnp.dot(h, wg_ref[...], preferred_element_type=jnp.float32)
    up = jnp.dot(h, wu_ref[...], preferred_element_type=jnp.float32)
    acc_ref[...] += jnp.dot((_silu(gt) * up).astype(jnp.bfloat16), wd_ref[...], preferred_element_type=jnp.float32)

    @pl.when(j == pl.num_programs(2) - 1)
    def _():
        o_ref[0] = _post_norm_rows(x_ref[0], vec(2) * acc_ref[...], g_ref[...], b_ref[...])


def ffn_postnorm(x, mod_lat, mod_ctx, w_gate, w_up, w_down, ln_g, ln_b, n_ctx):
    b, length, d = x.shape
    ff = w_gate.shape[1]
    tm = _pick_tile(length, (384, 512, 256))
    tf = _pick_tile(ff, (1408, 512, 256))
    vec = pl.BlockSpec((1, d), lambda bi, i, j: (0, 0))
    return pl.pallas_call(
        functools.partial(_ffn_postnorm_kernel, n_ctx=n_ctx),
        grid=(b, length // tm, ff // tf),
        in_specs=[pl.BlockSpec((1, tm, d), lambda bi, i, j: (bi, i, 0)),
                  pl.BlockSpec((1, 3, d), lambda bi, i, j: (bi, 0, 0)),
                  pl.BlockSpec((3, d), lambda bi, i, j: (0, 0)),
                  pl.BlockSpec((d, tf), lambda bi, i, j: (0, j)),
                  pl.BlockSpec((d, tf), lambda bi, i, j: (0, j)),
                  pl.BlockSpec((tf, d), lambda bi, i, j: (j, 0)), vec, vec],
        out_specs=pl.BlockSpec((1, tm, d), lambda bi, i, j: (bi, i, 0)),
        out_shape=jax.ShapeDtypeStruct((b, length, d), jnp.float32),
        scratch_shapes=[pltpu.VMEM((tm, d), jnp.bfloat16), pltpu.VMEM((tm, d), jnp.float32)],
        compiler_params=pltpu.CompilerParams(dimension_semantics=("parallel", "parallel", "arbitrary"),
                                             vmem_limit_bytes=VMEM_LIMIT_BYTES),
        name="ffn_postnorm",
    )(x, mod_lat, mod_ctx, w_gate.astype(jnp.bfloat16), w_up.astype(jnp.bfloat16), w_down.astype(jnp.bfloat16),
      ln_g[None, :], ln_b[None, :])


def _even_layer(x, xc, sc, scc, mod_w, mod_b, w_in, mla_q_norm, mla_wq_up, mla_kv_norm, mla_wkv_up,
                rwkv_conv, rwkv_w0_f, rwkv_w2_f, rwkv_w0_b, rwkv_w2_b, rwkv_a0_f, rwkv_a2_f, rwkv_a0_b,
                rwkv_a2_b, rwkv_g2, rwkv_k_k, rwkv_k_a, rwkv_r_k, rwkv_lnx_g, rwkv_lnx_b, w_out,
                ln1_g, ln1_b, ffn_w_gate, ffn_w_up, ffn_w_down, ln2_g, ln2_b):
    n_ctx = xc.shape[1]
    m = _adaln(sc, mod_w, mod_b)
    mc = _adaln(scc, mod_w, mod_b)
    xa = jnp.concatenate([xc, x], axis=1)
    is_ctx = (jnp.arange(xa.shape[1]) < n_ctx)[None, :, None]

    def rows(k):
        return jnp.where(is_ctx, mc[k][None, None, :], m[k][:, None, :])

    h = _modulate(xa, rows(0), rows(1)).astype(jnp.bfloat16)
    o_mla = _mla_mixer(h, n_ctx, w_in, mla_q_norm, mla_wq_up, mla_kv_norm, mla_wkv_up)
    o_rwkv = _rwkv7_mixer(mm(h, w_in[:, MLA_IN:]), n_ctx, rwkv_conv, rwkv_w0_f, rwkv_w2_f, rwkv_w0_b, rwkv_w2_b,
                          rwkv_a0_f, rwkv_a2_f, rwkv_a0_b, rwkv_a2_b, rwkv_g2, rwkv_k_k, rwkv_k_a,
                          rwkv_r_k, rwkv_lnx_g, rwkv_lnx_b)
    mix = jnp.concatenate([o_mla, o_rwkv], axis=-1).astype(jnp.bfloat16)
    xa = matmul_postnorm(mix, w_out, xa, m[2], mc[2], ln1_g, ln1_b, n_ctx)
    xa = ffn_postnorm(xa, jnp.stack(m[3:6], axis=1), jnp.stack(mc[3:6], axis=0),
                      ffn_w_gate, ffn_w_up, ffn_w_down, ln2_g, ln2_b, n_ctx)
    return xa[:, n_ctx:], xa[:, :n_ctx]


def _odd_layer(x, xc, sc, scc, mod_w, mod_b, w_in, ret_decay_f, ret_decay_b, w_out, ln1_g, ln1_b,
               router, moe_w_gate, moe_w_up, moe_w_down, ln2_g, ln2_b):
    m = _adaln(sc, mod_w, mod_b)
    c_shift, c_scale = jnp.split(
        jnp.dot(scc, mod_w[:, :2 * D_MODEL], precision=lax.Precision.HIGHEST) + mod_b[:2 * D_MODEL], 2)
    z = mm(_modulate(x, m[0][:, None, :], m[1][:, None, :]), w_in)
    zc = mm(_modulate(xc, c_shift, c_scale), w_in[:, RET_QK:2 * RET_QK + RET_VD])
    mix = retention_mixer(z, zc, ret_decay_f, ret_decay_b)
    x = matmul_postnorm(mix, w_out, x, m[2], jnp.zeros_like(m[2][0]), ln1_g, ln1_b, 0)
    y = moe_swiglu(_modulate(x, m[3][:, None, :], m[4][:, None, :]), router, moe_w_gate, moe_w_up, moe_w_down)
    return _post_norm(x, m[5][:, None, :] * y, ln2_g, ln2_b)


def kernel(x, c, ctx, c_ctx, l0_mod_w, l0_mod_b, l0_w_in, l0_mla_q_norm, l0_mla_wq_up, l0_mla_kv_norm, l0_mla_wkv_up, l0_rwkv_conv, l0_rwkv_w0_f, l0_rwkv_w2_f, l0_rwkv_w0_b, l0_rwkv_w2_b, l0_rwkv_a0_f, l0_rwkv_a2_f, l0_rwkv_a0_b, l0_rwkv_a2_b, l0_rwkv_g2, l0_rwkv_k_k, l0_rwkv_k_a, l0_rwkv_r_k, l0_rwkv_lnx_g, l0_rwkv_lnx_b, l0_w_out, l0_ln1_g, l0_ln1_b, l0_ffn_w_gate, l0_ffn_w_up, l0_ffn_w_down, l0_ln2_g, l0_ln2_b, l1_mod_w, l1_mod_b, l1_w_in, l1_ret_decay_f, l1_ret_decay_b, l1_w_out, l1_ln1_g, l1_ln1_b, l1_router, l1_moe_w_gate, l1_moe_w_up, l1_moe_w_down, l1_ln2_g, l1_ln2_b):
    even_params = (l0_mod_w, l0_mod_b, l0_w_in, l0_mla_q_norm, l0_mla_wq_up, l0_mla_kv_norm, l0_mla_wkv_up,
                   l0_rwkv_conv, l0_rwkv_w0_f, l0_rwkv_w2_f, l0_rwkv_w0_b, l0_rwkv_w2_b, l0_rwkv_a0_f,
                   l0_rwkv_a2_f, l0_rwkv_a0_b, l0_rwkv_a2_b, l0_rwkv_g2, l0_rwkv_k_k, l0_rwkv_k_a, l0_rwkv_r_k,
                   l0_rwkv_lnx_g, l0_rwkv_lnx_b, l0_w_out, l0_ln1_g, l0_ln1_b, l0_ffn_w_gate, l0_ffn_w_up,
                   l0_ffn_w_down, l0_ln2_g, l0_ln2_b)
    odd_params = (l1_mod_w, l1_mod_b, l1_w_in, l1_ret_decay_f, l1_ret_decay_b, l1_w_out, l1_ln1_g, l1_ln1_b,
                  l1_router, l1_moe_w_gate, l1_moe_w_up, l1_moe_w_down, l1_ln2_g, l1_ln2_b)
    sc = _silu(c)
    scc = _silu(c_ctx)
    x, xc = _even_layer(x, ctx, sc, scc, *even_params)
    return _odd_layer(x, xc, sc, scc, *odd_params)
```

```python
import functools

import jax
import jax.numpy as jnp
import numpy as np
from jax import lax
from jax.experimental import pallas as pl
from jax.experimental.pallas import tpu as pltpu

D_MODEL = 1024
DEPTH = 2
GRID_W = 64
ROPE_BASE = 10000.0
DEEPNORM_ALPHA = (2 * DEPTH) ** 0.25
LN_EPS = 1e-5

MLA_HEADS = 8
MLA_Q_RANK = 256
MLA_KV_RANK = 128
MLA_NOPE = 64
MLA_ROPE = 32
MLA_V = 64
MLA_IN = MLA_Q_RANK + MLA_KV_RANK + MLA_ROPE
MLA_OUT = MLA_HEADS * MLA_V

RWKV_HEADS = 8
RWKV_HEAD = 64
RWKV_DIM = RWKV_HEADS * RWKV_HEAD
RWKV_DECAY_LORA = 64
RWKV_AAA_LORA = 64
RWKV_GATE_LORA = 128
RWKV_LNX_EPS = 64e-5

RET_HEADS = 4
RET_KEY = 256
RET_VAL = 512
RET_TC = 256
RET_QK = RET_HEADS * RET_KEY
RET_VD = RET_HEADS * RET_VAL

N_EXPERTS = 8
TOP_K = 2
MOE_TM = 1024
MOE_TF = 512

LANES = 128
VMEM_LIMIT_BYTES = 48 * 1024 * 1024


def _matmul_kernel(x_ref, w_ref, o_ref):
    o_ref[...] = jnp.dot(x_ref[...].astype(jnp.bfloat16), w_ref[...],
                         preferred_element_type=jnp.float32).astype(o_ref.dtype)


def _pick_tile(n, candidates):
    for c in candidates:
        if n % c == 0:
            return c
    raise ValueError(f"no tile for {n}")


def pmatmul(x, w, out_dtype=jnp.float32):
    m, k = x.shape
    n = w.shape[1]
    n_pad = (-n) % LANES
    w = w.astype(jnp.bfloat16)
    if n_pad:
        w = jnp.pad(w, ((0, 0), (0, n_pad)))
    np_ = n + n_pad
    tm = _pick_tile(m, (1024, 512, 256, 128, 8) if k <= 1024 else (512, 256, 128, 8))
    tn = _pick_tile(np_, (1024, 768, 640, 512, 384, 256, 128))
    out = pl.pallas_call(
        _matmul_kernel,
        grid=(m // tm, np_ // tn),
        in_specs=[pl.BlockSpec((tm, k), lambda i, j: (i, 0)),
                  pl.BlockSpec((k, tn), lambda i, j: (0, j))],
        out_specs=pl.BlockSpec((tm, tn), lambda i, j: (i, j)),
        out_shape=jax.ShapeDtypeStruct((m, np_), out_dtype),
        compiler_params=pltpu.CompilerParams(
            dimension_semantics=("parallel", "parallel"),
            vmem_limit_bytes=VMEM_LIMIT_BYTES),
        name="matmul",
    )(x, w)
    return out[:, :n] if n_pad else out


def mm(x, w, out_dtype=jnp.float32):
    lead = x.shape[:-1]
    out = pmatmul(x.reshape(-1, x.shape[-1]).astype(jnp.bfloat16), w, out_dtype)
    return out.reshape(*lead, w.shape[1])


def _silu(t):
    return t * jax.nn.sigmoid(t)


def _normalize(t, eps):
    mu = jnp.mean(t, axis=-1, keepdims=True)
    var = jnp.mean(jnp.square(t - mu), axis=-1, keepdims=True)
    return (t - mu) * lax.rsqrt(var + eps)


def _layer_norm(t, g, b):
    return _normalize(t, LN_EPS) * g + b


def _rms_norm(t, g, eps=1e-6):
    return t * lax.rsqrt(jnp.mean(t * t, axis=-1, keepdims=True) + eps) * g


def _l2_normalize(t, eps=1e-12):
    return t / jnp.maximum(jnp.linalg.norm(t, axis=-1, keepdims=True), eps)


def _post_norm(x, update, g, b):
    return _layer_norm(DEEPNORM_ALPHA * x + update, g, b)


def _modulate(h, shift, scale):
    return h * (1.0 + scale) + shift


def _adaln(cond, mod_w, mod_b):
    return jnp.split(jnp.dot(cond, mod_w, precision=lax.Precision.HIGHEST) + mod_b, 6, axis=-1)


def _grid_positions(n_tokens):
    rows = n_tokens // GRID_W
    row = jnp.repeat(jnp.arange(rows, dtype=jnp.float32), GRID_W)
    col = jnp.tile(jnp.arange(GRID_W, dtype=jnp.float32), rows)
    return row, col


SOFTMAX_FLOOR = -1e30


FLASH_ROW_GROUPS = 4


def _flash_kernel(q_ref, kt_ref, v_ref, o_ref, m_ref, acc_ref, *, dv):
    j = pl.program_id(3)

    @pl.when(j == 0)
    def _():
        m_ref[...] = jnp.full_like(m_ref, SOFTMAX_FLOOR)
        acc_ref[...] = jnp.zeros_like(acc_ref)

    rows = q_ref.shape[2] // FLASH_ROW_GROUPS
    kt = kt_ref[0, 0]
    v = v_ref[0, 0]
    for u in range(FLASH_ROW_GROUPS):
        sl = slice(u * rows, (u + 1) * rows)
        s = jnp.dot(q_ref[0, 0, sl, :], kt, preferred_element_type=jnp.float32)
        m_prev = m_ref[sl, :]
        m_new = jnp.maximum(m_prev, jnp.max(s, axis=-1, keepdims=True))
        alpha = jnp.exp(m_prev - m_new)
        p = jnp.exp(s - m_new).astype(jnp.bfloat16)
        acc_ref[sl, :] = alpha * acc_ref[sl, :] + jnp.dot(p, v, preferred_element_type=jnp.float32)
        m_ref[sl, :] = m_new

    @pl.when(j == pl.num_programs(3) - 1)
    def _():
        acc = acc_ref[...]
        o_ref[0, 0] = acc[:, :dv] / acc[:, dv:dv + 1]


def flash_attention(q, kt, v1, dv, tq, tk):
    b, h, t, dq = q.shape
    s = kt.shape[3]
    return pl.pallas_call(
        functools.partial(_flash_kernel, dv=dv),
        grid=(b, h, t // tq, s // tk),
        in_specs=[pl.BlockSpec((1, 1, tq, dq), lambda bi, hi, i, j: (bi, hi, i, 0)),
                  pl.BlockSpec((1, 1, dq, tk), lambda bi, hi, i, j: (bi, hi, 0, j)),
                  pl.BlockSpec((1, 1, tk, LANES), lambda bi, hi, i, j: (bi, hi, j, 0))],
        out_specs=pl.BlockSpec((1, 1, tq, dv), lambda bi, hi, i, j: (bi, hi, i, 0)),
        out_shape=jax.ShapeDtypeStruct((b, h, t, dv), jnp.float32),
        scratch_shapes=[pltpu.VMEM((tq, 1), jnp.float32), pltpu.VMEM((tq, LANES), jnp.float32)],
        compiler_params=pltpu.CompilerParams(
            dimension_semantics=("parallel", "parallel", "parallel", "arbitrary"),
            vmem_limit_bytes=VMEM_LIMIT_BYTES),
        name="flash_attention",
    )(q, kt, v1)


def _block_attention(q, k, v):
    dq = q.shape[-1]
    dv = v.shape[-1]
    b, s, h, _ = k.shape
    qh = jnp.transpose(q * dq ** -0.5, (0, 2, 1, 3)).astype(jnp.bfloat16)
    kt = jnp.transpose(k, (0, 2, 3, 1)).astype(jnp.bfloat16)
    v1 = jnp.concatenate([v, jnp.ones((b, s, h, 1), v.dtype), jnp.zeros((b, s, h, LANES - dv - 1), v.dtype)],
                         axis=-1)
    v1 = jnp.transpose(v1, (0, 2, 1, 3)).astype(jnp.bfloat16)
    tq = _pick_tile(q.shape[1], (1024, 256))
    o = flash_attention(qh, kt, v1, dv, tq, s)
    return jnp.transpose(o, (0, 2, 1, 3))


_Q8 = MLA_ROPE // 4
ROPE_PARTNER = np.concatenate([np.arange(_Q8, 2 * _Q8), np.arange(0, _Q8),
                               np.arange(3 * _Q8, 4 * _Q8), np.arange(2 * _Q8, 3 * _Q8)])
ROPE_SIGN = np.concatenate([-np.ones(_Q8), np.ones(_Q8), -np.ones(_Q8), np.ones(_Q8)]).astype(np.float32)


def _mla_rope_tables(n_ctx, t):
    row, col = _grid_positions(t)
    zero = jnp.zeros((n_ctx,), jnp.float32)
    row, col = jnp.concatenate([zero, row]), jnp.concatenate([zero, col])
    inv_freq = ROPE_BASE ** (-jnp.arange(_Q8, dtype=jnp.float32) / _Q8)
    ang = jnp.concatenate([row[:, None] * inv_freq] * 2 + [col[:, None] * inv_freq] * 2, axis=1)
    return jnp.cos(ang), jnp.sin(ang)


def _mla_mixer(h, n_ctx, w_in, q_norm, wq_up, kv_norm, wkv_up):
    b, length, _ = h.shape
    cos, sin = _mla_rope_tables(n_ctx, length - n_ctx)
    k_cols = np.arange(MLA_Q_RANK + MLA_KV_RANK, MLA_IN)
    w_z = jnp.concatenate([w_in[:, :MLA_IN], w_in[:, k_cols[ROPE_PARTNER]] * ROPE_SIGN], axis=1)
    z = mm(h, w_z)
    cq, ckv = z[..., :MLA_Q_RANK], z[..., MLA_Q_RANK:MLA_Q_RANK + MLA_KV_RANK]
    k_rope = z[..., k_cols[0]:MLA_IN] * cos + z[..., MLA_IN:] * sin
    dq = MLA_NOPE + MLA_ROPE
    head_cols = np.arange(MLA_HEADS)[:, None] * dq
    q_rope_cols = (head_cols + MLA_NOPE + np.arange(MLA_ROPE)[None, :])
    wq_partner = jnp.zeros_like(wq_up).at[:, q_rope_cols.reshape(-1)].set(
        wq_up[:, (head_cols + MLA_NOPE + ROPE_PARTNER[None, :]).reshape(-1)] * jnp.tile(ROPE_SIGN, MLA_HEADS))
    q2 = mm(_rms_norm(cq, q_norm), jnp.concatenate([wq_up, wq_partner], axis=1))
    one, nil = jnp.ones((length, MLA_NOPE), jnp.float32), jnp.zeros((length, MLA_NOPE), jnp.float32)
    cos_q = jnp.tile(jnp.concatenate([one, cos], axis=1), (1, MLA_HEADS))
    sin_q = jnp.tile(jnp.concatenate([nil, sin], axis=1), (1, MLA_HEADS))
    q = (q2[..., :MLA_HEADS * dq] * cos_q + q2[..., MLA_HEADS * dq:] * sin_q).reshape(b, length, MLA_HEADS, dq)
    kv = mm(_rms_norm(ckv, kv_norm), wkv_up).reshape(b, length, MLA_HEADS, MLA_NOPE + MLA_V)
    k = jnp.concatenate([kv[..., :MLA_NOPE],
                         jnp.broadcast_to(k_rope[:, :, None, :], (b, length, MLA_HEADS, MLA_ROPE))], axis=-1)
    v = kv[..., MLA_NOPE:]
    o = _block_attention(q[:, n_ctx:], k, v)
    oc = _block_attention(q[:, :n_ctx], k[:, :n_ctx], v[:, :n_ctx])
    return jnp.concatenate([oc, o], axis=1).reshape(b, length, MLA_OUT)


SCAN_BLOCK = 128
SCAN_UNROLL = 4
N_PAIRS = RWKV_HEADS // 2
PAIR_ROWS = N_PAIRS * RWKV_HEAD


def _to_state_tiles(vblk):
    halves = [[], []]
    for p in range(N_PAIRS):
        tr = vblk[:, p * LANES:(p + 1) * LANES].T
        for half in range(2):
            c = slice(half * RWKV_HEAD, (half + 1) * RWKV_HEAD)
            halves[half].append(jnp.concatenate([tr[:RWKV_HEAD, c], tr[RWKV_HEAD:, c]], axis=1))
    return [jnp.concatenate(h, axis=0) for h in halves]


def _from_state_tiles(tiles):
    cols = []
    for p in range(N_PAIRS):
        r = slice(p * RWKV_HEAD, (p + 1) * RWKV_HEAD)
        tr = jnp.concatenate([tiles[0][r, :], tiles[1][r, :]], axis=0).T
        cols.append(jnp.concatenate(
            [jnp.concatenate([tr[:RWKV_HEAD, :RWKV_HEAD], tr[:RWKV_HEAD, RWKV_HEAD:]], axis=0),
             jnp.concatenate([tr[RWKV_HEAD:, :RWKV_HEAD], tr[RWKV_HEAD:, RWKV_HEAD:]], axis=0)], axis=1))
    return jnp.concatenate(cols, axis=1)


def _scan_kernel(rf_ref, rb_ref, af_ref, ab_ref, vf_ref, vb_ref, wf_ref, wb_ref, kf_ref, kb_ref,
                 bf_ref, bb_ref, wred_ref, yf_ref, yb_ref, s_ref, vt_ref, yc_ref):
    nb = rf_ref.shape[0]

    @pl.when(pl.program_id(0) == 0)
    def _():
        s_ref[...] = jnp.zeros_like(s_ref)

    lane = lax.broadcasted_iota(jnp.int32, (PAIR_ROWS, LANES), 1)
    wred = wred_ref[...]
    refs = ((rf_ref, af_ref, vf_ref, wf_ref, kf_ref, bf_ref, yf_ref),
            (rb_ref, ab_ref, vb_ref, wb_ref, kb_ref, bb_ref, yb_ref))

    for d in range(2):
        for bi in range(nb):
            tiles = _to_state_tiles(refs[d][2][bi])
            for half in range(2):
                vt_ref[d, bi, half] = tiles[half].astype(jnp.bfloat16)

    def rows(ref, bi, t):
        row = ref[bi, pl.ds(t, 1), :]
        return jnp.concatenate(
            [jnp.broadcast_to(row[:, p * LANES:(p + 1) * LANES], (RWKV_HEAD, LANES)) for p in range(N_PAIRS)],
            axis=0)

    def collect(d, bi, t, yb, valid):
        mask = jnp.logical_and(lane % RWKV_HEAD == t % RWKV_HEAD, valid)
        half = t // RWKV_HEAD
        yc_ref[d, bi, half] = jnp.where(mask, yb, yc_ref[d, bi, half])

    def body(i, carry):
        for d in range(2):
            r_ref, a_ref, _, w_ref, k_ref, b_ref, _ = refs[d]
            t = i if d == 0 else SCAN_BLOCK - 1 - i
            tp = jnp.maximum(i - 1, 0) if d == 0 else jnp.minimum(SCAN_BLOCK - i, SCAN_BLOCK - 1)
            sel = lane % RWKV_HEAD == t % RWKV_HEAD
            prs = []
            for bi in range(nb):
                s = s_ref[d, bi]
                pa = (s * rows(a_ref, bi, t)).astype(jnp.bfloat16)
                prs.append((s * rows(r_ref, bi, tp)).astype(jnp.bfloat16))
                pv = jnp.where(sel, vt_ref[d, bi, t // RWKV_HEAD], jnp.zeros((), jnp.bfloat16))
                red = jnp.dot(jnp.concatenate([pa, pv], axis=1), wred, preferred_element_type=jnp.float32)
                s_ref[d, bi] = (s * rows(w_ref, bi, t) + red[:, :LANES] * rows(b_ref, bi, t)
                                + red[:, LANES:] * rows(k_ref, bi, t))
            for b0 in range(0, nb, 2):
                ys = jnp.dot(jnp.concatenate(prs[b0:b0 + 2], axis=1), wred, preferred_element_type=jnp.float32)
                collect(d, b0, tp, ys[:, :LANES], i >= 1)
                collect(d, b0 + 1, tp, ys[:, LANES:], i >= 1)
        return carry

    lax.fori_loop(0, SCAN_BLOCK, body, 0, unroll=SCAN_UNROLL)

    for d in range(2):
        r_ref, y_ref = refs[d][0], refs[d][6]
        t_last = SCAN_BLOCK - 1 if d == 0 else 0
        for b0 in range(0, nb, 2):
            prs = [(s_ref[d, bi] * rows(r_ref, bi, t_last)).astype(jnp.bfloat16) for bi in (b0, b0 + 1)]
            ys = jnp.dot(jnp.concatenate(prs, axis=1), wred, preferred_element_type=jnp.float32)
            collect(d, b0, t_last, ys[:, :LANES], True)
            collect(d, b0 + 1, t_last, ys[:, LANES:], True)
        for bi in range(nb):
            y_ref[bi] = _from_state_tiles([yc_ref[d, bi, 0], yc_ref[d, bi, 1]])


def rwkv_scan(r, v, a, w_f, k_f, b_f, w_b, k_b, b_b, n_ctx):
    nb, length, _ = r.shape
    nblk = length // SCAN_BLOCK
    nblk_ctx = n_ctx // SCAN_BLOCK
    j = np.arange(2 * LANES)
    wred = (j[:, None] // RWKV_HEAD) == (j[None, :] // RWKV_HEAD)

    def fwd(i):
        return i

    def bwd(i):
        return jnp.where(i < nblk_ctx, nblk_ctx - 1 - i, nblk + nblk_ctx - 1 - i)

    def row_spec(blk):
        return pl.BlockSpec((nb, SCAN_BLOCK, RWKV_DIM), lambda i: (0, blk(i), 0))

    y_shape = jax.ShapeDtypeStruct((nb, length, RWKV_DIM), jnp.float32)
    tile_shape = (2, nb, 2, PAIR_ROWS, LANES)
    return pl.pallas_call(
        _scan_kernel,
        grid=(nblk,),
        in_specs=[row_spec(fwd), row_spec(bwd)] * 6 + [pl.BlockSpec((2 * LANES, 2 * LANES), lambda i: (0, 0))],
        out_specs=[row_spec(fwd), row_spec(bwd)],
        out_shape=[y_shape, y_shape],
        scratch_shapes=[pltpu.VMEM((2, nb, PAIR_ROWS, LANES), jnp.float32),
                        pltpu.VMEM(tile_shape, jnp.bfloat16), pltpu.VMEM(tile_shape, jnp.float32)],
        compiler_params=pltpu.CompilerParams(dimension_semantics=("arbitrary",),
                                             vmem_limit_bytes=VMEM_LIMIT_BYTES),
        name="rwkv_scan",
    )(r, r, a, a, v, v, w_f, w_b, k_f, k_b, b_f, b_b, jnp.asarray(wred, jnp.bfloat16))


FEAT_ROWS = 256


def _head_sums(x, ones_ref):
    hi = x.astype(jnp.bfloat16)
    lo = (x - hi.astype(jnp.float32)).astype(jnp.bfloat16)
    ones = ones_ref[...]
    return (jnp.dot(hi, ones, preferred_element_type=jnp.float32)
            + jnp.dot(lo, ones, preferred_element_type=jnp.float32))


def _softplus(x):
    return jnp.maximum(x, 0.0) + jnp.log(1.0 + jnp.exp(-jnp.abs(x)))


def _sigmoid(x):
    return 1.0 / (1.0 + jnp.exp(-x))


def _features_kernel(z_ref, prev_ref, next_ref, conv_ref, lora_ref, g2_ref, vec_ref, ones_ref,
                     r_ref, a_ref, v_ref, wf_ref, kf_ref, bf_ref, wb_ref, kb_ref, bb_ref, g_ref, rk_ref):
    rows = z_ref.shape[1]
    c0 = 3 * RWKV_DIM
    raw = z_ref[0, :, :c0]
    row_id = lax.broadcasted_iota(jnp.int32, (rows, 1), 0)
    before = jnp.where(row_id == 0, prev_ref[0, 0], pltpu.roll(raw, 1, axis=0))
    after = jnp.where(row_id == rows - 1, next_ref[0, 0], pltpu.roll(raw, rows - 1, axis=0))
    rkv = before * conv_ref[0:1, :] + raw * conv_ref[1:2, :] + after * conv_ref[2:3, :]
    r, k, v = rkv[:, :RWKV_DIM], rkv[:, RWKV_DIM:2 * RWKV_DIM], rkv[:, 2 * RWKV_DIM:]

    def vec(i):
        return vec_ref[i:i + 1, :]

    k_k, k_a, r_k, w0_f, w0_b, a0_f, a0_b = (vec(i) for i in range(7))
    kraw = k * k_k
    kk = kraw / jnp.maximum(jnp.sqrt(_head_sums(kraw * kraw, ones_ref)), 1e-12)

    lo = z_ref[0, :, c0:c0 + 2 * LANES]
    lo = jnp.where(lax.broadcasted_iota(jnp.int32, lo.shape, 1) < LANES, jnp.tanh(lo), lo)
    proj = jnp.dot(lo.astype(jnp.bfloat16), lora_ref[...], preferred_element_type=jnp.float32)
    gd = z_ref[0, :, c0 + 2 * LANES:]
    g_ref[0] = jnp.dot(_sigmoid(gd).astype(jnp.bfloat16), g2_ref[...], preferred_element_type=jnp.float32)

    ksum = None
    for d, (w0, a0, w_ref, k_ref, b_ref) in enumerate(((w0_f, a0_f, wf_ref, kf_ref, bf_ref),
                                                       (w0_b, a0_b, wb_ref, kb_ref, bb_ref))):
        logw = -_softplus(-(w0 + proj[:, d * RWKV_DIM:(d + 1) * RWKV_DIM])) - 0.5
        lr = _sigmoid(a0 + proj[:, (2 + d) * RWKV_DIM:(3 + d) * RWKV_DIM])
        kd = k * (1.0 + (lr - 1.0) * k_a)
        w_ref[0] = jnp.exp(-jnp.exp(logw))
        k_ref[0] = kd
        b_ref[0] = kk * lr
        ksum = kd if ksum is None else ksum + kd
    r_ref[0] = r
    a_ref[0] = -kk
    v_ref[0] = v
    rk_ref[0] = r * ksum * r_k


def _readout_kernel(yf_ref, yb_ref, rk_ref, v_ref, g_ref, vec_ref, ones_ref, o_ref):
    y = yf_ref[0] + yb_ref[0]
    mu = _head_sums(y, ones_ref) * (1.0 / RWKV_HEAD)
    yc = y - mu
    var = _head_sums(yc * yc, ones_ref) * (1.0 / RWKV_HEAD)
    yn = yc * lax.rsqrt(var + RWKV_LNX_EPS) * vec_ref[0:1, :] + vec_ref[1:2, :]
    o_ref[0] = ((yn + _head_sums(rk_ref[0], ones_ref) * v_ref[0]) * g_ref[0]).astype(o_ref.dtype)


def _rwkv7_mixer(z, n_ctx, conv_w, w0_f, w2_f, w0_b, w2_b, a0_f, a2_f, a0_b, a2_b, g2, k_k, k_a,
                 r_k, lnx_g, lnx_b):
    b, length, zin = z.shape
    c0 = 3 * RWKV_DIM
    nblk = length // FEAT_ROWS
    edge = jnp.arange(nblk) * FEAT_ROWS
    zero_row = jnp.zeros((b, 1, c0), jnp.float32)
    last_rows = z[:, FEAT_ROWS - 1::FEAT_ROWS, :c0]
    first_rows = z[:, ::FEAT_ROWS, :c0]
    prev_rows = jnp.concatenate([zero_row, last_rows[:, :-1]], axis=1)
    prev_rows = jnp.where(((edge == 0) | (edge == n_ctx))[None, :, None], 0.0, prev_rows)
    next_rows = jnp.concatenate([first_rows[:, 1:], zero_row], axis=1)
    next_rows = jnp.where(((edge + FEAT_ROWS == n_ctx) | (edge + FEAT_ROWS == length))[None, :, None], 0.0, next_rows)
    zl = jnp.zeros((RWKV_DECAY_LORA, RWKV_DIM), jnp.float32)
    lora = jnp.concatenate([jnp.concatenate([w2_f, zl, zl, zl], axis=1), jnp.concatenate([zl, w2_b, zl, zl], axis=1),
                            jnp.concatenate([zl, zl, a2_f, zl], axis=1), jnp.concatenate([zl, zl, zl, a2_b], axis=1)],
                           axis=0).astype(jnp.bfloat16)
    j = np.arange(RWKV_DIM)
    head_ones = jnp.asarray((j[:, None] // RWKV_HEAD) == (j[None, :] // RWKV_HEAD), jnp.bfloat16)
    vecs = jnp.stack([k_k, k_a, r_k.reshape(-1), w0_f, w0_b, a0_f, a0_b, jnp.zeros_like(k_k)], axis=0)
    blk = pl.BlockSpec((1, FEAT_ROWS, RWKV_DIM), lambda bi, i: (bi, i, 0))
    full = lambda shape: pl.BlockSpec(shape, lambda bi, i: (0,) * len(shape))
    edge_spec = pl.BlockSpec((1, 1, 1, c0), lambda bi, i: (bi, i, 0, 0))
    out_shape = jax.ShapeDtypeStruct((b, length, RWKV_DIM), jnp.float32)
    r, a, v, w_f, k_f, b_f, w_b, k_b, b_b, g, rk = pl.pallas_call(
        _features_kernel,
        grid=(b, nblk),
        in_specs=[pl.BlockSpec((1, FEAT_ROWS, zin), lambda bi, i: (bi, i, 0)), edge_spec, edge_spec,
                  full((3, c0)), full((4 * RWKV_DECAY_LORA, 4 * RWKV_DIM)), full((RWKV_GATE_LORA, RWKV_DIM)),
                  full((8, RWKV_DIM)), full((RWKV_DIM, RWKV_DIM))],
        out_specs=[blk] * 11,
        out_shape=[out_shape] * 11,
        compiler_params=pltpu.CompilerParams(dimension_semantics=("parallel", "parallel"),
                                             vmem_limit_bytes=VMEM_LIMIT_BYTES),
        name="rwkv_features",
    )(z, prev_rows[:, :, None, :], next_rows[:, :, None, :], conv_w, lora, g2.astype(jnp.bfloat16), vecs, head_ones)
    y_f, y_b = rwkv_scan(r, v, a, w_f, k_f, b_f, w_b, k_b, b_b, n_ctx)
    return pl.pallas_call(
        _readout_kernel,
        grid=(b, nblk),
        in_specs=[blk] * 5 + [full((2, RWKV_DIM)), full((RWKV_DIM, RWKV_DIM))],
        out_specs=blk,
        out_shape=jax.ShapeDtypeStruct((b, length, RWKV_DIM), jnp.bfloat16),
        compiler_params=pltpu.CompilerParams(dimension_semantics=("parallel", "parallel"),
                                             vmem_limit_bytes=VMEM_LIMIT_BYTES),
        name="rwkv_readout",
    )(y_f, y_b, rk, v, g, jnp.stack([lnx_g, lnx_b], axis=0), head_ones)


def _rope_halves(t, cos, sin):
    parts = []
    for s in range(2):
        u = t[:, s * LANES:(s + 1) * LANES]
        parts.append(u * cos[:, s * LANES:(s + 1) * LANES]
                     + pltpu.roll(u, LANES // 2, axis=1) * sin[:, s * LANES:(s + 1) * LANES])
    return jnp.concatenate(parts, axis=1)


def _retention_kernel(*refs, has_prev):
    if has_prev:
        (q_ref, k_ref, v_ref, g_ref, kc_ref, vc_ref, cos_ref, sin_ref, dmat_ref, qdec_ref, kdec_ref,
         kcdec_ref, cdec_ref, prev_ref, o_ref, s_ref) = refs
    else:
        (q_ref, k_ref, v_ref, g_ref, kc_ref, vc_ref, cos_ref, sin_ref, dmat_ref, qdec_ref, kdec_ref,
         kcdec_ref, cdec_ref, o_ref, s_ref) = refs
        prev_ref = None
    scale = RET_KEY ** -0.5

    @pl.when(pl.program_id(2) == 0)
    def _():
        kc = (kc_ref[0].astype(jnp.float32) * kcdec_ref[0] * scale).astype(jnp.bfloat16)
        s_ref[...] = lax.dot_general(kc, vc_ref[0].astype(jnp.bfloat16), (((0,), (0,)), ((), ())),
                                     preferred_element_type=jnp.float32)

    cos = cos_ref[...]
    sin = sin_ref[...]
    q = _rope_halves(q_ref[0].astype(jnp.float32), cos, sin)
    k = _rope_halves(k_ref[0].astype(jnp.float32), cos, sin) * scale
    v = v_ref[0].astype(jnp.bfloat16)
    s = s_ref[...]
    att = lax.dot_general(q.astype(jnp.bfloat16), k.astype(jnp.bfloat16), (((1,), (1,)), ((), ())),
                          preferred_element_type=jnp.float32) * dmat_ref[0]
    o = (jnp.dot(att.astype(jnp.bfloat16), v, preferred_element_type=jnp.float32)
         + jnp.dot((q * qdec_ref[0]).astype(jnp.bfloat16), s.astype(jnp.bfloat16),
                   preferred_element_type=jnp.float32))
    s_ref[...] = s * cdec_ref[0] + lax.dot_general((k * kdec_ref[0]).astype(jnp.bfloat16), v,
                                                   (((0,), (0,)), ((), ())),
                                                   preferred_element_type=jnp.float32)
    mu = jnp.mean(o, axis=-1, keepdims=True)
    oc = o - mu
    var = jnp.mean(oc * oc, axis=-1, keepdims=True)
    out = _silu(g_ref[0].astype(jnp.float32)) * (oc * lax.rsqrt(var + 1e-6))
    if prev_ref is not None:
        out = out + prev_ref[0]
    o_ref[0] = out.astype(o_ref.dtype)


def _retention_tables(gamma, reverse, n_ctx):
    log_g = jnp.log(gamma)[:, None, None]
    i = jnp.arange(RET_TC, dtype=jnp.float32)
    rel = (i[None, :] - i[:, None]) if reverse else (i[:, None] - i[None, :])
    dmat = jnp.where(rel >= 0, jnp.exp(jnp.maximum(rel, 0.0)[None] * log_g), 0.0)
    q_pow = (RET_TC - i) if reverse else (i + 1.0)
    k_pow = i if reverse else (RET_TC - 1.0 - i)
    m = jnp.arange(n_ctx, dtype=jnp.float32)
    c_pow = m if reverse else (n_ctx - 1.0 - m)
    bc = lambda p: jnp.broadcast_to(jnp.exp(p[None, :, None] * log_g), (RET_HEADS, p.shape[0], RET_KEY))
    cdec = jnp.broadcast_to(jnp.exp(RET_TC * log_g), (RET_HEADS, 1, RET_VAL))
    return dmat, bc(q_pow), bc(k_pow), bc(c_pow), cdec


def _rope_tables(n_tokens):
    pos_row = (jnp.arange(n_tokens) // GRID_W).astype(jnp.float32)
    pos_col = (jnp.arange(n_tokens) % GRID_W).astype(jnp.float32)
    nf = RET_KEY // 4
    inv_freq = ROPE_BASE ** (-jnp.arange(nf, dtype=jnp.float32) / nf)
    cos, sin = [], []
    for pos in (pos_row, pos_col):
        ang = pos[:, None] * inv_freq[None, :]
        cos += [jnp.cos(ang), jnp.cos(ang)]
        sin += [-jnp.sin(ang), jnp.sin(ang)]
    return jnp.concatenate(cos, axis=1), jnp.concatenate(sin, axis=1)


def _retention_direction(z, zc, cos, sin, gamma, reverse, prev):
    b, t, _ = z.shape
    n_ctx = zc.shape[1]
    nc = t // RET_TC
    dmat, qdec, kdec, kcdec, cdec = _retention_tables(gamma, reverse, n_ctx)
    ch = (lambda c: nc - 1 - c) if reverse else (lambda c: c)
    kq, kv = RET_QK // RET_KEY, (2 * RET_QK) // RET_VAL
    g_off = (2 * RET_QK + (2 if reverse else 1) * RET_VD) // RET_VAL
    in_specs = [
        pl.BlockSpec((1, RET_TC, RET_KEY), lambda bi, h, c: (bi, ch(c), h)),
        pl.BlockSpec((1, RET_TC, RET_KEY), lambda bi, h, c: (bi, ch(c), kq + h)),
        pl.BlockSpec((1, RET_TC, RET_VAL), lambda bi, h, c: (bi, ch(c), kv + h)),
        pl.BlockSpec((1, RET_TC, RET_VAL), lambda bi, h, c: (bi, ch(c), g_off + h)),
        pl.BlockSpec((1, n_ctx, RET_KEY), lambda bi, h, c: (bi, 0, h)),
        pl.BlockSpec((1, n_ctx, RET_VAL), lambda bi, h, c: (bi, 0, RET_QK // RET_VAL + h)),
        pl.BlockSpec((RET_TC, RET_KEY), lambda bi, h, c: (ch(c), 0)),
        pl.BlockSpec((RET_TC, RET_KEY), lambda bi, h, c: (ch(c), 0)),
        pl.BlockSpec((1, RET_TC, RET_TC), lambda bi, h, c: (h, 0, 0)),
        pl.BlockSpec((1, RET_TC, RET_KEY), lambda bi, h, c: (h, 0, 0)),
        pl.BlockSpec((1, RET_TC, RET_KEY), lambda bi, h, c: (h, 0, 0)),
        pl.BlockSpec((1, n_ctx, RET_KEY), lambda bi, h, c: (h, 0, 0)),
        pl.BlockSpec((1, 1, RET_VAL), lambda bi, h, c: (h, 0, 0)),
    ]
    args = [z, z, z, z, zc, zc, cos, sin, dmat, qdec, kdec, kcdec, cdec]
    if prev is not None:
        in_specs.append(pl.BlockSpec((1, RET_TC, RET_VAL), lambda bi, h, c: (bi, ch(c), h)))
        args.append(prev)
    return pl.pallas_call(
        functools.partial(_retention_kernel, has_prev=prev is not None),
        grid=(b, RET_HEADS, nc),
        in_specs=in_specs,
        out_specs=pl.BlockSpec((1, RET_TC, RET_VAL), lambda bi, h, c: (bi, ch(c), h)),
        out_shape=jax.ShapeDtypeStruct((b, t, RET_VD), jnp.float32 if prev is None else jnp.bfloat16),
        scratch_shapes=[pltpu.VMEM((RET_KEY, RET_VAL), jnp.float32)],
        compiler_params=pltpu.CompilerParams(dimension_semantics=("parallel", "parallel", "arbitrary"),
                                             vmem_limit_bytes=VMEM_LIMIT_BYTES),
        name="retention_bwd" if reverse else "retention_fwd",
    )(*args)


def retention_mixer(z, zc, decay_f, decay_b):
    cos, sin = _rope_tables(z.shape[1])
    gamma_f = 1.0 - jnp.exp2(-decay_f)
    gamma_b = 1.0 - jnp.exp2(-decay_b)
    part = _retention_direction(z, zc, cos, sin, gamma_b, True, None)
    return _retention_direction(z, zc, cos, sin, gamma_f, False, part)


def _moe_kernel(te_ref, tv_ref, x_ref, wg_ref, wu_ref, wd_ref, o_ref, acc_ref):
    i = pl.program_id(0)
    j = pl.program_id(1)

    @pl.when(tv_ref[i] > 0)
    def _():
        @pl.when(j == 0)
        def _():
            acc_ref[...] = jnp.zeros_like(acc_ref)
        x = x_ref[...].astype(jnp.bfloat16)
        g = jnp.dot(x, wg_ref[0].astype(jnp.bfloat16), preferred_element_type=jnp.float32)
        u = jnp.dot(x, wu_ref[0].astype(jnp.bfloat16), preferred_element_type=jnp.float32)
        a = (_silu(g) * u).astype(jnp.bfloat16)
        acc_ref[...] += jnp.dot(a, wd_ref[0].astype(jnp.bfloat16), preferred_element_type=jnp.float32)

    last = j == pl.num_programs(1) - 1

    @pl.when(jnp.logical_and(last, tv_ref[i] > 0))
    def _():
        o_ref[...] = acc_ref[...]

    @pl.when(jnp.logical_and(last, tv_ref[i] == 0))
    def _():
        o_ref[...] = jnp.zeros_like(o_ref)


def moe_experts(xb, tile_expert, tile_valid, w_gate, w_up, w_down):
    slots, d = xb.shape
    ff = w_gate.shape[2]
    n_tiles = slots // MOE_TM
    grid_spec = pltpu.PrefetchScalarGridSpec(
        num_scalar_prefetch=2,
        grid=(n_tiles, ff // MOE_TF),
        in_specs=[
            pl.BlockSpec((MOE_TM, d), lambda i, j, te, tv: (i, 0)),
            pl.BlockSpec((1, d, MOE_TF), lambda i, j, te, tv: (te[i], 0, j)),
            pl.BlockSpec((1, d, MOE_TF), lambda i, j, te, tv: (te[i], 0, j)),
            pl.BlockSpec((1, MOE_TF, d), lambda i, j, te, tv: (te[i], j, 0)),
        ],
        out_specs=pl.BlockSpec((MOE_TM, d), lambda i, j, te, tv: (i, 0)),
        scratch_shapes=[pltpu.VMEM((MOE_TM, d), jnp.float32)],
    )
    return pl.pallas_call(
        _moe_kernel,
        grid_spec=grid_spec,
        out_shape=jax.ShapeDtypeStruct((slots, d), jnp.float32),
        compiler_params=pltpu.CompilerParams(dimension_semantics=("arbitrary", "arbitrary"),
                                             vmem_limit_bytes=VMEM_LIMIT_BYTES),
        name="moe_experts",
    )(tile_expert, tile_valid, xb, w_gate, w_up, w_down)


def moe_swiglu(h, router, w_gate, w_up, w_down):
    b, t, d = h.shape
    n = b * t
    hf = h.reshape(n, d)
    logits = jnp.dot(hf, router, precision=lax.Precision.HIGHEST)
    top_val, top_idx = lax.top_k(logits, TOP_K)
    gate = jax.nn.softmax(top_val, axis=-1)
    flat_e = top_idx.reshape(-1)
    onehot = (flat_e[:, None] == jnp.arange(N_EXPERTS, dtype=flat_e.dtype)[None, :]).astype(jnp.int32)
    csum = jnp.cumsum(onehot, axis=0)
    counts = csum[-1]
    rank = jnp.sum((csum - onehot) * onehot, axis=1)
    padded = (counts + MOE_TM - 1) // MOE_TM * MOE_TM
    pad_end = jnp.cumsum(padded)
    pad_start = pad_end - padded
    slot = (pad_start[flat_e] + rank).astype(jnp.int32)
    n_tiles = (n * TOP_K) // MOE_TM + N_EXPERTS
    slots = n_tiles * MOE_TM
    tok = jnp.arange(n * TOP_K, dtype=jnp.int32) // TOP_K
    slot_tok = jnp.zeros((slots,), jnp.int32).at[slot].set(tok)
    tile_start = jnp.arange(n_tiles, dtype=jnp.int32) * MOE_TM
    tile_expert = jnp.minimum(jnp.sum(pad_end[None, :] <= tile_start[:, None], axis=1),
                              N_EXPERTS - 1).astype(jnp.int32)
    tile_valid = (tile_start < pad_end[-1]).astype(jnp.int32)
    yb = moe_experts(hf[slot_tok], tile_expert, tile_valid, w_gate, w_up, w_down)
    slot2 = slot.reshape(n, TOP_K)
    out = gate[:, 0:1] * yb[slot2[:, 0]] + gate[:, 1:2] * yb[slot2[:, 1]]
    return out.reshape(b, t, d)


def _row_select(i, tm, n_ctx, ctx_vec, lat_vec):
    if n_ctx == 0:
        return lat_vec
    row = i * tm + lax.broadcasted_iota(jnp.int32, (tm, 1), 0)
    return jnp.where(row < n_ctx, ctx_vec, lat_vec)


def _post_norm_rows(x, update, g, b):
    y = DEEPNORM_ALPHA * x + update
    mu = jnp.mean(y, axis=-1, keepdims=True)
    yc = y - mu
    var = jnp.mean(yc * yc, axis=-1, keepdims=True)
    return yc * lax.rsqrt(var + LN_EPS) * g + b


def _matmul_postnorm_kernel(a_ref, w_ref, x_ref, lat_ref, ctx_ref, g_ref, b_ref, o_ref, *, n_ctx):
    tm = a_ref.shape[1]
    o = jnp.dot(a_ref[0].astype(jnp.bfloat16), w_ref[...], preferred_element_type=jnp.float32)
    gate = _row_select(pl.program_id(1), tm, n_ctx, ctx_ref[...], lat_ref[0])
    o_ref[0] = _post_norm_rows(x_ref[0], gate * o, g_ref[...], b_ref[...])


def matmul_postnorm(a, w, x, gate_lat, gate_ctx, ln_g, ln_b, n_ctx):
    b, length, k = a.shape
    d = w.shape[1]
    tm = _pick_tile(length, (768, 512, 256))
    vec = pl.BlockSpec((1, d), lambda bi, i: (0, 0))
    return pl.pallas_call(
        functools.partial(_matmul_postnorm_kernel, n_ctx=n_ctx),
        grid=(b, length // tm),
        in_specs=[pl.BlockSpec((1, tm, k), lambda bi, i: (bi, i, 0)),
                  pl.BlockSpec((k, d), lambda bi, i: (0, 0)),
                  pl.BlockSpec((1, tm, d), lambda bi, i: (bi, i, 0)),
                  pl.BlockSpec((1, 1, d), lambda bi, i: (bi, 0, 0)), vec, vec, vec],
        out_specs=pl.BlockSpec((1, tm, d), lambda bi, i: (bi, i, 0)),
        out_shape=jax.ShapeDtypeStruct((b, length, d), jnp.float32),
        compiler_params=pltpu.CompilerParams(dimension_semantics=("parallel", "parallel"),
                                             vmem_limit_bytes=VMEM_LIMIT_BYTES),
        name="matmul_postnorm",
    )(a, w.astype(jnp.bfloat16), x, gate_lat[:, None, :], gate_ctx[None, :], ln_g[None, :], ln_b[None, :])


def _ffn_postnorm_kernel(x_ref, lat_ref, ctx_ref, wg_ref, wu_ref, wd_ref, g_ref, b_ref, o_ref, h_ref, acc_ref,
                         *, n_ctx):
    i = pl.program_id(1)
    j = pl.program_id(2)
    tm = x_ref.shape[1]

    def vec(k):
        return _row_select(i, tm, n_ctx, ctx_ref[k:k + 1, :], lat_ref[0, k:k + 1, :])

    @pl.when(j == 0)
    def _():
        h_ref[...] = (x_ref[0] * (1.0 + vec(1)) + vec(0)).astype(jnp.bfloat16)
        acc_ref[...] = jnp.zeros_like(acc_ref)

    h = h_ref[...]
    gt = jnp.dot(h, wg_ref[...], preferred_element_type=jnp.float32)
    up = jnp.dot(h, wu_ref[...], preferred_element_type=jnp.float32)
    acc_ref[...] += jnp.dot((_silu(gt) * up).astype(jnp.bfloat16), wd_ref[...], preferred_element_type=jnp.float32)

    @pl.when(j == pl.num_programs(2) - 1)
    def _():
        o_ref[0] = _post_norm_rows(x_ref[0], vec(2) * acc_ref[...], g_ref[...], b_ref[...])


def ffn_postnorm(x, mod_lat, mod_ctx, w_gate, w_up, w_down, ln_g, ln_b, n_ctx):
    b, length, d = x.shape
    ff = w_gate.shape[1]
    tm = _pick_tile(length, (384, 512, 256))
    tf = _pick_tile(ff, (1408, 512, 256))
    vec = pl.BlockSpec((1, d), lambda bi, i, j: (0, 0))
    return pl.pallas_call(
        functools.partial(_ffn_postnorm_kernel, n_ctx=n_ctx),
        grid=(b, length // tm, ff // tf),
        in_specs=[pl.BlockSpec((1, tm, d), lambda bi, i, j: (bi, i, 0)),
                  pl.BlockSpec((1, 3, d), lambda bi, i, j: (bi, 0, 0)),
                  pl.BlockSpec((3, d), lambda bi, i, j: (0, 0)),
                  pl.BlockSpec((d, tf), lambda bi, i, j: (0, j)),
                  pl.BlockSpec((d, tf), lambda bi, i, j: (0, j)),
                  pl.BlockSpec((tf, d), lambda bi, i, j: (j, 0)), vec, vec],
        out_specs=pl.BlockSpec((1, tm, d), lambda bi, i, j: (bi, i, 0)),
        out_shape=jax.ShapeDtypeStruct((b, length, d), jnp.float32),
        scratch_shapes=[pltpu.VMEM((tm, d), jnp.bfloat16), pltpu.VMEM((tm, d), jnp.float32)],
        compiler_params=pltpu.CompilerParams(dimension_semantics=("parallel", "parallel", "arbitrary"),
                                             vmem_limit_bytes=VMEM_LIMIT_BYTES),
        name="ffn_postnorm",
    )(x, mod_lat, mod_ctx, w_gate.astype(jnp.bfloat16), w_up.astype(jnp.bfloat16), w_down.astype(jnp.bfloat16),
      ln_g[None, :], ln_b[None, :])


def _even_layer(x, xc, sc, scc, mod_w, mod_b, w_in, mla_q_norm, mla_wq_up, mla_kv_norm, mla_wkv_up,
                rwkv_conv, rwkv_w0_f, rwkv_w2_f, rwkv_w0_b, rwkv_w2_b, rwkv_a0_f, rwkv_a2_f, rwkv_a0_b,
                rwkv_a2_b, rwkv_g2, rwkv_k_k, rwkv_k_a, rwkv_r_k, rwkv_lnx_g, rwkv_lnx_b, w_out,
                ln1_g, ln1_b, ffn_w_gate, ffn_w_up, ffn_w_down, ln2_g, ln2_b):
    n_ctx = xc.shape[1]
    m = _adaln(sc, mod_w, mod_b)
    mc = _adaln(scc, mod_w, mod_b)
    xa = jnp.concatenate([xc, x], axis=1)
    is_ctx = (jnp.arange(xa.shape[1]) < n_ctx)[None, :, None]

    def rows(k):
        return jnp.where(is_ctx, mc[k][None, None, :], m[k][:, None, :])

    h = _modulate(xa, rows(0), rows(1)).astype(jnp.bfloat16)
    o_mla = _mla_mixer(h, n_ctx, w_in, mla_q_norm, mla_wq_up, mla_kv_norm, mla_wkv_up)
    o_rwkv = _rwkv7_mixer(mm(h, w_in[:, MLA_IN:]), n_ctx, rwkv_conv, rwkv_w0_f, rwkv_w2_f, rwkv_w0_b, rwkv_w2_b,
                          rwkv_a0_f, rwkv_a2_f, rwkv_a0_b, rwkv_a2_b, rwkv_g2, rwkv_k_k, rwkv_k_a,
                          rwkv_r_k, rwkv_lnx_g, rwkv_lnx_b)
    mix = jnp.concatenate([o_mla.astype(jnp.bfloat16), o_rwkv], axis=-1)
    xa = matmul_postnorm(mix, w_out, xa, m[2], mc[2], ln1_g, ln1_b, n_ctx)
    xa = ffn_postnorm(xa, jnp.stack(m[3:6], axis=1), jnp.stack(mc[3:6], axis=0),
                      ffn_w_gate, ffn_w_up, ffn_w_down, ln2_g, ln2_b, n_ctx)
    return xa[:, n_ctx:], xa[:, :n_ctx]


def _odd_layer(x, xc, sc, scc, mod_w, mod_b, w_in, ret_decay_f, ret_decay_b, w_out, ln1_g, ln1_b,
               router, moe_w_gate, moe_w_up, moe_w_down, ln2_g, ln2_b):
    m = _adaln(sc, mod_w, mod_b)
    c_shift, c_scale = jnp.split(
        jnp.dot(scc, mod_w[:, :2 * D_MODEL], precision=lax.Precision.HIGHEST) + mod_b[:2 * D_MODEL], 2)
    z = mm(_modulate(x, m[0][:, None, :], m[1][:, None, :]), w_in, jnp.bfloat16)
    zc = mm(_modulate(xc, c_shift, c_scale), w_in[:, RET_QK:2 * RET_QK + RET_VD], jnp.bfloat16)
    mix = retention_mixer(z, zc, ret_decay_f, ret_decay_b)
    x = matmul_postnorm(mix, w_out, x, m[2], jnp.zeros_like(m[2][0]), ln1_g, ln1_b, 0)
    y = moe_swiglu(_modulate(x, m[3][:, None, :], m[4][:, None, :]), router, moe_w_gate, moe_w_up, moe_w_down)
    return _post_norm(x, m[5][:, None, :] * y, ln2_g, ln2_b)


def kernel(x, c, ctx, c_ctx, l0_mod_w, l0_mod_b, l0_w_in, l0_mla_q_norm, l0_mla_wq_up, l0_mla_kv_norm, l0_mla_wkv_up, l0_rwkv_conv, l0_rwkv_w0_f, l0_rwkv_w2_f, l0_rwkv_w0_b, l0_rwkv_w2_b, l0_rwkv_a0_f, l0_rwkv_a2_f, l0_rwkv_a0_b, l0_rwkv_a2_b, l0_rwkv_g2, l0_rwkv_k_k, l0_rwkv_k_a, l0_rwkv_r_k, l0_rwkv_lnx_g, l0_rwkv_lnx_b, l0_w_out, l0_ln1_g, l0_ln1_b, l0_ffn_w_gate, l0_ffn_w_up, l0_ffn_w_down, l0_ln2_g, l0_ln2_b, l1_mod_w, l1_mod_b, l1_w_in, l1_ret_decay_f, l1_ret_decay_b, l1_w_out, l1_ln1_g, l1_ln1_b, l1_router, l1_moe_w_gate, l1_moe_w_up, l1_moe_w_down, l1_ln2_g, l1_ln2_b):
    even_params = (l0_mod_w, l0_mod_b, l0_w_in, l0_mla_q_norm, l0_mla_wq_up, l0_mla_kv_norm, l0_mla_wkv_up,
                   l0_rwkv_conv, l0_rwkv_w0_f, l0_rwkv_w2_f, l0_rwkv_w0_b, l0_rwkv_w2_b, l0_rwkv_a0_f,
                   l0_rwkv_a2_f, l0_rwkv_a0_b, l0_rwkv_a2_b, l0_rwkv_g2, l0_rwkv_k_k, l0_rwkv_k_a, l0_rwkv_r_k,
                   l0_rwkv_lnx_g, l0_rwkv_lnx_b, l0_w_out, l0_ln1_g, l0_ln1_b, l0_ffn_w_gate, l0_ffn_w_up,
                   l0_ffn_w_down, l0_ln2_g, l0_ln2_b)
    odd_params = (l1_mod_w, l1_mod_b, l1_w_in, l1_ret_decay_f, l1_ret_decay_b, l1_w_out, l1_ln1_g, l1_ln1_b,
                  l1_router, l1_moe_w_gate, l1_moe_w_up, l1_moe_w_down, l1_ln2_g, l1_ln2_b)
    sc = _silu(c)
    scc = _silu(c_ctx)
    x, xc = _even_layer(x, ctx, sc, scc, *even_params)
    return _odd_layer(x, xc, sc, scc, *odd_params)
```

```python
import functools

import jax
import jax.numpy as jnp
import numpy as np
from jax import lax
from jax.experimental import pallas as pl
from jax.experimental.pallas import tpu as pltpu

D_MODEL = 1024
DEPTH = 2
GRID_W = 64
ROPE_BASE = 10000.0
DEEPNORM_ALPHA = (2 * DEPTH) ** 0.25
LN_EPS = 1e-5

MLA_HEADS = 8
MLA_Q_RANK = 256
MLA_KV_RANK = 128
MLA_NOPE = 64
MLA_ROPE = 32
MLA_V = 64
MLA_IN = MLA_Q_RANK + MLA_KV_RANK + MLA_ROPE
MLA_OUT = MLA_HEADS * MLA_V

RWKV_HEADS = 8
RWKV_HEAD = 64
RWKV_DIM = RWKV_HEADS * RWKV_HEAD
RWKV_DECAY_LORA = 64
RWKV_AAA_LORA = 64
RWKV_GATE_LORA = 128
RWKV_LNX_EPS = 64e-5

RET_HEADS = 4
RET_KEY = 256
RET_VAL = 512
RET_TC = 256
RET_QK = RET_HEADS * RET_KEY
RET_VD = RET_HEADS * RET_VAL

N_EXPERTS = 8
TOP_K = 2
MOE_TM = 1024
MOE_TF = 512

LANES = 128
VMEM_LIMIT_BYTES = 48 * 1024 * 1024


def _matmul_kernel(x_ref, w_ref, o_ref):
    o_ref[...] = jnp.dot(x_ref[...].astype(jnp.bfloat16), w_ref[...],
                         preferred_element_type=jnp.float32).astype(o_ref.dtype)


def _pick_tile(n, candidates):
    for c in candidates:
        if n % c == 0:
            return c
    raise ValueError(f"no tile for {n}")


def pmatmul(x, w, out_dtype=jnp.float32):
    m, k = x.shape
    n = w.shape[1]
    n_pad = (-n) % LANES
    w = w.astype(jnp.bfloat16)
    if n_pad:
        w = jnp.pad(w, ((0, 0), (0, n_pad)))
    np_ = n + n_pad
    tm = _pick_tile(m, (1024, 512, 256, 128, 8) if k <= 1024 else (512, 256, 128, 8))
    tn = _pick_tile(np_, (1024, 768, 640, 512, 384, 256, 128))
    out = pl.pallas_call(
        _matmul_kernel,
        grid=(m // tm, np_ // tn),
        in_specs=[pl.BlockSpec((tm, k), lambda i, j: (i, 0)),
                  pl.BlockSpec((k, tn), lambda i, j: (0, j))],
        out_specs=pl.BlockSpec((tm, tn), lambda i, j: (i, j)),
        out_shape=jax.ShapeDtypeStruct((m, np_), out_dtype),
        compiler_params=pltpu.CompilerParams(
            dimension_semantics=("parallel", "parallel"),
            vmem_limit_bytes=VMEM_LIMIT_BYTES),
        name="matmul",
    )(x, w)
    return out[:, :n] if n_pad else out


def mm(x, w, out_dtype=jnp.float32):
    lead = x.shape[:-1]
    out = pmatmul(x.reshape(-1, x.shape[-1]).astype(jnp.bfloat16), w, out_dtype)
    return out.reshape(*lead, w.shape[1])


def _silu(t):
    return t * jax.nn.sigmoid(t)


def _normalize(t, eps):
    mu = jnp.mean(t, axis=-1, keepdims=True)
    var = jnp.mean(jnp.square(t - mu), axis=-1, keepdims=True)
    return (t - mu) * lax.rsqrt(var + eps)


def _layer_norm(t, g, b):
    return _normalize(t, LN_EPS) * g + b


def _rms_norm(t, g, eps=1e-6):
    return t * lax.rsqrt(jnp.mean(t * t, axis=-1, keepdims=True) + eps) * g


def _l2_normalize(t, eps=1e-12):
    return t / jnp.maximum(jnp.linalg.norm(t, axis=-1, keepdims=True), eps)


def _post_norm(x, update, g, b):
    return _layer_norm(DEEPNORM_ALPHA * x + update, g, b)


def _modulate(h, shift, scale):
    return h * (1.0 + scale) + shift


def _adaln(cond, mod_w, mod_b):
    return jnp.split(jnp.dot(cond, mod_w, precision=lax.Precision.HIGHEST) + mod_b, 6, axis=-1)


def _grid_positions(n_tokens):
    rows = n_tokens // GRID_W
    row = jnp.repeat(jnp.arange(rows, dtype=jnp.float32), GRID_W)
    col = jnp.tile(jnp.arange(GRID_W, dtype=jnp.float32), rows)
    return row, col


SOFTMAX_FLOOR = -1e30


FLASH_ROW_GROUPS = 4


def _flash_kernel(q_ref, kt_ref, v_ref, o_ref, m_ref, acc_ref, *, dv):
    j = pl.program_id(3)

    @pl.when(j == 0)
    def _():
        m_ref[...] = jnp.full_like(m_ref, SOFTMAX_FLOOR)
        acc_ref[...] = jnp.zeros_like(acc_ref)

    rows = q_ref.shape[2] // FLASH_ROW_GROUPS
    kt = kt_ref[0, 0]
    v = v_ref[0, 0]
    for u in range(FLASH_ROW_GROUPS):
        sl = slice(u * rows, (u + 1) * rows)
        s = jnp.dot(q_ref[0, 0, sl, :], kt, preferred_element_type=jnp.float32)
        m_prev = m_ref[sl, :]
        m_new = jnp.maximum(m_prev, jnp.max(s, axis=-1, keepdims=True))
        alpha = jnp.exp(m_prev - m_new)
        p = jnp.exp(s - m_new).astype(jnp.bfloat16)
        acc_ref[sl, :] = alpha * acc_ref[sl, :] + jnp.dot(p, v, preferred_element_type=jnp.float32)
        m_ref[sl, :] = m_new

    @pl.when(j == pl.num_programs(3) - 1)
    def _():
        acc = acc_ref[...]
        o_ref[0, 0] = acc[:, :dv] / acc[:, dv:dv + 1]


def flash_attention(q, kt, v1, dv, tq, tk):
    b, h, t, dq = q.shape
    s = kt.shape[3]
    return pl.pallas_call(
        functools.partial(_flash_kernel, dv=dv),
        grid=(b, h, t // tq, s // tk),
        in_specs=[pl.BlockSpec((1, 1, tq, dq), lambda bi, hi, i, j: (bi, hi, i, 0)),
                  pl.BlockSpec((1, 1, dq, tk), lambda bi, hi, i, j: (bi, hi, 0, j)),
                  pl.BlockSpec((1, 1, tk, LANES), lambda bi, hi, i, j: (bi, hi, j, 0))],
        out_specs=pl.BlockSpec((1, 1, tq, dv), lambda bi, hi, i, j: (bi, hi, i, 0)),
        out_shape=jax.ShapeDtypeStruct((b, h, t, dv), jnp.float32),
        scratch_shapes=[pltpu.VMEM((tq, 1), jnp.float32), pltpu.VMEM((tq, LANES), jnp.float32)],
        compiler_params=pltpu.CompilerParams(
            dimension_semantics=("parallel", "parallel", "parallel", "arbitrary"),
            vmem_limit_bytes=VMEM_LIMIT_BYTES),
        name="flash_attention",
    )(q, kt, v1)


def _block_attention(q, k, v):
    dq = q.shape[-1]
    dv = v.shape[-1]
    b, s, h, _ = k.shape
    qh = jnp.transpose(q * dq ** -0.5, (0, 2, 1, 3)).astype(jnp.bfloat16)
    kt = jnp.transpose(k, (0, 2, 3, 1)).astype(jnp.bfloat16)
    v1 = jnp.concatenate([v, jnp.ones((b, s, h, 1), v.dtype), jnp.zeros((b, s, h, LANES - dv - 1), v.dtype)],
                         axis=-1)
    v1 = jnp.transpose(v1, (0, 2, 1, 3)).astype(jnp.bfloat16)
    tq = _pick_tile(q.shape[1], (1024, 256))
    o = flash_attention(qh, kt, v1, dv, tq, s)
    return jnp.transpose(o, (0, 2, 1, 3))


_Q8 = MLA_ROPE // 4
ROPE_PARTNER = np.concatenate([np.arange(_Q8, 2 * _Q8), np.arange(0, _Q8),
                               np.arange(3 * _Q8, 4 * _Q8), np.arange(2 * _Q8, 3 * _Q8)])
ROPE_SIGN = np.concatenate([-np.ones(_Q8), np.ones(_Q8), -np.ones(_Q8), np.ones(_Q8)]).astype(np.float32)


def _mla_rope_tables(n_ctx, t):
    row, col = _grid_positions(t)
    zero = jnp.zeros((n_ctx,), jnp.float32)
    row, col = jnp.concatenate([zero, row]), jnp.concatenate([zero, col])
    inv_freq = ROPE_BASE ** (-jnp.arange(_Q8, dtype=jnp.float32) / _Q8)
    ang = jnp.concatenate([row[:, None] * inv_freq] * 2 + [col[:, None] * inv_freq] * 2, axis=1)
    return jnp.cos(ang), jnp.sin(ang)


def _mla_mixer(h, n_ctx, w_in, q_norm, wq_up, kv_norm, wkv_up):
    b, length, _ = h.shape
    cos, sin = _mla_rope_tables(n_ctx, length - n_ctx)
    k_cols = np.arange(MLA_Q_RANK + MLA_KV_RANK, MLA_IN)
    w_z = jnp.concatenate([w_in[:, :MLA_IN], w_in[:, k_cols[ROPE_PARTNER]] * ROPE_SIGN], axis=1)
    z = mm(h, w_z)
    cq, ckv = z[..., :MLA_Q_RANK], z[..., MLA_Q_RANK:MLA_Q_RANK + MLA_KV_RANK]
    k_rope = z[..., k_cols[0]:MLA_IN] * cos + z[..., MLA_IN:] * sin
    dq = MLA_NOPE + MLA_ROPE
    head_cols = np.arange(MLA_HEADS)[:, None] * dq
    q_rope_cols = (head_cols + MLA_NOPE + np.arange(MLA_ROPE)[None, :])
    wq_partner = jnp.zeros_like(wq_up).at[:, q_rope_cols.reshape(-1)].set(
        wq_up[:, (head_cols + MLA_NOPE + ROPE_PARTNER[None, :]).reshape(-1)] * jnp.tile(ROPE_SIGN, MLA_HEADS))
    q2 = mm(_rms_norm(cq, q_norm), jnp.concatenate([wq_up, wq_partner], axis=1))
    one, nil = jnp.ones((length, MLA_NOPE), jnp.float32), jnp.zeros((length, MLA_NOPE), jnp.float32)
    cos_q = jnp.tile(jnp.concatenate([one, cos], axis=1), (1, MLA_HEADS))
    sin_q = jnp.tile(jnp.concatenate([nil, sin], axis=1), (1, MLA_HEADS))
    q = (q2[..., :MLA_HEADS * dq] * cos_q + q2[..., MLA_HEADS * dq:] * sin_q).reshape(b, length, MLA_HEADS, dq)
    kv = mm(_rms_norm(ckv, kv_norm), wkv_up).reshape(b, length, MLA_HEADS, MLA_NOPE + MLA_V)
    k = jnp.concatenate([kv[..., :MLA_NOPE],
                         jnp.broadcast_to(k_rope[:, :, None, :], (b, length, MLA_HEADS, MLA_ROPE))], axis=-1)
    v = kv[..., MLA_NOPE:]
    o = _block_attention(q[:, n_ctx:], k, v)
    oc = _block_attention(q[:, :n_ctx], k[:, :n_ctx], v[:, :n_ctx])
    return jnp.concatenate([oc, o], axis=1).reshape(b, length, MLA_OUT)


SCAN_BLOCK = 128
SCAN_GROUP = 8
N_PAIRS = RWKV_HEADS // 2
PAIR_ROWS = N_PAIRS * RWKV_HEAD


def _to_state_tiles(vblk):
    halves = [[], []]
    for p in range(N_PAIRS):
        tr = vblk[:, p * LANES:(p + 1) * LANES].T
        for half in range(2):
            c = slice(half * RWKV_HEAD, (half + 1) * RWKV_HEAD)
            halves[half].append(jnp.concatenate([tr[:RWKV_HEAD, c], tr[RWKV_HEAD:, c]], axis=1))
    return [jnp.concatenate(h, axis=0) for h in halves]


def _from_state_tiles(tiles):
    cols = []
    for p in range(N_PAIRS):
        r = slice(p * RWKV_HEAD, (p + 1) * RWKV_HEAD)
        tr = jnp.concatenate([tiles[0][r, :], tiles[1][r, :]], axis=0).T
        cols.append(jnp.concatenate(
            [jnp.concatenate([tr[:RWKV_HEAD, :RWKV_HEAD], tr[:RWKV_HEAD, RWKV_HEAD:]], axis=0),
             jnp.concatenate([tr[RWKV_HEAD:, :RWKV_HEAD], tr[RWKV_HEAD:, RWKV_HEAD:]], axis=0)], axis=1))
    return jnp.concatenate(cols, axis=1)


def _scan_kernel(rf_ref, rb_ref, af_ref, ab_ref, vf_ref, vb_ref, wf_ref, wb_ref, kf_ref, kb_ref,
                 bf_ref, bb_ref, wred_ref, yf_ref, yb_ref, s_ref, vt_ref, yc_ref):
    nb = rf_ref.shape[0]

    @pl.when(pl.program_id(0) == 0)
    def _():
        s_ref[...] = jnp.zeros_like(s_ref)

    lane = lax.broadcasted_iota(jnp.int32, (PAIR_ROWS, LANES), 1)
    wred = wred_ref[...]
    refs = ((rf_ref, af_ref, vf_ref, wf_ref, kf_ref, bf_ref, yf_ref),
            (rb_ref, ab_ref, vb_ref, wb_ref, kb_ref, bb_ref, yb_ref))

    for d in range(2):
        for bi in range(nb):
            tiles = _to_state_tiles(refs[d][2][bi])
            for half in range(2):
                vt_ref[d, bi, half] = tiles[half].astype(jnp.bfloat16)

    def rows(ref, bi, base, u):
        return jnp.concatenate(
            [jnp.broadcast_to(ref.at[bi, pl.ds(base, SCAN_GROUP), :][u:u + 1, p * LANES:(p + 1) * LANES],
                              (RWKV_HEAD, LANES)) for p in range(N_PAIRS)], axis=0)

    def collect(d, bi, t, yb, valid):
        mask = jnp.logical_and(lane % RWKV_HEAD == t % RWKV_HEAD, valid)
        half = t // RWKV_HEAD
        yc_ref[d, bi, half] = jnp.where(mask, yb, yc_ref[d, bi, half])

    n_groups = SCAN_BLOCK // SCAN_GROUP

    def group(gi, carry):
        for u in range(SCAN_GROUP):
            for d in range(2):
                r_ref, a_ref, _, w_ref, k_ref, b_ref, _ = refs[d]
                g = gi if d == 0 else n_groups - 1 - gi
                base = pl.multiple_of(g * SCAN_GROUP, SCAN_GROUP)
                uu = u if d == 0 else SCAN_GROUP - 1 - u
                t = base + uu
                if u == 0:
                    gp = jnp.maximum(gi - 1, 0) if d == 0 else jnp.minimum(n_groups - gi, n_groups - 1)
                    pbase, pu = pl.multiple_of(gp * SCAN_GROUP, SCAN_GROUP), (SCAN_GROUP - 1 if d == 0 else 0)
                else:
                    pbase, pu = base, (uu - 1 if d == 0 else uu + 1)
                tp = pbase + pu
                sel = lane % RWKV_HEAD == t % RWKV_HEAD
                prs = []
                for bi in range(nb):
                    s = s_ref[d, bi]
                    pa = (s * rows(a_ref, bi, base, uu)).astype(jnp.bfloat16)
                    prs.append((s * rows(r_ref, bi, pbase, pu)).astype(jnp.bfloat16))
                    pv = jnp.where(sel, vt_ref[d, bi, t // RWKV_HEAD], jnp.zeros((), jnp.bfloat16))
                    red = jnp.dot(jnp.concatenate([pa, pv], axis=1), wred, preferred_element_type=jnp.float32)
                    s_ref[d, bi] = (s * rows(w_ref, bi, base, uu) + red[:, :LANES] * rows(b_ref, bi, base, uu)
                                    + red[:, LANES:] * rows(k_ref, bi, base, uu))
                valid = True if u > 0 else gi >= 1
                for b0 in range(0, nb, 2):
                    ys = jnp.dot(jnp.concatenate(prs[b0:b0 + 2], axis=1), wred, preferred_element_type=jnp.float32)
                    collect(d, b0, tp, ys[:, :LANES], valid)
                    collect(d, b0 + 1, tp, ys[:, LANES:], valid)
        return carry

    lax.fori_loop(0, n_groups, group, 0)

    for d in range(2):
        r_ref, y_ref = refs[d][0], refs[d][6]
        t_last = SCAN_BLOCK - 1 if d == 0 else 0
        for b0 in range(0, nb, 2):
            prs = [(s_ref[d, bi] * rows(r_ref, bi, t_last - t_last % SCAN_GROUP, t_last % SCAN_GROUP)).astype(jnp.bfloat16)
                   for bi in (b0, b0 + 1)]
            ys = jnp.dot(jnp.concatenate(prs, axis=1), wred, preferred_element_type=jnp.float32)
            collect(d, b0, t_last, ys[:, :LANES], True)
            collect(d, b0 + 1, t_last, ys[:, LANES:], True)
        for bi in range(nb):
            y_ref[bi] = _from_state_tiles([yc_ref[d, bi, 0], yc_ref[d, bi, 1]])


def rwkv_scan(r, v, a, w_f, k_f, b_f, w_b, k_b, b_b, n_ctx):
    nb, length, _ = r.shape
    nblk = length // SCAN_BLOCK
    nblk_ctx = n_ctx // SCAN_BLOCK
    j = np.arange(2 * LANES)
    wred = (j[:, None] // RWKV_HEAD) == (j[None, :] // RWKV_HEAD)

    def fwd(i):
        return i

    def bwd(i):
        return jnp.where(i < nblk_ctx, nblk_ctx - 1 - i, nblk + nblk_ctx - 1 - i)

    def row_spec(blk):
        return pl.BlockSpec((nb, SCAN_BLOCK, RWKV_DIM), lambda i: (0, blk(i), 0))

    y_shape = jax.ShapeDtypeStruct((nb, length, RWKV_DIM), jnp.float32)
    tile_shape = (2, nb, 2, PAIR_ROWS, LANES)
    return pl.pallas_call(
        _scan_kernel,
        grid=(nblk,),
        in_specs=[row_spec(fwd), row_spec(bwd)] * 6 + [pl.BlockSpec((2 * LANES, 2 * LANES), lambda i: (0, 0))],
        out_specs=[row_spec(fwd), row_spec(bwd)],
        out_shape=[y_shape, y_shape],
        scratch_shapes=[pltpu.VMEM((2, nb, PAIR_ROWS, LANES), jnp.float32),
                        pltpu.VMEM(tile_shape, jnp.bfloat16), pltpu.VMEM(tile_shape, jnp.float32)],
        compiler_params=pltpu.CompilerParams(dimension_semantics=("arbitrary",),
                                             vmem_limit_bytes=VMEM_LIMIT_BYTES),
        name="rwkv_scan",
    )(r, r, a, a, v, v, w_f, w_b, k_f, k_b, b_f, b_b, jnp.asarray(wred, jnp.bfloat16))


FEAT_ROWS = 256


def _head_sums(x, ones_ref):
    hi = x.astype(jnp.bfloat16)
    lo = (x - hi.astype(jnp.float32)).astype(jnp.bfloat16)
    ones = ones_ref[...]
    return (jnp.dot(hi, ones, preferred_element_type=jnp.float32)
            + jnp.dot(lo, ones, preferred_element_type=jnp.float32))


def _softplus(x):
    return jnp.maximum(x, 0.0) + jnp.log(1.0 + jnp.exp(-jnp.abs(x)))


def _sigmoid(x):
    return 1.0 / (1.0 + jnp.exp(-x))


def _features_kernel(z_ref, prev_ref, next_ref, conv_ref, lora_ref, g2_ref, vec_ref, ones_ref,
                     r_ref, a_ref, v_ref, wf_ref, kf_ref, bf_ref, wb_ref, kb_ref, bb_ref, g_ref, rk_ref):
    rows = z_ref.shape[1]
    c0 = 3 * RWKV_DIM
    raw = z_ref[0, :, :c0]
    row_id = lax.broadcasted_iota(jnp.int32, (rows, 1), 0)
    before = jnp.where(row_id == 0, prev_ref[0, 0], pltpu.roll(raw, 1, axis=0))
    after = jnp.where(row_id == rows - 1, next_ref[0, 0], pltpu.roll(raw, rows - 1, axis=0))
    rkv = before * conv_ref[0:1, :] + raw * conv_ref[1:2, :] + after * conv_ref[2:3, :]
    r, k, v = rkv[:, :RWKV_DIM], rkv[:, RWKV_DIM:2 * RWKV_DIM], rkv[:, 2 * RWKV_DIM:]

    def vec(i):
        return vec_ref[i:i + 1, :]

    k_k, k_a, r_k, w0_f, w0_b, a0_f, a0_b = (vec(i) for i in range(7))
    kraw = k * k_k
    kk = kraw / jnp.maximum(jnp.sqrt(_head_sums(kraw * kraw, ones_ref)), 1e-12)

    lo = z_ref[0, :, c0:c0 + 2 * LANES]
    lo = jnp.where(lax.broadcasted_iota(jnp.int32, lo.shape, 1) < LANES, jnp.tanh(lo), lo)
    proj = jnp.dot(lo.astype(jnp.bfloat16), lora_ref[...], preferred_element_type=jnp.float32)
    gd = z_ref[0, :, c0 + 2 * LANES:]
    g_ref[0] = jnp.dot(_sigmoid(gd).astype(jnp.bfloat16), g2_ref[...], preferred_element_type=jnp.float32)

    ksum = None
    for d, (w0, a0, w_ref, k_ref, b_ref) in enumerate(((w0_f, a0_f, wf_ref, kf_ref, bf_ref),
                                                       (w0_b, a0_b, wb_ref, kb_ref, bb_ref))):
        logw = -_softplus(-(w0 + proj[:, d * RWKV_DIM:(d + 1) * RWKV_DIM])) - 0.5
        lr = _sigmoid(a0 + proj[:, (2 + d) * RWKV_DIM:(3 + d) * RWKV_DIM])
        kd = k * (1.0 + (lr - 1.0) * k_a)
        w_ref[0] = jnp.exp(-jnp.exp(logw))
        k_ref[0] = kd
        b_ref[0] = kk * lr
        ksum = kd if ksum is None else ksum + kd
    r_ref[0] = r
    a_ref[0] = -kk
    v_ref[0] = v
    rk_ref[0] = r * ksum * r_k


def _readout_kernel(yf_ref, yb_ref, rk_ref, v_ref, g_ref, vec_ref, ones_ref, o_ref):
    y = yf_ref[0] + yb_ref[0]
    mu = _head_sums(y, ones_ref) * (1.0 / RWKV_HEAD)
    yc = y - mu
    var = _head_sums(yc * yc, ones_ref) * (1.0 / RWKV_HEAD)
    yn = yc * lax.rsqrt(var + RWKV_LNX_EPS) * vec_ref[0:1, :] + vec_ref[1:2, :]
    o_ref[0] = ((yn + _head_sums(rk_ref[0], ones_ref) * v_ref[0]) * g_ref[0]).astype(o_ref.dtype)


def _rwkv7_mixer(z, n_ctx, conv_w, w0_f, w2_f, w0_b, w2_b, a0_f, a2_f, a0_b, a2_b, g2, k_k, k_a,
                 r_k, lnx_g, lnx_b):
    b, length, zin = z.shape
    c0 = 3 * RWKV_DIM
    nblk = length // FEAT_ROWS
    edge = jnp.arange(nblk) * FEAT_ROWS
    zero_row = jnp.zeros((b, 1, c0), jnp.float32)
    last_rows = z[:, FEAT_ROWS - 1::FEAT_ROWS, :c0]
    first_rows = z[:, ::FEAT_ROWS, :c0]
    prev_rows = jnp.concatenate([zero_row, last_rows[:, :-1]], axis=1)
    prev_rows = jnp.where(((edge == 0) | (edge == n_ctx))[None, :, None], 0.0, prev_rows)
    next_rows = jnp.concatenate([first_rows[:, 1:], zero_row], axis=1)
    next_rows = jnp.where(((edge + FEAT_ROWS == n_ctx) | (edge + FEAT_ROWS == length))[None, :, None], 0.0, next_rows)
    zl = jnp.zeros((RWKV_DECAY_LORA, RWKV_DIM), jnp.float32)
    lora = jnp.concatenate([jnp.concatenate([w2_f, zl, zl, zl], axis=1), jnp.concatenate([zl, w2_b, zl, zl], axis=1),
                            jnp.concatenate([zl, zl, a2_f, zl], axis=1), jnp.concatenate([zl, zl, zl, a2_b], axis=1)],
                           axis=0).astype(jnp.bfloat16)
    j = np.arange(RWKV_DIM)
    head_ones = jnp.asarray((j[:, None] // RWKV_HEAD) == (j[None, :] // RWKV_HEAD), jnp.bfloat16)
    vecs = jnp.stack([k_k, k_a, r_k.reshape(-1), w0_f, w0_b, a0_f, a0_b, jnp.zeros_like(k_k)], axis=0)
    blk = pl.BlockSpec((1, FEAT_ROWS, RWKV_DIM), lambda bi, i: (bi, i, 0))
    full = lambda shape: pl.BlockSpec(shape, lambda bi, i: (0,) * len(shape))
    edge_spec = pl.BlockSpec((1, 1, 1, c0), lambda bi, i: (bi, i, 0, 0))
    out_shape = jax.ShapeDtypeStruct((b, length, RWKV_DIM), jnp.float32)
    r, a, v, w_f, k_f, b_f, w_b, k_b, b_b, g, rk = pl.pallas_call(
        _features_kernel,
        grid=(b, nblk),
        in_specs=[pl.BlockSpec((1, FEAT_ROWS, zin), lambda bi, i: (bi, i, 0)), edge_spec, edge_spec,
                  full((3, c0)), full((4 * RWKV_DECAY_LORA, 4 * RWKV_DIM)), full((RWKV_GATE_LORA, RWKV_DIM)),
                  full((8, RWKV_DIM)), full((RWKV_DIM, RWKV_DIM))],
        out_specs=[blk] * 11,
        out_shape=[out_shape] * 11,
        compiler_params=pltpu.CompilerParams(dimension_semantics=("parallel", "parallel"),
                                             vmem_limit_bytes=VMEM_LIMIT_BYTES),
        name="rwkv_features",
    )(z, prev_rows[:, :, None, :], next_rows[:, :, None, :], conv_w, lora, g2.astype(jnp.bfloat16), vecs, head_ones)
    y_f, y_b = rwkv_scan(r, v, a, w_f, k_f, b_f, w_b, k_b, b_b, n_ctx)
    return pl.pallas_call(
        _readout_kernel,
        grid=(b, nblk),
        in_specs=[blk] * 5 + [full((2, RWKV_DIM)), full((RWKV_DIM, RWKV_DIM))],
        out_specs=blk,
        out_shape=jax.ShapeDtypeStruct((b, length, RWKV_DIM), jnp.bfloat16),
        compiler_params=pltpu.CompilerParams(dimension_semantics=("parallel", "parallel"),
                                             vmem_limit_bytes=VMEM_LIMIT_BYTES),
        name="rwkv_readout",
    )(y_f, y_b, rk, v, g, jnp.stack([lnx_g, lnx_b], axis=0), head_ones)


def _rope_halves(t, cos, sin):
    parts = []
    for s in range(2):
        u = t[:, s * LANES:(s + 1) * LANES]
        parts.append(u * cos[:, s * LANES:(s + 1) * LANES]
                     + pltpu.roll(u, LANES // 2, axis=1) * sin[:, s * LANES:(s + 1) * LANES])
    return jnp.concatenate(parts, axis=1)


def _retention_kernel(*refs, has_prev):
    if has_prev:
        (q_ref, k_ref, v_ref, g_ref, kc_ref, vc_ref, cos_ref, sin_ref, dmat_ref, qdec_ref, kdec_ref,
         kcdec_ref, cdec_ref, prev_ref, o_ref, s_ref) = refs
    else:
        (q_ref, k_ref, v_ref, g_ref, kc_ref, vc_ref, cos_ref, sin_ref, dmat_ref, qdec_ref, kdec_ref,
         kcdec_ref, cdec_ref, o_ref, s_ref) = refs
        prev_ref = None
    scale = RET_KEY ** -0.5

    @pl.when(pl.program_id(2) == 0)
    def _():
        kc = (kc_ref[0].astype(jnp.float32) * kcdec_ref[0] * scale).astype(jnp.bfloat16)
        s_ref[...] = lax.dot_general(kc, vc_ref[0].astype(jnp.bfloat16), (((0,), (0,)), ((), ())),
                                     preferred_element_type=jnp.float32)

    cos = cos_ref[...]
    sin = sin_ref[...]
    q = _rope_halves(q_ref[0].astype(jnp.float32), cos, sin)
    k = _rope_halves(k_ref[0].astype(jnp.float32), cos, sin) * scale
    v = v_ref[0].astype(jnp.bfloat16)
    s = s_ref[...]
    att = lax.dot_general(q.astype(jnp.bfloat16), k.astype(jnp.bfloat16), (((1,), (1,)), ((), ())),
                          preferred_element_type=jnp.float32) * dmat_ref[0]
    o = (jnp.dot(att.astype(jnp.bfloat16), v, preferred_element_type=jnp.float32)
         + jnp.dot((q * qdec_ref[0]).astype(jnp.bfloat16), s.astype(jnp.bfloat16),
                   preferred_element_type=jnp.float32))
    s_ref[...] = s * cdec_ref[0] + lax.dot_general((k * kdec_ref[0]).astype(jnp.bfloat16), v,
                                                   (((0,), (0,)), ((), ())),
                                                   preferred_element_type=jnp.float32)
    mu = jnp.mean(o, axis=-1, keepdims=True)
    oc = o - mu
    var = jnp.mean(oc * oc, axis=-1, keepdims=True)
    out = _silu(g_ref[0].astype(jnp.float32)) * (oc * lax.rsqrt(var + 1e-6))
    if prev_ref is not None:
        out = out + prev_ref[0]
    o_ref[0] = out.astype(o_ref.dtype)


def _retention_tables(gamma, reverse, n_ctx):
    log_g = jnp.log(gamma)[:, None, None]
    i = jnp.arange(RET_TC, dtype=jnp.float32)
    rel = (i[None, :] - i[:, None]) if reverse else (i[:, None] - i[None, :])
    dmat = jnp.where(rel >= 0, jnp.exp(jnp.maximum(rel, 0.0)[None] * log_g), 0.0)
    q_pow = (RET_TC - i) if reverse else (i + 1.0)
    k_pow = i if reverse else (RET_TC - 1.0 - i)
    m = jnp.arange(n_ctx, dtype=jnp.float32)
    c_pow = m if reverse else (n_ctx - 1.0 - m)
    bc = lambda p: jnp.broadcast_to(jnp.exp(p[None, :, None] * log_g), (RET_HEADS, p.shape[0], RET_KEY))
    cdec = jnp.broadcast_to(jnp.exp(RET_TC * log_g), (RET_HEADS, 1, RET_VAL))
    return dmat, bc(q_pow), bc(k_pow), bc(c_pow), cdec


def _rope_tables(n_tokens):
    pos_row = (jnp.arange(n_tokens) // GRID_W).astype(jnp.float32)
    pos_col = (jnp.arange(n_tokens) % GRID_W).astype(jnp.float32)
    nf = RET_KEY // 4
    inv_freq = ROPE_BASE ** (-jnp.arange(nf, dtype=jnp.float32) / nf)
    cos, sin = [], []
    for pos in (pos_row, pos_col):
        ang = pos[:, None] * inv_freq[None, :]
        cos += [jnp.cos(ang), jnp.cos(ang)]
        sin += [-jnp.sin(ang), jnp.sin(ang)]
    return jnp.concatenate(cos, axis=1), jnp.concatenate(sin, axis=1)


def _retention_direction(z, zc, cos, sin, gamma, reverse, prev):
    b, t, _ = z.shape
    n_ctx = zc.shape[1]
    nc = t // RET_TC
    dmat, qdec, kdec, kcdec, cdec = _retention_tables(gamma, reverse, n_ctx)
    ch = (lambda c: nc - 1 - c) if reverse else (lambda c: c)
    kq, kv = RET_QK // RET_KEY, (2 * RET_QK) // RET_VAL
    g_off = (2 * RET_QK + (2 if reverse else 1) * RET_VD) // RET_VAL
    in_specs = [
        pl.BlockSpec((1, RET_TC, RET_KEY), lambda bi, h, c: (bi, ch(c), h)),
        pl.BlockSpec((1, RET_TC, RET_KEY), lambda bi, h, c: (bi, ch(c), kq + h)),
        pl.BlockSpec((1, RET_TC, RET_VAL), lambda bi, h, c: (bi, ch(c), kv + h)),
        pl.BlockSpec((1, RET_TC, RET_VAL), lambda bi, h, c: (bi, ch(c), g_off + h)),
        pl.BlockSpec((1, n_ctx, RET_KEY), lambda bi, h, c: (bi, 0, h)),
        pl.BlockSpec((1, n_ctx, RET_VAL), lambda bi, h, c: (bi, 0, RET_QK // RET_VAL + h)),
        pl.BlockSpec((RET_TC, RET_KEY), lambda bi, h, c: (ch(c), 0)),
        pl.BlockSpec((RET_TC, RET_KEY), lambda bi, h, c: (ch(c), 0)),
        pl.BlockSpec((1, RET_TC, RET_TC), lambda bi, h, c: (h, 0, 0)),
        pl.BlockSpec((1, RET_TC, RET_KEY), lambda bi, h, c: (h, 0, 0)),
        pl.BlockSpec((1, RET_TC, RET_KEY), lambda bi, h, c: (h, 0, 0)),
        pl.BlockSpec((1, n_ctx, RET_KEY), lambda bi, h, c: (h, 0, 0)),
        pl.BlockSpec((1, 1, RET_VAL), lambda bi, h, c: (h, 0, 0)),
    ]
    args = [z, z, z, z, zc, zc, cos, sin, dmat, qdec, kdec, kcdec, cdec]
    if prev is not None:
        in_specs.append(pl.BlockSpec((1, RET_TC, RET_VAL), lambda bi, h, c: (bi, ch(c), h)))
        args.append(prev)
    return pl.pallas_call(
        functools.partial(_retention_kernel, has_prev=prev is not None),
        grid=(b, RET_HEADS, nc),
        in_specs=in_specs,
        out_specs=pl.BlockSpec((1, RET_TC, RET_VAL), lambda bi, h, c: (bi, ch(c), h)),
        out_shape=jax.ShapeDtypeStruct((b, t, RET_VD), jnp.float32 if prev is None else jnp.bfloat16),
        scratch_shapes=[pltpu.VMEM((RET_KEY, RET_VAL), jnp.float32)],
        compiler_params=pltpu.CompilerParams(dimension_semantics=("parallel", "parallel", "arbitrary"),
                                             vmem_limit_bytes=VMEM_LIMIT_BYTES),
        name="retention_bwd" if reverse else "retention_fwd",
    )(*args)


def retention_mixer(z, zc, decay_f, decay_b):
    cos, sin = _rope_tables(z.shape[1])
    gamma_f = 1.0 - jnp.exp2(-decay_f)
    gamma_b = 1.0 - jnp.exp2(-decay_b)
    part = _retention_direction(z, zc, cos, sin, gamma_b, True, None)
    return _retention_direction(z, zc, cos, sin, gamma_f, False, part)


def _moe_kernel(te_ref, tv_ref, tok_ref, h_ref, wg_ref, wu_ref, wd_ref, o_ref, x_ref, acc_ref, sem):
    i = pl.program_id(0)
    j = pl.program_id(1)
    rows = x_ref.shape[0]

    def row_copy(r, src_row):
        return pltpu.make_async_copy(h_ref.at[pl.ds(src_row, 1)], x_ref.at[pl.ds(r, 1)], sem)

    @pl.when(tv_ref[i] > 0)
    def _():
        @pl.when(j == 0)
        def _():
            def start(r, carry):
                row_copy(r, tok_ref[0, 0, r]).start()
                return carry

            def wait(r, carry):
                row_copy(r, 0).wait()
                return carry

            lax.fori_loop(0, rows, start, 0)
            lax.fori_loop(0, rows, wait, 0)
            acc_ref[...] = jnp.zeros_like(acc_ref)

        x = x_ref[...].astype(jnp.bfloat16)
        g = jnp.dot(x, wg_ref[0].astype(jnp.bfloat16), preferred_element_type=jnp.float32)
        u = jnp.dot(x, wu_ref[0].astype(jnp.bfloat16), preferred_element_type=jnp.float32)
        a = (_silu(g) * u).astype(jnp.bfloat16)
        acc_ref[...] += jnp.dot(a, wd_ref[0].astype(jnp.bfloat16), preferred_element_type=jnp.float32)

    last = j == pl.num_programs(1) - 1

    @pl.when(jnp.logical_and(last, tv_ref[i] > 0))
    def _():
        o_ref[...] = acc_ref[...]

    @pl.when(jnp.logical_and(last, tv_ref[i] == 0))
    def _():
        o_ref[...] = jnp.zeros_like(o_ref)


def moe_experts(h, slot_tok, tile_expert, tile_valid, w_gate, w_up, w_down):
    n_tiles = slot_tok.shape[0]
    d = h.shape[1]
    ff = w_gate.shape[2]
    grid_spec = pltpu.PrefetchScalarGridSpec(
        num_scalar_prefetch=2,
        grid=(n_tiles, ff // MOE_TF),
        in_specs=[
            pl.BlockSpec((1, 1, MOE_TM), lambda i, j, te, tv: (i, 0, 0), memory_space=pltpu.SMEM),
            pl.BlockSpec(memory_space=pl.ANY),
            pl.BlockSpec((1, d, MOE_TF), lambda i, j, te, tv: (te[i], 0, j)),
            pl.BlockSpec((1, d, MOE_TF), lambda i, j, te, tv: (te[i], 0, j)),
            pl.BlockSpec((1, MOE_TF, d), lambda i, j, te, tv: (te[i], j, 0)),
        ],
        out_specs=pl.BlockSpec((MOE_TM, d), lambda i, j, te, tv: (i, 0)),
        scratch_shapes=[pltpu.VMEM((MOE_TM, d), jnp.float32), pltpu.VMEM((MOE_TM, d), jnp.float32),
                        pltpu.SemaphoreType.DMA(())],
    )
    return pl.pallas_call(
        _moe_kernel,
        grid_spec=grid_spec,
        out_shape=jax.ShapeDtypeStruct((n_tiles * MOE_TM, d), jnp.float32),
        compiler_params=pltpu.CompilerParams(dimension_semantics=("arbitrary", "arbitrary"),
                                             vmem_limit_bytes=VMEM_LIMIT_BYTES),
        name="moe_experts",
    )(tile_expert, tile_valid, slot_tok, h, w_gate, w_up, w_down)


def moe_swiglu(h, router, w_gate, w_up, w_down):
    b, t, d = h.shape
    n = b * t
    hf = h.reshape(n, d)
    logits = jnp.dot(hf, router, precision=lax.Precision.HIGHEST)
    top_val, top_idx = lax.top_k(logits, TOP_K)
    gate = jax.nn.softmax(top_val, axis=-1)
    flat_e = top_idx.reshape(-1)
    onehot = (flat_e[:, None] == jnp.arange(N_EXPERTS, dtype=flat_e.dtype)[None, :]).astype(jnp.int32)
    csum = jnp.cumsum(onehot, axis=0)
    counts = csum[-1]
    rank = jnp.sum((csum - onehot) * onehot, axis=1)
    padded = (counts + MOE_TM - 1) // MOE_TM * MOE_TM
    pad_end = jnp.cumsum(padded)
    pad_start = pad_end - padded
    slot = (pad_start[flat_e] + rank).astype(jnp.int32)
    n_tiles = (n * TOP_K) // MOE_TM + N_EXPERTS
    slots = n_tiles * MOE_TM
    tok = jnp.arange(n * TOP_K, dtype=jnp.int32) // TOP_K
    slot_tok = jnp.zeros((slots,), jnp.int32).at[slot].set(tok)
    tile_start = jnp.arange(n_tiles, dtype=jnp.int32) * MOE_TM
    tile_expert = jnp.minimum(jnp.sum(pad_end[None, :] <= tile_start[:, None], axis=1),
                              N_EXPERTS - 1).astype(jnp.int32)
    tile_valid = (tile_start < pad_end[-1]).astype(jnp.int32)
    yb = moe_experts(hf, slot_tok.reshape(n_tiles, 1, MOE_TM), tile_expert, tile_valid, w_gate, w_up, w_down)
    slot2 = slot.reshape(n, TOP_K)
    out = gate[:, 0:1] * yb[slot2[:, 0]] + gate[:, 1:2] * yb[slot2[:, 1]]
    return out.reshape(b, t, d)


def _row_select(i, tm, n_ctx, ctx_vec, lat_vec):
    if n_ctx == 0:
        return lat_vec
    row = i * tm + lax.broadcasted_iota(jnp.int32, (tm, 1), 0)
    return jnp.where(row < n_ctx, ctx_vec, lat_vec)


def _post_norm_rows(x, update, g, b):
    y = DEEPNORM_ALPHA * x + update
    mu = jnp.mean(y, axis=-1, keepdims=True)
    yc = y - mu
    var = jnp.mean(yc * yc, axis=-1, keepdims=True)
    return yc * lax.rsqrt(var + LN_EPS) * g + b


def _matmul_postnorm_kernel(a_ref, w_ref, x_ref, lat_ref, ctx_ref, g_ref, b_ref, o_ref, *, n_ctx):
    tm = a_ref.shape[1]
    o = jnp.dot(a_ref[0].astype(jnp.bfloat16), w_ref[...], preferred_element_type=jnp.float32)
    gate = _row_select(pl.program_id(1), tm, n_ctx, ctx_ref[...], lat_ref[0])
    o_ref[0] = _post_norm_rows(x_ref[0], gate * o, g_ref[...], b_ref[...])


def matmul_postnorm(a, w, x, gate_lat, gate_ctx, ln_g, ln_b, n_ctx):
    b, length, k = a.shape
    d = w.shape[1]
    tm = _pick_tile(length, (768, 512, 256))
    vec = pl.BlockSpec((1, d), lambda bi, i: (0, 0))
    return pl.pallas_call(
        functools.partial(_matmul_postnorm_kernel, n_ctx=n_ctx),
        grid=(b, length // tm),
        in_specs=[pl.BlockSpec((1, tm, k), lambda bi, i: (bi, i, 0)),
                  pl.BlockSpec((k, d), lambda bi, i: (0, 0)),
                  pl.BlockSpec((1, tm, d), lambda bi, i: (bi, i, 0)),
                  pl.BlockSpec((1, 1, d), lambda bi, i: (bi, 0, 0)), vec, vec, vec],
        out_specs=pl.BlockSpec((1, tm, d), lambda bi, i: (bi, i, 0)),
        out_shape=jax.ShapeDtypeStruct((b, length, d), jnp.float32),
        compiler_params=pltpu.CompilerParams(dimension_semantics=("parallel", "parallel"),
                                             vmem_limit_bytes=VMEM_LIMIT_BYTES),
        name="matmul_postnorm",
    )(a, w.astype(jnp.bfloat16), x, gate_lat[:, None, :], gate_ctx[None, :], ln_g[None, :], ln_b[None, :])


def _ffn_postnorm_kernel(x_ref, lat_ref, ctx_ref, wg_ref, wu_ref, wd_ref, g_ref, b_ref, o_ref, h_ref, acc_ref,
                         *, n_ctx):
    i = pl.program_id(1)
    j = pl.program_id(2)
    tm = x_ref.shape[1]

    def vec(k):
        return _row_select(i, tm, n_ctx, ctx_ref[k:k + 1, :], lat_ref[0, k:k + 1, :])

    @pl.when(j == 0)
    def _():
        h_ref[...] = (x_ref[0] * (1.0 + vec(1)) + vec(0)).astype(jnp.bfloat16)
        acc_ref[...] = jnp.zeros_like(acc_ref)

    h = h_ref[...]
    gt = jnp.dot(h, wg_ref[...], preferred_element_type=jnp.float32)
    up = jnp.dot(h, wu_ref[...], preferred_element_type=jnp.float32)
    acc_ref[...] += jnp.dot((_silu(gt) * up).astype(jnp.bfloat16), wd_ref[...], preferred_element_type=jnp.float32)

    @pl.when(j == pl.num_programs(2) - 1)
    def _():
        o_ref[0] = _post_norm_rows(x_ref[0], vec(2) * acc_ref[...], g_ref[...], b_ref[...])


def ffn_postnorm(x, mod_lat, mod_ctx, w_gate, w_up, w_down, ln_g, ln_b, n_ctx):
    b, length, d = x.shape
    ff = w_gate.shape[1]
    tm = _pick_tile(length, (384, 512, 256))
    tf = _pick_tile(ff, (1408, 512, 256))
    vec = pl.BlockSpec((1, d), lambda bi, i, j: (0, 0))
    return pl.pallas_call(
        functools.partial(_ffn_postnorm_kernel, n_ctx=n_ctx),
        grid=(b, length // tm, ff // tf),
        in_specs=[pl.BlockSpec((1, tm, d), lambda bi, i, j: (bi, i, 0)),
                  pl.BlockSpec((1, 3, d), lambda bi, i, j: (bi, 0, 0)),
                  pl.BlockSpec((3, d), lambda bi, i, j: (0, 0)),
                  pl.BlockSpec((d, tf), lambda bi, i, j: (0, j)),
                  pl.BlockSpec((d, tf), lambda bi, i, j: (0, j)),
                  pl.BlockSpec((tf, d), lambda bi, i, j: (j, 0)), vec, vec],
        out_specs=pl.BlockSpec((1, tm, d), lambda bi, i, j: (bi, i, 0)),
        out_shape=jax.ShapeDtypeStruct((b, length, d), jnp.float32),
        scratch_shapes=[pltpu.VMEM((tm, d), jnp.bfloat16), pltpu.VMEM((tm, d), jnp.float32)],
        compiler_params=pltpu.CompilerParams(dimension_semantics=("parallel", "parallel", "arbitrary"),
                                             vmem_limit_bytes=VMEM_LIMIT_BYTES),
        name="ffn_postnorm",
    )(x, mod_lat, mod_ctx, w_gate.astype(jnp.bfloat16), w_up.astype(jnp.bfloat16), w_down.astype(jnp.bfloat16),
      ln_g[None, :], ln_b[None, :])


def _even_layer(x, xc, sc, scc, mod_w, mod_b, w_in, mla_q_norm, mla_wq_up, mla_kv_norm, mla_wkv_up,
                rwkv_conv, rwkv_w0_f, rwkv_w2_f, rwkv_w0_b, rwkv_w2_b, rwkv_a0_f, rwkv_a2_f, rwkv_a0_b,
                rwkv_a2_b, rwkv_g2, rwkv_k_k, rwkv_k_a, rwkv_r_k, rwkv_lnx_g, rwkv_lnx_b, w_out,
                ln1_g, ln1_b, ffn_w_gate, ffn_w_up, ffn_w_down, ln2_g, ln2_b):
    n_ctx = xc.shape[1]
    m = _adaln(sc, mod_w, mod_b)
    mc = _adaln(scc, mod_w, mod_b)
    xa = jnp.concatenate([xc, x], axis=1)
    is_ctx = (jnp.arange(xa.shape[1]) < n_ctx)[None, :, None]

    def rows(k):
        return jnp.where(is_ctx, mc[k][None, None, :], m[k][:, None, :])

    h = _modulate(xa, rows(0), rows(1)).astype(jnp.bfloat16)
    o_mla = _mla_mixer(h, n_ctx, w_in, mla_q_norm, mla_wq_up, mla_kv_norm, mla_wkv_up)
    o_rwkv = _rwkv7_mixer(mm(h, w_in[:, MLA_IN:]), n_ctx, rwkv_conv, rwkv_w0_f, rwkv_w2_f, rwkv_w0_b, rwkv_w2_b,
                          rwkv_a0_f, rwkv_a2_f, rwkv_a0_b, rwkv_a2_b, rwkv_g2, rwkv_k_k, rwkv_k_a,
                          rwkv_r_k, rwkv_lnx_g, rwkv_lnx_b)
    mix = jnp.concatenate([o_mla.astype(jnp.bfloat16), o_rwkv], axis=-1)
    xa = matmul_postnorm(mix, w_out, xa, m[2], mc[2], ln1_g, ln1_b, n_ctx)
    xa = ffn_postnorm(xa, jnp.stack(m[3:6], axis=1), jnp.stack(mc[3:6], axis=0),
                      ffn_w_gate, ffn_w_up, ffn_w_down, ln2_g, ln2_b, n_ctx)
    return xa[:, n_ctx:], xa[:, :n_ctx]


def _odd_layer(x, xc, sc, scc, mod_w, mod_b, w_in, ret_decay_f, ret_decay_b, w_out, ln1_g, ln1_b,
               router, moe_w_gate, moe_w_up, moe_w_down, ln2_g, ln2_b):
    m = _adaln(sc, mod_w, mod_b)
    c_shift, c_scale = jnp.split(
        jnp.dot(scc, mod_w[:, :2 * D_MODEL], precision=lax.Precision.HIGHEST) + mod_b[:2 * D_MODEL], 2)
    z = mm(_modulate(x, m[0][:, None, :], m[1][:, None, :]), w_in, jnp.bfloat16)
    zc = mm(_modulate(xc, c_shift, c_scale), w_in[:, RET_QK:2 * RET_QK + RET_VD], jnp.bfloat16)
    mix = retention_mixer(z, zc, ret_decay_f, ret_decay_b)
    x = matmul_postnorm(mix, w_out, x, m[2], jnp.zeros_like(m[2][0]), ln1_g, ln1_b, 0)
    y = moe_swiglu(_modulate(x, m[3][:, None, :], m[4][:, None, :]), router, moe_w_gate, moe_w_up, moe_w_down)
    return _post_norm(x, m[5][:, None, :] * y, ln2_g, ln2_b)


def kernel(x, c, ctx, c_ctx, l0_mod_w, l0_mod_b, l0_w_in, l0_mla_q_norm, l0_mla_wq_up, l0_mla_kv_norm, l0_mla_wkv_up, l0_rwkv_conv, l0_rwkv_w0_f, l0_rwkv_w2_f, l0_rwkv_w0_b, l0_rwkv_w2_b, l0_rwkv_a0_f, l0_rwkv_a2_f, l0_rwkv_a0_b, l0_rwkv_a2_b, l0_rwkv_g2, l0_rwkv_k_k, l0_rwkv_k_a, l0_rwkv_r_k, l0_rwkv_lnx_g, l0_rwkv_lnx_b, l0_w_out, l0_ln1_g, l0_ln1_b, l0_ffn_w_gate, l0_ffn_w_up, l0_ffn_w_down, l0_ln2_g, l0_ln2_b, l1_mod_w, l1_mod_b, l1_w_in, l1_ret_decay_f, l1_ret_decay_b, l1_w_out, l1_ln1_g, l1_ln1_b, l1_router, l1_moe_w_gate, l1_moe_w_up, l1_moe_w_down, l1_ln2_g, l1_ln2_b):
    even_params = (l0_mod_w, l0_mod_b, l0_w_in, l0_mla_q_norm, l0_mla_wq_up, l0_mla_kv_norm, l0_mla_wkv_up,
                   l0_rwkv_conv, l0_rwkv_w0_f, l0_rwkv_w2_f, l0_rwkv_w0_b, l0_rwkv_w2_b, l0_rwkv_a0_f,
                   l0_rwkv_a2_f, l0_rwkv_a0_b, l0_rwkv_a2_b, l0_rwkv_g2, l0_rwkv_k_k, l0_rwkv_k_a, l0_rwkv_r_k,
                   l0_rwkv_lnx_g, l0_rwkv_lnx_b, l0_w_out, l0_ln1_g, l0_ln1_b, l0_ffn_w_gate, l0_ffn_w_up,
                   l0_ffn_w_down, l0_ln2_g, l0_ln2_b)
    odd_params = (l1_mod_w, l1_mod_b, l1_w_in, l1_ret_decay_f, l1_ret_decay_b, l1_w_out, l1_ln1_g, l1_ln1_b,
                  l1_router, l1_moe_w_gate, l1_moe_w_up, l1_moe_w_down, l1_ln2_g, l1_ln2_b)
    sc = _silu(c)
    scc = _silu(c_ctx)
    x, xc = _even_layer(x, ctx, sc, scc, *even_params)
    return _odd_layer(x, xc, sc, scc, *odd_params)
```

```python
import functools

import jax
import jax.numpy as jnp
import numpy as np
from jax import lax
from jax.experimental import pallas as pl
from jax.experimental.pallas import tpu as pltpu

D_MODEL = 1024
DEPTH = 2
GRID_W = 64
ROPE_BASE = 10000.0
DEEPNORM_ALPHA = (2 * DEPTH) ** 0.25
LN_EPS = 1e-5

MLA_HEADS = 8
MLA_Q_RANK = 256
MLA_KV_RANK = 128
MLA_NOPE = 64
MLA_ROPE = 32
MLA_V = 64
MLA_IN = MLA_Q_RANK + MLA_KV_RANK + MLA_ROPE
MLA_OUT = MLA_HEADS * MLA_V

RWKV_HEADS = 8
RWKV_HEAD = 64
RWKV_DIM = RWKV_HEADS * RWKV_HEAD
RWKV_DECAY_LORA = 64
RWKV_AAA_LORA = 64
RWKV_GATE_LORA = 128
RWKV_LNX_EPS = 64e-5

RET_HEADS = 4
RET_KEY = 256
RET_VAL = 512
RET_TC = 256
RET_QK = RET_HEADS * RET_KEY
RET_VD = RET_HEADS * RET_VAL

N_EXPERTS = 8
TOP_K = 2
MOE_TM = 1024
MOE_TF = 512
MOE_DMA_UNROLL = 8

LANES = 128
VMEM_LIMIT_BYTES = 48 * 1024 * 1024


def _matmul_kernel(x_ref, w_ref, o_ref):
    o_ref[...] = jnp.dot(x_ref[...].astype(jnp.bfloat16), w_ref[...],
                         preferred_element_type=jnp.float32).astype(o_ref.dtype)


def _pick_tile(n, candidates):
    for c in candidates:
        if n % c == 0:
            return c
    raise ValueError(f"no tile for {n}")


def pmatmul(x, w, out_dtype=jnp.float32):
    m, k = x.shape
    n = w.shape[1]
    n_pad = (-n) % LANES
    w = w.astype(jnp.bfloat16)
    if n_pad:
        w = jnp.pad(w, ((0, 0), (0, n_pad)))
    np_ = n + n_pad
    tm = _pick_tile(m, (1024, 512, 256, 128, 8) if k <= 1024 else (512, 256, 128, 8))
    tn = _pick_tile(np_, (1024, 768, 640, 512, 384, 256, 128))
    out = pl.pallas_call(
        _matmul_kernel,
        grid=(m // tm, np_ // tn),
        in_specs=[pl.BlockSpec((tm, k), lambda i, j: (i, 0)),
                  pl.BlockSpec((k, tn), lambda i, j: (0, j))],
        out_specs=pl.BlockSpec((tm, tn), lambda i, j: (i, j)),
        out_shape=jax.ShapeDtypeStruct((m, np_), out_dtype),
        compiler_params=pltpu.CompilerParams(
            dimension_semantics=("parallel", "parallel"),
            vmem_limit_bytes=VMEM_LIMIT_BYTES),
        name="matmul",
    )(x, w)
    return out[:, :n] if n_pad else out


def mm(x, w, out_dtype=jnp.float32):
    lead = x.shape[:-1]
    out = pmatmul(x.reshape(-1, x.shape[-1]).astype(jnp.bfloat16), w, out_dtype)
    return out.reshape(*lead, w.shape[1])


def _silu(t):
    return t * jax.nn.sigmoid(t)


def _normalize(t, eps):
    mu = jnp.mean(t, axis=-1, keepdims=True)
    var = jnp.mean(jnp.square(t - mu), axis=-1, keepdims=True)
    return (t - mu) * lax.rsqrt(var + eps)


def _layer_norm(t, g, b):
    return _normalize(t, LN_EPS) * g + b


def _rms_norm(t, g, eps=1e-6):
    return t * lax.rsqrt(jnp.mean(t * t, axis=-1, keepdims=True) + eps) * g


def _post_norm(x, update, g, b):
    return _layer_norm(DEEPNORM_ALPHA * x + update, g, b)


def _modulate(h, shift, scale):
    return h * (1.0 + scale) + shift


def _adaln(cond, mod_w, mod_b):
    return jnp.split(jnp.dot(cond, mod_w, precision=lax.Precision.HIGHEST) + mod_b, 6, axis=-1)


def _grid_positions(n_tokens):
    rows = n_tokens // GRID_W
    row = jnp.repeat(jnp.arange(rows, dtype=jnp.float32), GRID_W)
    col = jnp.tile(jnp.arange(GRID_W, dtype=jnp.float32), rows)
    return row, col


SOFTMAX_FLOOR = -1e30


FLASH_ROW_GROUPS = 4


def _flash_kernel(q_ref, kt_ref, v_ref, o_ref, m_ref, acc_ref, *, dv):
    j = pl.program_id(3)

    @pl.when(j == 0)
    def _():
        m_ref[...] = jnp.full_like(m_ref, SOFTMAX_FLOOR)
        acc_ref[...] = jnp.zeros_like(acc_ref)

    rows = q_ref.shape[2] // FLASH_ROW_GROUPS
    kt = kt_ref[0, 0]
    v = v_ref[0, 0]
    for u in range(FLASH_ROW_GROUPS):
        sl = slice(u * rows, (u + 1) * rows)
        s = jnp.dot(q_ref[0, 0, sl, :], kt, preferred_element_type=jnp.float32)
        m_prev = m_ref[sl, :]
        m_new = jnp.maximum(m_prev, jnp.max(s, axis=-1, keepdims=True))
        alpha = jnp.exp(m_prev - m_new)
        p = jnp.exp(s - m_new).astype(jnp.bfloat16)
        acc_ref[sl, :] = alpha * acc_ref[sl, :] + jnp.dot(p, v, preferred_element_type=jnp.float32)
        m_ref[sl, :] = m_new

    @pl.when(j == pl.num_programs(3) - 1)
    def _():
        acc = acc_ref[...]
        o_ref[0, 0] = acc[:, :dv] / acc[:, dv:dv + 1]


def flash_attention(q, kt, v1, dv, tq, tk):
    b, h, t, dq = q.shape
    s = kt.shape[3]
    return pl.pallas_call(
        functools.partial(_flash_kernel, dv=dv),
        grid=(b, h, t // tq, s // tk),
        in_specs=[pl.BlockSpec((1, 1, tq, dq), lambda bi, hi, i, j: (bi, hi, i, 0)),
                  pl.BlockSpec((1, 1, dq, tk), lambda bi, hi, i, j: (bi, hi, 0, j)),
                  pl.BlockSpec((1, 1, tk, LANES), lambda bi, hi, i, j: (bi, hi, j, 0))],
        out_specs=pl.BlockSpec((1, 1, tq, dv), lambda bi, hi, i, j: (bi, hi, i, 0)),
        out_shape=jax.ShapeDtypeStruct((b, h, t, dv), jnp.float32),
        scratch_shapes=[pltpu.VMEM((tq, 1), jnp.float32), pltpu.VMEM((tq, LANES), jnp.float32)],
        compiler_params=pltpu.CompilerParams(
            dimension_semantics=("parallel", "parallel", "parallel", "arbitrary"),
            vmem_limit_bytes=VMEM_LIMIT_BYTES),
        name="flash_attention",
    )(q, kt, v1)


def _block_attention(q, k, v):
    dq = q.shape[-1]
    dv = v.shape[-1]
    b, s, h, _ = k.shape
    qh = jnp.transpose(q * dq ** -0.5, (0, 2, 1, 3)).astype(jnp.bfloat16)
    kt = jnp.transpose(k, (0, 2, 3, 1)).astype(jnp.bfloat16)
    v1 = jnp.concatenate([v, jnp.ones((b, s, h, 1), v.dtype), jnp.zeros((b, s, h, LANES - dv - 1), v.dtype)],
                         axis=-1)
    v1 = jnp.transpose(v1, (0, 2, 1, 3)).astype(jnp.bfloat16)
    tq = _pick_tile(q.shape[1], (1024, 256))
    o = flash_attention(qh, kt, v1, dv, tq, s)
    return jnp.transpose(o, (0, 2, 1, 3))


_Q8 = MLA_ROPE // 4
ROPE_PARTNER = np.concatenate([np.arange(_Q8, 2 * _Q8), np.arange(0, _Q8),
                               np.arange(3 * _Q8, 4 * _Q8), np.arange(2 * _Q8, 3 * _Q8)])
ROPE_SIGN = np.concatenate([-np.ones(_Q8), np.ones(_Q8), -np.ones(_Q8), np.ones(_Q8)]).astype(np.float32)


def _mla_rope_tables(n_ctx, t):
    row, col = _grid_positions(t)
    zero = jnp.zeros((n_ctx,), jnp.float32)
    row, col = jnp.concatenate([zero, row]), jnp.concatenate([zero, col])
    inv_freq = ROPE_BASE ** (-jnp.arange(_Q8, dtype=jnp.float32) / _Q8)
    ang = jnp.concatenate([row[:, None] * inv_freq] * 2 + [col[:, None] * inv_freq] * 2, axis=1)
    return jnp.cos(ang), jnp.sin(ang)


def _mla_mixer(h, n_ctx, w_in, q_norm, wq_up, kv_norm, wkv_up):
    b, length, _ = h.shape
    cos, sin = _mla_rope_tables(n_ctx, length - n_ctx)
    k_cols = np.arange(MLA_Q_RANK + MLA_KV_RANK, MLA_IN)
    w_z = jnp.concatenate([w_in[:, :MLA_IN], w_in[:, k_cols[ROPE_PARTNER]] * ROPE_SIGN], axis=1)
    z = mm(h, w_z)
    cq, ckv = z[..., :MLA_Q_RANK], z[..., MLA_Q_RANK:MLA_Q_RANK + MLA_KV_RANK]
    k_rope = z[..., k_cols[0]:MLA_IN] * cos + z[..., MLA_IN:] * sin
    dq = MLA_NOPE + MLA_ROPE
    head_cols = np.arange(MLA_HEADS)[:, None] * dq
    q_rope_cols = (head_cols + MLA_NOPE + np.arange(MLA_ROPE)[None, :])
    wq_partner = jnp.zeros_like(wq_up).at[:, q_rope_cols.reshape(-1)].set(
        wq_up[:, (head_cols + MLA_NOPE + ROPE_PARTNER[None, :]).reshape(-1)] * jnp.tile(ROPE_SIGN, MLA_HEADS))
    q2 = mm(_rms_norm(cq, q_norm), jnp.concatenate([wq_up, wq_partner], axis=1))
    one, nil = jnp.ones((length, MLA_NOPE), jnp.float32), jnp.zeros((length, MLA_NOPE), jnp.float32)
    cos_q = jnp.tile(jnp.concatenate([one, cos], axis=1), (1, MLA_HEADS))
    sin_q = jnp.tile(jnp.concatenate([nil, sin], axis=1), (1, MLA_HEADS))
    q = (q2[..., :MLA_HEADS * dq] * cos_q + q2[..., MLA_HEADS * dq:] * sin_q).reshape(b, length, MLA_HEADS, dq)
    kv = mm(_rms_norm(ckv, kv_norm), wkv_up).reshape(b, length, MLA_HEADS, MLA_NOPE + MLA_V)
    k = jnp.concatenate([kv[..., :MLA_NOPE],
                         jnp.broadcast_to(k_rope[:, :, None, :], (b, length, MLA_HEADS, MLA_ROPE))], axis=-1)
    v = kv[..., MLA_NOPE:]
    o = _block_attention(q[:, n_ctx:], k, v)
    oc = _block_attention(q[:, :n_ctx], k[:, :n_ctx], v[:, :n_ctx])
    return jnp.concatenate([oc, o], axis=1).reshape(b, length, MLA_OUT)


SCAN_BLOCK = 128
SCAN_GROUP = 8
N_PAIRS = RWKV_HEADS // 2
PAIR_ROWS = N_PAIRS * RWKV_HEAD


def _to_state_tiles(vblk):
    halves = [[], []]
    for p in range(N_PAIRS):
        tr = vblk[:, p * LANES:(p + 1) * LANES].T
        for half in range(2):
            c = slice(half * RWKV_HEAD, (half + 1) * RWKV_HEAD)
            halves[half].append(jnp.concatenate([tr[:RWKV_HEAD, c], tr[RWKV_HEAD:, c]], axis=1))
    return [jnp.concatenate(h, axis=0) for h in halves]


def _from_state_tiles(tiles):
    cols = []
    for p in range(N_PAIRS):
        r = slice(p * RWKV_HEAD, (p + 1) * RWKV_HEAD)
        tr = jnp.concatenate([tiles[0][r, :], tiles[1][r, :]], axis=0).T
        cols.append(jnp.concatenate(
            [jnp.concatenate([tr[:RWKV_HEAD, :RWKV_HEAD], tr[:RWKV_HEAD, RWKV_HEAD:]], axis=0),
             jnp.concatenate([tr[RWKV_HEAD:, :RWKV_HEAD], tr[RWKV_HEAD:, RWKV_HEAD:]], axis=0)], axis=1))
    return jnp.concatenate(cols, axis=1)


def _scan_kernel(rf_ref, rb_ref, af_ref, ab_ref, vf_ref, vb_ref, wf_ref, wb_ref, kf_ref, kb_ref,
                 bf_ref, bb_ref, wred_ref, yf_ref, yb_ref, s_ref, vt_ref, yc_ref):
    nb = rf_ref.shape[0]

    @pl.when(pl.program_id(0) == 0)
    def _():
        s_ref[...] = jnp.zeros_like(s_ref)

    lane = lax.broadcasted_iota(jnp.int32, (PAIR_ROWS, LANES), 1)
    wred = wred_ref[...]
    refs = ((rf_ref, af_ref, vf_ref, wf_ref, kf_ref, bf_ref, yf_ref),
            (rb_ref, ab_ref, vb_ref, wb_ref, kb_ref, bb_ref, yb_ref))

    for d in range(2):
        for bi in range(nb):
            tiles = _to_state_tiles(refs[d][2][bi])
            for half in range(2):
                vt_ref[d, bi, half] = tiles[half].astype(jnp.bfloat16)

    def rows(ref, bi, base, u):
        return jnp.concatenate(
            [jnp.broadcast_to(ref.at[bi, pl.ds(base, SCAN_GROUP), :][u:u + 1, p * LANES:(p + 1) * LANES],
                              (RWKV_HEAD, LANES)) for p in range(N_PAIRS)], axis=0)

    def collect(d, bi, t, yb, valid):
        mask = jnp.logical_and(lane % RWKV_HEAD == t % RWKV_HEAD, valid)
        half = t // RWKV_HEAD
        yc_ref[d, bi, half] = jnp.where(mask, yb, yc_ref[d, bi, half])

    n_groups = SCAN_BLOCK // SCAN_GROUP

    def group(gi, carry):
        for u in range(SCAN_GROUP):
            for d in range(2):
                r_ref, a_ref, _, w_ref, k_ref, b_ref, _ = refs[d]
                g = gi if d == 0 else n_groups - 1 - gi
                base = pl.multiple_of(g * SCAN_GROUP, SCAN_GROUP)
                uu = u if d == 0 else SCAN_GROUP - 1 - u
                t = base + uu
                if u == 0:
                    gp = jnp.maximum(gi - 1, 0) if d == 0 else jnp.minimum(n_groups - gi, n_groups - 1)
                    pbase, pu = pl.multiple_of(gp * SCAN_GROUP, SCAN_GROUP), (SCAN_GROUP - 1 if d == 0 else 0)
                else:
                    pbase, pu = base, (uu - 1 if d == 0 else uu + 1)
                tp = pbase + pu
                sel = lane % RWKV_HEAD == t % RWKV_HEAD
                prs = []
                for bi in range(nb):
                    s = s_ref[d, bi]
                    pa = (s * rows(a_ref, bi, base, uu)).astype(jnp.bfloat16)
                    prs.append((s * rows(r_ref, bi, pbase, pu)).astype(jnp.bfloat16))
                    pv = jnp.where(sel, vt_ref[d, bi, t // RWKV_HEAD], jnp.zeros((), jnp.bfloat16))
                    red = jnp.dot(jnp.concatenate([pa, pv], axis=1), wred, preferred_element_type=jnp.float32)
                    s_ref[d, bi] = (s * rows(w_ref, bi, base, uu) + red[:, :LANES] * rows(b_ref, bi, base, uu)
                                    + red[:, LANES:] * rows(k_ref, bi, base, uu))
                valid = True if u > 0 else gi >= 1
                for b0 in range(0, nb, 2):
                    ys = jnp.dot(jnp.concatenate(prs[b0:b0 + 2], axis=1), wred, preferred_element_type=jnp.float32)
                    collect(d, b0, tp, ys[:, :LANES], valid)
                    collect(d, b0 + 1, tp, ys[:, LANES:], valid)
        return carry

    lax.fori_loop(0, n_groups, group, 0)

    for d in range(2):
        r_ref, y_ref = refs[d][0], refs[d][6]
        t_last = SCAN_BLOCK - 1 if d == 0 else 0
        for b0 in range(0, nb, 2):
            prs = [(s_ref[d, bi] * rows(r_ref, bi, t_last - t_last % SCAN_GROUP, t_last % SCAN_GROUP)).astype(jnp.bfloat16)
                   for bi in (b0, b0 + 1)]
            ys = jnp.dot(jnp.concatenate(prs, axis=1), wred, preferred_element_type=jnp.float32)
            collect(d, b0, t_last, ys[:, :LANES], True)
            collect(d, b0 + 1, t_last, ys[:, LANES:], True)
        for bi in range(nb):
            y_ref[bi] = _from_state_tiles([yc_ref[d, bi, 0], yc_ref[d, bi, 1]])


def rwkv_scan(r, v, a, w_f, k_f, b_f, w_b, k_b, b_b, n_ctx):
    nb, length, _ = r.shape
    nblk = length // SCAN_BLOCK
    nblk_ctx = n_ctx // SCAN_BLOCK
    j = np.arange(2 * LANES)
    wred = (j[:, None] // RWKV_HEAD) == (j[None, :] // RWKV_HEAD)

    def fwd(i):
        return i

    def bwd(i):
        return jnp.where(i < nblk_ctx, nblk_ctx - 1 - i, nblk + nblk_ctx - 1 - i)

    def row_spec(blk):
        return pl.BlockSpec((nb, SCAN_BLOCK, RWKV_DIM), lambda i: (0, blk(i), 0))

    y_shape = jax.ShapeDtypeStruct((nb, length, RWKV_DIM), jnp.float32)
    tile_shape = (2, nb, 2, PAIR_ROWS, LANES)
    return pl.pallas_call(
        _scan_kernel,
        grid=(nblk,),
        in_specs=[row_spec(fwd), row_spec(bwd)] * 6 + [pl.BlockSpec((2 * LANES, 2 * LANES), lambda i: (0, 0))],
        out_specs=[row_spec(fwd), row_spec(bwd)],
        out_shape=[y_shape, y_shape],
        scratch_shapes=[pltpu.VMEM((2, nb, PAIR_ROWS, LANES), jnp.float32),
                        pltpu.VMEM(tile_shape, jnp.bfloat16), pltpu.VMEM(tile_shape, jnp.float32)],
        compiler_params=pltpu.CompilerParams(dimension_semantics=("arbitrary",),
                                             vmem_limit_bytes=VMEM_LIMIT_BYTES),
        name="rwkv_scan",
    )(r, r, a, a, v, v, w_f, w_b, k_f, k_b, b_f, b_b, jnp.asarray(wred, jnp.bfloat16))


FEAT_ROWS = 256


def _head_sums(x, ones_ref):
    hi = x.astype(jnp.bfloat16)
    lo = (x - hi.astype(jnp.float32)).astype(jnp.bfloat16)
    ones = ones_ref[...]
    return (jnp.dot(hi, ones, preferred_element_type=jnp.float32)
            + jnp.dot(lo, ones, preferred_element_type=jnp.float32))


def _softplus(x):
    return jnp.maximum(x, 0.0) + jnp.log(1.0 + jnp.exp(-jnp.abs(x)))


def _sigmoid(x):
    return 1.0 / (1.0 + jnp.exp(-x))


def _features_kernel(z_ref, prev_ref, next_ref, conv_ref, lora_ref, g2_ref, vec_ref, ones_ref,
                     r_ref, a_ref, v_ref, wf_ref, kf_ref, bf_ref, wb_ref, kb_ref, bb_ref, g_ref, rk_ref):
    rows = z_ref.shape[1]
    c0 = 3 * RWKV_DIM
    raw = z_ref[0, :, :c0]
    row_id = lax.broadcasted_iota(jnp.int32, (rows, 1), 0)
    before = jnp.where(row_id == 0, prev_ref[0, 0], pltpu.roll(raw, 1, axis=0))
    after = jnp.where(row_id == rows - 1, next_ref[0, 0], pltpu.roll(raw, rows - 1, axis=0))
    rkv = before * conv_ref[0:1, :] + raw * conv_ref[1:2, :] + after * conv_ref[2:3, :]
    r, k, v = rkv[:, :RWKV_DIM], rkv[:, RWKV_DIM:2 * RWKV_DIM], rkv[:, 2 * RWKV_DIM:]

    def vec(i):
        return vec_ref[i:i + 1, :]

    k_k, k_a, r_k, w0_f, w0_b, a0_f, a0_b = (vec(i) for i in range(7))
    kraw = k * k_k
    kk = kraw / jnp.maximum(jnp.sqrt(_head_sums(kraw * kraw, ones_ref)), 1e-12)

    lo = z_ref[0, :, c0:c0 + 2 * LANES]
    lo = jnp.where(lax.broadcasted_iota(jnp.int32, lo.shape, 1) < LANES, jnp.tanh(lo), lo)
    proj = jnp.dot(lo.astype(jnp.bfloat16), lora_ref[...], preferred_element_type=jnp.float32)
    gd = z_ref[0, :, c0 + 2 * LANES:]
    g_ref[0] = jnp.dot(_sigmoid(gd).astype(jnp.bfloat16), g2_ref[...], preferred_element_type=jnp.float32)

    ksum = None
    for d, (w0, a0, w_ref, k_ref, b_ref) in enumerate(((w0_f, a0_f, wf_ref, kf_ref, bf_ref),
                                                       (w0_b, a0_b, wb_ref, kb_ref, bb_ref))):
        logw = -_softplus(-(w0 + proj[:, d * RWKV_DIM:(d + 1) * RWKV_DIM])) - 0.5
        lr = _sigmoid(a0 + proj[:, (2 + d) * RWKV_DIM:(3 + d) * RWKV_DIM])
        kd = k * (1.0 + (lr - 1.0) * k_a)
        w_ref[0] = jnp.exp(-jnp.exp(logw))
        k_ref[0] = kd
        b_ref[0] = kk * lr
        ksum = kd if ksum is None else ksum + kd
    r_ref[0] = r
    a_ref[0] = -kk
    v_ref[0] = v
    rk_ref[0] = r * ksum * r_k


def _readout_kernel(yf_ref, yb_ref, rk_ref, v_ref, g_ref, vec_ref, ones_ref, o_ref):
    y = yf_ref[0] + yb_ref[0]
    mu = _head_sums(y, ones_ref) * (1.0 / RWKV_HEAD)
    yc = y - mu
    var = _head_sums(yc * yc, ones_ref) * (1.0 / RWKV_HEAD)
    yn = yc * lax.rsqrt(var + RWKV_LNX_EPS) * vec_ref[0:1, :] + vec_ref[1:2, :]
    o_ref[0] = ((yn + _head_sums(rk_ref[0], ones_ref) * v_ref[0]) * g_ref[0]).astype(o_ref.dtype)


def _rwkv7_mixer(z, n_ctx, conv_w, w0_f, w2_f, w0_b, w2_b, a0_f, a2_f, a0_b, a2_b, g2, k_k, k_a,
                 r_k, lnx_g, lnx_b):
    b, length, zin = z.shape
    c0 = 3 * RWKV_DIM
    nblk = length // FEAT_ROWS
    edge = jnp.arange(nblk) * FEAT_ROWS
    zero_row = jnp.zeros((b, 1, c0), jnp.float32)
    last_rows = z[:, FEAT_ROWS - 1::FEAT_ROWS, :c0]
    first_rows = z[:, ::FEAT_ROWS, :c0]
    prev_rows = jnp.concatenate([zero_row, last_rows[:, :-1]], axis=1)
    prev_rows = jnp.where(((edge == 0) | (edge == n_ctx))[None, :, None], 0.0, prev_rows)
    next_rows = jnp.concatenate([first_rows[:, 1:], zero_row], axis=1)
    next_rows = jnp.where(((edge + FEAT_ROWS == n_ctx) | (edge + FEAT_ROWS == length))[None, :, None], 0.0, next_rows)
    zl = jnp.zeros((RWKV_DECAY_LORA, RWKV_DIM), jnp.float32)
    lora = jnp.concatenate([jnp.concatenate([w2_f, zl, zl, zl], axis=1), jnp.concatenate([zl, w2_b, zl, zl], axis=1),
                            jnp.concatenate([zl, zl, a2_f, zl], axis=1), jnp.concatenate([zl, zl, zl, a2_b], axis=1)],
                           axis=0).astype(jnp.bfloat16)
    j = np.arange(RWKV_DIM)
    head_ones = jnp.asarray((j[:, None] // RWKV_HEAD) == (j[None, :] // RWKV_HEAD), jnp.bfloat16)
    vecs = jnp.stack([k_k, k_a, r_k.reshape(-1), w0_f, w0_b, a0_f, a0_b, jnp.zeros_like(k_k)], axis=0)
    blk = pl.BlockSpec((1, FEAT_ROWS, RWKV_DIM), lambda bi, i: (bi, i, 0))
    full = lambda shape: pl.BlockSpec(shape, lambda bi, i: (0,) * len(shape))
    edge_spec = pl.BlockSpec((1, 1, 1, c0), lambda bi, i: (bi, i, 0, 0))
    out_shape = jax.ShapeDtypeStruct((b, length, RWKV_DIM), jnp.float32)
    r, a, v, w_f, k_f, b_f, w_b, k_b, b_b, g, rk = pl.pallas_call(
        _features_kernel,
        grid=(b, nblk),
        in_specs=[pl.BlockSpec((1, FEAT_ROWS, zin), lambda bi, i: (bi, i, 0)), edge_spec, edge_spec,
                  full((3, c0)), full((4 * RWKV_DECAY_LORA, 4 * RWKV_DIM)), full((RWKV_GATE_LORA, RWKV_DIM)),
                  full((8, RWKV_DIM)), full((RWKV_DIM, RWKV_DIM))],
        out_specs=[blk] * 11,
        out_shape=[out_shape] * 11,
        compiler_params=pltpu.CompilerParams(dimension_semantics=("parallel", "parallel"),
                                             vmem_limit_bytes=VMEM_LIMIT_BYTES),
        name="rwkv_features",
    )(z, prev_rows[:, :, None, :], next_rows[:, :, None, :], conv_w, lora, g2.astype(jnp.bfloat16), vecs, head_ones)
    y_f, y_b = rwkv_scan(r, v, a, w_f, k_f, b_f, w_b, k_b, b_b, n_ctx)
    return pl.pallas_call(
        _readout_kernel,
        grid=(b, nblk),
        in_specs=[blk] * 5 + [full((2, RWKV_DIM)), full((RWKV_DIM, RWKV_DIM))],
        out_specs=blk,
        out_shape=jax.ShapeDtypeStruct((b, length, RWKV_DIM), jnp.bfloat16),
        compiler_params=pltpu.CompilerParams(dimension_semantics=("parallel", "parallel"),
                                             vmem_limit_bytes=VMEM_LIMIT_BYTES),
        name="rwkv_readout",
    )(y_f, y_b, rk, v, g, jnp.stack([lnx_g, lnx_b], axis=0), head_ones)


def _rope_halves(t, cos, sin):
    parts = []
    for s in range(2):
        u = t[:, s * LANES:(s + 1) * LANES]
        parts.append(u * cos[:, s * LANES:(s + 1) * LANES]
                     + pltpu.roll(u, LANES // 2, axis=1) * sin[:, s * LANES:(s + 1) * LANES])
    return jnp.concatenate(parts, axis=1)


def _retention_kernel(*refs, has_prev):
    if has_prev:
        (q_ref, k_ref, v_ref, g_ref, kc_ref, vc_ref, cos_ref, sin_ref, dmat_ref, qdec_ref, kdec_ref,
         kcdec_ref, cdec_ref, prev_ref, o_ref, s_ref) = refs
    else:
        (q_ref, k_ref, v_ref, g_ref, kc_ref, vc_ref, cos_ref, sin_ref, dmat_ref, qdec_ref, kdec_ref,
         kcdec_ref, cdec_ref, o_ref, s_ref) = refs
        prev_ref = None
    scale = RET_KEY ** -0.5

    @pl.when(pl.program_id(2) == 0)
    def _():
        kc = (kc_ref[0].astype(jnp.float32) * kcdec_ref[0] * scale).astype(jnp.bfloat16)
        s_ref[...] = lax.dot_general(kc, vc_ref[0].astype(jnp.bfloat16), (((0,), (0,)), ((), ())),
                                     preferred_element_type=jnp.float32)

    cos = cos_ref[...]
    sin = sin_ref[...]
    q = _rope_halves(q_ref[0].astype(jnp.float32), cos, sin)
    k = _rope_halves(k_ref[0].astype(jnp.float32), cos, sin) * scale
    v = v_ref[0].astype(jnp.bfloat16)
    s = s_ref[...]
    att = lax.dot_general(q.astype(jnp.bfloat16), k.astype(jnp.bfloat16), (((1,), (1,)), ((), ())),
                          preferred_element_type=jnp.float32) * dmat_ref[0]
    o = (jnp.dot(att.astype(jnp.bfloat16), v, preferred_element_type=jnp.float32)
         + jnp.dot((q * qdec_ref[0]).astype(jnp.bfloat16), s.astype(jnp.bfloat16),
                   preferred_element_type=jnp.float32))
    s_ref[...] = s * cdec_ref[0] + lax.dot_general((k * kdec_ref[0]).astype(jnp.bfloat16), v,
                                                   (((0,), (0,)), ((), ())),
                                                   preferred_element_type=jnp.float32)
    mu = jnp.mean(o, axis=-1, keepdims=True)
    oc = o - mu
    var = jnp.mean(oc * oc, axis=-1, keepdims=True)
    out = _silu(g_ref[0].astype(jnp.float32)) * (oc * lax.rsqrt(var + 1e-6))
    if prev_ref is not None:
        out = out + prev_ref[0]
    o_ref[0] = out.astype(o_ref.dtype)


def _retention_tables(gamma, reverse, n_ctx):
    log_g = jnp.log(gamma)[:, None, None]
    i = jnp.arange(RET_TC, dtype=jnp.float32)
    rel = (i[None, :] - i[:, None]) if reverse else (i[:, None] - i[None, :])
    dmat = jnp.where(rel >= 0, jnp.exp(jnp.maximum(rel, 0.0)[None] * log_g), 0.0)
    q_pow = (RET_TC - i) if reverse else (i + 1.0)
    k_pow = i if reverse else (RET_TC - 1.0 - i)
    m = jnp.arange(n_ctx, dtype=jnp.float32)
    c_pow = m if reverse else (n_ctx - 1.0 - m)
    bc = lambda p: jnp.broadcast_to(jnp.exp(p[None, :, None] * log_g), (RET_HEADS, p.shape[0], RET_KEY))
    cdec = jnp.broadcast_to(jnp.exp(RET_TC * log_g), (RET_HEADS, 1, RET_VAL))
    return dmat, bc(q_pow), bc(k_pow), bc(c_pow), cdec


def _rope_tables(n_tokens):
    pos_row = (jnp.arange(n_tokens) // GRID_W).astype(jnp.float32)
    pos_col = (jnp.arange(n_tokens) % GRID_W).astype(jnp.float32)
    nf = RET_KEY // 4
    inv_freq = ROPE_BASE ** (-jnp.arange(nf, dtype=jnp.float32) / nf)
    cos, sin = [], []
    for pos in (pos_row, pos_col):
        ang = pos[:, None] * inv_freq[None, :]
        cos += [jnp.cos(ang), jnp.cos(ang)]
        sin += [-jnp.sin(ang), jnp.sin(ang)]
    return jnp.concatenate(cos, axis=1), jnp.concatenate(sin, axis=1)


def _retention_direction(z, zc, cos, sin, gamma, reverse, prev):
    b, t, _ = z.shape
    n_ctx = zc.shape[1]
    nc = t // RET_TC
    dmat, qdec, kdec, kcdec, cdec = _retention_tables(gamma, reverse, n_ctx)
    ch = (lambda c: nc - 1 - c) if reverse else (lambda c: c)
    kq, kv = RET_QK // RET_KEY, (2 * RET_QK) // RET_VAL
    g_off = (2 * RET_QK + (2 if reverse else 1) * RET_VD) // RET_VAL
    in_specs = [
        pl.BlockSpec((1, RET_TC, RET_KEY), lambda bi, h, c: (bi, ch(c), h)),
        pl.BlockSpec((1, RET_TC, RET_KEY), lambda bi, h, c: (bi, ch(c), kq + h)),
        pl.BlockSpec((1, RET_TC, RET_VAL), lambda bi, h, c: (bi, ch(c), kv + h)),
        pl.BlockSpec((1, RET_TC, RET_VAL), lambda bi, h, c: (bi, ch(c), g_off + h)),
        pl.BlockSpec((1, n_ctx, RET_KEY), lambda bi, h, c: (bi, 0, h)),
        pl.BlockSpec((1, n_ctx, RET_VAL), lambda bi, h, c: (bi, 0, RET_QK // RET_VAL + h)),
        pl.BlockSpec((RET_TC, RET_KEY), lambda bi, h, c: (ch(c), 0)),
        pl.BlockSpec((RET_TC, RET_KEY), lambda bi, h, c: (ch(c), 0)),
        pl.BlockSpec((1, RET_TC, RET_TC), lambda bi, h, c: (h, 0, 0)),
        pl.BlockSpec((1, RET_TC, RET_KEY), lambda bi, h, c: (h, 0, 0)),
        pl.BlockSpec((1, RET_TC, RET_KEY), lambda bi, h, c: (h, 0, 0)),
        pl.BlockSpec((1, n_ctx, RET_KEY), lambda bi, h, c: (h, 0, 0)),
        pl.BlockSpec((1, 1, RET_VAL), lambda bi, h, c: (h, 0, 0)),
    ]
    args = [z, z, z, z, zc, zc, cos, sin, dmat, qdec, kdec, kcdec, cdec]
    if prev is not None:
        in_specs.append(pl.BlockSpec((1, RET_TC, RET_VAL), lambda bi, h, c: (bi, ch(c), h)))
        args.append(prev)
    return pl.pallas_call(
        functools.partial(_retention_kernel, has_prev=prev is not None),
        grid=(b, RET_HEADS, nc),
        in_specs=in_specs,
        out_specs=pl.BlockSpec((1, RET_TC, RET_VAL), lambda bi, h, c: (bi, ch(c), h)),
        out_shape=jax.ShapeDtypeStruct((b, t, RET_VD), jnp.float32 if prev is None else jnp.bfloat16),
        scratch_shapes=[pltpu.VMEM((RET_KEY, RET_VAL), jnp.float32)],
        compiler_params=pltpu.CompilerParams(dimension_semantics=("parallel", "parallel", "arbitrary"),
                                             vmem_limit_bytes=VMEM_LIMIT_BYTES),
        name="retention_bwd" if reverse else "retention_fwd",
    )(*args)


def retention_mixer(z, zc, decay_f, decay_b):
    cos, sin = _rope_tables(z.shape[1])
    gamma_f = 1.0 - jnp.exp2(-decay_f)
    gamma_b = 1.0 - jnp.exp2(-decay_b)
    part = _retention_direction(z, zc, cos, sin, gamma_b, True, None)
    return _retention_direction(z, zc, cos, sin, gamma_f, False, part)


def _moe_kernel(te_ref, tv_ref, tok_ref, tok_next_ref, h_ref, wg_ref, wu_ref, wd_ref, o_ref, x_ref, acc_ref, sem):
    i = pl.program_id(0)
    j = pl.program_id(1)
    rows = x_ref.shape[1]
    slot = i % 2

    def gather(toks, buf, wait):
        def one(r, carry):
            cp = pltpu.make_async_copy(h_ref.at[pl.ds(0 if wait else toks[0, 0, r], 1)],
                                       x_ref.at[buf, pl.ds(r, 1)], sem.at[buf])
            cp.wait() if wait else cp.start()
            return carry
        lax.fori_loop(0, rows, one, 0, unroll=MOE_DMA_UNROLL)

    @pl.when(tv_ref[i] > 0)
    def _():
        @pl.when(j == 0)
        def _():
            @pl.when(i == 0)
            def _():
                gather(tok_ref, slot, False)

            gather(tok_ref, slot, True)
            nxt = jnp.minimum(i + 1, pl.num_programs(0) - 1)

            @pl.when(jnp.logical_and(i + 1 < pl.num_programs(0), tv_ref[nxt] > 0))
            def _():
                gather(tok_next_ref, 1 - slot, False)

            acc_ref[...] = jnp.zeros_like(acc_ref)

        x = x_ref[slot].astype(jnp.bfloat16)
        g = jnp.dot(x, wg_ref[0].astype(jnp.bfloat16), preferred_element_type=jnp.float32)
        u = jnp.dot(x, wu_ref[0].astype(jnp.bfloat16), preferred_element_type=jnp.float32)
        a = (_silu(g) * u).astype(jnp.bfloat16)
        acc_ref[...] += jnp.dot(a, wd_ref[0].astype(jnp.bfloat16), preferred_element_type=jnp.float32)

    last = j == pl.num_programs(1) - 1

    @pl.when(jnp.logical_and(last, tv_ref[i] > 0))
    def _():
        o_ref[...] = acc_ref[...]

    @pl.when(jnp.logical_and(last, tv_ref[i] == 0))
    def _():
        o_ref[...] = jnp.zeros_like(o_ref)


def moe_experts(h, slot_tok, tile_expert, tile_valid, w_gate, w_up, w_down):
    n_tiles = slot_tok.shape[0]
    d = h.shape[1]
    ff = w_gate.shape[2]
    grid_spec = pltpu.PrefetchScalarGridSpec(
        num_scalar_prefetch=2,
        grid=(n_tiles, ff // MOE_TF),
        in_specs=[
            pl.BlockSpec((1, 1, MOE_TM), lambda i, j, te, tv: (i, 0, 0), memory_space=pltpu.SMEM),
            pl.BlockSpec((1, 1, MOE_TM), lambda i, j, te, tv: (jnp.minimum(i + 1, n_tiles - 1), 0, 0),
                         memory_space=pltpu.SMEM),
            pl.BlockSpec(memory_space=pl.ANY),
            pl.BlockSpec((1, d, MOE_TF), lambda i, j, te, tv: (te[i], 0, j)),
            pl.BlockSpec((1, d, MOE_TF), lambda i, j, te, tv: (te[i], 0, j)),
            pl.BlockSpec((1, MOE_TF, d), lambda i, j, te, tv: (te[i], j, 0)),
        ],
        out_specs=pl.BlockSpec((MOE_TM, d), lambda i, j, te, tv: (i, 0)),
        scratch_shapes=[pltpu.VMEM((2, MOE_TM, d), jnp.float32), pltpu.VMEM((MOE_TM, d), jnp.float32),
                        pltpu.SemaphoreType.DMA((2,))],
    )
    return pl.pallas_call(
        _moe_kernel,
        grid_spec=grid_spec,
        out_shape=jax.ShapeDtypeStruct((n_tiles * MOE_TM, d), jnp.float32),
        compiler_params=pltpu.CompilerParams(dimension_semantics=("arbitrary", "arbitrary"),
                                             vmem_limit_bytes=VMEM_LIMIT_BYTES),
        name="moe_experts",
    )(tile_expert, tile_valid, slot_tok, slot_tok, h, w_gate, w_up, w_down)


def moe_swiglu(h, router, w_gate, w_up, w_down):
    b, t, d = h.shape
    n = b * t
    hf = h.reshape(n, d)
    logits = jnp.dot(hf, router, precision=lax.Precision.HIGHEST)
    top_val, top_idx = lax.top_k(logits, TOP_K)
    gate = jax.nn.softmax(top_val, axis=-1)
    flat_e = top_idx.reshape(-1)
    onehot = (flat_e[:, None] == jnp.arange(N_EXPERTS, dtype=flat_e.dtype)[None, :]).astype(jnp.int32)
    csum = jnp.cumsum(onehot, axis=0)
    counts = csum[-1]
    rank = jnp.sum((csum - onehot) * onehot, axis=1)
    padded = (counts + MOE_TM - 1) // MOE_TM * MOE_TM
    pad_end = jnp.cumsum(padded)
    pad_start = pad_end - padded
    slot = (pad_start[flat_e] + rank).astype(jnp.int32)
    n_tiles = (n * TOP_K) // MOE_TM + N_EXPERTS
    slots = n_tiles * MOE_TM
    tok = jnp.arange(n * TOP_K, dtype=jnp.int32) // TOP_K
    slot_tok = jnp.zeros((slots,), jnp.int32).at[slot].set(tok)
    tile_start = jnp.arange(n_tiles, dtype=jnp.int32) * MOE_TM
    tile_expert = jnp.minimum(jnp.sum(pad_end[None, :] <= tile_start[:, None], axis=1),
                              N_EXPERTS - 1).astype(jnp.int32)
    tile_valid = (tile_start < pad_end[-1]).astype(jnp.int32)
    yb = moe_experts(hf, slot_tok.reshape(n_tiles, 1, MOE_TM), tile_expert, tile_valid, w_gate, w_up, w_down)
    slot2 = slot.reshape(n, TOP_K)
    out = gate[:, 0:1] * yb[slot2[:, 0]] + gate[:, 1:2] * yb[slot2[:, 1]]
    return out.reshape(b, t, d)


def _row_select(i, tm, n_ctx, ctx_vec, lat_vec):
    if n_ctx == 0:
        return lat_vec
    row = i * tm + lax.broadcasted_iota(jnp.int32, (tm, 1), 0)
    return jnp.where(row < n_ctx, ctx_vec, lat_vec)


def _post_norm_rows(x, update, g, b):
    y = DEEPNORM_ALPHA * x + update
    mu = jnp.mean(y, axis=-1, keepdims=True)
    yc = y - mu
    var = jnp.mean(yc * yc, axis=-1, keepdims=True)
    return yc * lax.rsqrt(var + LN_EPS) * g + b


def _matmul_postnorm_kernel(a_ref, w_ref, x_ref, lat_ref, ctx_ref, g_ref, b_ref, o_ref, *, n_ctx):
    tm = a_ref.shape[1]
    o = jnp.dot(a_ref[0].astype(jnp.bfloat16), w_ref[...], preferred_element_type=jnp.float32)
    gate = _row_select(pl.program_id(1), tm, n_ctx, ctx_ref[...], lat_ref[0])
    o_ref[0] = _post_norm_rows(x_ref[0], gate * o, g_ref[...], b_ref[...])


def matmul_postnorm(a, w, x, gate_lat, gate_ctx, ln_g, ln_b, n_ctx):
    b, length, k = a.shape
    d = w.shape[1]
    tm = _pick_tile(length, (768, 512, 256))
    vec = pl.BlockSpec((1, d), lambda bi, i: (0, 0))
    return pl.pallas_call(
        functools.partial(_matmul_postnorm_kernel, n_ctx=n_ctx),
        grid=(b, length // tm),
        in_specs=[pl.BlockSpec((1, tm, k), lambda bi, i: (bi, i, 0)),
                  pl.BlockSpec((k, d), lambda bi, i: (0, 0)),
                  pl.BlockSpec((1, tm, d), lambda bi, i: (bi, i, 0)),
                  pl.BlockSpec((1, 1, d), lambda bi, i: (bi, 0, 0)), vec, vec, vec],
        out_specs=pl.BlockSpec((1, tm, d), lambda bi, i: (bi, i, 0)),
        out_shape=jax.ShapeDtypeStruct((b, length, d), jnp.float32),
        compiler_params=pltpu.CompilerParams(dimension_semantics=("parallel", "parallel"),
                                             vmem_limit_bytes=VMEM_LIMIT_BYTES),
        name="matmul_postnorm",
    )(a, w.astype(jnp.bfloat16), x, gate_lat[:, None, :], gate_ctx[None, :], ln_g[None, :], ln_b[None, :])


def _ffn_postnorm_kernel(x_ref, lat_ref, ctx_ref, wg_ref, wu_ref, wd_ref, g_ref, b_ref, o_ref, h_ref, acc_ref,
                         *, n_ctx):
    i = pl.program_id(1)
    j = pl.program_id(2)
    tm = x_ref.shape[1]

    def vec(k):
        return _row_select(i, tm, n_ctx, ctx_ref[k:k + 1, :], lat_ref[0, k:k + 1, :])

    @pl.when(j == 0)
    def _():
        h_ref[...] = (x_ref[0] * (1.0 + vec(1)) + vec(0)).astype(jnp.bfloat16)
        acc_ref[...] = jnp.zeros_like(acc_ref)

    h = h_ref[...]
    gt = jnp.dot(h, wg_ref[...], preferred_element_type=jnp.float32)
    up = jnp.dot(h, wu_ref[...], preferred_element_type=jnp.float32)
    acc_ref[...] += jnp.dot((_silu(gt) * up).astype(jnp.bfloat16), wd_ref[...], preferred_element_type=jnp.float32)

    @pl.when(j == pl.num_programs(2) - 1)
    def _():
        o_ref[0] = _post_norm_rows(x_ref[0], vec(2) * acc_ref[...], g_ref[...], b_ref[...])


def ffn_postnorm(x, mod_lat, mod_ctx, w_gate, w_up, w_down, ln_g, ln_b, n_ctx):
    b, length, d = x.shape
    ff = w_gate.shape[1]
    tm = _pick_tile(length, (384, 512, 256))
    tf = _pick_tile(ff, (1408, 512, 256))
    vec = pl.BlockSpec((1, d), lambda bi, i, j: (0, 0))
    return pl.pallas_call(
        functools.partial(_ffn_postnorm_kernel, n_ctx=n_ctx),
        grid=(b, length // tm, ff // tf),
        in_specs=[pl.BlockSpec((1, tm, d), lambda bi, i, j: (bi, i, 0)),
                  pl.BlockSpec((1, 3, d), lambda bi, i, j: (bi, 0, 0)),
                  pl.BlockSpec((3, d), lambda bi, i, j: (0, 0)),
                  pl.BlockSpec((d, tf), lambda bi, i, j: (0, j)),
                  pl.BlockSpec((d, tf), lambda bi, i, j: (0, j)),
                  pl.BlockSpec((tf, d), lambda bi, i, j: (j, 0)), vec, vec],
        out_specs=pl.BlockSpec((1, tm, d), lambda bi, i, j: (bi, i, 0)),
        out_shape=jax.ShapeDtypeStruct((b, length, d), jnp.float32),
        scratch_shapes=[pltpu.VMEM((tm, d), jnp.bfloat16), pltpu.VMEM((tm, d), jnp.float32)],
        compiler_params=pltpu.CompilerParams(dimension_semantics=("parallel", "parallel", "arbitrary"),
                                             vmem_limit_bytes=VMEM_LIMIT_BYTES),
        name="ffn_postnorm",
    )(x, mod_lat, mod_ctx, w_gate.astype(jnp.bfloat16), w_up.astype(jnp.bfloat16), w_down.astype(jnp.bfloat16),
      ln_g[None, :], ln_b[None, :])


def _even_layer(x, xc, sc, scc, mod_w, mod_b, w_in, mla_q_norm, mla_wq_up, mla_kv_norm, mla_wkv_up,
                rwkv_conv, rwkv_w0_f, rwkv_w2_f, rwkv_w0_b, rwkv_w2_b, rwkv_a0_f, rwkv_a2_f, rwkv_a0_b,
                rwkv_a2_b, rwkv_g2, rwkv_k_k, rwkv_k_a, rwkv_r_k, rwkv_lnx_g, rwkv_lnx_b, w_out,
                ln1_g, ln1_b, ffn_w_gate, ffn_w_up, ffn_w_down, ln2_g, ln2_b):
    n_ctx = xc.shape[1]
    m = _adaln(sc, mod_w, mod_b)
    mc = _adaln(scc, mod_w, mod_b)
    xa = jnp.concatenate([xc, x], axis=1)
    is_ctx = (jnp.arange(xa.shape[1]) < n_ctx)[None, :, None]

    def rows(k):
        return jnp.where(is_ctx, mc[k][None, None, :], m[k][:, None, :])

    h = _modulate(xa, rows(0), rows(1)).astype(jnp.bfloat16)
    o_mla = _mla_mixer(h, n_ctx, w_in, mla_q_norm, mla_wq_up, mla_kv_norm, mla_wkv_up)
    o_rwkv = _rwkv7_mixer(mm(h, w_in[:, MLA_IN:]), n_ctx, rwkv_conv, rwkv_w0_f, rwkv_w2_f, rwkv_w0_b, rwkv_w2_b,
                          rwkv_a0_f, rwkv_a2_f, rwkv_a0_b, rwkv_a2_b, rwkv_g2, rwkv_k_k, rwkv_k_a,
                          rwkv_r_k, rwkv_lnx_g, rwkv_lnx_b)
    mix = jnp.concatenate([o_mla.astype(jnp.bfloat16), o_rwkv], axis=-1)
    xa = matmul_postnorm(mix, w_out, xa, m[2], mc[2], ln1_g, ln1_b, n_ctx)
    xa = ffn_postnorm(xa, jnp.stack(m[3:6], axis=1), jnp.stack(mc[3:6], axis=0),
                      ffn_w_gate, ffn_w_up, ffn_w_down, ln2_g, ln2_b, n_ctx)
    return xa[:, n_ctx:], xa[:, :n_ctx]


def _odd_layer(x, xc, sc, scc, mod_w, mod_b, w_in, ret_decay_f, ret_decay_b, w_out, ln1_g, ln1_b,
               router, moe_w_gate, moe_w_up, moe_w_down, ln2_g, ln2_b):
    m = _adaln(sc, mod_w, mod_b)
    c_shift, c_scale = jnp.split(
        jnp.dot(scc, mod_w[:, :2 * D_MODEL], precision=lax.Precision.HIGHEST) + mod_b[:2 * D_MODEL], 2)
    z = mm(_modulate(x, m[0][:, None, :], m[1][:, None, :]), w_in, jnp.bfloat16)
    zc = mm(_modulate(xc, c_shift, c_scale), w_in[:, RET_QK:2 * RET_QK + RET_VD], jnp.bfloat16)
    mix = retention_mixer(z, zc, ret_decay_f, ret_decay_b)
    x = matmul_postnorm(mix, w_out, x, m[2], jnp.zeros_like(m[2][0]), ln1_g, ln1_b, 0)
    y = moe_swiglu(_modulate(x, m[3][:, None, :], m[4][:, None, :]), router, moe_w_gate, moe_w_up, moe_w_down)
    return _post_norm(x, m[5][:, None, :] * y, ln2_g, ln2_b)


def kernel(x, c, ctx, c_ctx, l0_mod_w, l0_mod_b, l0_w_in, l0_mla_q_norm, l0_mla_wq_up, l0_mla_kv_norm, l0_mla_wkv_up, l0_rwkv_conv, l0_rwkv_w0_f, l0_rwkv_w2_f, l0_rwkv_w0_b, l0_rwkv_w2_b, l0_rwkv_a0_f, l0_rwkv_a2_f, l0_rwkv_a0_b, l0_rwkv_a2_b, l0_rwkv_g2, l0_rwkv_k_k, l0_rwkv_k_a, l0_rwkv_r_k, l0_rwkv_lnx_g, l0_rwkv_lnx_b, l0_w_out, l0_ln1_g, l0_ln1_b, l0_ffn_w_gate, l0_ffn_w_up, l0_ffn_w_down, l0_ln2_g, l0_ln2_b, l1_mod_w, l1_mod_b, l1_w_in, l1_ret_decay_f, l1_ret_decay_b, l1_w_out, l1_ln1_g, l1_ln1_b, l1_router, l1_moe_w_gate, l1_moe_w_up, l1_moe_w_down, l1_ln2_g, l1_ln2_b):
    even_params = (l0_mod_w, l0_mod_b, l0_w_in, l0_mla_q_norm, l0_mla_wq_up, l0_mla_kv_norm, l0_mla_wkv_up,
                   l0_rwkv_conv, l0_rwkv_w0_f, l0_rwkv_w2_f, l0_rwkv_w0_b, l0_rwkv_w2_b, l0_rwkv_a0_f,
                   l0_rwkv_a2_f, l0_rwkv_a0_b, l0_rwkv_a2_b, l0_rwkv_g2, l0_rwkv_k_k, l0_rwkv_k_a, l0_rwkv_r_k,
                   l0_rwkv_lnx_g, l0_rwkv_lnx_b, l0_w_out, l0_ln1_g, l0_ln1_b, l0_ffn_w_gate, l0_ffn_w_up,
                   l0_ffn_w_down, l0_ln2_g, l0_ln2_b)
    odd_params = (l1_mod_w, l1_mod_b, l1_w_in, l1_ret_decay_f, l1_ret_decay_b, l1_w_out, l1_ln1_g, l1_ln1_b,
                  l1_router, l1_moe_w_gate, l1_moe_w_up, l1_moe_w_down, l1_ln2_g, l1_ln2_b)
    sc = _silu(c)
    scc = _silu(c_ctx)
    x, xc = _even_layer(x, ctx, sc, scc, *even_params)
    return _odd_layer(x, xc, sc, scc, *odd_params)
```

```python
import functools

import jax
import jax.numpy as jnp
import numpy as np
from jax import lax
from jax.experimental import pallas as pl
from jax.experimental.pallas import tpu as pltpu

D_MODEL = 1024
DEPTH = 2
GRID_W = 64
ROPE_BASE = 10000.0
DEEPNORM_ALPHA = (2 * DEPTH) ** 0.25
LN_EPS = 1e-5

MLA_HEADS = 8
MLA_Q_RANK = 256
MLA_KV_RANK = 128
MLA_NOPE = 64
MLA_ROPE = 32
MLA_V = 64
MLA_IN = MLA_Q_RANK + MLA_KV_RANK + MLA_ROPE
MLA_OUT = MLA_HEADS * MLA_V

RWKV_HEADS = 8
RWKV_HEAD = 64
RWKV_DIM = RWKV_HEADS * RWKV_HEAD
RWKV_DECAY_LORA = 64
RWKV_AAA_LORA = 64
RWKV_GATE_LORA = 128
RWKV_LNX_EPS = 64e-5

RET_HEADS = 4
RET_KEY = 256
RET_VAL = 512
RET_TC = 512
RET_QK = RET_HEADS * RET_KEY
RET_VD = RET_HEADS * RET_VAL

N_EXPERTS = 8
TOP_K = 2
MOE_TM = 1024
MOE_TF = 512
MOE_DMA_UNROLL = 8

LANES = 128
VMEM_LIMIT_BYTES = 48 * 1024 * 1024


def _matmul_kernel(x_ref, w_ref, o_ref):
    o_ref[...] = jnp.dot(x_ref[...].astype(jnp.bfloat16), w_ref[...],
                         preferred_element_type=jnp.float32).astype(o_ref.dtype)


def _pick_tile(n, candidates):
    for c in candidates:
        if n % c == 0:
            return c
    raise ValueError(f"no tile for {n}")


def pmatmul(x, w, out_dtype=jnp.float32):
    m, k = x.shape
    n = w.shape[1]
    n_pad = (-n) % LANES
    w = w.astype(jnp.bfloat16)
    if n_pad:
        w = jnp.pad(w, ((0, 0), (0, n_pad)))
    np_ = n + n_pad
    tm = _pick_tile(m, (1024, 512, 256, 128, 8) if k <= 1024 else (512, 256, 128, 8))
    tn = _pick_tile(np_, (1024, 768, 640, 512, 384, 256, 128))
    out = pl.pallas_call(
        _matmul_kernel,
        grid=(m // tm, np_ // tn),
        in_specs=[pl.BlockSpec((tm, k), lambda i, j: (i, 0)),
                  pl.BlockSpec((k, tn), lambda i, j: (0, j))],
        out_specs=pl.BlockSpec((tm, tn), lambda i, j: (i, j)),
        out_shape=jax.ShapeDtypeStruct((m, np_), out_dtype),
        compiler_params=pltpu.CompilerParams(
            dimension_semantics=("parallel", "parallel"),
            vmem_limit_bytes=VMEM_LIMIT_BYTES),
        name="matmul",
    )(x, w)
    return out[:, :n] if n_pad else out


def mm(x, w, out_dtype=jnp.float32):
    lead = x.shape[:-1]
    out = pmatmul(x.reshape(-1, x.shape[-1]).astype(jnp.bfloat16), w, out_dtype)
    return out.reshape(*lead, w.shape[1])


def _silu(t):
    return t * jax.nn.sigmoid(t)


def _normalize(t, eps):
    mu = jnp.mean(t, axis=-1, keepdims=True)
    var = jnp.mean(jnp.square(t - mu), axis=-1, keepdims=True)
    return (t - mu) * lax.rsqrt(var + eps)


def _layer_norm(t, g, b):
    return _normalize(t, LN_EPS) * g + b


def _rms_norm(t, g, eps=1e-6):
    return t * lax.rsqrt(jnp.mean(t * t, axis=-1, keepdims=True) + eps) * g


def _post_norm(x, update, g, b):
    return _layer_norm(DEEPNORM_ALPHA * x + update, g, b)


def _modulate(h, shift, scale):
    return h * (1.0 + scale) + shift


def _adaln(cond, mod_w, mod_b):
    return jnp.split(jnp.dot(cond, mod_w, precision=lax.Precision.HIGHEST) + mod_b, 6, axis=-1)


def _grid_positions(n_tokens):
    rows = n_tokens // GRID_W
    row = jnp.repeat(jnp.arange(rows, dtype=jnp.float32), GRID_W)
    col = jnp.tile(jnp.arange(GRID_W, dtype=jnp.float32), rows)
    return row, col


SOFTMAX_FLOOR = -1e30


FLASH_ROW_GROUPS = 4


def _flash_kernel(q_ref, kt_ref, v_ref, o_ref, m_ref, acc_ref, *, dv):
    j = pl.program_id(3)

    @pl.when(j == 0)
    def _():
        m_ref[...] = jnp.full_like(m_ref, SOFTMAX_FLOOR)
        acc_ref[...] = jnp.zeros_like(acc_ref)

    rows = q_ref.shape[2] // FLASH_ROW_GROUPS
    kt = kt_ref[0, 0]
    v = v_ref[0, 0]
    for u in range(FLASH_ROW_GROUPS):
        sl = slice(u * rows, (u + 1) * rows)
        s = jnp.dot(q_ref[0, 0, sl, :], kt, preferred_element_type=jnp.float32)
        m_prev = m_ref[sl, :]
        m_new = jnp.maximum(m_prev, jnp.max(s, axis=-1, keepdims=True))
        alpha = jnp.exp(m_prev - m_new)
        p = jnp.exp(s - m_new).astype(jnp.bfloat16)
        acc_ref[sl, :] = alpha * acc_ref[sl, :] + jnp.dot(p, v, preferred_element_type=jnp.float32)
        m_ref[sl, :] = m_new

    @pl.when(j == pl.num_programs(3) - 1)
    def _():
        acc = acc_ref[...]
        o_ref[0, 0] = acc[:, :dv] / acc[:, dv:dv + 1]


def flash_attention(q, kt, v1, dv, tq, tk):
    b, h, t, dq = q.shape
    s = kt.shape[3]
    return pl.pallas_call(
        functools.partial(_flash_kernel, dv=dv),
        grid=(b, h, t // tq, s // tk),
        in_specs=[pl.BlockSpec((1, 1, tq, dq), lambda bi, hi, i, j: (bi, hi, i, 0)),
                  pl.BlockSpec((1, 1, dq, tk), lambda bi, hi, i, j: (bi, hi, 0, j)),
                  pl.BlockSpec((1, 1, tk, LANES), lambda bi, hi, i, j: (bi, hi, j, 0))],
        out_specs=pl.BlockSpec((1, 1, tq, dv), lambda bi, hi, i, j: (bi, hi, i, 0)),
        out_shape=jax.ShapeDtypeStruct((b, h, t, dv), jnp.float32),
        scratch_shapes=[pltpu.VMEM((tq, 1), jnp.float32), pltpu.VMEM((tq, LANES), jnp.float32)],
        compiler_params=pltpu.CompilerParams(
            dimension_semantics=("parallel", "parallel", "parallel", "arbitrary"),
            vmem_limit_bytes=VMEM_LIMIT_BYTES),
        name="flash_attention",
    )(q, kt, v1)


def _block_attention(q, k, v):
    dq = q.shape[-1]
    dv = v.shape[-1]
    b, s, h, _ = k.shape
    qh = jnp.transpose(q * dq ** -0.5, (0, 2, 1, 3)).astype(jnp.bfloat16)
    kt = jnp.transpose(k, (0, 2, 3, 1)).astype(jnp.bfloat16)
    v1 = jnp.concatenate([v, jnp.ones((b, s, h, 1), v.dtype), jnp.zeros((b, s, h, LANES - dv - 1), v.dtype)],
                         axis=-1)
    v1 = jnp.transpose(v1, (0, 2, 1, 3)).astype(jnp.bfloat16)
    tq = _pick_tile(q.shape[1], (1024, 256))
    o = flash_attention(qh, kt, v1, dv, tq, s)
    return jnp.transpose(o, (0, 2, 1, 3))


_Q8 = MLA_ROPE // 4
ROPE_PARTNER = np.concatenate([np.arange(_Q8, 2 * _Q8), np.arange(0, _Q8),
                               np.arange(3 * _Q8, 4 * _Q8), np.arange(2 * _Q8, 3 * _Q8)])
ROPE_SIGN = np.concatenate([-np.ones(_Q8), np.ones(_Q8), -np.ones(_Q8), np.ones(_Q8)]).astype(np.float32)


def _mla_rope_tables(n_ctx, t):
    row, col = _grid_positions(t)
    zero = jnp.zeros((n_ctx,), jnp.float32)
    row, col = jnp.concatenate([zero, row]), jnp.concatenate([zero, col])
    inv_freq = ROPE_BASE ** (-jnp.arange(_Q8, dtype=jnp.float32) / _Q8)
    ang = jnp.concatenate([row[:, None] * inv_freq] * 2 + [col[:, None] * inv_freq] * 2, axis=1)
    return jnp.cos(ang), jnp.sin(ang)


def _mla_mixer(h, n_ctx, w_in, q_norm, wq_up, kv_norm, wkv_up):
    b, length, _ = h.shape
    cos, sin = _mla_rope_tables(n_ctx, length - n_ctx)
    k_cols = np.arange(MLA_Q_RANK + MLA_KV_RANK, MLA_IN)
    w_z = jnp.concatenate([w_in[:, :MLA_IN], w_in[:, k_cols[ROPE_PARTNER]] * ROPE_SIGN], axis=1)
    z = mm(h, w_z)
    cq, ckv = z[..., :MLA_Q_RANK], z[..., MLA_Q_RANK:MLA_Q_RANK + MLA_KV_RANK]
    k_rope = z[..., k_cols[0]:MLA_IN] * cos + z[..., MLA_IN:] * sin
    dq = MLA_NOPE + MLA_ROPE
    head_cols = np.arange(MLA_HEADS)[:, None] * dq
    q_rope_cols = (head_cols + MLA_NOPE + np.arange(MLA_ROPE)[None, :])
    wq_partner = jnp.zeros_like(wq_up).at[:, q_rope_cols.reshape(-1)].set(
        wq_up[:, (head_cols + MLA_NOPE + ROPE_PARTNER[None, :]).reshape(-1)] * jnp.tile(ROPE_SIGN, MLA_HEADS))
    q2 = mm(_rms_norm(cq, q_norm), jnp.concatenate([wq_up, wq_partner], axis=1))
    one, nil = jnp.ones((length, MLA_NOPE), jnp.float32), jnp.zeros((length, MLA_NOPE), jnp.float32)
    cos_q = jnp.tile(jnp.concatenate([one, cos], axis=1), (1, MLA_HEADS))
    sin_q = jnp.tile(jnp.concatenate([nil, sin], axis=1), (1, MLA_HEADS))
    q = (q2[..., :MLA_HEADS * dq] * cos_q + q2[..., MLA_HEADS * dq:] * sin_q).reshape(b, length, MLA_HEADS, dq)
    kv = mm(_rms_norm(ckv, kv_norm), wkv_up).reshape(b, length, MLA_HEADS, MLA_NOPE + MLA_V)
    k = jnp.concatenate([kv[..., :MLA_NOPE],
                         jnp.broadcast_to(k_rope[:, :, None, :], (b, length, MLA_HEADS, MLA_ROPE))], axis=-1)
    v = kv[..., MLA_NOPE:]
    o = _block_attention(q[:, n_ctx:], k, v)
    oc = _block_attention(q[:, :n_ctx], k[:, :n_ctx], v[:, :n_ctx])
    return jnp.concatenate([oc, o], axis=1).reshape(b, length, MLA_OUT)


SCAN_BLOCK = 128
SCAN_GROUP = 8
N_PAIRS = RWKV_HEADS // 2
PAIR_ROWS = N_PAIRS * RWKV_HEAD


def _to_state_tiles(vblk):
    halves = [[], []]
    for p in range(N_PAIRS):
        tr = vblk[:, p * LANES:(p + 1) * LANES].T
        for half in range(2):
            c = slice(half * RWKV_HEAD, (half + 1) * RWKV_HEAD)
            halves[half].append(jnp.concatenate([tr[:RWKV_HEAD, c], tr[RWKV_HEAD:, c]], axis=1))
    return [jnp.concatenate(h, axis=0) for h in halves]


def _from_state_tiles(tiles):
    cols = []
    for p in range(N_PAIRS):
        r = slice(p * RWKV_HEAD, (p + 1) * RWKV_HEAD)
        tr = jnp.concatenate([tiles[0][r, :], tiles[1][r, :]], axis=0).T
        cols.append(jnp.concatenate(
            [jnp.concatenate([tr[:RWKV_HEAD, :RWKV_HEAD], tr[:RWKV_HEAD, RWKV_HEAD:]], axis=0),
             jnp.concatenate([tr[RWKV_HEAD:, :RWKV_HEAD], tr[RWKV_HEAD:, RWKV_HEAD:]], axis=0)], axis=1))
    return jnp.concatenate(cols, axis=1)


def _scan_kernel(rf_ref, rb_ref, af_ref, ab_ref, vf_ref, vb_ref, wf_ref, wb_ref, kf_ref, kb_ref,
                 bf_ref, bb_ref, wred_ref, yf_ref, yb_ref, s_ref, vt_ref, yc_ref):
    nb = rf_ref.shape[0]

    @pl.when(pl.program_id(0) == 0)
    def _():
        s_ref[...] = jnp.zeros_like(s_ref)

    lane = lax.broadcasted_iota(jnp.int32, (PAIR_ROWS, LANES), 1)
    wred = wred_ref[...]
    refs = ((rf_ref, af_ref, vf_ref, wf_ref, kf_ref, bf_ref, yf_ref),
            (rb_ref, ab_ref, vb_ref, wb_ref, kb_ref, bb_ref, yb_ref))

    for d in range(2):
        for bi in range(nb):
            tiles = _to_state_tiles(refs[d][2][bi])
            for half in range(2):
                vt_ref[d, bi, half] = tiles[half].astype(jnp.bfloat16)

    def rows(ref, bi, base, u):
        return jnp.concatenate(
            [jnp.broadcast_to(ref.at[bi, pl.ds(base, SCAN_GROUP), :][u:u + 1, p * LANES:(p + 1) * LANES],
                              (RWKV_HEAD, LANES)) for p in range(N_PAIRS)], axis=0)

    def collect(d, bi, t, yb, valid):
        mask = jnp.logical_and(lane % RWKV_HEAD == t % RWKV_HEAD, valid)
        half = t // RWKV_HEAD
        yc_ref[d, bi, half] = jnp.where(mask, yb, yc_ref[d, bi, half])

    n_groups = SCAN_BLOCK // SCAN_GROUP

    def group(gi, carry):
        for u in range(SCAN_GROUP):
            for d in range(2):
                r_ref, a_ref, _, w_ref, k_ref, b_ref, _ = refs[d]
                g = gi if d == 0 else n_groups - 1 - gi
                base = pl.multiple_of(g * SCAN_GROUP, SCAN_GROUP)
                uu = u if d == 0 else SCAN_GROUP - 1 - u
                t = base + uu
                if u == 0:
                    gp = jnp.maximum(gi - 1, 0) if d == 0 else jnp.minimum(n_groups - gi, n_groups - 1)
                    pbase, pu = pl.multiple_of(gp * SCAN_GROUP, SCAN_GROUP), (SCAN_GROUP - 1 if d == 0 else 0)
                else:
                    pbase, pu = base, (uu - 1 if d == 0 else uu + 1)
                tp = pbase + pu
                sel = lane % RWKV_HEAD == t % RWKV_HEAD
                prs = []
                for bi in range(nb):
                    s = s_ref[d, bi]
                    pa = (s * rows(a_ref, bi, base, uu)).astype(jnp.bfloat16)
                    prs.append((s * rows(r_ref, bi, pbase, pu)).astype(jnp.bfloat16))
                    pv = jnp.where(sel, vt_ref[d, bi, t // RWKV_HEAD], jnp.zeros((), jnp.bfloat16))
                    red = jnp.dot(jnp.concatenate([pa, pv], axis=1), wred, preferred_element_type=jnp.float32)
                    s_ref[d, bi] = (s * rows(w_ref, bi, base, uu) + red[:, :LANES] * rows(b_ref, bi, base, uu)
                                    + red[:, LANES:] * rows(k_ref, bi, base, uu))
                valid = True if u > 0 else gi >= 1
                for b0 in range(0, nb, 2):
                    ys = jnp.dot(jnp.concatenate(prs[b0:b0 + 2], axis=1), wred, preferred_element_type=jnp.float32)
                    collect(d, b0, tp, ys[:, :LANES], valid)
                    collect(d, b0 + 1, tp, ys[:, LANES:], valid)
        return carry

    lax.fori_loop(0, n_groups, group, 0)

    for d in range(2):
        r_ref, y_ref = refs[d][0], refs[d][6]
        t_last = SCAN_BLOCK - 1 if d == 0 else 0
        for b0 in range(0, nb, 2):
            prs = [(s_ref[d, bi] * rows(r_ref, bi, t_last - t_last % SCAN_GROUP, t_last % SCAN_GROUP)).astype(jnp.bfloat16)
                   for bi in (b0, b0 + 1)]
            ys = jnp.dot(jnp.concatenate(prs, axis=1), wred, preferred_element_type=jnp.float32)
            collect(d, b0, t_last, ys[:, :LANES], True)
            collect(d, b0 + 1, t_last, ys[:, LANES:], True)
        for bi in range(nb):
            y_ref[bi] = _from_state_tiles([yc_ref[d, bi, 0], yc_ref[d, bi, 1]])


def rwkv_scan(r, v, a, w_f, k_f, b_f, w_b, k_b, b_b, n_ctx):
    nb, length, _ = r.shape
    nblk = length // SCAN_BLOCK
    nblk_ctx = n_ctx // SCAN_BLOCK
    j = np.arange(2 * LANES)
    wred = (j[:, None] // RWKV_HEAD) == (j[None, :] // RWKV_HEAD)

    def fwd(i):
        return i

    def bwd(i):
        return jnp.where(i < nblk_ctx, nblk_ctx - 1 - i, nblk + nblk_ctx - 1 - i)

    def row_spec(blk):
        return pl.BlockSpec((nb, SCAN_BLOCK, RWKV_DIM), lambda i: (0, blk(i), 0))

    y_shape = jax.ShapeDtypeStruct((nb, length, RWKV_DIM), jnp.float32)
    tile_shape = (2, nb, 2, PAIR_ROWS, LANES)
    return pl.pallas_call(
        _scan_kernel,
        grid=(nblk,),
        in_specs=[row_spec(fwd), row_spec(bwd)] * 6 + [pl.BlockSpec((2 * LANES, 2 * LANES), lambda i: (0, 0))],
        out_specs=[row_spec(fwd), row_spec(bwd)],
        out_shape=[y_shape, y_shape],
        scratch_shapes=[pltpu.VMEM((2, nb, PAIR_ROWS, LANES), jnp.float32),
                        pltpu.VMEM(tile_shape, jnp.bfloat16), pltpu.VMEM(tile_shape, jnp.float32)],
        compiler_params=pltpu.CompilerParams(dimension_semantics=("arbitrary",),
                                             vmem_limit_bytes=VMEM_LIMIT_BYTES),
        name="rwkv_scan",
    )(r, r, a, a, v, v, w_f, w_b, k_f, k_b, b_f, b_b, jnp.asarray(wred, jnp.bfloat16))


FEAT_ROWS = 256


def _head_sums(x, ones_ref):
    hi = x.astype(jnp.bfloat16)
    lo = (x - hi.astype(jnp.float32)).astype(jnp.bfloat16)
    ones = ones_ref[...]
    return (jnp.dot(hi, ones, preferred_element_type=jnp.float32)
            + jnp.dot(lo, ones, preferred_element_type=jnp.float32))


def _softplus(x):
    return jnp.maximum(x, 0.0) + jnp.log(1.0 + jnp.exp(-jnp.abs(x)))


def _sigmoid(x):
    return 1.0 / (1.0 + jnp.exp(-x))


def _features_kernel(z_ref, prev_ref, next_ref, conv_ref, lora_ref, g2_ref, vec_ref, ones_ref,
                     r_ref, a_ref, v_ref, wf_ref, kf_ref, bf_ref, wb_ref, kb_ref, bb_ref, g_ref, rk_ref):
    rows = z_ref.shape[1]
    c0 = 3 * RWKV_DIM
    raw = z_ref[0, :, :c0]
    row_id = lax.broadcasted_iota(jnp.int32, (rows, 1), 0)
    before = jnp.where(row_id == 0, prev_ref[0, 0], pltpu.roll(raw, 1, axis=0))
    after = jnp.where(row_id == rows - 1, next_ref[0, 0], pltpu.roll(raw, rows - 1, axis=0))
    rkv = before * conv_ref[0:1, :] + raw * conv_ref[1:2, :] + after * conv_ref[2:3, :]
    r, k, v = rkv[:, :RWKV_DIM], rkv[:, RWKV_DIM:2 * RWKV_DIM], rkv[:, 2 * RWKV_DIM:]

    def vec(i):
        return vec_ref[i:i + 1, :]

    k_k, k_a, r_k, w0_f, w0_b, a0_f, a0_b = (vec(i) for i in range(7))
    kraw = k * k_k
    kk = kraw / jnp.maximum(jnp.sqrt(_head_sums(kraw * kraw, ones_ref)), 1e-12)

    lo = z_ref[0, :, c0:c0 + 2 * LANES]
    lo = jnp.where(lax.broadcasted_iota(jnp.int32, lo.shape, 1) < LANES, jnp.tanh(lo), lo)
    proj = jnp.dot(lo.astype(jnp.bfloat16), lora_ref[...], preferred_element_type=jnp.float32)
    gd = z_ref[0, :, c0 + 2 * LANES:]
    g_ref[0] = jnp.dot(_sigmoid(gd).astype(jnp.bfloat16), g2_ref[...], preferred_element_type=jnp.float32)

    ksum = None
    for d, (w0, a0, w_ref, k_ref, b_ref) in enumerate(((w0_f, a0_f, wf_ref, kf_ref, bf_ref),
                                                       (w0_b, a0_b, wb_ref, kb_ref, bb_ref))):
        logw = -_softplus(-(w0 + proj[:, d * RWKV_DIM:(d + 1) * RWKV_DIM])) - 0.5
        lr = _sigmoid(a0 + proj[:, (2 + d) * RWKV_DIM:(3 + d) * RWKV_DIM])
        kd = k * (1.0 + (lr - 1.0) * k_a)
        w_ref[0] = jnp.exp(-jnp.exp(logw))
        k_ref[0] = kd
        b_ref[0] = kk * lr
        ksum = kd if ksum is None else ksum + kd
    r_ref[0] = r
    a_ref[0] = -kk
    v_ref[0] = v
    rk_ref[0] = r * ksum * r_k


def _readout_kernel(yf_ref, yb_ref, rk_ref, v_ref, g_ref, vec_ref, ones_ref, o_ref):
    y = yf_ref[0] + yb_ref[0]
    mu = _head_sums(y, ones_ref) * (1.0 / RWKV_HEAD)
    yc = y - mu
    var = _head_sums(yc * yc, ones_ref) * (1.0 / RWKV_HEAD)
    yn = yc * lax.rsqrt(var + RWKV_LNX_EPS) * vec_ref[0:1, :] + vec_ref[1:2, :]
    o_ref[0] = ((yn + _head_sums(rk_ref[0], ones_ref) * v_ref[0]) * g_ref[0]).astype(o_ref.dtype)


def _rwkv7_mixer(z, n_ctx, conv_w, w0_f, w2_f, w0_b, w2_b, a0_f, a2_f, a0_b, a2_b, g2, k_k, k_a,
                 r_k, lnx_g, lnx_b):
    b, length, zin = z.shape
    c0 = 3 * RWKV_DIM
    nblk = length // FEAT_ROWS
    edge = jnp.arange(nblk) * FEAT_ROWS
    zero_row = jnp.zeros((b, 1, c0), jnp.float32)
    last_rows = z[:, FEAT_ROWS - 1::FEAT_ROWS, :c0]
    first_rows = z[:, ::FEAT_ROWS, :c0]
    prev_rows = jnp.concatenate([zero_row, last_rows[:, :-1]], axis=1)
    prev_rows = jnp.where(((edge == 0) | (edge == n_ctx))[None, :, None], 0.0, prev_rows)
    next_rows = jnp.concatenate([first_rows[:, 1:], zero_row], axis=1)
    next_rows = jnp.where(((edge + FEAT_ROWS == n_ctx) | (edge + FEAT_ROWS == length))[None, :, None], 0.0, next_rows)
    zl = jnp.zeros((RWKV_DECAY_LORA, RWKV_DIM), jnp.float32)
    lora = jnp.concatenate([jnp.concatenate([w2_f, zl, zl, zl], axis=1), jnp.concatenate([zl, w2_b, zl, zl], axis=1),
                            jnp.concatenate([zl, zl, a2_f, zl], axis=1), jnp.concatenate([zl, zl, zl, a2_b], axis=1)],
                           axis=0).astype(jnp.bfloat16)
    j = np.arange(RWKV_DIM)
    head_ones = jnp.asarray((j[:, None] // RWKV_HEAD) == (j[None, :] // RWKV_HEAD), jnp.bfloat16)
    vecs = jnp.stack([k_k, k_a, r_k.reshape(-1), w0_f, w0_b, a0_f, a0_b, jnp.zeros_like(k_k)], axis=0)
    blk = pl.BlockSpec((1, FEAT_ROWS, RWKV_DIM), lambda bi, i: (bi, i, 0))
    full = lambda shape: pl.BlockSpec(shape, lambda bi, i: (0,) * len(shape))
    edge_spec = pl.BlockSpec((1, 1, 1, c0), lambda bi, i: (bi, i, 0, 0))
    out_shape = jax.ShapeDtypeStruct((b, length, RWKV_DIM), jnp.float32)
    r, a, v, w_f, k_f, b_f, w_b, k_b, b_b, g, rk = pl.pallas_call(
        _features_kernel,
        grid=(b, nblk),
        in_specs=[pl.BlockSpec((1, FEAT_ROWS, zin), lambda bi, i: (bi, i, 0)), edge_spec, edge_spec,
                  full((3, c0)), full((4 * RWKV_DECAY_LORA, 4 * RWKV_DIM)), full((RWKV_GATE_LORA, RWKV_DIM)),
                  full((8, RWKV_DIM)), full((RWKV_DIM, RWKV_DIM))],
        out_specs=[blk] * 11,
        out_shape=[out_shape] * 11,
        compiler_params=pltpu.CompilerParams(dimension_semantics=("parallel", "parallel"),
                                             vmem_limit_bytes=VMEM_LIMIT_BYTES),
        name="rwkv_features",
    )(z, prev_rows[:, :, None, :], next_rows[:, :, None, :], conv_w, lora, g2.astype(jnp.bfloat16), vecs, head_ones)
    y_f, y_b = rwkv_scan(r, v, a, w_f, k_f, b_f, w_b, k_b, b_b, n_ctx)
    return pl.pallas_call(
        _readout_kernel,
        grid=(b, nblk),
        in_specs=[blk] * 5 + [full((2, RWKV_DIM)), full((RWKV_DIM, RWKV_DIM))],
        out_specs=blk,
        out_shape=jax.ShapeDtypeStruct((b, length, RWKV_DIM), jnp.bfloat16),
        compiler_params=pltpu.CompilerParams(dimension_semantics=("parallel", "parallel"),
                                             vmem_limit_bytes=VMEM_LIMIT_BYTES),
        name="rwkv_readout",
    )(y_f, y_b, rk, v, g, jnp.stack([lnx_g, lnx_b], axis=0), head_ones)


def _rope_halves(t, cos, sin):
    parts = []
    for s in range(2):
        u = t[:, s * LANES:(s + 1) * LANES]
        parts.append(u * cos[:, s * LANES:(s + 1) * LANES]
                     + pltpu.roll(u, LANES // 2, axis=1) * sin[:, s * LANES:(s + 1) * LANES])
    return jnp.concatenate(parts, axis=1)


def _retention_kernel(*refs, has_prev):
    if has_prev:
        (q_ref, k_ref, v_ref, g_ref, kc_ref, vc_ref, cos_ref, sin_ref, dmat_ref, qdec_ref, kdec_ref,
         kcdec_ref, cdec_ref, prev_ref, o_ref, s_ref) = refs
    else:
        (q_ref, k_ref, v_ref, g_ref, kc_ref, vc_ref, cos_ref, sin_ref, dmat_ref, qdec_ref, kdec_ref,
         kcdec_ref, cdec_ref, o_ref, s_ref) = refs
        prev_ref = None
    scale = RET_KEY ** -0.5

    @pl.when(pl.program_id(2) == 0)
    def _():
        kc = (kc_ref[0].astype(jnp.float32) * kcdec_ref[0] * scale).astype(jnp.bfloat16)
        s_ref[...] = lax.dot_general(kc, vc_ref[0].astype(jnp.bfloat16), (((0,), (0,)), ((), ())),
                                     preferred_element_type=jnp.float32)

    cos = cos_ref[...]
    sin = sin_ref[...]
    q = _rope_halves(q_ref[0].astype(jnp.float32), cos, sin)
    k = _rope_halves(k_ref[0].astype(jnp.float32), cos, sin) * scale
    v = v_ref[0].astype(jnp.bfloat16)
    s = s_ref[...]
    att = lax.dot_general(q.astype(jnp.bfloat16), k.astype(jnp.bfloat16), (((1,), (1,)), ((), ())),
                          preferred_element_type=jnp.float32) * dmat_ref[0]
    o = (jnp.dot(att.astype(jnp.bfloat16), v, preferred_element_type=jnp.float32)
         + jnp.dot((q * qdec_ref[0]).astype(jnp.bfloat16), s.astype(jnp.bfloat16),
                   preferred_element_type=jnp.float32))
    s_ref[...] = s * cdec_ref[0] + lax.dot_general((k * kdec_ref[0]).astype(jnp.bfloat16), v,
                                                   (((0,), (0,)), ((), ())),
                                                   preferred_element_type=jnp.float32)
    mu = jnp.mean(o, axis=-1, keepdims=True)
    oc = o - mu
    var = jnp.mean(oc * oc, axis=-1, keepdims=True)
    out = _silu(g_ref[0].astype(jnp.float32)) * (oc * lax.rsqrt(var + 1e-6))
    if prev_ref is not None:
        out = out + prev_ref[0]
    o_ref[0] = out.astype(o_ref.dtype)


def _retention_tables(gamma, reverse, n_ctx):
    log_g = jnp.log(gamma)[:, None, None]
    i = jnp.arange(RET_TC, dtype=jnp.float32)
    rel = (i[None, :] - i[:, None]) if reverse else (i[:, None] - i[None, :])
    dmat = jnp.where(rel >= 0, jnp.exp(jnp.maximum(rel, 0.0)[None] * log_g), 0.0)
    q_pow = (RET_TC - i) if reverse else (i + 1.0)
    k_pow = i if reverse else (RET_TC - 1.0 - i)
    m = jnp.arange(n_ctx, dtype=jnp.float32)
    c_pow = m if reverse else (n_ctx - 1.0 - m)
    bc = lambda p: jnp.broadcast_to(jnp.exp(p[None, :, None] * log_g), (RET_HEADS, p.shape[0], RET_KEY))
    cdec = jnp.broadcast_to(jnp.exp(RET_TC * log_g), (RET_HEADS, 1, RET_VAL))
    return dmat, bc(q_pow), bc(k_pow), bc(c_pow), cdec


def _rope_tables(n_tokens):
    pos_row = (jnp.arange(n_tokens) // GRID_W).astype(jnp.float32)
    pos_col = (jnp.arange(n_tokens) % GRID_W).astype(jnp.float32)
    nf = RET_KEY // 4
    inv_freq = ROPE_BASE ** (-jnp.arange(nf, dtype=jnp.float32) / nf)
    cos, sin = [], []
    for pos in (pos_row, pos_col):
        ang = pos[:, None] * inv_freq[None, :]
        cos += [jnp.cos(ang), jnp.cos(ang)]
        sin += [-jnp.sin(ang), jnp.sin(ang)]
    return jnp.concatenate(cos, axis=1), jnp.concatenate(sin, axis=1)


def _retention_direction(z, zc, cos, sin, gamma, reverse, prev):
    b, t, _ = z.shape
    n_ctx = zc.shape[1]
    nc = t // RET_TC
    dmat, qdec, kdec, kcdec, cdec = _retention_tables(gamma, reverse, n_ctx)
    ch = (lambda c: nc - 1 - c) if reverse else (lambda c: c)
    kq, kv = RET_QK // RET_KEY, (2 * RET_QK) // RET_VAL
    g_off = (2 * RET_QK + (2 if reverse else 1) * RET_VD) // RET_VAL
    in_specs = [
        pl.BlockSpec((1, RET_TC, RET_KEY), lambda bi, h, c: (bi, ch(c), h)),
        pl.BlockSpec((1, RET_TC, RET_KEY), lambda bi, h, c: (bi, ch(c), kq + h)),
        pl.BlockSpec((1, RET_TC, RET_VAL), lambda bi, h, c: (bi, ch(c), kv + h)),
        pl.BlockSpec((1, RET_TC, RET_VAL), lambda bi, h, c: (bi, ch(c), g_off + h)),
        pl.BlockSpec((1, n_ctx, RET_KEY), lambda bi, h, c: (bi, 0, h)),
        pl.BlockSpec((1, n_ctx, RET_VAL), lambda bi, h, c: (bi, 0, RET_QK // RET_VAL + h)),
        pl.BlockSpec((RET_TC, RET_KEY), lambda bi, h, c: (ch(c), 0)),
        pl.BlockSpec((RET_TC, RET_KEY), lambda bi, h, c: (ch(c), 0)),
        pl.BlockSpec((1, RET_TC, RET_TC), lambda bi, h, c: (h, 0, 0)),
        pl.BlockSpec((1, RET_TC, RET_KEY), lambda bi, h, c: (h, 0, 0)),
        pl.BlockSpec((1, RET_TC, RET_KEY), lambda bi, h, c: (h, 0, 0)),
        pl.BlockSpec((1, n_ctx, RET_KEY), lambda bi, h, c: (h, 0, 0)),
        pl.BlockSpec((1, 1, RET_VAL), lambda bi, h, c: (h, 0, 0)),
    ]
    args = [z, z, z, z, zc, zc, cos, sin, dmat, qdec, kdec, kcdec, cdec]
    if prev is not None:
        in_specs.append(pl.BlockSpec((1, RET_TC, RET_VAL), lambda bi, h, c: (bi, ch(c), h)))
        args.append(prev)
    return pl.pallas_call(
        functools.partial(_retention_kernel, has_prev=prev is not None),
        grid=(b, RET_HEADS, nc),
        in_specs=in_specs,
        out_specs=pl.BlockSpec((1, RET_TC, RET_VAL), lambda bi, h, c: (bi, ch(c), h)),
        out_shape=jax.ShapeDtypeStruct((b, t, RET_VD), jnp.float32 if prev is None else jnp.bfloat16),
        scratch_shapes=[pltpu.VMEM((RET_KEY, RET_VAL), jnp.float32)],
        compiler_params=pltpu.CompilerParams(dimension_semantics=("parallel", "parallel", "arbitrary"),
                                             vmem_limit_bytes=VMEM_LIMIT_BYTES),
        name="retention_bwd" if reverse else "retention_fwd",
    )(*args)


def retention_mixer(z, zc, decay_f, decay_b):
    cos, sin = _rope_tables(z.shape[1])
    gamma_f = 1.0 - jnp.exp2(-decay_f)
    gamma_b = 1.0 - jnp.exp2(-decay_b)
    part = _retention_direction(z, zc, cos, sin, gamma_b, True, None)
    return _retention_direction(z, zc, cos, sin, gamma_f, False, part)


def _moe_kernel(te_ref, tv_ref, tok_ref, tok_next_ref, h_ref, wg_ref, wu_ref, wd_ref, o_ref, x_ref, acc_ref, sem):
    i = pl.program_id(0)
    j = pl.program_id(1)
    rows = x_ref.shape[1]
    slot = i % 2

    def gather(toks, buf, wait):
        def one(r, carry):
            cp = pltpu.make_async_copy(h_ref.at[pl.ds(0 if wait else toks[0, 0, r], 1)],
                                       x_ref.at[buf, pl.ds(r, 1)], sem.at[buf])
            cp.wait() if wait else cp.start()
            return carry
        lax.fori_loop(0, rows, one, 0, unroll=MOE_DMA_UNROLL)

    @pl.when(tv_ref[i] > 0)
    def _():
        @pl.when(j == 0)
        def _():
            @pl.when(i == 0)
            def _():
                gather(tok_ref, slot, False)

            gather(tok_ref, slot, True)
            nxt = jnp.minimum(i + 1, pl.num_programs(0) - 1)

            @pl.when(jnp.logical_and(i + 1 < pl.num_programs(0), tv_ref[nxt] > 0))
            def _():
                gather(tok_next_ref, 1 - slot, False)

            acc_ref[...] = jnp.zeros_like(acc_ref)

        x = x_ref[slot].astype(jnp.bfloat16)
        g = jnp.dot(x, wg_ref[0].astype(jnp.bfloat16), preferred_element_type=jnp.float32)
        u = jnp.dot(x, wu_ref[0].astype(jnp.bfloat16), preferred_element_type=jnp.float32)
        a = (_silu(g) * u).astype(jnp.bfloat16)
        acc_ref[...] += jnp.dot(a, wd_ref[0].astype(jnp.bfloat16), preferred_element_type=jnp.float32)

    last = j == pl.num_programs(1) - 1

    @pl.when(jnp.logical_and(last, tv_ref[i] > 0))
    def _():
        o_ref[...] = acc_ref[...]

    @pl.when(jnp.logical_and(last, tv_ref[i] == 0))
    def _():
        o_ref[...] = jnp.zeros_like(o_ref)


def moe_experts(h, slot_tok, tile_expert, tile_valid, w_gate, w_up, w_down):
    n_tiles = slot_tok.shape[0]
    d = h.shape[1]
    ff = w_gate.shape[2]
    grid_spec = pltpu.PrefetchScalarGridSpec(
        num_scalar_prefetch=2,
        grid=(n_tiles, ff // MOE_TF),
        in_specs=[
            pl.BlockSpec((1, 1, MOE_TM), lambda i, j, te, tv: (i, 0, 0), memory_space=pltpu.SMEM),
            pl.BlockSpec((1, 1, MOE_TM), lambda i, j, te, tv: (jnp.minimum(i + 1, n_tiles - 1), 0, 0),
                         memory_space=pltpu.SMEM),
            pl.BlockSpec(memory_space=pl.ANY),
            pl.BlockSpec((1, d, MOE_TF), lambda i, j, te, tv: (te[i], 0, j)),
            pl.BlockSpec((1, d, MOE_TF), lambda i, j, te, tv: (te[i], 0, j)),
            pl.BlockSpec((1, MOE_TF, d), lambda i, j, te, tv: (te[i], j, 0)),
        ],
        out_specs=pl.BlockSpec((MOE_TM, d), lambda i, j, te, tv: (i, 0)),
        scratch_shapes=[pltpu.VMEM((2, MOE_TM, d), jnp.float32), pltpu.VMEM((MOE_TM, d), jnp.float32),
                        pltpu.SemaphoreType.DMA((2,))],
    )
    return pl.pallas_call(
        _moe_kernel,
        grid_spec=grid_spec,
        out_shape=jax.ShapeDtypeStruct((n_tiles * MOE_TM, d), jnp.float32),
        compiler_params=pltpu.CompilerParams(dimension_semantics=("arbitrary", "arbitrary"),
                                             vmem_limit_bytes=VMEM_LIMIT_BYTES),
        name="moe_experts",
    )(tile_expert, tile_valid, slot_tok, slot_tok, h, w_gate, w_up, w_down)


def moe_swiglu(h, router, w_gate, w_up, w_down):
    b, t, d = h.shape
    n = b * t
    hf = h.reshape(n, d)
    logits = jnp.dot(hf, router, precision=lax.Precision.HIGHEST)
    top_val, top_idx = lax.top_k(logits, TOP_K)
    gate = jax.nn.softmax(top_val, axis=-1)
    flat_e = top_idx.reshape(-1)
    onehot = (flat_e[:, None] == jnp.arange(N_EXPERTS, dtype=flat_e.dtype)[None, :]).astype(jnp.int32)
    csum = jnp.cumsum(onehot, axis=0)
    counts = csum[-1]
    rank = jnp.sum((csum - onehot) * onehot, axis=1)
    padded = (counts + MOE_TM - 1) // MOE_TM * MOE_TM
    pad_end = jnp.cumsum(padded)
    pad_start = pad_end - padded
    slot = (pad_start[flat_e] + rank).astype(jnp.int32)
    n_tiles = (n * TOP_K) // MOE_TM + N_EXPERTS
    slots = n_tiles * MOE_TM
    tok = jnp.arange(n * TOP_K, dtype=jnp.int32) // TOP_K
    slot_tok = jnp.zeros((slots,), jnp.int32).at[slot].set(tok)
    tile_start = jnp.arange(n_tiles, dtype=jnp.int32) * MOE_TM
    tile_expert = jnp.minimum(jnp.sum(pad_end[None, :] <= tile_start[:, None], axis=1),
                              N_EXPERTS - 1).astype(jnp.int32)
    tile_valid = (tile_start < pad_end[-1]).astype(jnp.int32)
    yb = moe_experts(hf, slot_tok.reshape(n_tiles, 1, MOE_TM), tile_expert, tile_valid, w_gate, w_up, w_down)
    slot2 = slot.reshape(n, TOP_K)
    out = gate[:, 0:1] * yb[slot2[:, 0]] + gate[:, 1:2] * yb[slot2[:, 1]]
    return out.reshape(b, t, d)


def _row_select(i, tm, n_ctx, ctx_vec, lat_vec):
    if n_ctx == 0:
        return lat_vec
    row = i * tm + lax.broadcasted_iota(jnp.int32, (tm, 1), 0)
    return jnp.where(row < n_ctx, ctx_vec, lat_vec)


def _post_norm_rows(x, update, g, b):
    y = DEEPNORM_ALPHA * x + update
    mu = jnp.mean(y, axis=-1, keepdims=True)
    yc = y - mu
    var = jnp.mean(yc * yc, axis=-1, keepdims=True)
    return yc * lax.rsqrt(var + LN_EPS) * g + b


def _matmul_postnorm_kernel(a_ref, w_ref, x_ref, lat_ref, ctx_ref, g_ref, b_ref, o_ref, *, n_ctx):
    tm = a_ref.shape[1]
    o = jnp.dot(a_ref[0].astype(jnp.bfloat16), w_ref[...], preferred_element_type=jnp.float32)
    gate = _row_select(pl.program_id(1), tm, n_ctx, ctx_ref[...], lat_ref[0])
    o_ref[0] = _post_norm_rows(x_ref[0], gate * o, g_ref[...], b_ref[...])


def matmul_postnorm(a, w, x, gate_lat, gate_ctx, ln_g, ln_b, n_ctx):
    b, length, k = a.shape
    d = w.shape[1]
    tm = _pick_tile(length, (768, 512, 256))
    vec = pl.BlockSpec((1, d), lambda bi, i: (0, 0))
    return pl.pallas_call(
        functools.partial(_matmul_postnorm_kernel, n_ctx=n_ctx),
        grid=(b, length // tm),
        in_specs=[pl.BlockSpec((1, tm, k), lambda bi, i: (bi, i, 0)),
                  pl.BlockSpec((k, d), lambda bi, i: (0, 0)),
                  pl.BlockSpec((1, tm, d), lambda bi, i: (bi, i, 0)),
                  pl.BlockSpec((1, 1, d), lambda bi, i: (bi, 0, 0)), vec, vec, vec],
        out_specs=pl.BlockSpec((1, tm, d), lambda bi, i: (bi, i, 0)),
        out_shape=jax.ShapeDtypeStruct((b, length, d), jnp.float32),
        compiler_params=pltpu.CompilerParams(dimension_semantics=("parallel", "parallel"),
                                             vmem_limit_bytes=VMEM_LIMIT_BYTES),
        name="matmul_postnorm",
    )(a, w.astype(jnp.bfloat16), x, gate_lat[:, None, :], gate_ctx[None, :], ln_g[None, :], ln_b[None, :])


def _ffn_postnorm_kernel(x_ref, lat_ref, ctx_ref, wg_ref, wu_ref, wd_ref, g_ref, b_ref, o_ref, h_ref, acc_ref,
                         *, n_ctx):
    i = pl.program_id(1)
    j = pl.program_id(2)
    tm = x_ref.shape[1]

    def vec(k):
        return _row_select(i, tm, n_ctx, ctx_ref[k:k + 1, :], lat_ref[0, k:k + 1, :])

    @pl.when(j == 0)
    def _():
        h_ref[...] = (x_ref[0] * (1.0 + vec(1)) + vec(0)).astype(jnp.bfloat16)
        acc_ref[...] = jnp.zeros_like(acc_ref)

    h = h_ref[...]
    gt = jnp.dot(h, wg_ref[...], preferred_element_type=jnp.float32)
    up = jnp.dot(h, wu_ref[...], preferred_element_type=jnp.float32)
    acc_ref[...] += jnp.dot((_silu(gt) * up).astype(jnp.bfloat16), wd_ref[...], preferred_element_type=jnp.float32)

    @pl.when(j == pl.num_programs(2) - 1)
    def _():
        o_ref[0] = _post_norm_rows(x_ref[0], vec(2) * acc_ref[...], g_ref[...], b_ref[...])


def ffn_postnorm(x, mod_lat, mod_ctx, w_gate, w_up, w_down, ln_g, ln_b, n_ctx):
    b, length, d = x.shape
    ff = w_gate.shape[1]
    tm = _pick_tile(length, (384, 512, 256))
    tf = _pick_tile(ff, (1408, 512, 256))
    vec = pl.BlockSpec((1, d), lambda bi, i, j: (0, 0))
    return pl.pallas_call(
        functools.partial(_ffn_postnorm_kernel, n_ctx=n_ctx),
        grid=(b, length // tm, ff // tf),
        in_specs=[pl.BlockSpec((1, tm, d), lambda bi, i, j: (bi, i, 0)),
                  pl.BlockSpec((1, 3, d), lambda bi, i, j: (bi, 0, 0)),
                  pl.BlockSpec((3, d), lambda bi, i, j: (0, 0)),
                  pl.BlockSpec((d, tf), lambda bi, i, j: (0, j)),
                  pl.BlockSpec((d, tf), lambda bi, i, j: (0, j)),
                  pl.BlockSpec((tf, d), lambda bi, i, j: (j, 0)), vec, vec],
        out_specs=pl.BlockSpec((1, tm, d), lambda bi, i, j: (bi, i, 0)),
        out_shape=jax.ShapeDtypeStruct((b, length, d), jnp.float32),
        scratch_shapes=[pltpu.VMEM((tm, d), jnp.bfloat16), pltpu.VMEM((tm, d), jnp.float32)],
        compiler_params=pltpu.CompilerParams(dimension_semantics=("parallel", "parallel", "arbitrary"),
                                             vmem_limit_bytes=VMEM_LIMIT_BYTES),
        name="ffn_postnorm",
    )(x, mod_lat, mod_ctx, w_gate.astype(jnp.bfloat16), w_up.astype(jnp.bfloat16), w_down.astype(jnp.bfloat16),
      ln_g[None, :], ln_b[None, :])


def _even_layer(x, xc, sc, scc, mod_w, mod_b, w_in, mla_q_norm, mla_wq_up, mla_kv_norm, mla_wkv_up,
                rwkv_conv, rwkv_w0_f, rwkv_w2_f, rwkv_w0_b, rwkv_w2_b, rwkv_a0_f, rwkv_a2_f, rwkv_a0_b,
                rwkv_a2_b, rwkv_g2, rwkv_k_k, rwkv_k_a, rwkv_r_k, rwkv_lnx_g, rwkv_lnx_b, w_out,
                ln1_g, ln1_b, ffn_w_gate, ffn_w_up, ffn_w_down, ln2_g, ln2_b):
    n_ctx = xc.shape[1]
    m = _adaln(sc, mod_w, mod_b)
    mc = _adaln(scc, mod_w, mod_b)
    xa = jnp.concatenate([xc, x], axis=1)
    is_ctx = (jnp.arange(xa.shape[1]) < n_ctx)[None, :, None]

    def rows(k):
        return jnp.where(is_ctx, mc[k][None, None, :], m[k][:, None, :])

    h = _modulate(xa, rows(0), rows(1)).astype(jnp.bfloat16)
    o_mla = _mla_mixer(h, n_ctx, w_in, mla_q_norm, mla_wq_up, mla_kv_norm, mla_wkv_up)
    o_rwkv = _rwkv7_mixer(mm(h, w_in[:, MLA_IN:]), n_ctx, rwkv_conv, rwkv_w0_f, rwkv_w2_f, rwkv_w0_b, rwkv_w2_b,
                          rwkv_a0_f, rwkv_a2_f, rwkv_a0_b, rwkv_a2_b, rwkv_g2, rwkv_k_k, rwkv_k_a,
                          rwkv_r_k, rwkv_lnx_g, rwkv_lnx_b)
    mix = jnp.concatenate([o_mla.astype(jnp.bfloat16), o_rwkv], axis=-1)
    xa = matmul_postnorm(mix, w_out, xa, m[2], mc[2], ln1_g, ln1_b, n_ctx)
    xa = ffn_postnorm(xa, jnp.stack(m[3:6], axis=1), jnp.stack(mc[3:6], axis=0),
                      ffn_w_gate, ffn_w_up, ffn_w_down, ln2_g, ln2_b, n_ctx)
    return xa[:, n_ctx:], xa[:, :n_ctx]


def _odd_layer(x, xc, sc, scc, mod_w, mod_b, w_in, ret_decay_f, ret_decay_b, w_out, ln1_g, ln1_b,
               router, moe_w_gate, moe_w_up, moe_w_down, ln2_g, ln2_b):
    m = _adaln(sc, mod_w, mod_b)
    c_shift, c_scale = jnp.split(
        jnp.dot(scc, mod_w[:, :2 * D_MODEL], precision=lax.Precision.HIGHEST) + mod_b[:2 * D_MODEL], 2)
    z = mm(_modulate(x, m[0][:, None, :], m[1][:, None, :]), w_in, jnp.bfloat16)
    zc = mm(_modulate(xc, c_shift, c_scale), w_in[:, RET_QK:2 * RET_QK + RET_VD], jnp.bfloat16)
    mix = retention_mixer(z, zc, ret_decay_f, ret_decay_b)
    x = matmul_postnorm(mix, w_out, x, m[2], jnp.zeros_like(m[2][0]), ln1_g, ln1_b, 0)
    y = moe_swiglu(_modulate(x, m[3][:, None, :], m[4][:, None, :]), router, moe_w_gate, moe_w_up, moe_w_down)
    return _post_norm(x, m[5][:, None, :] * y, ln2_g, ln2_b)


def kernel(x, c, ctx, c_ctx, l0_mod_w, l0_mod_b, l0_w_in, l0_mla_q_norm, l0_mla_wq_up, l0_mla_kv_norm, l0_mla_wkv_up, l0_rwkv_conv, l0_rwkv_w0_f, l0_rwkv_w2_f, l0_rwkv_w0_b, l0_rwkv_w2_b, l0_rwkv_a0_f, l0_rwkv_a2_f, l0_rwkv_a0_b, l0_rwkv_a2_b, l0_rwkv_g2, l0_rwkv_k_k, l0_rwkv_k_a, l0_rwkv_r_k, l0_rwkv_lnx_g, l0_rwkv_lnx_b, l0_w_out, l0_ln1_g, l0_ln1_b, l0_ffn_w_gate, l0_ffn_w_up, l0_ffn_w_down, l0_ln2_g, l0_ln2_b, l1_mod_w, l1_mod_b, l1_w_in, l1_ret_decay_f, l1_ret_decay_b, l1_w_out, l1_ln1_g, l1_ln1_b, l1_router, l1_moe_w_gate, l1_moe_w_up, l1_moe_w_down, l1_ln2_g, l1_ln2_b):
    even_params = (l0_mod_w, l0_mod_b, l0_w_in, l0_mla_q_norm, l0_mla_wq_up, l0_mla_kv_norm, l0_mla_wkv_up,
                   l0_rwkv_conv, l0_rwkv_w0_f, l0_rwkv_w2_f, l0_rwkv_w0_b, l0_rwkv_w2_b, l0_rwkv_a0_f,
                   l0_rwkv_a2_f, l0_rwkv_a0_b, l0_rwkv_a2_b, l0_rwkv_g2, l0_rwkv_k_k, l0_rwkv_k_a, l0_rwkv_r_k,
                   l0_rwkv_lnx_g, l0_rwkv_lnx_b, l0_w_out, l0_ln1_g, l0_ln1_b, l0_ffn_w_gate, l0_ffn_w_up,
                   l0_ffn_w_down, l0_ln2_g, l0_ln2_b)
    odd_params = (l1_mod_w, l1_mod_b, l1_w_in, l1_ret_decay_f, l1_ret_decay_b, l1_w_out, l1_ln1_g, l1_ln1_b,
                  l1_router, l1_moe_w_gate, l1_moe_w_up, l1_moe_w_down, l1_ln2_g, l1_ln2_b)
    sc = _silu(c)
    scc = _silu(c_ctx)
    x, xc = _even_layer(x, ctx, sc, scc, *even_params)
    return _odd_layer(x, xc, sc, scc, *odd_params)
```

```python
import functools

import jax
import jax.numpy as jnp
import numpy as np
from jax import lax
from jax.experimental import pallas as pl
from jax.experimental.pallas import tpu as pltpu

D_MODEL = 1024
DEPTH = 2
GRID_W = 64
ROPE_BASE = 10000.0
DEEPNORM_ALPHA = (2 * DEPTH) ** 0.25
LN_EPS = 1e-5

MLA_HEADS = 8
MLA_Q_RANK = 256
MLA_KV_RANK = 128
MLA_NOPE = 64
MLA_ROPE = 32
MLA_V = 64
MLA_IN = MLA_Q_RANK + MLA_KV_RANK + MLA_ROPE
MLA_OUT = MLA_HEADS * MLA_V

RWKV_HEADS = 8
RWKV_HEAD = 64
RWKV_DIM = RWKV_HEADS * RWKV_HEAD
RWKV_DECAY_LORA = 64
RWKV_AAA_LORA = 64
RWKV_GATE_LORA = 128
RWKV_LNX_EPS = 64e-5

RET_HEADS = 4
RET_KEY = 256
RET_VAL = 512
RET_TC = 512
RET_QK = RET_HEADS * RET_KEY
RET_VD = RET_HEADS * RET_VAL

N_EXPERTS = 8
TOP_K = 2
MOE_TM = 1024
MOE_TF = 512
MOE_DMA_UNROLL = 8

LANES = 128
VMEM_LIMIT_BYTES = 48 * 1024 * 1024


def _matmul_kernel(x_ref, w_ref, o_ref):
    o_ref[...] = jnp.dot(x_ref[...].astype(jnp.bfloat16), w_ref[...],
                         preferred_element_type=jnp.float32).astype(o_ref.dtype)


def _pick_tile(n, candidates):
    for c in candidates:
        if n % c == 0:
            return c
    raise ValueError(f"no tile for {n}")


def pmatmul(x, w, out_dtype=jnp.float32):
    m, k = x.shape
    n = w.shape[1]
    n_pad = (-n) % LANES
    w = w.astype(jnp.bfloat16)
    if n_pad:
        w = jnp.pad(w, ((0, 0), (0, n_pad)))
    np_ = n + n_pad
    tm = _pick_tile(m, (1024, 512, 256, 128, 8) if k <= 1024 else (512, 256, 128, 8))
    tn = _pick_tile(np_, (1024, 768, 640, 512, 384, 256, 128))
    out = pl.pallas_call(
        _matmul_kernel,
        grid=(m // tm, np_ // tn),
        in_specs=[pl.BlockSpec((tm, k), lambda i, j: (i, 0)),
                  pl.BlockSpec((k, tn), lambda i, j: (0, j))],
        out_specs=pl.BlockSpec((tm, tn), lambda i, j: (i, j)),
        out_shape=jax.ShapeDtypeStruct((m, np_), out_dtype),
        compiler_params=pltpu.CompilerParams(
            dimension_semantics=("parallel", "parallel"),
            vmem_limit_bytes=VMEM_LIMIT_BYTES),
        name="matmul",
    )(x, w)
    return out[:, :n] if n_pad else out


def mm(x, w, out_dtype=jnp.float32):
    lead = x.shape[:-1]
    out = pmatmul(x.reshape(-1, x.shape[-1]).astype(jnp.bfloat16), w, out_dtype)
    return out.reshape(*lead, w.shape[1])


def _silu(t):
    return t * jax.nn.sigmoid(t)


def _normalize(t, eps):
    mu = jnp.mean(t, axis=-1, keepdims=True)
    var = jnp.mean(jnp.square(t - mu), axis=-1, keepdims=True)
    return (t - mu) * lax.rsqrt(var + eps)


def _layer_norm(t, g, b):
    return _normalize(t, LN_EPS) * g + b


def _rms_norm(t, g, eps=1e-6):
    return t * lax.rsqrt(jnp.mean(t * t, axis=-1, keepdims=True) + eps) * g


def _post_norm(x, update, g, b):
    return _layer_norm(DEEPNORM_ALPHA * x + update, g, b)


def _modulate(h, shift, scale):
    return h * (1.0 + scale) + shift


def _adaln(cond, mod_w, mod_b):
    return jnp.split(jnp.dot(cond, mod_w, precision=lax.Precision.HIGHEST) + mod_b, 6, axis=-1)


def _grid_positions(n_tokens):
    rows = n_tokens // GRID_W
    row = jnp.repeat(jnp.arange(rows, dtype=jnp.float32), GRID_W)
    col = jnp.tile(jnp.arange(GRID_W, dtype=jnp.float32), rows)
    return row, col


SOFTMAX_FLOOR = -1e30


FLASH_ROW_GROUPS = 4


def _flash_kernel(q_ref, kt_ref, v_ref, o_ref, m_ref, acc_ref, *, dv):
    j = pl.program_id(3)

    @pl.when(j == 0)
    def _():
        m_ref[...] = jnp.full_like(m_ref, SOFTMAX_FLOOR)
        acc_ref[...] = jnp.zeros_like(acc_ref)

    rows = q_ref.shape[2] // FLASH_ROW_GROUPS
    kt = kt_ref[0, 0]
    v = v_ref[0, 0]
    for u in range(FLASH_ROW_GROUPS):
        sl = slice(u * rows, (u + 1) * rows)
        s = jnp.dot(q_ref[0, 0, sl, :], kt, preferred_element_type=jnp.float32)
        m_prev = m_ref[sl, :]
        m_new = jnp.maximum(m_prev, jnp.max(s, axis=-1, keepdims=True))
        alpha = jnp.exp(m_prev - m_new)
        p = jnp.exp(s - m_new).astype(jnp.bfloat16)
        acc_ref[sl, :] = alpha * acc_ref[sl, :] + jnp.dot(p, v, preferred_element_type=jnp.float32)
        m_ref[sl, :] = m_new

    @pl.when(j == pl.num_programs(3) - 1)
    def _():
        acc = acc_ref[...]
        o_ref[0, 0] = acc[:, :dv] / acc[:, dv:dv + 1]


def flash_attention(q, kt, v1, dv, tq, tk):
    b, h, t, dq = q.shape
    s = kt.shape[3]
    return pl.pallas_call(
        functools.partial(_flash_kernel, dv=dv),
        grid=(b, h, t // tq, s // tk),
        in_specs=[pl.BlockSpec((1, 1, tq, dq), lambda bi, hi, i, j: (bi, hi, i, 0)),
                  pl.BlockSpec((1, 1, dq, tk), lambda bi, hi, i, j: (bi, hi, 0, j)),
                  pl.BlockSpec((1, 1, tk, LANES), lambda bi, hi, i, j: (bi, hi, j, 0))],
        out_specs=pl.BlockSpec((1, 1, tq, dv), lambda bi, hi, i, j: (bi, hi, i, 0)),
        out_shape=jax.ShapeDtypeStruct((b, h, t, dv), jnp.float32),
        scratch_shapes=[pltpu.VMEM((tq, 1), jnp.float32), pltpu.VMEM((tq, LANES), jnp.float32)],
        compiler_params=pltpu.CompilerParams(
            dimension_semantics=("parallel", "parallel", "parallel", "arbitrary"),
            vmem_limit_bytes=VMEM_LIMIT_BYTES),
        name="flash_attention",
    )(q, kt, v1)


def _block_attention(q, k, v):
    dq = q.shape[-1]
    dv = v.shape[-1]
    b, s, h, _ = k.shape
    qh = jnp.transpose(q * dq ** -0.5, (0, 2, 1, 3)).astype(jnp.bfloat16)
    kt = jnp.transpose(k, (0, 2, 3, 1)).astype(jnp.bfloat16)
    v1 = jnp.concatenate([v, jnp.ones((b, s, h, 1), v.dtype), jnp.zeros((b, s, h, LANES - dv - 1), v.dtype)],
                         axis=-1)
    v1 = jnp.transpose(v1, (0, 2, 1, 3)).astype(jnp.bfloat16)
    tq = _pick_tile(q.shape[1], (1024, 256))
    o = flash_attention(qh, kt, v1, dv, tq, s)
    return jnp.transpose(o, (0, 2, 1, 3))


_Q8 = MLA_ROPE // 4
ROPE_PARTNER = np.concatenate([np.arange(_Q8, 2 * _Q8), np.arange(0, _Q8),
                               np.arange(3 * _Q8, 4 * _Q8), np.arange(2 * _Q8, 3 * _Q8)])
ROPE_SIGN = np.concatenate([-np.ones(_Q8), np.ones(_Q8), -np.ones(_Q8), np.ones(_Q8)]).astype(np.float32)


def _mla_rope_tables(n_ctx, t):
    row, col = _grid_positions(t)
    zero = jnp.zeros((n_ctx,), jnp.float32)
    row, col = jnp.concatenate([zero, row]), jnp.concatenate([zero, col])
    inv_freq = ROPE_BASE ** (-jnp.arange(_Q8, dtype=jnp.float32) / _Q8)
    ang = jnp.concatenate([row[:, None] * inv_freq] * 2 + [col[:, None] * inv_freq] * 2, axis=1)
    return jnp.cos(ang), jnp.sin(ang)


def _mla_mixer(h, n_ctx, w_in, q_norm, wq_up, kv_norm, wkv_up):
    b, length, _ = h.shape
    cos, sin = _mla_rope_tables(n_ctx, length - n_ctx)
    k_cols = np.arange(MLA_Q_RANK + MLA_KV_RANK, MLA_IN)
    w_z = jnp.concatenate([w_in[:, :MLA_IN], w_in[:, k_cols[ROPE_PARTNER]] * ROPE_SIGN], axis=1)
    z = mm(h, w_z)
    cq, ckv = z[..., :MLA_Q_RANK], z[..., MLA_Q_RANK:MLA_Q_RANK + MLA_KV_RANK]
    k_rope = z[..., k_cols[0]:MLA_IN] * cos + z[..., MLA_IN:] * sin
    dq = MLA_NOPE + MLA_ROPE
    head_cols = np.arange(MLA_HEADS)[:, None] * dq
    q_rope_cols = (head_cols + MLA_NOPE + np.arange(MLA_ROPE)[None, :])
    wq_partner = jnp.zeros_like(wq_up).at[:, q_rope_cols.reshape(-1)].set(
        wq_up[:, (head_cols + MLA_NOPE + ROPE_PARTNER[None, :]).reshape(-1)] * jnp.tile(ROPE_SIGN, MLA_HEADS))
    q2 = mm(_rms_norm(cq, q_norm), jnp.concatenate([wq_up, wq_partner], axis=1))
    one, nil = jnp.ones((length, MLA_NOPE), jnp.float32), jnp.zeros((length, MLA_NOPE), jnp.float32)
    cos_q = jnp.tile(jnp.concatenate([one, cos], axis=1), (1, MLA_HEADS))
    sin_q = jnp.tile(jnp.concatenate([nil, sin], axis=1), (1, MLA_HEADS))
    q = (q2[..., :MLA_HEADS * dq] * cos_q + q2[..., MLA_HEADS * dq:] * sin_q).reshape(b, length, MLA_HEADS, dq)
    kv = mm(_rms_norm(ckv, kv_norm), wkv_up).reshape(b, length, MLA_HEADS, MLA_NOPE + MLA_V)
    k = jnp.concatenate([kv[..., :MLA_NOPE],
                         jnp.broadcast_to(k_rope[:, :, None, :], (b, length, MLA_HEADS, MLA_ROPE))], axis=-1)
    v = kv[..., MLA_NOPE:]
    o = _block_attention(q[:, n_ctx:], k, v)
    oc = _block_attention(q[:, :n_ctx], k[:, :n_ctx], v[:, :n_ctx])
    return jnp.concatenate([oc, o], axis=1).reshape(b, length, MLA_OUT)


SCAN_BLOCK = 128
SCAN_GROUP = 16
N_PAIRS = RWKV_HEADS // 2
PAIR_ROWS = N_PAIRS * RWKV_HEAD


def _to_state_tiles(vblk):
    halves = [[], []]
    for p in range(N_PAIRS):
        tr = vblk[:, p * LANES:(p + 1) * LANES].T
        for half in range(2):
            c = slice(half * RWKV_HEAD, (half + 1) * RWKV_HEAD)
            halves[half].append(jnp.concatenate([tr[:RWKV_HEAD, c], tr[RWKV_HEAD:, c]], axis=1))
    return [jnp.concatenate(h, axis=0) for h in halves]


def _from_state_tiles(tiles):
    cols = []
    for p in range(N_PAIRS):
        r = slice(p * RWKV_HEAD, (p + 1) * RWKV_HEAD)
        tr = jnp.concatenate([tiles[0][r, :], tiles[1][r, :]], axis=0).T
        cols.append(jnp.concatenate(
            [jnp.concatenate([tr[:RWKV_HEAD, :RWKV_HEAD], tr[:RWKV_HEAD, RWKV_HEAD:]], axis=0),
             jnp.concatenate([tr[RWKV_HEAD:, :RWKV_HEAD], tr[RWKV_HEAD:, RWKV_HEAD:]], axis=0)], axis=1))
    return jnp.concatenate(cols, axis=1)


def _scan_kernel(rf_ref, rb_ref, af_ref, ab_ref, vf_ref, vb_ref, wf_ref, wb_ref, kf_ref, kb_ref,
                 bf_ref, bb_ref, wred_ref, yf_ref, yb_ref, s_ref, vt_ref, yc_ref):
    nb = rf_ref.shape[0]

    @pl.when(pl.program_id(0) == 0)
    def _():
        s_ref[...] = jnp.zeros_like(s_ref)

    lane = lax.broadcasted_iota(jnp.int32, (PAIR_ROWS, LANES), 1)
    wred = wred_ref[...]
    refs = ((rf_ref, af_ref, vf_ref, wf_ref, kf_ref, bf_ref, yf_ref),
            (rb_ref, ab_ref, vb_ref, wb_ref, kb_ref, bb_ref, yb_ref))

    for d in range(2):
        for bi in range(nb):
            tiles = _to_state_tiles(refs[d][2][bi])
            for half in range(2):
                vt_ref[d, bi, half] = tiles[half].astype(jnp.bfloat16)

    def rows(ref, bi, base, u):
        return jnp.concatenate(
            [jnp.broadcast_to(ref.at[bi, pl.ds(base, SCAN_GROUP), :][u:u + 1, p * LANES:(p + 1) * LANES],
                              (RWKV_HEAD, LANES)) for p in range(N_PAIRS)], axis=0)

    def collect(d, bi, t, yb, valid):
        mask = jnp.logical_and(lane % RWKV_HEAD == t % RWKV_HEAD, valid)
        half = t // RWKV_HEAD
        yc_ref[d, bi, half] = jnp.where(mask, yb, yc_ref[d, bi, half])

    n_groups = SCAN_BLOCK // SCAN_GROUP

    def group(gi, carry):
        for u in range(SCAN_GROUP):
            for d in range(2):
                r_ref, a_ref, _, w_ref, k_ref, b_ref, _ = refs[d]
                g = gi if d == 0 else n_groups - 1 - gi
                base = pl.multiple_of(g * SCAN_GROUP, SCAN_GROUP)
                uu = u if d == 0 else SCAN_GROUP - 1 - u
                t = base + uu
                if u == 0:
                    gp = jnp.maximum(gi - 1, 0) if d == 0 else jnp.minimum(n_groups - gi, n_groups - 1)
                    pbase, pu = pl.multiple_of(gp * SCAN_GROUP, SCAN_GROUP), (SCAN_GROUP - 1 if d == 0 else 0)
                else:
                    pbase, pu = base, (uu - 1 if d == 0 else uu + 1)
                tp = pbase + pu
                sel = lane % RWKV_HEAD == t % RWKV_HEAD
                prs = []
                for bi in range(nb):
                    s = s_ref[d, bi]
                    pa = (s * rows(a_ref, bi, base, uu)).astype(jnp.bfloat16)
                    prs.append((s * rows(r_ref, bi, pbase, pu)).astype(jnp.bfloat16))
                    pv = jnp.where(sel, vt_ref[d, bi, t // RWKV_HEAD], jnp.zeros((), jnp.bfloat16))
                    red = jnp.dot(jnp.concatenate([pa, pv], axis=1), wred, preferred_element_type=jnp.float32)
                    s_ref[d, bi] = (s * rows(w_ref, bi, base, uu) + red[:, :LANES] * rows(b_ref, bi, base, uu)
                                    + red[:, LANES:] * rows(k_ref, bi, base, uu))
                valid = True if u > 0 else gi >= 1
                for b0 in range(0, nb, 2):
                    ys = jnp.dot(jnp.concatenate(prs[b0:b0 + 2], axis=1), wred, preferred_element_type=jnp.float32)
                    collect(d, b0, tp, ys[:, :LANES], valid)
                    collect(d, b0 + 1, tp, ys[:, LANES:], valid)
        return carry

    lax.fori_loop(0, n_groups, group, 0)

    for d in range(2):
        r_ref, y_ref = refs[d][0], refs[d][6]
        t_last = SCAN_BLOCK - 1 if d == 0 else 0
        for b0 in range(0, nb, 2):
            prs = [(s_ref[d, bi] * rows(r_ref, bi, t_last - t_last % SCAN_GROUP, t_last % SCAN_GROUP)).astype(jnp.bfloat16)
                   for bi in (b0, b0 + 1)]
            ys = jnp.dot(jnp.concatenate(prs, axis=1), wred, preferred_element_type=jnp.float32)
            collect(d, b0, t_last, ys[:, :LANES], True)
            collect(d, b0 + 1, t_last, ys[:, LANES:], True)
        for bi in range(nb):
            y_ref[bi] = _from_state_tiles([yc_ref[d, bi, 0], yc_ref[d, bi, 1]])


def rwkv_scan(r, v, a, w_f, k_f, b_f, w_b, k_b, b_b, n_ctx):
    nb, length, _ = r.shape
    nblk = length // SCAN_BLOCK
    nblk_ctx = n_ctx // SCAN_BLOCK
    j = np.arange(2 * LANES)
    wred = (j[:, None] // RWKV_HEAD) == (j[None, :] // RWKV_HEAD)

    def fwd(i):
        return i

    def bwd(i):
        return jnp.where(i < nblk_ctx, nblk_ctx - 1 - i, nblk + nblk_ctx - 1 - i)

    def row_spec(blk):
        return pl.BlockSpec((nb, SCAN_BLOCK, RWKV_DIM), lambda i: (0, blk(i), 0))

    y_shape = jax.ShapeDtypeStruct((nb, length, RWKV_DIM), jnp.float32)
    tile_shape = (2, nb, 2, PAIR_ROWS, LANES)
    return pl.pallas_call(
        _scan_kernel,
        grid=(nblk,),
        in_specs=[row_spec(fwd), row_spec(bwd)] * 6 + [pl.BlockSpec((2 * LANES, 2 * LANES), lambda i: (0, 0))],
        out_specs=[row_spec(fwd), row_spec(bwd)],
        out_shape=[y_shape, y_shape],
        scratch_shapes=[pltpu.VMEM((2, nb, PAIR_ROWS, LANES), jnp.float32),
                        pltpu.VMEM(tile_shape, jnp.bfloat16), pltpu.VMEM(tile_shape, jnp.float32)],
        compiler_params=pltpu.CompilerParams(dimension_semantics=("arbitrary",),
                                             vmem_limit_bytes=VMEM_LIMIT_BYTES),
        name="rwkv_scan",
    )(r, r, a, a, v, v, w_f, w_b, k_f, k_b, b_f, b_b, jnp.asarray(wred, jnp.bfloat16))


FEAT_ROWS = 256


def _head_sums(x, ones_ref):
    hi = x.astype(jnp.bfloat16)
    lo = (x - hi.astype(jnp.float32)).astype(jnp.bfloat16)
    ones = ones_ref[...]
    return (jnp.dot(hi, ones, preferred_element_type=jnp.float32)
            + jnp.dot(lo, ones, preferred_element_type=jnp.float32))


def _softplus(x):
    return jnp.maximum(x, 0.0) + jnp.log(1.0 + jnp.exp(-jnp.abs(x)))


def _sigmoid(x):
    return 1.0 / (1.0 + jnp.exp(-x))


def _features_kernel(z_ref, prev_ref, next_ref, conv_ref, lora_ref, g2_ref, vec_ref, ones_ref,
                     r_ref, a_ref, v_ref, wf_ref, kf_ref, bf_ref, wb_ref, kb_ref, bb_ref, g_ref, rk_ref):
    rows = z_ref.shape[1]
    c0 = 3 * RWKV_DIM
    raw = z_ref[0, :, :c0]
    row_id = lax.broadcasted_iota(jnp.int32, (rows, 1), 0)
    before = jnp.where(row_id == 0, prev_ref[0, 0], pltpu.roll(raw, 1, axis=0))
    after = jnp.where(row_id == rows - 1, next_ref[0, 0], pltpu.roll(raw, rows - 1, axis=0))
    rkv = before * conv_ref[0:1, :] + raw * conv_ref[1:2, :] + after * conv_ref[2:3, :]
    r, k, v = rkv[:, :RWKV_DIM], rkv[:, RWKV_DIM:2 * RWKV_DIM], rkv[:, 2 * RWKV_DIM:]

    def vec(i):
        return vec_ref[i:i + 1, :]

    k_k, k_a, r_k, w0_f, w0_b, a0_f, a0_b = (vec(i) for i in range(7))
    kraw = k * k_k
    kk = kraw / jnp.maximum(jnp.sqrt(_head_sums(kraw * kraw, ones_ref)), 1e-12)

    lo = z_ref[0, :, c0:c0 + 2 * LANES]
    lo = jnp.where(lax.broadcasted_iota(jnp.int32, lo.shape, 1) < LANES, jnp.tanh(lo), lo)
    proj = jnp.dot(lo.astype(jnp.bfloat16), lora_ref[...], preferred_element_type=jnp.float32)
    gd = z_ref[0, :, c0 + 2 * LANES:]
    g_ref[0] = jnp.dot(_sigmoid(gd).astype(jnp.bfloat16), g2_ref[...], preferred_element_type=jnp.float32)

    ksum = None
    for d, (w0, a0, w_ref, k_ref, b_ref) in enumerate(((w0_f, a0_f, wf_ref, kf_ref, bf_ref),
                                                       (w0_b, a0_b, wb_ref, kb_ref, bb_ref))):
        logw = -_softplus(-(w0 + proj[:, d * RWKV_DIM:(d + 1) * RWKV_DIM])) - 0.5
        lr = _sigmoid(a0 + proj[:, (2 + d) * RWKV_DIM:(3 + d) * RWKV_DIM])
        kd = k * (1.0 + (lr - 1.0) * k_a)
        w_ref[0] = jnp.exp(-jnp.exp(logw))
        k_ref[0] = kd
        b_ref[0] = kk * lr
        ksum = kd if ksum is None else ksum + kd
    r_ref[0] = r
    a_ref[0] = -kk
    v_ref[0] = v
    rk_ref[0] = r * ksum * r_k


def _readout_kernel(yf_ref, yb_ref, rk_ref, v_ref, g_ref, vec_ref, ones_ref, o_ref):
    y = yf_ref[0] + yb_ref[0]
    mu = _head_sums(y, ones_ref) * (1.0 / RWKV_HEAD)
    yc = y - mu
    var = _head_sums(yc * yc, ones_ref) * (1.0 / RWKV_HEAD)
    yn = yc * lax.rsqrt(var + RWKV_LNX_EPS) * vec_ref[0:1, :] + vec_ref[1:2, :]
    o_ref[0] = ((yn + _head_sums(rk_ref[0], ones_ref) * v_ref[0]) * g_ref[0]).astype(o_ref.dtype)


def _rwkv7_mixer(z, n_ctx, conv_w, w0_f, w2_f, w0_b, w2_b, a0_f, a2_f, a0_b, a2_b, g2, k_k, k_a,
                 r_k, lnx_g, lnx_b):
    b, length, zin = z.shape
    c0 = 3 * RWKV_DIM
    nblk = length // FEAT_ROWS
    edge = jnp.arange(nblk) * FEAT_ROWS
    zero_row = jnp.zeros((b, 1, c0), jnp.float32)
    last_rows = z[:, FEAT_ROWS - 1::FEAT_ROWS, :c0]
    first_rows = z[:, ::FEAT_ROWS, :c0]
    prev_rows = jnp.concatenate([zero_row, last_rows[:, :-1]], axis=1)
    prev_rows = jnp.where(((edge == 0) | (edge == n_ctx))[None, :, None], 0.0, prev_rows)
    next_rows = jnp.concatenate([first_rows[:, 1:], zero_row], axis=1)
    next_rows = jnp.where(((edge + FEAT_ROWS == n_ctx) | (edge + FEAT_ROWS == length))[None, :, None], 0.0, next_rows)
    zl = jnp.zeros((RWKV_DECAY_LORA, RWKV_DIM), jnp.float32)
    lora = jnp.concatenate([jnp.concatenate([w2_f, zl, zl, zl], axis=1), jnp.concatenate([zl, w2_b, zl, zl], axis=1),
                            jnp.concatenate([zl, zl, a2_f, zl], axis=1), jnp.concatenate([zl, zl, zl, a2_b], axis=1)],
                           axis=0).astype(jnp.bfloat16)
    j = np.arange(RWKV_DIM)
    head_ones = jnp.asarray((j[:, None] // RWKV_HEAD) == (j[None, :] // RWKV_HEAD), jnp.bfloat16)
    vecs = jnp.stack([k_k, k_a, r_k.reshape(-1), w0_f, w0_b, a0_f, a0_b, jnp.zeros_like(k_k)], axis=0)
    blk = pl.BlockSpec((1, FEAT_ROWS, RWKV_DIM), lambda bi, i: (bi, i, 0))
    full = lambda shape: pl.BlockSpec(shape, lambda bi, i: (0,) * len(shape))
    edge_spec = pl.BlockSpec((1, 1, 1, c0), lambda bi, i: (bi, i, 0, 0))
    out_shape = jax.ShapeDtypeStruct((b, length, RWKV_DIM), jnp.float32)
    r, a, v, w_f, k_f, b_f, w_b, k_b, b_b, g, rk = pl.pallas_call(
        _features_kernel,
        grid=(b, nblk),
        in_specs=[pl.BlockSpec((1, FEAT_ROWS, zin), lambda bi, i: (bi, i, 0)), edge_spec, edge_spec,
                  full((3, c0)), full((4 * RWKV_DECAY_LORA, 4 * RWKV_DIM)), full((RWKV_GATE_LORA, RWKV_DIM)),
                  full((8, RWKV_DIM)), full((RWKV_DIM, RWKV_DIM))],
        out_specs=[blk] * 11,
        out_shape=[out_shape] * 11,
        compiler_params=pltpu.CompilerParams(dimension_semantics=("parallel", "parallel"),
                                             vmem_limit_bytes=VMEM_LIMIT_BYTES),
        name="rwkv_features",
    )(z, prev_rows[:, :, None, :], next_rows[:, :, None, :], conv_w, lora, g2.astype(jnp.bfloat16), vecs, head_ones)
    y_f, y_b = rwkv_scan(r, v, a, w_f, k_f, b_f, w_b, k_b, b_b, n_ctx)
    return pl.pallas_call(
        _readout_kernel,
        grid=(b, nblk),
        in_specs=[blk] * 5 + [full((2, RWKV_DIM)), full((RWKV_DIM, RWKV_DIM))],
        out_specs=blk,
        out_shape=jax.ShapeDtypeStruct((b, length, RWKV_DIM), jnp.bfloat16),
        compiler_params=pltpu.CompilerParams(dimension_semantics=("parallel", "parallel"),
                                             vmem_limit_bytes=VMEM_LIMIT_BYTES),
        name="rwkv_readout",
    )(y_f, y_b, rk, v, g, jnp.stack([lnx_g, lnx_b], axis=0), head_ones)


def _rope_halves(t, cos, sin):
    parts = []
    for s in range(2):
        u = t[:, s * LANES:(s + 1) * LANES]
        parts.append(u * cos[:, s * LANES:(s + 1) * LANES]
                     + pltpu.roll(u, LANES // 2, axis=1) * sin[:, s * LANES:(s + 1) * LANES])
    return jnp.concatenate(parts, axis=1)


def _retention_kernel(*refs, has_prev):
    if has_prev:
        (q_ref, k_ref, v_ref, g_ref, kc_ref, vc_ref, cos_ref, sin_ref, dmat_ref, qdec_ref, kdec_ref,
         kcdec_ref, cdec_ref, prev_ref, o_ref, s_ref) = refs
    else:
        (q_ref, k_ref, v_ref, g_ref, kc_ref, vc_ref, cos_ref, sin_ref, dmat_ref, qdec_ref, kdec_ref,
         kcdec_ref, cdec_ref, o_ref, s_ref) = refs
        prev_ref = None
    scale = RET_KEY ** -0.5

    @pl.when(pl.program_id(2) == 0)
    def _():
        kc = (kc_ref[0].astype(jnp.float32) * kcdec_ref[0] * scale).astype(jnp.bfloat16)
        s_ref[...] = lax.dot_general(kc, vc_ref[0].astype(jnp.bfloat16), (((0,), (0,)), ((), ())),
                                     preferred_element_type=jnp.float32)

    cos = cos_ref[...]
    sin = sin_ref[...]
    q = _rope_halves(q_ref[0].astype(jnp.float32), cos, sin)
    k = _rope_halves(k_ref[0].astype(jnp.float32), cos, sin) * scale
    v = v_ref[0].astype(jnp.bfloat16)
    s = s_ref[...]
    att = lax.dot_general(q.astype(jnp.bfloat16), k.astype(jnp.bfloat16), (((1,), (1,)), ((), ())),
                          preferred_element_type=jnp.float32) * dmat_ref[0]
    o = (jnp.dot(att.astype(jnp.bfloat16), v, preferred_element_type=jnp.float32)
         + jnp.dot((q * qdec_ref[0]).astype(jnp.bfloat16), s.astype(jnp.bfloat16),
                   preferred_element_type=jnp.float32))
    s_ref[...] = s * cdec_ref[0] + lax.dot_general((k * kdec_ref[0]).astype(jnp.bfloat16), v,
                                                   (((0,), (0,)), ((), ())),
                                                   preferred_element_type=jnp.float32)
    mu = jnp.mean(o, axis=-1, keepdims=True)
    oc = o - mu
    var = jnp.mean(oc * oc, axis=-1, keepdims=True)
    out = _silu(g_ref[0].astype(jnp.float32)) * (oc * lax.rsqrt(var + 1e-6))
    if prev_ref is not None:
        out = out + prev_ref[0]
    o_ref[0] = out.astype(o_ref.dtype)


def _retention_tables(gamma, reverse, n_ctx):
    log_g = jnp.log(gamma)[:, None, None]
    i = jnp.arange(RET_TC, dtype=jnp.float32)
    rel = (i[None, :] - i[:, None]) if reverse else (i[:, None] - i[None, :])
    dmat = jnp.where(rel >= 0, jnp.exp(jnp.maximum(rel, 0.0)[None] * log_g), 0.0)
    q_pow = (RET_TC - i) if reverse else (i + 1.0)
    k_pow = i if reverse else (RET_TC - 1.0 - i)
    m = jnp.arange(n_ctx, dtype=jnp.float32)
    c_pow = m if reverse else (n_ctx - 1.0 - m)
    bc = lambda p: jnp.broadcast_to(jnp.exp(p[None, :, None] * log_g), (RET_HEADS, p.shape[0], RET_KEY))
    cdec = jnp.broadcast_to(jnp.exp(RET_TC * log_g), (RET_HEADS, 1, RET_VAL))
    return dmat, bc(q_pow), bc(k_pow), bc(c_pow), cdec


def _rope_tables(n_tokens):
    pos_row = (jnp.arange(n_tokens) // GRID_W).astype(jnp.float32)
    pos_col = (jnp.arange(n_tokens) % GRID_W).astype(jnp.float32)
    nf = RET_KEY // 4
    inv_freq = ROPE_BASE ** (-jnp.arange(nf, dtype=jnp.float32) / nf)
    cos, sin = [], []
    for pos in (pos_row, pos_col):
        ang = pos[:, None] * inv_freq[None, :]
        cos += [jnp.cos(ang), jnp.cos(ang)]
        sin += [-jnp.sin(ang), jnp.sin(ang)]
    return jnp.concatenate(cos, axis=1), jnp.concatenate(sin, axis=1)


def _retention_direction(z, zc, cos, sin, gamma, reverse, prev):
    b, t, _ = z.shape
    n_ctx = zc.shape[1]
    nc = t // RET_TC
    dmat, qdec, kdec, kcdec, cdec = _retention_tables(gamma, reverse, n_ctx)
    ch = (lambda c: nc - 1 - c) if reverse else (lambda c: c)
    kq, kv = RET_QK // RET_KEY, (2 * RET_QK) // RET_VAL
    g_off = (2 * RET_QK + (2 if reverse else 1) * RET_VD) // RET_VAL
    in_specs = [
        pl.BlockSpec((1, RET_TC, RET_KEY), lambda bi, h, c: (bi, ch(c), h)),
        pl.BlockSpec((1, RET_TC, RET_KEY), lambda bi, h, c: (bi, ch(c), kq + h)),
        pl.BlockSpec((1, RET_TC, RET_VAL), lambda bi, h, c: (bi, ch(c), kv + h)),
        pl.BlockSpec((1, RET_TC, RET_VAL), lambda bi, h, c: (bi, ch(c), g_off + h)),
        pl.BlockSpec((1, n_ctx, RET_KEY), lambda bi, h, c: (bi, 0, h)),
        pl.BlockSpec((1, n_ctx, RET_VAL), lambda bi, h, c: (bi, 0, RET_QK // RET_VAL + h)),
        pl.BlockSpec((RET_TC, RET_KEY), lambda bi, h, c: (ch(c), 0)),
        pl.BlockSpec((RET_TC, RET_KEY), lambda bi, h, c: (ch(c), 0)),
        pl.BlockSpec((1, RET_TC, RET_TC), lambda bi, h, c: (h, 0, 0)),
        pl.BlockSpec((1, RET_TC, RET_KEY), lambda bi, h, c: (h, 0, 0)),
        pl.BlockSpec((1, RET_TC, RET_KEY), lambda bi, h, c: (h, 0, 0)),
        pl.BlockSpec((1, n_ctx, RET_KEY), lambda bi, h, c: (h, 0, 0)),
        pl.BlockSpec((1, 1, RET_VAL), lambda bi, h, c: (h, 0, 0)),
    ]
    args = [z, z, z, z, zc, zc, cos, sin, dmat, qdec, kdec, kcdec, cdec]
    if prev is not None:
        in_specs.append(pl.BlockSpec((1, RET_TC, RET_VAL), lambda bi, h, c: (bi, ch(c), h)))
        args.append(prev)
    return pl.pallas_call(
        functools.partial(_retention_kernel, has_prev=prev is not None),
        grid=(b, RET_HEADS, nc),
        in_specs=in_specs,
        out_specs=pl.BlockSpec((1, RET_TC, RET_VAL), lambda bi, h, c: (bi, ch(c), h)),
        out_shape=jax.ShapeDtypeStruct((b, t, RET_VD), jnp.float32 if prev is None else jnp.bfloat16),
        scratch_shapes=[pltpu.VMEM((RET_KEY, RET_VAL), jnp.float32)],
        compiler_params=pltpu.CompilerParams(dimension_semantics=("parallel", "parallel", "arbitrary"),
                                             vmem_limit_bytes=VMEM_LIMIT_BYTES),
        name="retention_bwd" if reverse else "retention_fwd",
    )(*args)


def retention_mixer(z, zc, decay_f, decay_b):
    cos, sin = _rope_tables(z.shape[1])
    gamma_f = 1.0 - jnp.exp2(-decay_f)
    gamma_b = 1.0 - jnp.exp2(-decay_b)
    part = _retention_direction(z, zc, cos, sin, gamma_b, True, None)
    return _retention_direction(z, zc, cos, sin, gamma_f, False, part)


def _moe_kernel(te_ref, tv_ref, tok_ref, tok_next_ref, h_ref, wg_ref, wu_ref, wd_ref, o_ref, x_ref, acc_ref, sem):
    i = pl.program_id(0)
    j = pl.program_id(1)
    rows = x_ref.shape[1]
    slot = i % 2

    def gather(toks, buf, wait):
        def one(r, carry):
            cp = pltpu.make_async_copy(h_ref.at[pl.ds(0 if wait else toks[0, 0, r], 1)],
                                       x_ref.at[buf, pl.ds(r, 1)], sem.at[buf])
            cp.wait() if wait else cp.start()
            return carry
        lax.fori_loop(0, rows, one, 0, unroll=MOE_DMA_UNROLL)

    @pl.when(tv_ref[i] > 0)
    def _():
        @pl.when(j == 0)
        def _():
            @pl.when(i == 0)
            def _():
                gather(tok_ref, slot, False)

            gather(tok_ref, slot, True)
            nxt = jnp.minimum(i + 1, pl.num_programs(0) - 1)

            @pl.when(jnp.logical_and(i + 1 < pl.num_programs(0), tv_ref[nxt] > 0))
            def _():
                gather(tok_next_ref, 1 - slot, False)

            acc_ref[...] = jnp.zeros_like(acc_ref)

        x = x_ref[slot].astype(jnp.bfloat16)
        g = jnp.dot(x, wg_ref[0].astype(jnp.bfloat16), preferred_element_type=jnp.float32)
        u = jnp.dot(x, wu_ref[0].astype(jnp.bfloat16), preferred_element_type=jnp.float32)
        a = (_silu(g) * u).astype(jnp.bfloat16)
        acc_ref[...] += jnp.dot(a, wd_ref[0].astype(jnp.bfloat16), preferred_element_type=jnp.float32)

    last = j == pl.num_programs(1) - 1

    @pl.when(jnp.logical_and(last, tv_ref[i] > 0))
    def _():
        o_ref[...] = acc_ref[...]

    @pl.when(jnp.logical_and(last, tv_ref[i] == 0))
    def _():
        o_ref[...] = jnp.zeros_like(o_ref)


def moe_experts(h, slot_tok, tile_expert, tile_valid, w_gate, w_up, w_down):
    n_tiles = slot_tok.shape[0]
    d = h.shape[1]
    ff = w_gate.shape[2]
    grid_spec = pltpu.PrefetchScalarGridSpec(
        num_scalar_prefetch=2,
        grid=(n_tiles, ff // MOE_TF),
        in_specs=[
            pl.BlockSpec((1, 1, MOE_TM), lambda i, j, te, tv: (i, 0, 0), memory_space=pltpu.SMEM),
            pl.BlockSpec((1, 1, MOE_TM), lambda i, j, te, tv: (jnp.minimum(i + 1, n_tiles - 1), 0, 0),
                         memory_space=pltpu.SMEM),
            pl.BlockSpec(memory_space=pl.ANY),
            pl.BlockSpec((1, d, MOE_TF), lambda i, j, te, tv: (te[i], 0, j)),
            pl.BlockSpec((1, d, MOE_TF), lambda i, j, te, tv: (te[i], 0, j)),
            pl.BlockSpec((1, MOE_TF, d), lambda i, j, te, tv: (te[i], j, 0)),
        ],
        out_specs=pl.BlockSpec((MOE_TM, d), lambda i, j, te, tv: (i, 0)),
        scratch_shapes=[pltpu.VMEM((2, MOE_TM, d), jnp.float32), pltpu.VMEM((MOE_TM, d), jnp.float32),
                        pltpu.SemaphoreType.DMA((2,))],
    )
    return pl.pallas_call(
        _moe_kernel,
        grid_spec=grid_spec,
        out_shape=jax.ShapeDtypeStruct((n_tiles * MOE_TM, d), jnp.float32),
        compiler_params=pltpu.CompilerParams(dimension_semantics=("arbitrary", "arbitrary"),
                                             vmem_limit_bytes=VMEM_LIMIT_BYTES),
        name="moe_experts",
    )(tile_expert, tile_valid, slot_tok, slot_tok, h, w_gate, w_up, w_down)


def moe_swiglu(h, router, w_gate, w_up, w_down):
    b, t, d = h.shape
    n = b * t
    hf = h.reshape(n, d)
    logits = jnp.dot(hf, router, precision=lax.Precision.HIGHEST)
    top_val, top_idx = lax.top_k(logits, TOP_K)
    gate = jax.nn.softmax(top_val, axis=-1)
    flat_e = top_idx.reshape(-1)
    onehot = (flat_e[:, None] == jnp.arange(N_EXPERTS, dtype=flat_e.dtype)[None, :]).astype(jnp.int32)
    csum = jnp.cumsum(onehot, axis=0)
    counts = csum[-1]
    rank = jnp.sum((csum - onehot) * onehot, axis=1)
    padded = (counts + MOE_TM - 1) // MOE_TM * MOE_TM
    pad_end = jnp.cumsum(padded)
    pad_start = pad_end - padded
    slot = (pad_start[flat_e] + rank).astype(jnp.int32)
    n_tiles = (n * TOP_K) // MOE_TM + N_EXPERTS
    slots = n_tiles * MOE_TM
    tok = jnp.arange(n * TOP_K, dtype=jnp.int32) // TOP_K
    slot_tok = jnp.zeros((slots,), jnp.int32).at[slot].set(tok)
    tile_start = jnp.arange(n_tiles, dtype=jnp.int32) * MOE_TM
    tile_expert = jnp.minimum(jnp.sum(pad_end[None, :] <= tile_start[:, None], axis=1),
                              N_EXPERTS - 1).astype(jnp.int32)
    tile_valid = (tile_start < pad_end[-1]).astype(jnp.int32)
    yb = moe_experts(hf, slot_tok.reshape(n_tiles, 1, MOE_TM), tile_expert, tile_valid, w_gate, w_up, w_down)
    slot2 = slot.reshape(n, TOP_K)
    out = gate[:, 0:1] * yb[slot2[:, 0]] + gate[:, 1:2] * yb[slot2[:, 1]]
    return out.reshape(b, t, d)


def _row_select(i, tm, n_ctx, ctx_vec, lat_vec):
    if n_ctx == 0:
        return lat_vec
    row = i * tm + lax.broadcasted_iota(jnp.int32, (tm, 1), 0)
    return jnp.where(row < n_ctx, ctx_vec, lat_vec)


def _post_norm_rows(x, update, g, b):
    y = DEEPNORM_ALPHA * x + update
    mu = jnp.mean(y, axis=-1, keepdims=True)
    yc = y - mu
    var = jnp.mean(yc * yc, axis=-1, keepdims=True)
    return yc * lax.rsqrt(var + LN_EPS) * g + b


def _matmul_postnorm_kernel(a_ref, w_ref, x_ref, lat_ref, ctx_ref, g_ref, b_ref, o_ref, *, n_ctx):
    tm = a_ref.shape[1]
    o = jnp.dot(a_ref[0].astype(jnp.bfloat16), w_ref[...], preferred_element_type=jnp.float32)
    gate = _row_select(pl.program_id(1), tm, n_ctx, ctx_ref[...], lat_ref[0])
    o_ref[0] = _post_norm_rows(x_ref[0], gate * o, g_ref[...], b_ref[...])


def matmul_postnorm(a, w, x, gate_lat, gate_ctx, ln_g, ln_b, n_ctx):
    b, length, k = a.shape
    d = w.shape[1]
    tm = _pick_tile(length, (768, 512, 256))
    vec = pl.BlockSpec((1, d), lambda bi, i: (0, 0))
    return pl.pallas_call(
        functools.partial(_matmul_postnorm_kernel, n_ctx=n_ctx),
        grid=(b, length // tm),
        in_specs=[pl.BlockSpec((1, tm, k), lambda bi, i: (bi, i, 0)),
                  pl.BlockSpec((k, d), lambda bi, i: (0, 0)),
                  pl.BlockSpec((1, tm, d), lambda bi, i: (bi, i, 0)),
                  pl.BlockSpec((1, 1, d), lambda bi, i: (bi, 0, 0)), vec, vec, vec],
        out_specs=pl.BlockSpec((1, tm, d), lambda bi, i: (bi, i, 0)),
        out_shape=jax.ShapeDtypeStruct((b, length, d), jnp.float32),
        compiler_params=pltpu.CompilerParams(dimension_semantics=("parallel", "parallel"),
                                             vmem_limit_bytes=VMEM_LIMIT_BYTES),
        name="matmul_postnorm",
    )(a, w.astype(jnp.bfloat16), x, gate_lat[:, None, :], gate_ctx[None, :], ln_g[None, :], ln_b[None, :])


def _ffn_postnorm_kernel(x_ref, lat_ref, ctx_ref, wg_ref, wu_ref, wd_ref, g_ref, b_ref, o_ref, h_ref, acc_ref,
                         *, n_ctx):
    i = pl.program_id(1)
    j = pl.program_id(2)
    tm = x_ref.shape[1]

    def vec(k):
        return _row_select(i, tm, n_ctx, ctx_ref[k:k + 1, :], lat_ref[0, k:k + 1, :])

    @pl.when(j == 0)
    def _():
        h_ref[...] = (x_ref[0] * (1.0 + vec(1)) + vec(0)).astype(jnp.bfloat16)
        acc_ref[...] = jnp.zeros_like(acc_ref)

    h = h_ref[...]
    gt = jnp.dot(h, wg_ref[...], preferred_element_type=jnp.float32)
    up = jnp.dot(h, wu_ref[...], preferred_element_type=jnp.float32)
    acc_ref[...] += jnp.dot((_silu(gt) * up).astype(jnp.bfloat16), wd_ref[...], preferred_element_type=jnp.float32)

    @pl.when(j == pl.num_programs(2) - 1)
    def _():
        o_ref[0] = _post_norm_rows(x_ref[0], vec(2) * acc_ref[...], g_ref[...], b_ref[...])


def ffn_postnorm(x, mod_lat, mod_ctx, w_gate, w_up, w_down, ln_g, ln_b, n_ctx):
    b, length, d = x.shape
    ff = w_gate.shape[1]
    tm = _pick_tile(length, (384, 512, 256))
    tf = _pick_tile(ff, (1408, 512, 256))
    vec = pl.BlockSpec((1, d), lambda bi, i, j: (0, 0))
    return pl.pallas_call(
        functools.partial(_ffn_postnorm_kernel, n_ctx=n_ctx),
        grid=(b, length // tm, ff // tf),
        in_specs=[pl.BlockSpec((1, tm, d), lambda bi, i, j: (bi, i, 0)),
                  pl.BlockSpec((1, 3, d), lambda bi, i, j: (bi, 0, 0)),
                  pl.BlockSpec((3, d), lambda bi, i, j: (0, 0)),
                  pl.BlockSpec((d, tf), lambda bi, i, j: (0, j)),
                  pl.BlockSpec((d, tf), lambda bi, i, j: (0, j)),
                  pl.BlockSpec((tf, d), lambda bi, i, j: (j, 0)), vec, vec],
        out_specs=pl.BlockSpec((1, tm, d), lambda bi, i, j: (bi, i, 0)),
        out_shape=jax.ShapeDtypeStruct((b, length, d), jnp.float32),
        scratch_shapes=[pltpu.VMEM((tm, d), jnp.bfloat16), pltpu.VMEM((tm, d), jnp.float32)],
        compiler_params=pltpu.CompilerParams(dimension_semantics=("parallel", "parallel", "arbitrary"),
                                             vmem_limit_bytes=VMEM_LIMIT_BYTES),
        name="ffn_postnorm",
    )(x, mod_lat, mod_ctx, w_gate.astype(jnp.bfloat16), w_up.astype(jnp.bfloat16), w_down.astype(jnp.bfloat16),
      ln_g[None, :], ln_b[None, :])


def _even_layer(x, xc, sc, scc, mod_w, mod_b, w_in, mla_q_norm, mla_wq_up, mla_kv_norm, mla_wkv_up,
                rwkv_conv, rwkv_w0_f, rwkv_w2_f, rwkv_w0_b, rwkv_w2_b, rwkv_a0_f, rwkv_a2_f, rwkv_a0_b,
                rwkv_a2_b, rwkv_g2, rwkv_k_k, rwkv_k_a, rwkv_r_k, rwkv_lnx_g, rwkv_lnx_b, w_out,
                ln1_g, ln1_b, ffn_w_gate, ffn_w_up, ffn_w_down, ln2_g, ln2_b):
    n_ctx = xc.shape[1]
    m = _adaln(sc, mod_w, mod_b)
    mc = _adaln(scc, mod_w, mod_b)
    xa = jnp.concatenate([xc, x], axis=1)
    is_ctx = (jnp.arange(xa.shape[1]) < n_ctx)[None, :, None]

    def rows(k):
        return jnp.where(is_ctx, mc[k][None, None, :], m[k][:, None, :])

    h = _modulate(xa, rows(0), rows(1)).astype(jnp.bfloat16)
    o_mla = _mla_mixer(h, n_ctx, w_in, mla_q_norm, mla_wq_up, mla_kv_norm, mla_wkv_up)
    o_rwkv = _rwkv7_mixer(mm(h, w_in[:, MLA_IN:]), n_ctx, rwkv_conv, rwkv_w0_f, rwkv_w2_f, rwkv_w0_b, rwkv_w2_b,
                          rwkv_a0_f, rwkv_a2_f, rwkv_a0_b, rwkv_a2_b, rwkv_g2, rwkv_k_k, rwkv_k_a,
                          rwkv_r_k, rwkv_lnx_g, rwkv_lnx_b)
    mix = jnp.concatenate([o_mla.astype(jnp.bfloat16), o_rwkv], axis=-1)
    xa = matmul_postnorm(mix, w_out, xa, m[2], mc[2], ln1_g, ln1_b, n_ctx)
    xa = ffn_postnorm(xa, jnp.stack(m[3:6], axis=1), jnp.stack(mc[3:6], axis=0),
                      ffn_w_gate, ffn_w_up, ffn_w_down, ln2_g, ln2_b, n_ctx)
    return xa[:, n_ctx:], xa[:, :n_ctx]


def _odd_layer(x, xc, sc, scc, mod_w, mod_b, w_in, ret_decay_f, ret_decay_b, w_out, ln1_g, ln1_b,
               router, moe_w_gate, moe_w_up, moe_w_down, ln2_g, ln2_b):
    m = _adaln(sc, mod_w, mod_b)
    c_shift, c_scale = jnp.split(
        jnp.dot(scc, mod_w[:, :2 * D_MODEL], precision=lax.Precision.HIGHEST) + mod_b[:2 * D_MODEL], 2)
    z = mm(_modulate(x, m[0][:, None, :], m[1][:, None, :]), w_in, jnp.bfloat16)
    zc = mm(_modulate(xc, c_shift, c_scale), w_in[:, RET_QK:2 * RET_QK + RET_VD], jnp.bfloat16)
    mix = retention_mixer(z, zc, ret_decay_f, ret_decay_b)
    x = matmul_postnorm(mix, w_out, x, m[2], jnp.zeros_like(m[2][0]), ln1_g, ln1_b, 0)
    y = moe_swiglu(_modulate(x, m[3][:, None, :], m[4][:, None, :]), router, moe_w_gate, moe_w_up, moe_w_down)
    return _post_norm(x, m[5][:, None, :] * y, ln2_g, ln2_b)


def kernel(x, c, ctx, c_ctx, l0_mod_w, l0_mod_b, l0_w_in, l0_mla_q_norm, l0_mla_wq_up, l0_mla_kv_norm, l0_mla_wkv_up, l0_rwkv_conv, l0_rwkv_w0_f, l0_rwkv_w2_f, l0_rwkv_w0_b, l0_rwkv_w2_b, l0_rwkv_a0_f, l0_rwkv_a2_f, l0_rwkv_a0_b, l0_rwkv_a2_b, l0_rwkv_g2, l0_rwkv_k_k, l0_rwkv_k_a, l0_rwkv_r_k, l0_rwkv_lnx_g, l0_rwkv_lnx_b, l0_w_out, l0_ln1_g, l0_ln1_b, l0_ffn_w_gate, l0_ffn_w_up, l0_ffn_w_down, l0_ln2_g, l0_ln2_b, l1_mod_w, l1_mod_b, l1_w_in, l1_ret_decay_f, l1_ret_decay_b, l1_w_out, l1_ln1_g, l1_ln1_b, l1_router, l1_moe_w_gate, l1_moe_w_up, l1_moe_w_down, l1_ln2_g, l1_ln2_b):
    even_params = (l0_mod_w, l0_mod_b, l0_w_in, l0_mla_q_norm, l0_mla_wq_up, l0_mla_kv_norm, l0_mla_wkv_up,
                   l0_rwkv_conv, l0_rwkv_w0_f, l0_rwkv_w2_f, l0_rwkv_w0_b, l0_rwkv_w2_b, l0_rwkv_a0_f,
                   l0_rwkv_a2_f, l0_rwkv_a0_b, l0_rwkv_a2_b, l0_rwkv_g2, l0_rwkv_k_k, l0_rwkv_k_a, l0_rwkv_r_k,
                   l0_rwkv_lnx_g, l0_rwkv_lnx_b, l0_w_out, l0_ln1_g, l0_ln1_b, l0_ffn_w_gate, l0_ffn_w_up,
                   l0_ffn_w_down, l0_ln2_g, l0_ln2_b)
    odd_params = (l1_mod_w, l1_mod_b, l1_w_in, l1_ret_decay_f, l1_ret_decay_b, l1_w_out, l1_ln1_g, l1_ln1_b,
                  l1_router, l1_moe_w_gate, l1_moe_w_up, l1_moe_w_down, l1_ln2_g, l1_ln2_b)
    sc = _silu(c)
    scc = _silu(c_ctx)
    x, xc = _even_layer(x, ctx, sc, scc, *even_params)
    return _odd_layer(x, xc, sc, scc, *odd_params)
```
